```python
import math
import jax
import jax.numpy as jnp
from jax import lax
import numpy as np

D_MODEL = 2048
BATCH = 16
SEQ = 2048
DEPTH = 4
DEC_BATCH = 32
DEC_SEQ = 32
PAST_LEN = 1024

CHUNK = 64
A_DK = 128
A_DV = 128
A_W = D_MODEL // 2
A_HEADS = A_W // A_DK
B_W = D_MODEL // 2
B_HEADDIM = 64
B_HEADS = B_W // B_HEADDIM
B_GROUPS = 2
B_DSTATE = 128
CONV_W = 4
B_CONV_DIM = B_W + 2 * B_GROUPS * B_DSTATE
C_W = D_MODEL // 2
C_GROUPS = 4
CMLP_CHUNK = 128
N_BRANCH = 3
D_FF = 4 * D_MODEL
IN_SIZES = (A_W, A_W, A_W, A_W, B_W, B_CONV_DIM, B_HEADS, C_W, C_W, N_BRANCH * D_MODEL)
IN_TOTAL = 4 * A_W + B_W + B_CONV_DIM + B_HEADS + 2 * C_W + N_BRANCH * D_MODEL
NORM_EPS = 1e-6
LB_FLOOR = 1e-30

kernel_name = 'hybrid_hgrn2_ssd_gmlp_stream_step'


def rms_norm(x, g):
    xf = x.astype(jnp.float32)
    y = xf * lax.rsqrt(jnp.mean(xf * xf, axis=-1, keepdims=True) + NORM_EPS)
    return (y * g.astype(jnp.float32)).astype(x.dtype)


def group_rms_norm(x, g, groups):
    shp = x.shape
    xg = x.astype(jnp.float32).reshape(shp[:-1] + (groups, shp[-1] // groups))
    y = xg * lax.rsqrt(jnp.mean(xg * xg, axis=-1, keepdims=True) + NORM_EPS)
    return y.reshape(shp) * g.astype(jnp.float32)


def layer_norm(x, g, b):
    xf = x.astype(jnp.float32)
    mu = jnp.mean(xf, axis=-1, keepdims=True)
    var = jnp.mean(jnp.square(xf - mu), axis=-1, keepdims=True)
    y = (xf - mu) * lax.rsqrt(var + NORM_EPS) * g.astype(jnp.float32) + b.astype(jnp.float32)
    return y.astype(x.dtype)


def masked_exp(diff, mask):
    return jnp.where(mask, jnp.exp(jnp.where(mask, diff, 0.0)), 0.0)


def segsum_exp(cs):
    t = cs.shape[-1]
    mask = jnp.tril(jnp.ones((t, t), dtype=bool))
    return masked_exp(cs[..., :, None] - cs[..., None, :], mask)


def gla_chunked(q, k, v, log_f, s0, qc):
    bsz, L, H, _ = q.shape
    n = L // qc
    causal = jnp.tril(jnp.ones((qc, qc), dtype=bool))[None, :, :, None, None]

    def to_blocks(t):
        return jnp.moveaxis(t.reshape((bsz, n, qc) + t.shape[2:]), 1, 0)

    def step(s, inp):
        qb, kb, vb, gb = inp
        b = jnp.cumsum(gb, axis=1)
        o_inter = jnp.einsum('bthk,bhkv->bthv', qb * jnp.exp(b), s)
        decay = masked_exp(b[:, :, None] - b[:, None, :], causal)
        att = jnp.einsum('bthk,bshk,btshk->bhts', qb, kb, decay)
        o_intra = jnp.einsum('bhts,bshv->bthv', att, vb)
        b_last = b[:, -1]
        s_new = jnp.exp(b_last)[..., None] * s + jnp.einsum(
            'bshk,bshv->bhkv', kb * jnp.exp(b_last[:, None] - b), vb)
        return s_new, o_inter + o_intra

    s_fin, o = lax.scan(step, s0, (to_blocks(q), to_blocks(k), to_blocks(v), to_blocks(log_f)))
    o = jnp.moveaxis(o, 0, 1).reshape(bsz, L, H, v.shape[-1])
    return o, s_fin


def ssd_chunked(xh, dt, a_neg, bh, ch, s0, qc):
    bsz, L, H, P = xh.shape
    nc = L // qc

    def blk(t):
        return t.reshape((bsz, nc, qc) + t.shape[2:])

    x_, dt_, b_, c_ = blk(xh), blk(dt), blk(bh), blk(ch)
    a = jnp.moveaxis(dt_ * a_neg, -1, 1)
    a_cum = jnp.cumsum(a, axis=-1)
    lmat = segsum_exp(a_cum)
    xdt = x_ * dt_[..., None]
    y_diag = jnp.einsum('bclhn,bcshn,bhcls,bcshp->bclhp', c_, b_, lmat, xdt)
    decay_states = jnp.exp(a_cum[..., -1:] - a_cum)
    states = jnp.einsum('bcshn,bhcs,bcshp->bchpn', b_, decay_states, xdt)
    states = jnp.concatenate([s0[:, None], states], axis=1)
    chunk_cs = jnp.cumsum(jnp.pad(a_cum[..., -1], ((0, 0), (0, 0), (1, 0))), axis=-1)
    decay_chunk = segsum_exp(chunk_cs)
    new_states = jnp.einsum('bhzc,bchpn->bzhpn', decay_chunk, states)
    states_in, s_fin = new_states[:, :-1], new_states[:, -1]
    y_off = jnp.einsum('bclhn,bchpn,bhcl->bclhp', c_, states_in, jnp.exp(a_cum))
    return (y_diag + y_off).reshape(bsz, L, H, P), s_fin


def chunk_token_mix(v, ws, bs):
    bsz, L, W = v.shape
    lc = min(L, CMLP_CHUNK)
    n = L // lc
    w = jnp.tril(ws)[:, :lc, :lc]
    vv = v.reshape(bsz, n, lc, C_GROUPS, W // C_GROUPS)
    out = jnp.einsum('gts,bnsgd->bntgd', w, vv) + bs[:, :lc].T[None, None, :, :, None]
    return out.reshape(bsz, L, W)


def sq_relu_mlp(h, w_up, w_down):
    return jnp.square(jax.nn.relu(h @ w_up)) @ w_down


def mixer_block(h, s_hgrn, s_ssm, conv_buf, lb, w_in, hgrn_onorm_g, ssm_conv_w, ssm_conv_b,
                ssm_dt_bias, ssm_a_log, ssm_d, ssm_onorm_g, cmlp_ln_g, cmlp_ln_b, cmlp_ws, cmlp_bs,
                w_branch, w_out):
    bsz, L, _ = h.shape
    dtp = h.dtype
    f32 = jnp.float32
    qc = min(L, CHUNK)
    split_at = [int(i) for i in np.cumsum(IN_SIZES)[:-1]]
    a_q, a_f, a_i, a_g, b_z, b_xbc, b_dt, c_u, c_v, gate_in = jnp.split(h @ w_in, split_at, axis=-1)

    lbf = lb.astype(f32).reshape(A_HEADS, A_DK)
    zf = a_f.astype(f32).reshape(bsz, L, A_HEADS, A_DK)
    log_lb = jnp.log(jnp.maximum(lbf, LB_FLOOR))
    log_f = jnp.logaddexp(log_lb, jnp.log1p(-lbf) + jax.nn.log_sigmoid(zf))
    k_in = (1.0 - lbf) * jax.nn.sigmoid(-zf)
    q = jax.nn.silu(a_q.astype(f32)).reshape(bsz, L, A_HEADS, A_DK)
    i_v = a_i.astype(f32).reshape(bsz, L, A_HEADS, A_DV)
    o_a, s_hgrn_new = gla_chunked(q, k_in, i_v, log_f, s_hgrn.astype(f32), qc)
    y_a = group_rms_norm(o_a.reshape(bsz, L, A_W), hgrn_onorm_g, A_HEADS) * jax.nn.silu(a_g.astype(f32))

    xpad = jnp.concatenate([conv_buf.astype(dtp), b_xbc], axis=1)
    conv = ssm_conv_b.astype(dtp)
    for j in range(CONV_W):
        conv = conv + xpad[:, j:j + L] * ssm_conv_w[j]
    xbc = jax.nn.silu(conv)
    conv_new = xpad[:, L:]
    x_s, b_s, c_s = jnp.split(xbc, [B_W, B_W + B_GROUPS * B_DSTATE], axis=-1)
    rep = B_HEADS // B_GROUPS
    xh = x_s.astype(f32).reshape(bsz, L, B_HEADS, B_HEADDIM)
    bh = jnp.repeat(b_s.astype(f32).reshape(bsz, L, B_GROUPS, B_DSTATE), rep, axis=2)
    ch = jnp.repeat(c_s.astype(f32).reshape(bsz, L, B_GROUPS, B_DSTATE), rep, axis=2)
    dt = jax.nn.softplus(b_dt.astype(f32) + ssm_dt_bias.astype(f32))
    a_neg = -jnp.exp(ssm_a_log.astype(f32))
    y_b, s_ssm_new = ssd_chunked(xh, dt, a_neg, bh, ch, s_ssm.astype(f32), qc)
    y_b = (y_b + ssm_d.astype(f32)[:, None] * xh).reshape(bsz, L, B_W)
    y_b = group_rms_norm(y_b * jax.nn.silu(b_z.astype(f32)), ssm_onorm_g, B_GROUPS)

    u = jax.nn.gelu(c_u, approximate=False)
    v = layer_norm(jax.nn.gelu(c_v, approximate=False), cmlp_ln_g, cmlp_ln_b)
    y_c = u * chunk_token_mix(v, cmlp_ws, cmlp_bs)

    gates = jax.nn.sigmoid(gate_in).reshape(bsz, L, N_BRANCH, D_MODEL)
    merged = (gates[:, :, 0] * (y_a.astype(dtp) @ w_branch[:A_W])
              + gates[:, :, 1] * (y_b.astype(dtp) @ w_branch[A_W:A_W + B_W])
              + gates[:, :, 2] * (y_c @ w_branch[A_W + B_W:]))
    return (merged @ w_out, s_hgrn_new, s_ssm_new, conv_new, v)


def run_trunk(x, c, st_hgrn, st_ssm, st_conv, lbs, keep_chunk_rows, norm1_g, norm2_g, w_mod, b_mod,
              w_in, hgrn_onorm_g, ssm_conv_w, ssm_conv_b, ssm_dt_bias, ssm_a_log, ssm_d, ssm_onorm_g,
              cmlp_ln_g, cmlp_ln_b, cmlp_ws, cmlp_bs, w_branch, w_out, w_up, w_down, final_g):
    cs = jax.nn.silu(c)
    hgrn_out, ssm_out, conv_out, v_out = [], [], [], []
    for l in range(DEPTH):
        mod = cs @ w_mod[l] + b_mod[l]
        sh1, sc1, g1, sh2, sc2, g2 = [m[:, None, :] for m in jnp.split(mod, 6, axis=-1)]
        h = rms_norm(x, norm1_g[l]) * (1 + sc1) + sh1
        y, s_h, s_s, cb, v = mixer_block(
            h, st_hgrn[l], st_ssm[l], st_conv[l], lbs[l], w_in[l], hgrn_onorm_g[l], ssm_conv_w[l],
            ssm_conv_b[l], ssm_dt_bias[l], ssm_a_log[l], ssm_d[l], ssm_onorm_g[l], cmlp_ln_g[l],
            cmlp_ln_b[l], cmlp_ws[l], cmlp_bs[l], w_branch[l], w_out[l])
        x = x + g1 * y
        h = rms_norm(x, norm2_g[l]) * (1 + sc2) + sh2
        x = x + g2 * sq_relu_mlp(h, w_up[l], w_down[l])
        hgrn_out.append(s_h.astype(x.dtype))
        ssm_out.append(s_s.astype(x.dtype))
        conv_out.append(cb.astype(x.dtype))
        if keep_chunk_rows:
            v_out.append(v)
    y = rms_norm(x, final_g)
    v_stack = jnp.stack(v_out) if keep_chunk_rows else None
    return (y, jnp.stack(hgrn_out), jnp.stack(ssm_out), jnp.stack(conv_out), v_stack)


def setup_inputs(seed: int = 0) -> dict:
    key = jax.random.key(seed)
    ks = jax.random.split(key, 32)
    f32 = jnp.float32

    def nrm(k, shape, s):
        return s * jax.random.normal(k, shape, f32)

    dt0 = jnp.exp(jax.random.uniform(ks[16], (DEPTH, B_HEADS), f32, math.log(1e-3), math.log(1e-1)))
    return {
        'x_prompt': nrm(ks[0], (BATCH, SEQ, D_MODEL), 1.0),
        'x_sample': nrm(ks[1], (DEC_BATCH, DEC_SEQ, D_MODEL), 1.0),
        'state_hgrn': nrm(ks[2], (DEPTH, DEC_BATCH, A_HEADS, A_DK, A_DV), 0.5),
        'state_ssm': nrm(ks[3], (DEPTH, DEC_BATCH, B_HEADS, B_HEADDIM, B_DSTATE), 0.1),
        'state_conv': nrm(ks[4], (DEPTH, DEC_BATCH, CONV_W - 1, B_CONV_DIM), 1.0),
        'c_prompt': nrm(ks[5], (BATCH, D_MODEL), 1.0),
        'c_sample': nrm(ks[6], (DEC_BATCH, D_MODEL), 1.0),
        'norm1_g': 1.0 + nrm(ks[7], (DEPTH, D_MODEL), 0.05),
        'norm2_g': 1.0 + nrm(ks[8], (DEPTH, D_MODEL), 0.05),
        'w_mod': nrm(ks[9], (DEPTH, D_MODEL, 6 * D_MODEL), 0.5 * D_MODEL ** -0.5),
        'b_mod': nrm(ks[10], (DEPTH, 6 * D_MODEL), 0.1),
        'w_in': nrm(ks[11], (DEPTH, D_MODEL, IN_TOTAL), D_MODEL ** -0.5),
        'hgrn_lb': nrm(ks[12], (DEPTH, A_W), 0.5),
        'hgrn_onorm_g': 1.0 + nrm(ks[13], (DEPTH, A_W), 0.05),
        'ssm_conv_w': nrm(ks[14], (DEPTH, CONV_W, B_CONV_DIM), CONV_W ** -0.5),
        'ssm_conv_b': nrm(ks[15], (DEPTH, B_CONV_DIM), 0.02),
        'ssm_dt_bias': dt0 + jnp.log(-jnp.expm1(-dt0)),
        'ssm_a_log': jnp.log(jax.random.uniform(ks[17], (DEPTH, B_HEADS), f32, 1.0, 16.0)),
        'ssm_d': 1.0 + nrm(ks[18], (DEPTH, B_HEADS), 0.1),
        'ssm_onorm_g': 1.0 + nrm(ks[19], (DEPTH, B_W), 0.05),
        'cmlp_ln_g': 1.0 + nrm(ks[20], (DEPTH, C_W), 0.05),
        'cmlp_ln_b': nrm(ks[21], (DEPTH, C_W), 0.02),
        'cmlp_ws': nrm(ks[22], (DEPTH, C_GROUPS, CMLP_CHUNK, CMLP_CHUNK), CMLP_CHUNK ** -0.5),
        'cmlp_bs': 1.0 + nrm(ks[23], (DEPTH, C_GROUPS, CMLP_CHUNK), 0.1),
        'w_branch': nrm(ks[24], (DEPTH, A_W + B_W + C_W, D_MODEL), A_W ** -0.5),
        'w_out': nrm(ks[25], (DEPTH, D_MODEL, D_MODEL), D_MODEL ** -0.5),
        'w_up': nrm(ks[26], (DEPTH, D_MODEL, D_FF), D_MODEL ** -0.5),
        'w_down': nrm(ks[27], (DEPTH, D_FF, D_MODEL), D_FF ** -0.5),
        'final_g': 1.0 + nrm(ks[28], (D_MODEL,), 0.05),
    }


def reference(x_prompt, x_sample, state_hgrn, state_ssm, state_conv, c_prompt, c_sample, norm1_g,
              norm2_g, w_mod, b_mod, w_in, hgrn_lb, hgrn_onorm_g, ssm_conv_w, ssm_conv_b, ssm_dt_bias,
              ssm_a_log, ssm_d, ssm_onorm_g, cmlp_ln_g, cmlp_ln_b, cmlp_ws, cmlp_bs, w_branch, w_out,
              w_up, w_down, final_g):
    p = jax.nn.softmax(hgrn_lb.astype(jnp.float32), axis=0)
    lbs = jnp.cumsum(p, axis=0) - p[0]
    weights = (norm1_g, norm2_g, w_mod, b_mod, w_in, hgrn_onorm_g, ssm_conv_w, ssm_conv_b, ssm_dt_bias,
               ssm_a_log, ssm_d, ssm_onorm_g, cmlp_ln_g, cmlp_ln_b, cmlp_ws, cmlp_bs, w_branch, w_out,
               w_up, w_down, final_g)
    zero_hgrn = jnp.zeros((DEPTH, BATCH, A_HEADS, A_DK, A_DV), jnp.float32)
    zero_ssm = jnp.zeros((DEPTH, BATCH, B_HEADS, B_HEADDIM, B_DSTATE), jnp.float32)
    zero_conv = jnp.zeros((DEPTH, BATCH, CONV_W - 1, B_CONV_DIM), x_prompt.dtype)
    y_prompt, hgrn_p, ssm_p, conv_p, _ = run_trunk(
        x_prompt, c_prompt, zero_hgrn, zero_ssm, zero_conv, lbs, False, *weights)
    y_sample, hgrn_s, ssm_s, conv_s, cmlp_v_s = run_trunk(
        x_sample, c_sample, state_hgrn, state_ssm, state_conv, lbs, True, *weights)
    return (y_prompt, y_sample, hgrn_p, ssm_p, conv_p, hgrn_s, ssm_s, conv_s, cmlp_v_s)
```

```python
import functools
import math

import numpy as np
import jax
import jax.numpy as jnp
from jax import lax
from jax.experimental import pallas as pl
from jax.experimental.pallas import tpu as pltpu

F32 = jnp.float32
BF16 = jnp.bfloat16

A_DK = 128
B_HEADDIM = 64
B_GROUPS = 2
B_DSTATE = 128
CONV_W = 4
C_GROUPS = 4
CMLP_CHUNK = 128
N_BRANCH = 3
SCAN_CHUNK = 64
NORM_EPS = 1e-6
LB_FLOOR = 1e-30

LANES = 128
SUBLANES = 8
VMEM_LIMIT = 56 * 1024 * 1024
DIAG = SUBLANES
DT_PAD = 512


def _cparams(sem):
    return pltpu.CompilerParams(dimension_semantics=sem, vmem_limit_bytes=VMEM_LIMIT)


def _split3(x):
    hi = x.astype(BF16)
    r1 = x - hi.astype(F32)
    mid = r1.astype(BF16)
    lo = (r1 - mid.astype(F32)).astype(BF16)
    return hi, mid, lo


def _dot3(sel, x):
    hi, mid, lo = _split3(x)
    d = lambda p: jnp.dot(sel, p, preferred_element_type=F32)
    return d(hi) + d(mid) + d(lo)


def _silu(x):
    return x / (1.0 + jnp.exp(-x))


def _softplus(x):
    return jnp.maximum(x, 0.0) + jnp.log1p(jnp.exp(-jnp.abs(x)))


def _gelu(x):
    return 0.5 * x * (1.0 + lax.erf(x * (1.0 / math.sqrt(2.0))))


def _lb_kernel(lb_ref, out_ref, *, depth):
    x = lb_ref[...]
    m = jnp.max(x, axis=0, keepdims=True)
    e = jnp.exp(x - m)
    p = e / jnp.sum(e, axis=0, keepdims=True)
    acc = jnp.zeros_like(p[0:1])
    zeros5 = jnp.zeros((SUBLANES - 3, x.shape[1]), F32)
    for l in range(depth):
        acc = acc + p[l:l + 1]
        lb = acc - p[0:1]
        out_ref[l] = jnp.concatenate(
            [jnp.log(jnp.maximum(lb, LB_FLOOR)), jnp.log1p(-lb), 1.0 - lb, zeros5], axis=0)


def _lb_consts(hgrn_lb):
    depth, aw = hgrn_lb.shape
    return pl.pallas_call(
        functools.partial(_lb_kernel, depth=depth),
        out_shape=jax.ShapeDtypeStruct((depth, SUBLANES, aw), F32),
        name="hgrn_lb",
    )(hgrn_lb.astype(F32))


def _mod_kernel(c_ref, w_ref, b_ref, o_ref):
    cs = _silu(c_ref[...]).astype(BF16)
    o_ref[...] = jnp.dot(cs, w_ref[...].astype(BF16), preferred_element_type=F32) + b_ref[...]


def _modulation(c_all, w_mod, b_mod):
    depth, d, n6 = w_mod.shape
    s = c_all.shape[0]
    tn = 1024
    return pl.pallas_call(
        _mod_kernel,
        grid=(depth, n6 // tn),
        in_specs=[
            pl.BlockSpec((s, d), lambda l, j: (0, 0)),
            pl.BlockSpec((None, d, tn), lambda l, j: (l, 0, j)),
            pl.BlockSpec((None, 1, tn), lambda l, j: (l, 0, j)),
        ],
        out_specs=pl.BlockSpec((None, s, tn), lambda l, j: (l, 0, j)),
        out_shape=jax.ShapeDtypeStruct((depth, s, n6), F32),
        compiler_params=_cparams(("arbitrary", "arbitrary")),
        name="adaln_mod",
    )(c_all, w_mod, b_mod.reshape(depth, 1, n6))


def _row_tiling(n_tok, seq_len, tm_max):
    tm = min(tm_max, n_tok)
    if seq_len >= tm:
        assert seq_len % tm == 0
        return tm, 1, seq_len // tm
    assert tm % seq_len == 0
    return tm, tm // seq_len, 1


def _mod_spec(k, nseg, tiles_per_seq, d, grid_rank):
    if grid_rank == 1:
        return pl.BlockSpec((None, nseg, 1, d), lambda i: (k, i // tiles_per_seq, 0, 0))
    return pl.BlockSpec((None, nseg, 1, d), lambda i, j: (k, i // tiles_per_seq, 0, 0))


def _per_seq(x, nseg):
    tm, d = x.shape
    return x.reshape(nseg, tm // nseg, d)


def _norm_mod_kernel(x_ref, g_ref, sc_ref, sh_ref, o_ref, *, nseg):
    x = x_ref[...]
    y = x * lax.rsqrt(jnp.mean(x * x, axis=-1, keepdims=True) + NORM_EPS) * g_ref[...]
    h = _per_seq(y, nseg) * (1.0 + sc_ref[...]) + sh_ref[...]
    o_ref[...] = h.reshape(x.shape).astype(o_ref.dtype)


def _norm_mod(x, g, mod6, k_scale, k_shift, seq_len):
    n, d = x.shape
    tm, nseg, tps = _row_tiling(n, seq_len, 512)
    return pl.pallas_call(
        functools.partial(_norm_mod_kernel, nseg=nseg),
        grid=(n // tm,),
        in_specs=[
            pl.BlockSpec((tm, d), lambda i: (i, 0)),
            pl.BlockSpec((1, d), lambda i: (0, 0)),
            _mod_spec(k_scale, nseg, tps, d, 1),
            _mod_spec(k_shift, nseg, tps, d, 1),
        ],
        out_specs=pl.BlockSpec((tm, d), lambda i: (i, 0)),
        out_shape=jax.ShapeDtypeStruct((n, d), BF16),
        compiler_params=_cparams(("arbitrary",)),
        name="norm_mod",
    )(x, g.reshape(1, d), mod6, mod6)


def _final_norm_kernel(x_ref, g_ref, o_ref):
    x = x_ref[...]
    o_ref[...] = x * lax.rsqrt(jnp.mean(x * x, axis=-1, keepdims=True) + NORM_EPS) * g_ref[...]


def _final_norm(x, g):
    n, d = x.shape
    tm = min(512, n)
    return pl.pallas_call(
        _final_norm_kernel,
        grid=(n // tm,),
        in_specs=[pl.BlockSpec((tm, d), lambda i: (i, 0)), pl.BlockSpec((1, d), lambda i: (0, 0))],
        out_specs=pl.BlockSpec((tm, d), lambda i: (i, 0)),
        out_shape=jax.ShapeDtypeStruct((n, d), F32),
        compiler_params=_cparams(("arbitrary",)),
        name="final_norm",
    )(x, g.reshape(1, d))


def _matmul_kernel(a_ref, w_ref, o_ref):
    o_ref[...] = jnp.dot(a_ref[...], w_ref[...], preferred_element_type=F32).astype(o_ref.dtype)


def _matmul(a, w, tn, out_dtype=F32):
    n, k = a.shape
    m = w.shape[1]
    tm = min(1024, n)
    return pl.pallas_call(
        _matmul_kernel,
        grid=(n // tm, m // tn),
        in_specs=[pl.BlockSpec((tm, k), lambda i, j: (i, 0)), pl.BlockSpec((k, tn), lambda i, j: (0, j))],
        out_specs=pl.BlockSpec((tm, tn), lambda i, j: (i, j)),
        out_shape=jax.ShapeDtypeStruct((n, m), out_dtype),
        compiler_params=_cparams(("arbitrary", "arbitrary")),
        name="in_proj",
    )(a, w)


def _matmul_res_kernel(a_ref, w_ref, x_ref, gate_ref, o_ref, *, nseg):
    y = jnp.dot(a_ref[...], w_ref[...], preferred_element_type=F32)
    x = x_ref[...]
    o_ref[...] = (_per_seq(x, nseg) + gate_ref[...] * _per_seq(y, nseg)).reshape(x.shape)


def _matmul_residual(a, w, x, mod6, k_gate, seq_len):
    n, k = a.shape
    d = w.shape[1]
    tn = 1024
    tm, nseg, tps = _row_tiling(n, seq_len, 1024)
    gate_spec = pl.BlockSpec((None, nseg, 1, tn), lambda i, j: (k_gate, i // tps, 0, j))
    return pl.pallas_call(
        functools.partial(_matmul_res_kernel, nseg=nseg),
        grid=(n // tm, d // tn),
        in_specs=[
            pl.BlockSpec((tm, k), lambda i, j: (i, 0)),
            pl.BlockSpec((k, tn), lambda i, j: (0, j)),
            pl.BlockSpec((tm, tn), lambda i, j: (i, j)),
            gate_spec,
        ],
        out_specs=pl.BlockSpec((tm, tn), lambda i, j: (i, j)),
        out_shape=jax.ShapeDtypeStruct((n, d), F32),
        compiler_params=_cparams(("arbitrary", "arbitrary")),
        name="out_proj",
    )(a, w, x, mod6)


def _merge_kernel(ya_ref, yb_ref, yc_ref, wa_ref, wb_ref, wc_ref, ga_ref, gb_ref, gc_ref, o_ref):
    acc = None
    for y_ref, w_ref, g_ref in ((ya_ref, wa_ref, ga_ref), (yb_ref, wb_ref, gb_ref), (yc_ref, wc_ref, gc_ref)):
        gate = 1.0 / (1.0 + jnp.exp(-g_ref[...]))
        term = gate * jnp.dot(y_ref[...], w_ref[...], preferred_element_type=F32)
        acc = term if acc is None else acc + term
    o_ref[...] = acc.astype(o_ref.dtype)


def _merge(ya, yb, yc, w_branch, p, gate_col0, d):
    n, bw = ya.shape
    tn = 512
    tm = min(1024, n)
    g0 = gate_col0 // tn
    gsteps = d // tn
    y_spec = pl.BlockSpec((tm, bw), lambda i, j: (i, 0))

    def w_spec(k):
        return pl.BlockSpec((bw, tn), lambda i, j: (k, j))

    def g_spec(k):
        return pl.BlockSpec((tm, tn), lambda i, j: (i, g0 + k * gsteps + j))

    return pl.pallas_call(
        _merge_kernel,
        grid=(n // tm, d // tn),
        in_specs=[y_spec, y_spec, y_spec, w_spec(0), w_spec(1), w_spec(2), g_spec(0), g_spec(1), g_spec(2)],
        out_specs=pl.BlockSpec((tm, tn), lambda i, j: (i, j)),
        out_shape=jax.ShapeDtypeStruct((n, d), BF16),
        compiler_params=_cparams(("arbitrary", "arbitrary")),
        name="merge",
    )(ya, yb, yc, w_branch, w_branch, w_branch, p, p, p)


def _mlp_kernel(h_ref, wu_ref, wd_ref, x_ref, gate_ref, o_ref, *, nseg):
    f = pl.program_id(1)
    a = jnp.maximum(jnp.dot(h_ref[...], wu_ref[...], preferred_element_type=F32), 0.0)
    part = jnp.dot((a * a).astype(BF16), wd_ref[...], preferred_element_type=F32)

    @pl.when(f == 0)
    def _():
        o_ref[...] = part

    @pl.when(f > 0)
    def _():
        o_ref[...] += part

    @pl.when(f == pl.num_programs(1) - 1)
    def _():
        x = x_ref[...]
        o_ref[...] = (_per_seq(x, nseg) + gate_ref[...] * _per_seq(o_ref[...], nseg)).reshape(x.shape)


def _mlp(h, w_up, w_down, x, mod6, k_gate, seq_len):
    n, d = h.shape
    ff = w_up.shape[1]
    tf = 512
    tm, nseg, tps = _row_tiling(n, seq_len, 512)
    return pl.pallas_call(
        functools.partial(_mlp_kernel, nseg=nseg),
        grid=(n // tm, ff // tf),
        in_specs=[
            pl.BlockSpec((tm, d), lambda i, f: (i, 0)),
            pl.BlockSpec((d, tf), lambda i, f: (0, f)),
            pl.BlockSpec((tf, d), lambda i, f: (f, 0)),
            pl.BlockSpec((tm, d), lambda i, f: (i, 0)),
            _mod_spec(k_gate, nseg, tps, d, 2),
        ],
        out_specs=pl.BlockSpec((tm, d), lambda i, f: (i, 0)),
        out_shape=jax.ShapeDtypeStruct((n, d), F32),
        compiler_params=_cparams(("arbitrary", "arbitrary")),
        name="mlp",
    )(h, w_up, w_down, x, mod6)


def _hgrn_levels(c):
    lv, m = [], c // 2
    while m >= DIAG:
        lv.append(m)
        m //= 2
    return tuple(lv)


def _hgrn_select(c):
    t = np.arange(c)[:, None]
    j = np.arange(c)[None, :]
    mats = [j <= t]
    for m in _hgrn_levels(c):
        mats.append(j <= (t // (2 * m)) * (2 * m) + m - 1)
    mats.append(np.ones((c, c), bool))
    return jnp.asarray(np.concatenate(mats, axis=0).astype(np.float32), dtype=BF16)


def _hgrn_kernel(*refs, c, n_chunks, has_init):
    if has_init:
        q_ref, f_ref, i_ref, g_ref, lbc_ref, gn_ref, sel_ref, s0_ref, ya_ref, s_ref, bpad, kpad = refs
    else:
        q_ref, f_ref, i_ref, g_ref, lbc_ref, gn_ref, sel_ref, ya_ref, s_ref, bpad, kpad = refs
    levels = _hgrn_levels(c)
    nl = len(levels)

    @pl.when(pl.program_id(2) == 0)
    def _():
        s_ref[...] = s0_ref[...] if has_init else jnp.zeros(s_ref.shape, F32)

    bpad[0:DIAG, :] = jnp.zeros((DIAG, A_DK), F32)
    kpad[0:DIAG, :] = jnp.zeros((DIAG, A_DK), F32)

    log_lb = lbc_ref[0:1, :]
    log1m_lb = lbc_ref[1:2, :]
    one_m_lb = lbc_ref[2:3, :]
    gn = gn_ref[...]
    sel = sel_ref[...]
    row = lax.broadcasted_iota(jnp.int32, (c, A_DK), 0)
    rr = lax.broadcasted_iota(jnp.int32, (c, c), 0)
    cc = lax.broadcasted_iota(jnp.int32, (c, c), 1)

    def chunk(ci, carry):
        r0 = pl.multiple_of(ci * c, c)
        rows = pl.ds(r0, c)
        z = f_ref[rows, :]
        aq = q_ref[rows, :]
        v = i_ref[rows, :]
        ag = g_ref[rows, :]

        log_sig = jnp.minimum(z, 0.0) - jnp.log1p(jnp.exp(-jnp.abs(z)))
        bb = log1m_lb + log_sig
        log_f = jnp.maximum(log_lb, bb) + jnp.log1p(jnp.exp(-jnp.abs(log_lb - bb)))
        k = one_m_lb / (1.0 + jnp.exp(z))
        q = _silu(aq)

        cs = _dot3(sel, log_f)
        b = cs[0:c]
        b_tot = cs[(nl + 1) * c:(nl + 2) * c]
        vb = v.astype(BF16)
        s_prev = s_ref[0, 0]

        o = jnp.dot((q * jnp.exp(b)).astype(BF16), s_prev.astype(BF16), preferred_element_type=F32)

        att = jnp.zeros((c, c), F32)
        for li, m in enumerate(levels):
            ref_b = cs[(li + 1) * c:(li + 2) * c]
            e = jnp.exp(-jnp.abs(b - ref_b))
            upper = ((row // m) % 2) == 1
            qs = jnp.where(upper, q * e, 0.0).astype(BF16)
            ks = jnp.where(upper, 0.0, k * e).astype(BF16)
            a_l = lax.dot_general(qs, ks, (((1,), (1,)), ((), ())), preferred_element_type=F32)
            att = att + jnp.where((rr // (2 * m)) == (cc // (2 * m)), a_l, 0.0)

        bpad[DIAG:DIAG + c, :] = b
        kpad[DIAG:DIAG + c, :] = k
        for dlt in range(DIAG):
            if dlt == 0:
                w = q * k
            else:
                ksh = kpad[DIAG - dlt:DIAG - dlt + c, :]
                bsh = bpad[DIAG - dlt:DIAG - dlt + c, :]
                w = q * ksh * jnp.exp(b - bsh)
            col = jnp.sum(w, axis=-1, keepdims=True)
            att = jnp.where((cc == rr - dlt) & ((rr % DIAG) >= dlt), col, att)

        o = o + jnp.dot(att.astype(BF16), vb, preferred_element_type=F32)

        kd = (k * jnp.exp(b_tot - b)).astype(BF16)
        decay_col = jnp.broadcast_to(jnp.exp(b_tot[0:1, :]), (A_DK, A_DK)).T
        s_ref[0, 0] = decay_col * s_prev + lax.dot_general(
            kd, vb, (((0,), (0,)), ((), ())), preferred_element_type=F32)

        y = o * lax.rsqrt(jnp.mean(o * o, axis=-1, keepdims=True) + NORM_EPS) * gn * _silu(ag)
        ya_ref[rows, :] = y.astype(ya_ref.dtype)
        return carry

    lax.fori_loop(0, n_chunks, chunk, 0)


def _hgrn(p, lbc, gn, s0, bsz, seq_len, heads):
    c = min(seq_len, SCAN_CHUNK)
    t_blk = min(seq_len, 256)
    nt = seq_len // t_blk
    aw = heads * A_DK

    def col_spec(seg):
        return pl.BlockSpec((t_blk, A_DK), lambda b, h, t: (b * nt + t, seg * heads + h))

    sel = _hgrn_select(c)
    in_specs = [col_spec(0), col_spec(1), col_spec(2), col_spec(3),
                pl.BlockSpec((SUBLANES, A_DK), lambda b, h, t: (0, h)),
                pl.BlockSpec((1, A_DK), lambda b, h, t: (0, h)),
                pl.BlockSpec(sel.shape, lambda b, h, t: (0, 0))]
    args = [p, p, p, p, lbc, gn.reshape(1, aw), sel]
    state_spec = pl.BlockSpec((1, 1, A_DK, A_DK), lambda b, h, t: (b, h, 0, 0))
    if s0 is not None:
        in_specs.append(state_spec)
        args.append(s0)
    return pl.pallas_call(
        functools.partial(_hgrn_kernel, c=c, n_chunks=t_blk // c, has_init=s0 is not None),
        grid=(bsz, heads, nt),
        in_specs=in_specs,
        out_specs=[pl.BlockSpec((t_blk, A_DK), lambda b, h, t: (b * nt + t, h)), state_spec],
        out_shape=[jax.ShapeDtypeStruct((bsz * seq_len, aw), BF16),
                   jax.ShapeDtypeStruct((bsz, heads, A_DK, A_DK), F32)],
        scratch_shapes=[pltpu.VMEM((c + DIAG, A_DK), F32), pltpu.VMEM((c + DIAG, A_DK), F32)],
        compiler_params=_cparams(("arbitrary", "arbitrary", "arbitrary")),
        name="hgrn2",
    )(*args)


def _ssd_kernel(*refs, t, bw, has_init):
    if has_init:
        (z_ref, xs_ref, bc_ref, dt_ref, cw_ref, cb_ref, dtb_ref, alog_ref, dsk_ref, gn_ref, exp_ref, sel_ref,
         s0_ref, c0_ref, yb_ref, s_out_ref, conv_out_ref, xpad, st) = refs
    else:
        (z_ref, xs_ref, bc_ref, dt_ref, cw_ref, cb_ref, dtb_ref, alog_ref, dsk_ref, gn_ref, exp_ref, sel_ref,
         yb_ref, s_out_ref, conv_out_ref, xpad, st) = refs
    ti = pl.program_id(1)
    gw = bw // B_GROUPS
    n_bc = B_GROUPS * B_DSTATE
    pad = SUBLANES

    @pl.when(ti == 0)
    def _():
        if has_init:
            xpad[0:pad, :] = c0_ref[0]
            st[...] = s0_ref[0].reshape(bw, B_DSTATE).T
        else:
            xpad[0:pad, :] = jnp.zeros((pad, xpad.shape[1]), F32)
            st[...] = jnp.zeros(st.shape, F32)

    xpad[pad:pad + t, 0:bw] = xs_ref[...]
    xpad[pad:pad + t, bw:bw + 2 * n_bc] = bc_ref[...]
    conv = cb_ref[...]
    for j in range(CONV_W):
        lo = pad - (CONV_W - 1) + j
        conv = conv + xpad[lo:lo + t, :] * cw_ref[j:j + 1, :]
    xbc = _silu(conv)
    x = xbc[:, 0:bw]

    dt = _softplus(dt_ref[...] + dtb_ref[...])
    a = dt * (-jnp.exp(alog_ref[...]))
    cs = _dot3(sel_ref[...], a)
    a_cum = cs[0:t]
    a_tot = cs[t:2 * t]
    ex = _dot3_rhs(jnp.concatenate([dt, a_cum, a_tot - a_cum], axis=0), exp_ref[...])
    dt_e = ex[0:t]
    acum_e = ex[t:2 * t]
    dec_e = ex[2 * t:3 * t]
    atot_e = acum_e[t - 1:t, :]

    xdt = x * dt_e
    xw = (xdt * jnp.exp(dec_e)).astype(BF16)
    xdt_b = xdt.astype(BF16)
    a_cum_t = a_cum.T
    rr = lax.broadcasted_iota(jnp.int32, (t, t), 0)
    cc = lax.broadcasted_iota(jnp.int32, (t, t), 1)
    causal = cc <= rr
    lane = lax.broadcasted_iota(jnp.int32, (t, LANES), 1)
    heads_per_group = gw // B_HEADDIM
    pairs_per_group = gw // LANES

    y_groups = []
    for g in range(B_GROUPS):
        bg = xbc[:, bw + g * B_DSTATE:bw + (g + 1) * B_DSTATE].astype(BF16)
        cg = xbc[:, bw + n_bc + g * B_DSTATE:bw + n_bc + (g + 1) * B_DSTATE].astype(BF16)
        gs = slice(g * gw, (g + 1) * gw)
        st_g = st[:, gs]
        scores = lax.dot_general(cg, bg, (((1,), (1,)), ((), ())), preferred_element_type=F32)
        y_off = jnp.dot(cg, st_g.astype(BF16), preferred_element_type=F32) * jnp.exp(acum_e[:, gs])
        st[:, gs] = jnp.exp(atot_e[:, gs]) * st_g + lax.dot_general(
            bg, xw[:, gs], (((0,), (0,)), ((), ())), preferred_element_type=F32)
        y_pairs = []
        for pr in range(pairs_per_group):
            cols = slice(g * gw + pr * LANES, g * gw + (pr + 1) * LANES)
            xp = xdt_b[:, cols]
            acc = None
            for half in range(LANES // B_HEADDIM):
                h = g * heads_per_group + pr * (LANES // B_HEADDIM) + half
                diff = a_cum[:, h:h + 1] - a_cum_t[h:h + 1, :]
                lmat = jnp.where(causal, jnp.exp(jnp.where(causal, diff, 0.0)), 0.0)
                m = (scores * lmat).astype(BF16)
                in_half = (lane // B_HEADDIM) == half
                term = jnp.dot(m, jnp.where(in_half, xp, jnp.zeros_like(xp)), preferred_element_type=F32)
                acc = term if acc is None else acc + term
            y_pairs.append(acc)
        y_groups.append(jnp.concatenate(y_pairs, axis=1) + y_off)
    y = jnp.concatenate(y_groups, axis=1) + dsk_ref[...] * x
    y = y * _silu(z_ref[...])
    outs = []
    for g in range(B_GROUPS):
        yg = y[:, g * gw:(g + 1) * gw]
        outs.append(yg * lax.rsqrt(jnp.mean(yg * yg, axis=-1, keepdims=True) + NORM_EPS))
    yb_ref[...] = (jnp.concatenate(outs, axis=1) * gn_ref[...]).astype(yb_ref.dtype)

    tail = xpad[t:t + pad, :]
    xpad[0:pad, :] = tail

    @pl.when(ti == pl.num_programs(1) - 1)
    def _():
        conv_out_ref[0] = tail
        s_out_ref[0] = st[...].T.reshape(s_out_ref.shape[1:])


def _dot3_rhs(x, sel):
    hi, mid, lo = _split3(x)
    d = lambda p: jnp.dot(p, sel, preferred_element_type=F32)
    return d(hi) + d(mid) + d(lo)


def _ssd(p, cols, prm, s0, c0, bsz, seq_len):
    bw = prm["bw"]
    heads = bw // B_HEADDIM
    n_bc = B_GROUPS * B_DSTATE
    cdim = bw + 2 * n_bc
    t = min(seq_len, 128)
    nt = seq_len // t
    tri = np.arange(t)[None, :] <= np.arange(t)[:, None]
    sel = jnp.asarray(np.concatenate([tri, np.ones((t, t), bool)], 0).astype(np.float32), dtype=BF16)
    expand = np.zeros((LANES, bw), np.float32)
    expand[np.arange(bw) // B_HEADDIM, np.arange(bw)] = 1.0
    expand = jnp.asarray(expand, dtype=BF16)

    def blk(width, off):
        return pl.BlockSpec((t, width), lambda b, ti: (b * nt + ti, off // width))

    def full(shape):
        return pl.BlockSpec(shape, lambda b, ti: (0,) * len(shape))

    in_specs = [blk(bw, cols["z"]), blk(bw, cols["xs"]), blk(2 * n_bc, cols["bc"]), blk(LANES, cols["dt"]),
                full((CONV_W, cdim)), full((1, cdim)), full((1, LANES)), full((1, LANES)),
                full((1, bw)), full((1, bw)), full(expand.shape), full(sel.shape)]
    args = [p, p, p, p, prm["conv_w"], prm["conv_b"], prm["dt_bias"], prm["a_log"], prm["d_skip"], prm["gn"],
            expand, sel]
    state_spec = pl.BlockSpec((1, heads, B_HEADDIM, B_DSTATE), lambda b, ti: (b, 0, 0, 0))
    conv_spec = pl.BlockSpec((1, SUBLANES, cdim), lambda b, ti: (b, 0, 0))
    if s0 is not None:
        in_specs += [state_spec, conv_spec]
        args += [s0, c0]
    return pl.pallas_call(
        functools.partial(_ssd_kernel, t=t, bw=bw, has_init=s0 is not None),
        grid=(bsz, nt),
        in_specs=in_specs,
        out_specs=[pl.BlockSpec((t, bw), lambda b, ti: (b * nt + ti, 0)), state_spec, conv_spec],
        out_shape=[jax.ShapeDtypeStruct((bsz * seq_len, bw), BF16),
                   jax.ShapeDtypeStruct((bsz, heads, B_HEADDIM, B_DSTATE), F32),
                   jax.ShapeDtypeStruct((bsz, SUBLANES, cdim), F32)],
        scratch_shapes=[pltpu.VMEM((t + SUBLANES, cdim), F32), pltpu.VMEM((B_DSTATE, bw), F32)],
        compiler_params=_cparams(("arbitrary", "arbitrary")),
        name="ssd",
    )(*args)


def _cmlp_kernel(u_ref, v_ref, lng_ref, lnb_ref, ws_ref, bst_ref, *out_refs, t, keep_v):
    yc_ref = out_refs[0]
    u = _gelu(u_ref[...])
    gv = _gelu(v_ref[...])
    mu = jnp.mean(gv, axis=-1, keepdims=True)
    dv = gv - mu
    var = jnp.mean(dv * dv, axis=-1, keepdims=True)
    v = dv * lax.rsqrt(var + NORM_EPS) * lng_ref[...] + lnb_ref[...]
    if keep_v:
        out_refs[1][...] = v
    vb = v.astype(BF16)
    cw = v.shape[1] // C_GROUPS
    rr = lax.broadcasted_iota(jnp.int32, (t, t), 0)
    cc = lax.broadcasted_iota(jnp.int32, (t, t), 1)
    mixed = []
    for g in range(C_GROUPS):
        w = jnp.where(cc <= rr, ws_ref[g, 0:t, 0:t], 0.0).astype(BF16)
        mixed.append(jnp.dot(w, vb[:, g * cw:(g + 1) * cw], preferred_element_type=F32) + bst_ref[0:t, g:g + 1])
    yc_ref[...] = (u * jnp.concatenate(mixed, axis=1)).astype(yc_ref.dtype)


def _cmlp(p, col_u, col_v, prm, bsz, seq_len, keep_v):
    cw = prm["cw"]
    t = min(seq_len, CMLP_CHUNK)
    nt = seq_len // t
    n = bsz * seq_len

    def full(shape):
        return pl.BlockSpec(shape, lambda i: (0,) * len(shape))

    out_specs = [pl.BlockSpec((t, cw), lambda i: (i, 0))]
    out_shape = [jax.ShapeDtypeStruct((n, cw), BF16)]
    if keep_v:
        out_specs.append(pl.BlockSpec((t, cw), lambda i: (i, 0)))
        out_shape.append(jax.ShapeDtypeStruct((n, cw), F32))
    return pl.pallas_call(
        functools.partial(_cmlp_kernel, t=t, keep_v=keep_v),
        grid=(bsz * nt,),
        in_specs=[pl.BlockSpec((t, cw), lambda i: (i, col_u // cw)),
                  pl.BlockSpec((t, cw), lambda i: (i, col_v // cw)),
                  full((1, cw)), full((1, cw)),
                  full((C_GROUPS, CMLP_CHUNK, CMLP_CHUNK)), full((CMLP_CHUNK, C_GROUPS))],
        out_specs=out_specs,
        out_shape=out_shape,
        compiler_params=_cparams(("arbitrary",)),
        name="cmlp",
    )(p, p, prm["ln_g"], prm["ln_b"], prm["ws"], prm["bs_t"])


def _run_trunk(x3, mod, st_hgrn, st_ssm, st_conv, keep_v, w):
    bsz, seq_len, d = x3.shape
    depth = mod.shape[0]
    aw = bw = cw = d // 2
    heads_a = aw // A_DK
    x = x3.reshape(bsz * seq_len, d)
    col = w["cols"]
    hgrn_out, ssm_out, conv_out, v_out = [], [], [], []
    for l in range(depth):
        mod6 = mod[l].reshape(bsz, 6, 1, d).transpose(1, 0, 2, 3)
        h = _norm_mod(x, w["norm1_g"][l], mod6, 1, 0, seq_len)
        p = _matmul(h, w["w_in"][l], w["in_tn"])

        y_a, s_h = _hgrn(p, w["lbc"][l], w["hgrn_onorm_g"][l],
                         None if st_hgrn is None else st_hgrn[l], bsz, seq_len, heads_a)
        ssd_prm = dict(bw=bw, conv_w=w["ssm_conv_w"][l], conv_b=w["ssm_conv_b"][l][None],
                       dt_bias=w["dt_bias_pad"][l][None], a_log=w["a_log_pad"][l][None],
                       d_skip=w["d_skip"][l][None], gn=w["ssm_onorm_g"][l][None])
        if st_ssm is None:
            s0 = c0 = None
        else:
            s0 = st_ssm[l]
            c0 = jnp.pad(st_conv[l], ((0, 0), (SUBLANES - (CONV_W - 1), 0), (0, 0)))
        y_b, s_s, conv_tail = _ssd(p, col, ssd_prm, s0, c0, bsz, seq_len)
        cm_prm = dict(cw=cw, ln_g=w["cmlp_ln_g"][l][None], ln_b=w["cmlp_ln_b"][l][None],
                      ws=w["cmlp_ws"][l], bs_t=w["cmlp_bs"][l].T)
        c_res = _cmlp(p, col["u"], col["v"], cm_prm, bsz, seq_len, keep_v)
        merged = _merge(y_a, y_b, c_res[0], w["w_branch"][l], p, col["gate"], d)
        x = _matmul_residual(merged, w["w_out"][l], x, mod6, 2, seq_len)
        h2 = _norm_mod(x, w["norm2_g"][l], mod6, 4, 3, seq_len)
        x = _mlp(h2, w["w_up"][l], w["w_down"][l], x, mod6, 5, seq_len)

        hgrn_out.append(s_h)
        ssm_out.append(s_s)
        conv_out.append(conv_tail[:, SUBLANES - (CONV_W - 1):, :])
        if keep_v:
            v_out.append(c_res[1].reshape(bsz, seq_len, cw))
    y = _final_norm(x, w["final_g"]).reshape(bsz, seq_len, d)
    return (y, jnp.stack(hgrn_out), jnp.stack(ssm_out), jnp.stack(conv_out),
            jnp.stack(v_out) if keep_v else None)


def kernel(x_prompt, x_sample, state_hgrn, state_ssm, state_conv, c_prompt, c_sample, norm1_g, norm2_g,
           w_mod, b_mod, w_in, hgrn_lb, hgrn_onorm_g, ssm_conv_w, ssm_conv_b, ssm_dt_bias, ssm_a_log, ssm_d,
           ssm_onorm_g, cmlp_ln_g, cmlp_ln_b, cmlp_ws, cmlp_bs, w_branch, w_out, w_up, w_down, final_g):
    d = x_prompt.shape[-1]
    depth = w_in.shape[0]
    aw = bw = cw = d // 2
    n_bc = B_GROUPS * B_DSTATE
    heads_b = bw // B_HEADDIM
    assert heads_b <= LANES and bw % LANES == 0 and DT_PAD % LANES == 0

    o_xbc = 4 * aw + bw
    o_dt = o_xbc + bw + 2 * n_bc
    o_u = o_dt + heads_b
    o_gate = o_u + 2 * cw
    o_end = o_gate + N_BRANCH * d
    w_in_r = jnp.concatenate(
        [w_in[:, :, :o_xbc], w_in[:, :, o_u:o_gate], w_in[:, :, o_gate:o_end],
         w_in[:, :, o_xbc:o_xbc + bw], w_in[:, :, o_xbc + bw:o_dt], w_in[:, :, o_dt:o_u],
         jnp.zeros((depth, d, DT_PAD - heads_b), w_in.dtype)], axis=2).astype(BF16)
    cols = dict(z=4 * aw, u=4 * aw + bw, v=4 * aw + bw + cw, gate=4 * aw + bw + 2 * cw)
    cols["xs"] = cols["gate"] + N_BRANCH * d
    cols["bc"] = cols["xs"] + bw
    cols["dt"] = cols["bc"] + 2 * n_bc
    in_total = cols["dt"] + DT_PAD
    in_tn = 1536
    assert in_total % in_tn == 0

    pad_h = lambda a: jnp.pad(a.astype(F32), ((0, 0), (0, LANES - heads_b)))
    w = dict(
        cols=cols, in_tn=in_tn, w_in=w_in_r,
        norm1_g=norm1_g, norm2_g=norm2_g, final_g=final_g,
        lbc=_lb_consts(hgrn_lb), hgrn_onorm_g=hgrn_onorm_g,
        ssm_conv_w=ssm_conv_w, ssm_conv_b=ssm_conv_b,
        dt_bias_pad=pad_h(ssm_dt_bias), a_log_pad=pad_h(ssm_a_log),
        d_skip=jnp.repeat(ssm_d.astype(F32), B_HEADDIM, axis=1), ssm_onorm_g=ssm_onorm_g,
        cmlp_ln_g=cmlp_ln_g, cmlp_ln_b=cmlp_ln_b, cmlp_ws=cmlp_ws, cmlp_bs=cmlp_bs,
        w_branch=w_branch.astype(BF16), w_out=w_out.astype(BF16),
        w_up=w_up.astype(BF16), w_down=w_down.astype(BF16),
    )

    nb = x_prompt.shape[0]
    mod = _modulation(jnp.concatenate([c_prompt, c_sample], axis=0), w_mod, b_mod)
    y_p, hgrn_p, ssm_p, conv_p, _ = _run_trunk(x_prompt, mod[:, :nb], None, None, None, False, w)
    y_s, hgrn_s, ssm_s, conv_s, v_s = _run_trunk(x_sample, mod[:, nb:], state_hgrn, state_ssm, state_conv, True, w)
    return (y_p, y_s, hgrn_p, ssm_p, conv_p, hgrn_s, ssm_s, conv_s, v_s)
```

```python
import functools
import math

import numpy as np
import jax
import jax.numpy as jnp
from jax import lax
from jax.experimental import pallas as pl
from jax.experimental.pallas import tpu as pltpu

F32 = jnp.float32
BF16 = jnp.bfloat16

A_DK = 128
B_HEADDIM = 64
B_GROUPS = 2
B_DSTATE = 128
CONV_W = 4
C_GROUPS = 4
CMLP_CHUNK = 128
N_BRANCH = 3
SCAN_CHUNK = 64
NORM_EPS = 1e-6
LB_FLOOR = 1e-30
LOG2E = 1.4426950408889634

LANES = 128
SUBLANES = 8
VMEM_LIMIT = 56 * 1024 * 1024
DIAG = SUBLANES
DT_PAD = 512
ROW_TILE = 1024
NORM_ROWS = 64


def _cparams(sem):
    return pltpu.CompilerParams(dimension_semantics=sem, vmem_limit_bytes=VMEM_LIMIT)


def _split3(x):
    hi = x.astype(BF16)
    r1 = x - hi.astype(F32)
    mid = r1.astype(BF16)
    lo = (r1 - mid.astype(F32)).astype(BF16)
    return hi, mid, lo


def _dot3(sel, x):
    hi, mid, lo = _split3(x)
    d = lambda p: jnp.dot(sel, p, preferred_element_type=F32)
    return d(hi) + d(mid) + d(lo)


def _dot3_rhs(x, sel):
    hi, mid, lo = _split3(x)
    d = lambda p: jnp.dot(p, sel, preferred_element_type=F32)
    return d(hi) + d(mid) + d(lo)


def _silu(x):
    return x / (1.0 + jnp.exp(-x))


def _softplus(x):
    return jnp.maximum(x, 0.0) + jnp.log1p(jnp.exp(-jnp.abs(x)))


def _gelu(x):
    return 0.5 * x * (1.0 + lax.erf(x * (1.0 / math.sqrt(2.0))))


def _lb_kernel(lb_ref, out_ref, *, depth):
    x = lb_ref[...]
    m = jnp.max(x, axis=0, keepdims=True)
    e = jnp.exp(x - m)
    p = e / jnp.sum(e, axis=0, keepdims=True)
    acc = jnp.zeros_like(p[0:1])
    zeros5 = jnp.zeros((SUBLANES - 3, x.shape[1]), F32)
    for l in range(depth):
        acc = acc + p[l:l + 1]
        lb = acc - p[0:1]
        out_ref[l] = jnp.concatenate(
            [jnp.log(jnp.maximum(lb, LB_FLOOR)), jnp.log1p(-lb), 1.0 - lb, zeros5], axis=0)


def _lb_consts(hgrn_lb):
    depth, aw = hgrn_lb.shape
    return pl.pallas_call(
        functools.partial(_lb_kernel, depth=depth),
        out_shape=jax.ShapeDtypeStruct((depth, SUBLANES, aw), F32),
        name="hgrn_lb",
    )(hgrn_lb.astype(F32))


def _mod_kernel(c_ref, w_ref, b_ref, o_ref):
    cs = _silu(c_ref[...]).astype(BF16)
    o_ref[...] = jnp.dot(cs, w_ref[...].astype(BF16), preferred_element_type=F32) + b_ref[...]


def _modulation(c_all, w_mod, b_mod):
    depth, d, n6 = w_mod.shape
    s = c_all.shape[0]
    tn = 1024
    return pl.pallas_call(
        _mod_kernel,
        grid=(depth, n6 // tn),
        in_specs=[
            pl.BlockSpec((s, d), lambda l, j: (0, 0)),
            pl.BlockSpec((None, d, tn), lambda l, j: (l, 0, j)),
            pl.BlockSpec((None, 1, tn), lambda l, j: (l, 0, j)),
        ],
        out_specs=pl.BlockSpec((None, s, tn), lambda l, j: (l, 0, j)),
        out_shape=jax.ShapeDtypeStruct((depth, s, n6), F32),
        compiler_params=_cparams(("arbitrary", "arbitrary")),
        name="adaln_mod",
    )(c_all, w_mod, b_mod.reshape(depth, 1, n6))


def _row_tiling(n_tok, seq_len, tm_max):
    tm = min(tm_max, n_tok)
    if seq_len >= tm:
        assert seq_len % tm == 0
        return tm, 1, seq_len // tm
    assert tm % seq_len == 0
    return tm, tm // seq_len, 1


def _per_seq(x, nseg):
    tm, d = x.shape
    return x.reshape(nseg, tm // nseg, d)


def _final_norm_kernel(x_ref, g_ref, o_ref):
    x = x_ref[...]
    o_ref[...] = x * lax.rsqrt(jnp.mean(x * x, axis=-1, keepdims=True) + NORM_EPS) * g_ref[...]


def _final_norm(x, g):
    n, d = x.shape
    tm = min(512, n)
    return pl.pallas_call(
        _final_norm_kernel,
        grid=(n // tm,),
        in_specs=[pl.BlockSpec((tm, d), lambda i: (i, 0)), pl.BlockSpec((1, d), lambda i: (0, 0))],
        out_specs=pl.BlockSpec((tm, d), lambda i: (i, 0)),
        out_shape=jax.ShapeDtypeStruct((n, d), F32),
        compiler_params=_cparams(("arbitrary",)),
        name="final_norm",
    )(x, g.reshape(1, d))


def _norm_proj_kernel(x_ref, g_ref, sc_ref, sh_ref, w_ref, o_ref, h_scr, *, nseg, sq_relu):
    tm = x_ref.shape[0]
    seg = tm // nseg
    rb = min(seg, NORM_ROWS)

    @pl.when(pl.program_id(1) == 0)
    def _():
        g = g_ref[...]

        def body(r, carry):
            rows = pl.ds(pl.multiple_of(r * rb, rb), rb)
            s = r // (seg // rb)
            x = x_ref[rows, :]
            y = x * lax.rsqrt(jnp.mean(x * x, axis=-1, keepdims=True) + NORM_EPS) * g
            h_scr[rows, :] = (y * (1.0 + sc_ref[s]) + sh_ref[s]).astype(BF16)
            return carry

        lax.fori_loop(0, tm // rb, body, 0)

    y = jnp.dot(h_scr[...], w_ref[...], preferred_element_type=F32)
    if sq_relu:
        y = jnp.maximum(y, 0.0)
        y = y * y
    o_ref[...] = y.astype(o_ref.dtype)


def _norm_proj(x, g, mod6, k_scale, k_shift, w_stack, layer, seq_len, tn, sq_relu, out_dtype, name):
    n, d = x.shape
    m = w_stack.shape[2]
    tm, nseg, tps = _row_tiling(n, seq_len, ROW_TILE)

    def mod_spec(k):
        return pl.BlockSpec((None, nseg, 1, d), lambda i, j: (k, i // tps, 0, 0))

    return pl.pallas_call(
        functools.partial(_norm_proj_kernel, nseg=nseg, sq_relu=sq_relu),
        grid=(n // tm, m // tn),
        in_specs=[
            pl.BlockSpec((tm, d), lambda i, j: (i, 0)),
            pl.BlockSpec((1, d), lambda i, j: (0, 0)),
            mod_spec(k_scale),
            mod_spec(k_shift),
            pl.BlockSpec((None, d, tn), lambda i, j: (layer, 0, j)),
        ],
        out_specs=pl.BlockSpec((tm, tn), lambda i, j: (i, j)),
        out_shape=jax.ShapeDtypeStruct((n, m), out_dtype),
        scratch_shapes=[pltpu.VMEM((tm, d), BF16)],
        compiler_params=_cparams(("arbitrary", "arbitrary")),
        name=name,
    )(x, g.reshape(1, d), mod6, mod6, w_stack)


def _proj_res_kernel(a_ref, w_ref, x_ref, gate_ref, o_ref, *, nseg, nk):
    part = jnp.dot(a_ref[...], w_ref[...], preferred_element_type=F32)

    def finish(acc):
        x = x_ref[...]
        o_ref[...] = (_per_seq(x, nseg) + gate_ref[...] * _per_seq(acc, nseg)).reshape(x.shape)

    if nk == 1:
        finish(part)
        return
    k = pl.program_id(2)

    @pl.when(k == 0)
    def _():
        o_ref[...] = part

    @pl.when((k > 0) & (k < nk - 1))
    def _():
        o_ref[...] += part

    @pl.when(k == nk - 1)
    def _():
        finish(o_ref[...] + part)


def _proj_residual(a, w_stack, layer, x, mod6, k_gate, seq_len, name):
    n, kdim = a.shape
    d = w_stack.shape[2]
    tn = 1024
    tk = min(kdim, 2048)
    nk = kdim // tk
    tm, nseg, tps = _row_tiling(n, seq_len, ROW_TILE)
    return pl.pallas_call(
        functools.partial(_proj_res_kernel, nseg=nseg, nk=nk),
        grid=(n // tm, d // tn, nk),
        in_specs=[
            pl.BlockSpec((tm, tk), lambda i, j, k: (i, k)),
            pl.BlockSpec((None, tk, tn), lambda i, j, k: (layer, k, j)),
            pl.BlockSpec((tm, tn), lambda i, j, k: (i, j)),
            pl.BlockSpec((None, nseg, 1, tn), lambda i, j, k: (k_gate, i // tps, 0, j)),
        ],
        out_specs=pl.BlockSpec((tm, tn), lambda i, j, k: (i, j)),
        out_shape=jax.ShapeDtypeStruct((n, d), F32),
        compiler_params=_cparams(("arbitrary", "arbitrary", "arbitrary")),
        name=name,
    )(a, w_stack, x, mod6)


def _merge_kernel(ya_ref, yb_ref, yc_ref, wa_ref, wb_ref, wc_ref, ga_ref, gb_ref, gc_ref, o_ref):
    acc = None
    for y_ref, w_ref, g_ref in ((ya_ref, wa_ref, ga_ref), (yb_ref, wb_ref, gb_ref), (yc_ref, wc_ref, gc_ref)):
        gate = 1.0 / (1.0 + jnp.exp(-g_ref[...]))
        term = gate * jnp.dot(y_ref[...], w_ref[...], preferred_element_type=F32)
        acc = term if acc is None else acc + term
    o_ref[...] = acc.astype(o_ref.dtype)


def _merge(ya, yb, yc, w_stack, layer, p, gate_col0, d):
    n, bw = ya.shape
    tn = 512
    tm = min(ROW_TILE, n)
    g0 = gate_col0 // tn
    gsteps = d // tn
    y_spec = pl.BlockSpec((tm, bw), lambda i, j: (i, 0))

    def w_spec(k):
        return pl.BlockSpec((None, None, bw, tn), lambda i, j: (layer, k, 0, j))

    def g_spec(k):
        return pl.BlockSpec((tm, tn), lambda i, j: (i, g0 + k * gsteps + j))

    return pl.pallas_call(
        _merge_kernel,
        grid=(n // tm, d // tn),
        in_specs=[y_spec, y_spec, y_spec, w_spec(0), w_spec(1), w_spec(2), g_spec(0), g_spec(1), g_spec(2)],
        out_specs=pl.BlockSpec((tm, tn), lambda i, j: (i, j)),
        out_shape=jax.ShapeDtypeStruct((n, d), BF16),
        compiler_params=_cparams(("arbitrary", "arbitrary")),
        name="merge",
    )(ya, yb, yc, w_stack, w_stack, w_stack, p, p, p)


def _hgrn_levels(c):
    lv, m = [], c // 2
    while m >= DIAG:
        lv.append(m)
        m //= 2
    return tuple(lv)


def _hgrn_select(c):
    t = np.arange(c)[:, None]
    j = np.arange(c)[None, :]
    mats = [j <= t]
    for m in _hgrn_levels(c):
        mats.append(j <= (t // (2 * m)) * (2 * m) + m - 1)
    mats.append(np.ones((c, c), bool))
    return jnp.asarray(np.concatenate(mats, axis=0).astype(np.float32), dtype=BF16)


def _hgrn_kernel(*refs, c, n_chunks, heads, has_init):
    if has_init:
        q_ref, f_ref, i_ref, g_ref, lbc_ref, gn_ref, sel_ref, s0_ref, ya_ref, s_ref, bpad, kpad = refs
    else:
        q_ref, f_ref, i_ref, g_ref, lbc_ref, gn_ref, sel_ref, ya_ref, s_ref, bpad, kpad = refs
    levels = _hgrn_levels(c)
    nl = len(levels)
    width = heads * A_DK
    head_lanes = [slice(hh * A_DK, (hh + 1) * A_DK) for hh in range(heads)]

    @pl.when(pl.program_id(1) == 0)
    def _():
        s_ref[...] = s0_ref[...] if has_init else jnp.zeros(s_ref.shape, F32)

    bpad[0:DIAG, :] = jnp.zeros((DIAG, width), F32)
    kpad[0:DIAG, :] = jnp.zeros((DIAG, width), F32)

    sel = sel_ref[...]
    log_lb = lbc_ref[0:1, :]
    log1m_lb = lbc_ref[1:2, :]
    one_m_lb = lbc_ref[2:3, :]
    row = lax.broadcasted_iota(jnp.int32, (c, width), 0)
    rr = lax.broadcasted_iota(jnp.int32, (c, c), 0)
    cc = lax.broadcasted_iota(jnp.int32, (c, c), 1)
    upper = [(row & m) != 0 for m in levels]
    same = [(rr ^ cc) < 2 * m for m in levels]
    dmat = jnp.where(((rr ^ cc) < DIAG) & (cc <= rr), rr - cc, -1)
    nt_dims = (((1,), (1,)), ((), ()))
    tn_dims = (((0,), (0,)), ((), ()))

    def chunk(ci, carry):
        rows = pl.ds(pl.multiple_of(ci * c, c), c)
        z = f_ref[rows, :]
        aq = q_ref[rows, :]

        log_sig = jnp.minimum(z, 0.0) - jnp.log(1.0 + jnp.exp(-jnp.abs(z)))
        bb = log1m_lb + log_sig
        log_f = jnp.maximum(log_lb, bb) + jnp.log(1.0 + jnp.exp(-jnp.abs(log_lb - bb)))
        k = one_m_lb / (1.0 + jnp.exp(z))
        q = _silu(aq)

        cs = _dot3(sel, log_f * LOG2E)
        b = cs[0:c]
        b_tot = cs[(nl + 1) * c:(nl + 2) * c]
        bpad[DIAG:DIAG + c, :] = b
        kpad[DIAG:DIAG + c, :] = k
        vb = i_ref[rows, :].astype(BF16)
        q_in = (q * jnp.exp2(b)).astype(BF16)
        k_out = (k * jnp.exp2(b_tot - b)).astype(BF16)

        att = [jnp.zeros((c, c), F32) for _ in range(heads)]
        for li in range(nl):
            e = jnp.exp2(-jnp.abs(b - cs[(li + 1) * c:(li + 2) * c]))
            qs = jnp.where(upper[li], q * e, 0.0).astype(BF16)
            ks = jnp.where(upper[li], 0.0, k * e).astype(BF16)
            for hh, lanes in enumerate(head_lanes):
                a_l = lax.dot_general(qs[:, lanes], ks[:, lanes], nt_dims, preferred_element_type=F32)
                att[hh] = att[hh] + jnp.where(same[li], a_l, 0.0)

        for dlt in range(DIAG):
            if dlt == 0:
                w = q * k
            else:
                ksh = kpad[DIAG - dlt:DIAG - dlt + c, :]
                bsh = bpad[DIAG - dlt:DIAG - dlt + c, :]
                w = q * ksh * jnp.exp2(b - bsh)
            for hh, lanes in enumerate(head_lanes):
                col = jnp.sum(w[:, lanes], axis=-1, keepdims=True)
                att[hh] = jnp.where(dmat == dlt, col, att[hh])

        decay_row = jnp.exp2(b_tot[0:1, :])
        outs = []
        for hh, lanes in enumerate(head_lanes):
            s_prev = s_ref[0, hh]
            o = jnp.dot(q_in[:, lanes], s_prev.astype(BF16), preferred_element_type=F32)
            o = o + jnp.dot(att[hh].astype(BF16), vb[:, lanes], preferred_element_type=F32)
            decay_col = jnp.broadcast_to(decay_row[:, lanes], (A_DK, A_DK)).T
            s_ref[0, hh] = decay_col * s_prev + lax.dot_general(
                k_out[:, lanes], vb[:, lanes], tn_dims, preferred_element_type=F32)
            outs.append(o * lax.rsqrt(jnp.mean(o * o, axis=-1, keepdims=True) + NORM_EPS))
        y = jnp.concatenate(outs, axis=1) * gn_ref[...] * _silu(g_ref[rows, :])
        ya_ref[rows, :] = y.astype(ya_ref.dtype)
        return carry

    lax.fori_loop(0, n_chunks, chunk, 0)


def _hgrn(p, lbc_stack, layer, gn, s0, bsz, seq_len, heads):
    c = min(seq_len, SCAN_CHUNK)
    t_blk = min(seq_len, 256)
    nt = seq_len // t_blk
    aw = heads * A_DK

    def col_spec(seg):
        return pl.BlockSpec((t_blk, aw), lambda b, t: (b * nt + t, seg))

    sel = _hgrn_select(c)
    in_specs = [col_spec(0), col_spec(1), col_spec(2), col_spec(3),
                pl.BlockSpec((None, SUBLANES, aw), lambda b, t: (layer, 0, 0)),
                pl.BlockSpec((1, aw), lambda b, t: (0, 0)),
                pl.BlockSpec(sel.shape, lambda b, t: (0, 0))]
    args = [p, p, p, p, lbc_stack, gn.reshape(1, aw), sel]
    state_spec = pl.BlockSpec((1, heads, A_DK, A_DK), lambda b, t: (b, 0, 0, 0))
    if s0 is not None:
        in_specs.append(state_spec)
        args.append(s0)
    return pl.pallas_call(
        functools.partial(_hgrn_kernel, c=c, n_chunks=t_blk // c, heads=heads, has_init=s0 is not None),
        grid=(bsz, nt),
        in_specs=in_specs,
        out_specs=[pl.BlockSpec((t_blk, aw), lambda b, t: (b * nt + t, 0)), state_spec],
        out_shape=[jax.ShapeDtypeStruct((bsz * seq_len, aw), BF16),
                   jax.ShapeDtypeStruct((bsz, heads, A_DK, A_DK), F32)],
        scratch_shapes=[pltpu.VMEM((c + DIAG, aw), F32), pltpu.VMEM((c + DIAG, aw), F32)],
        compiler_params=_cparams(("arbitrary", "arbitrary")),
        name="hgrn2",
    )(*args)


def _ssd_kernel(*refs, t, bw, has_init):
    if has_init:
        (z_ref, xs_ref, bc_ref, dt_ref, cw_ref, cb_ref, dtb_ref, alog_ref, dsk_ref, gn_ref, exp_ref, sel_ref,
         s0_ref, c0_ref, yb_ref, s_out_ref, conv_out_ref, xpad, st) = refs
    else:
        (z_ref, xs_ref, bc_ref, dt_ref, cw_ref, cb_ref, dtb_ref, alog_ref, dsk_ref, gn_ref, exp_ref, sel_ref,
         yb_ref, s_out_ref, conv_out_ref, xpad, st) = refs
    ti = pl.program_id(1)
    gw = bw // B_GROUPS
    n_bc = B_GROUPS * B_DSTATE
    pad = SUBLANES

    @pl.when(ti == 0)
    def _():
        if has_init:
            xpad[0:pad, :] = c0_ref[0]
            st[...] = s0_ref[0].reshape(bw, B_DSTATE).T
        else:
            xpad[0:pad, :] = jnp.zeros((pad, xpad.shape[1]), F32)
            st[...] = jnp.zeros(st.shape, F32)

    xpad[pad:pad + t, 0:bw] = xs_ref[...]
    xpad[pad:pad + t, bw:bw + 2 * n_bc] = bc_ref[...]
    conv = cb_ref[...]
    for j in range(CONV_W):
        lo = pad - (CONV_W - 1) + j
        conv = conv + xpad[lo:lo + t, :] * cw_ref[j:j + 1, :]
    xbc = _silu(conv)
    x = xbc[:, 0:bw]

    dt = _softplus(dt_ref[...] + dtb_ref[...])
    a = dt * (-jnp.exp(alog_ref[...]))
    cs = _dot3(sel_ref[...], a)
    a_cum = cs[0:t]
    a_tot = cs[t:2 * t]
    ex = _dot3_rhs(jnp.concatenate([dt, a_cum, a_tot - a_cum], axis=0), exp_ref[...])
    dt_e = ex[0:t]
    acum_e = ex[t:2 * t]
    dec_e = ex[2 * t:3 * t]
    atot_e = acum_e[t - 1:t, :]

    xdt = x * dt_e
    xw = (xdt * jnp.exp(dec_e)).astype(BF16)
    xdt_b = xdt.astype(BF16)
    a_cum_t = a_cum.T
    rr = lax.broadcasted_iota(jnp.int32, (t, t), 0)
    cc = lax.broadcasted_iota(jnp.int32, (t, t), 1)
    causal = cc <= rr
    lane = lax.broadcasted_iota(jnp.int32, (t, LANES), 1)
    heads_per_group = gw // B_HEADDIM
    pairs_per_group = gw // LANES

    y_groups = []
    for g in range(B_GROUPS):
        bg = xbc[:, bw + g * B_DSTATE:bw + (g + 1) * B_DSTATE].astype(BF16)
        cg = xbc[:, bw + n_bc + g * B_DSTATE:bw + n_bc + (g + 1) * B_DSTATE].astype(BF16)
        gs = slice(g * gw, (g + 1) * gw)
        st_g = st[:, gs]
        scores = lax.dot_general(cg, bg, (((1,), (1,)), ((), ())), preferred_element_type=F32)
        y_off = jnp.dot(cg, st_g.astype(BF16), preferred_element_type=F32) * jnp.exp(acum_e[:, gs])
        st[:, gs] = jnp.exp(atot_e[:, gs]) * st_g + lax.dot_general(
            bg, xw[:, gs], (((0,), (0,)), ((), ())), preferred_element_type=F32)
        y_pairs = []
        for pr in range(pairs_per_group):
            cols = slice(g * gw + pr * LANES, g * gw + (pr + 1) * LANES)
            xp = xdt_b[:, cols]
            acc = None
            for half in range(LANES // B_HEADDIM):
                h = g * heads_per_group + pr * (LANES // B_HEADDIM) + half
                diff = a_cum[:, h:h + 1] - a_cum_t[h:h + 1, :]
                lmat = jnp.where(causal, jnp.exp(jnp.where(causal, diff, 0.0)), 0.0)
                m = (scores * lmat).astype(BF16)
                in_half = (lane // B_HEADDIM) == half
                term = jnp.dot(m, jnp.where(in_half, xp, jnp.zeros_like(xp)), preferred_element_type=F32)
                acc = term if acc is None else acc + term
            y_pairs.append(acc)
        y_groups.append(jnp.concatenate(y_pairs, axis=1) + y_off)
    y = jnp.concatenate(y_groups, axis=1) + dsk_ref[...] * x
    y = y * _silu(z_ref[...])
    outs = []
    for g in range(B_GROUPS):
        yg = y[:, g * gw:(g + 1) * gw]
        outs.append(yg * lax.rsqrt(jnp.mean(yg * yg, axis=-1, keepdims=True) + NORM_EPS))
    yb_ref[...] = (jnp.concatenate(outs, axis=1) * gn_ref[...]).astype(yb_ref.dtype)

    tail = xpad[t:t + pad, :]
    xpad[0:pad, :] = tail

    @pl.when(ti == pl.num_programs(1) - 1)
    def _():
        conv_out_ref[0] = tail
        s_out_ref[0] = st[...].T.reshape(s_out_ref.shape[1:])


def _ssd(p, cols, prm, s0, c0, bsz, seq_len):
    bw = prm["bw"]
    heads = bw // B_HEADDIM
    n_bc = B_GROUPS * B_DSTATE
    cdim = bw + 2 * n_bc
    t = min(seq_len, 128)
    nt = seq_len // t
    tri = np.arange(t)[None, :] <= np.arange(t)[:, None]
    sel = jnp.asarray(np.concatenate([tri, np.ones((t, t), bool)], 0).astype(np.float32), dtype=BF16)
    expand = np.zeros((LANES, bw), np.float32)
    expand[np.arange(bw) // B_HEADDIM, np.arange(bw)] = 1.0
    expand = jnp.asarray(expand, dtype=BF16)

    def blk(width, off):
        return pl.BlockSpec((t, width), lambda b, ti: (b * nt + ti, off // width))

    def full(shape):
        return pl.BlockSpec(shape, lambda b, ti: (0,) * len(shape))

    in_specs = [blk(bw, cols["z"]), blk(bw, cols["xs"]), blk(2 * n_bc, cols["bc"]), blk(LANES, cols["dt"]),
                full((CONV_W, cdim)), full((1, cdim)), full((1, LANES)), full((1, LANES)),
                full((1, bw)), full((1, bw)), full(expand.shape), full(sel.shape)]
    args = [p, p, p, p, prm["conv_w"], prm["conv_b"], prm["dt_bias"], prm["a_log"], prm["d_skip"], prm["gn"],
            expand, sel]
    state_spec = pl.BlockSpec((1, heads, B_HEADDIM, B_DSTATE), lambda b, ti: (b, 0, 0, 0))
    conv_spec = pl.BlockSpec((1, SUBLANES, cdim), lambda b, ti: (b, 0, 0))
    if s0 is not None:
        in_specs += [state_spec, conv_spec]
        args += [s0, c0]
    return pl.pallas_call(
        functools.partial(_ssd_kernel, t=t, bw=bw, has_init=s0 is not None),
        grid=(bsz, nt),
        in_specs=in_specs,
        out_specs=[pl.BlockSpec((t, bw), lambda b, ti: (b * nt + ti, 0)), state_spec, conv_spec],
        out_shape=[jax.ShapeDtypeStruct((bsz * seq_len, bw), BF16),
                   jax.ShapeDtypeStruct((bsz, heads, B_HEADDIM, B_DSTATE), F32),
                   jax.ShapeDtypeStruct((bsz, SUBLANES, cdim), F32)],
        scratch_shapes=[pltpu.VMEM((t + SUBLANES, cdim), F32), pltpu.VMEM((B_DSTATE, bw), F32)],
        compiler_params=_cparams(("arbitrary", "arbitrary")),
        name="ssd",
    )(*args)


def _cmlp_kernel(u_ref, v_ref, lng_ref, lnb_ref, ws_ref, bst_ref, *out_refs, t, n_chunks, keep_v):
    yc_ref = out_refs[0]
    cw = u_ref.shape[1] // C_GROUPS
    rr = lax.broadcasted_iota(jnp.int32, (t, t), 0)
    cc = lax.broadcasted_iota(jnp.int32, (t, t), 1)
    wts = [jnp.where(cc <= rr, ws_ref[g, 0:t, 0:t], 0.0).astype(BF16) for g in range(C_GROUPS)]
    for ci in range(n_chunks):
        rows = slice(ci * t, (ci + 1) * t)
        u = _gelu(u_ref[rows, :])
        gv = _gelu(v_ref[rows, :])
        mu = jnp.mean(gv, axis=-1, keepdims=True)
        dv = gv - mu
        var = jnp.mean(dv * dv, axis=-1, keepdims=True)
        v = dv * lax.rsqrt(var + NORM_EPS) * lng_ref[...] + lnb_ref[...]
        if keep_v:
            out_refs[1][rows, :] = v
        vb = v.astype(BF16)
        for g in range(C_GROUPS):
            lanes = slice(g * cw, (g + 1) * cw)
            mixed = jnp.dot(wts[g], vb[:, lanes], preferred_element_type=F32) + bst_ref[0:t, g:g + 1]
            yc_ref[rows, lanes] = (u[:, lanes] * mixed).astype(yc_ref.dtype)


def _cmlp(p, col_u, col_v, prm, bsz, seq_len, keep_v):
    cw = prm["cw"]
    t = min(seq_len, CMLP_CHUNK)
    t_blk = min(seq_len, 4 * CMLP_CHUNK)
    n = bsz * seq_len

    def full(shape):
        return pl.BlockSpec(shape, lambda i: (0,) * len(shape))

    out_specs = [pl.BlockSpec((t_blk, cw), lambda i: (i, 0))]
    out_shape = [jax.ShapeDtypeStruct((n, cw), BF16)]
    if keep_v:
        out_specs.append(pl.BlockSpec((t_blk, cw), lambda i: (i, 0)))
        out_shape.append(jax.ShapeDtypeStruct((n, cw), F32))
    return pl.pallas_call(
        functools.partial(_cmlp_kernel, t=t, n_chunks=t_blk // t, keep_v=keep_v),
        grid=(n // t_blk,),
        in_specs=[pl.BlockSpec((t_blk, cw), lambda i: (i, col_u // cw)),
                  pl.BlockSpec((t_blk, cw), lambda i: (i, col_v // cw)),
                  full((1, cw)), full((1, cw)),
                  full((C_GROUPS, CMLP_CHUNK, CMLP_CHUNK)), full((CMLP_CHUNK, C_GROUPS))],
        out_specs=out_specs,
        out_shape=out_shape,
        compiler_params=_cparams(("arbitrary",)),
        name="cmlp",
    )(p, p, prm["ln_g"], prm["ln_b"], prm["ws"], prm["bs_t"])


def _run_trunk(x3, mod, st_hgrn, st_ssm, st_conv, keep_v, w):
    bsz, seq_len, d = x3.shape
    depth = mod.shape[0]
    aw = bw = cw = d // 2
    heads_a = aw // A_DK
    x = x3.reshape(bsz * seq_len, d)
    col = w["cols"]
    hgrn_out, ssm_out, conv_out, v_out = [], [], [], []
    for l in range(depth):
        mod6 = mod[l].reshape(bsz, 6, 1, d).transpose(1, 0, 2, 3)
        p = _norm_proj(x, w["norm1_g"][l], mod6, 1, 0, w["w_in"], l, seq_len, w["in_tn"], False, F32, "in_proj")

        y_a, s_h = _hgrn(p, w["lbc"], l, w["hgrn_onorm_g"][l],
                         None if st_hgrn is None else st_hgrn[l], bsz, seq_len, heads_a)
        ssd_prm = dict(bw=bw, conv_w=w["ssm_conv_w"][l], conv_b=w["ssm_conv_b"][l][None],
                       dt_bias=w["dt_bias_pad"][l][None], a_log=w["a_log_pad"][l][None],
                       d_skip=w["d_skip"][l][None], gn=w["ssm_onorm_g"][l][None])
        if st_ssm is None:
            s0 = c0 = None
        else:
            s0 = st_ssm[l]
            c0 = jnp.pad(st_conv[l], ((0, 0), (SUBLANES - (CONV_W - 1), 0), (0, 0)))
        y_b, s_s, conv_tail = _ssd(p, col, ssd_prm, s0, c0, bsz, seq_len)
        cm_prm = dict(cw=cw, ln_g=w["cmlp_ln_g"][l][None], ln_b=w["cmlp_ln_b"][l][None],
                      ws=w["cmlp_ws"][l], bs_t=w["cmlp_bs"][l].T)
        c_res = _cmlp(p, col["u"], col["v"], cm_prm, bsz, seq_len, keep_v)
        merged = _merge(y_a, y_b, c_res[0], w["w_branch"], l, p, col["gate"], d)
        x = _proj_residual(merged, w["w_out"], l, x, mod6, 2, seq_len, "out_proj")
        hid = _norm_proj(x, w["norm2_g"][l], mod6, 4, 3, w["w_up"], l, seq_len, 1024, True, BF16, "mlp_up")
        x = _proj_residual(hid, w["w_down"], l, x, mod6, 5, seq_len, "mlp_down")

        hgrn_out.append(s_h)
        ssm_out.append(s_s)
        conv_out.append(conv_tail[:, SUBLANES - (CONV_W - 1):, :])
        if keep_v:
            v_out.append(c_res[1].reshape(bsz, seq_len, cw))
    y = _final_norm(x, w["final_g"]).reshape(bsz, seq_len, d)
    return (y, jnp.stack(hgrn_out), jnp.stack(ssm_out), jnp.stack(conv_out),
            jnp.stack(v_out) if keep_v else None)


def kernel(x_prompt, x_sample, state_hgrn, state_ssm, state_conv, c_prompt, c_sample, norm1_g, norm2_g,
           w_mod, b_mod, w_in, hgrn_lb, hgrn_onorm_g, ssm_conv_w, ssm_conv_b, ssm_dt_bias, ssm_a_log, ssm_d,
           ssm_onorm_g, cmlp_ln_g, cmlp_ln_b, cmlp_ws, cmlp_bs, w_branch, w_out, w_up, w_down, final_g):
    d = x_prompt.shape[-1]
    depth = w_in.shape[0]
    aw = bw = cw = d // 2
    n_bc = B_GROUPS * B_DSTATE
    heads_b = bw // B_HEADDIM
    assert heads_b <= LANES and bw % LANES == 0 and DT_PAD % LANES == 0

    o_xbc = 4 * aw + bw
    o_dt = o_xbc + bw + 2 * n_bc
    o_u = o_dt + heads_b
    o_gate = o_u + 2 * cw
    o_end = o_gate + N_BRANCH * d
    w_in_r = jnp.concatenate(
        [w_in[:, :, :o_xbc], w_in[:, :, o_u:o_gate], w_in[:, :, o_gate:o_end],
         w_in[:, :, o_xbc:o_xbc + bw], w_in[:, :, o_xbc + bw:o_dt], w_in[:, :, o_dt:o_u],
         jnp.zeros((depth, d, DT_PAD - heads_b), w_in.dtype)], axis=2).astype(BF16)
    cols = dict(z=4 * aw, u=4 * aw + bw, v=4 * aw + bw + cw, gate=4 * aw + bw + 2 * cw)
    cols["xs"] = cols["gate"] + N_BRANCH * d
    cols["bc"] = cols["xs"] + bw
    cols["dt"] = cols["bc"] + 2 * n_bc
    in_total = cols["dt"] + DT_PAD
    in_tn = 1536
    assert in_total % in_tn == 0

    pad_h = lambda a: jnp.pad(a.astype(F32), ((0, 0), (0, LANES - heads_b)))
    w = dict(
        cols=cols, in_tn=in_tn, w_in=w_in_r,
        norm1_g=norm1_g, norm2_g=norm2_g, final_g=final_g,
        lbc=_lb_consts(hgrn_lb), hgrn_onorm_g=hgrn_onorm_g,
        ssm_conv_w=ssm_conv_w, ssm_conv_b=ssm_conv_b,
        dt_bias_pad=pad_h(ssm_dt_bias), a_log_pad=pad_h(ssm_a_log),
        d_skip=jnp.repeat(ssm_d.astype(F32), B_HEADDIM, axis=1), ssm_onorm_g=ssm_onorm_g,
        cmlp_ln_g=cmlp_ln_g, cmlp_ln_b=cmlp_ln_b, cmlp_ws=cmlp_ws, cmlp_bs=cmlp_bs,
        w_branch=w_branch.astype(BF16).reshape(depth, N_BRANCH, aw, d), w_out=w_out.astype(BF16),
        w_up=w_up.astype(BF16), w_down=w_down.astype(BF16),
    )

    nb = x_prompt.shape[0]
    mod = _modulation(jnp.concatenate([c_prompt, c_sample], axis=0), w_mod, b_mod)
    y_p, hgrn_p, ssm_p, conv_p, _ = _run_trunk(x_prompt, mod[:, :nb], None, None, None, False, w)
    y_s, hgrn_s, ssm_s, conv_s, v_s = _run_trunk(x_sample, mod[:, nb:], state_hgrn, state_ssm, state_conv, True, w)
    return (y_p, y_s, hgrn_p, ssm_p, conv_p, hgrn_s, ssm_s, conv_s, v_s)
```

```python
import functools
import math

import numpy as np
import jax
import jax.numpy as jnp
from jax import lax
from jax.experimental import pallas as pl
from jax.experimental.pallas import tpu as pltpu

F32 = jnp.float32
BF16 = jnp.bfloat16

A_DK = 128
B_HEADDIM = 64
B_GROUPS = 2
B_DSTATE = 128
CONV_W = 4
C_GROUPS = 4
CMLP_CHUNK = 128
N_BRANCH = 3
SCAN_CHUNK = 64
NORM_EPS = 1e-6
LB_FLOOR = 1e-30
LOG2E = 1.4426950408889634

LANES = 128
SUBLANES = 8
VMEM_LIMIT = 56 * 1024 * 1024
DIAG = SUBLANES
DT_PAD = 512
ROW_TILE = 1024
COL_TILE = 1024
MERGE_TILE = 512
DOWN_TILE = 256
NORM_ROWS = 64


def _cparams(sem):
    return pltpu.CompilerParams(dimension_semantics=sem, vmem_limit_bytes=VMEM_LIMIT)


def _split3(x):
    hi = x.astype(BF16)
    r1 = x - hi.astype(F32)
    mid = r1.astype(BF16)
    lo = (r1 - mid.astype(F32)).astype(BF16)
    return hi, mid, lo


def _dot3(sel, x):
    hi, mid, lo = _split3(x)
    d = lambda p: jnp.dot(sel, p, preferred_element_type=F32)
    return d(hi) + d(mid) + d(lo)


def _dot3_rhs(x, sel):
    hi, mid, lo = _split3(x)
    d = lambda p: jnp.dot(p, sel, preferred_element_type=F32)
    return d(hi) + d(mid) + d(lo)


def _silu(x):
    return x / (1.0 + jnp.exp(-x))


def _softplus(x):
    return jnp.maximum(x, 0.0) + jnp.log1p(jnp.exp(-jnp.abs(x)))


def _gelu(x):
    return 0.5 * x * (1.0 + lax.erf(x * (1.0 / math.sqrt(2.0))))


def _lb_kernel(lb_ref, out_ref, *, depth):
    x = lb_ref[...]
    m = jnp.max(x, axis=0, keepdims=True)
    e = jnp.exp(x - m)
    p = e / jnp.sum(e, axis=0, keepdims=True)
    acc = jnp.zeros_like(p[0:1])
    zeros5 = jnp.zeros((SUBLANES - 3, x.shape[1]), F32)
    for l in range(depth):
        acc = acc + p[l:l + 1]
        lb = acc - p[0:1]
        out_ref[l] = jnp.concatenate(
            [jnp.log(jnp.maximum(lb, LB_FLOOR)), jnp.log1p(-lb), 1.0 - lb, zeros5], axis=0)


def _lb_consts(hgrn_lb):
    depth, aw = hgrn_lb.shape
    return pl.pallas_call(
        functools.partial(_lb_kernel, depth=depth),
        out_shape=jax.ShapeDtypeStruct((depth, SUBLANES, aw), F32),
        name="hgrn_lb",
    )(hgrn_lb.astype(F32))


def _mod_kernel(c_ref, w_ref, b_ref, o_ref):
    cs = _silu(c_ref[...]).astype(BF16)
    o_ref[...] = jnp.dot(cs, w_ref[...].astype(BF16), preferred_element_type=F32) + b_ref[...]


def _modulation(c_all, w_mod, b_mod):
    depth, d, n6 = w_mod.shape
    s = c_all.shape[0]
    tn = 1024
    return pl.pallas_call(
        _mod_kernel,
        grid=(depth, n6 // tn),
        in_specs=[
            pl.BlockSpec((s, d), lambda l, j: (0, 0)),
            pl.BlockSpec((None, d, tn), lambda l, j: (l, 0, j)),
            pl.BlockSpec((None, 1, tn), lambda l, j: (l, 0, j)),
        ],
        out_specs=pl.BlockSpec((None, s, tn), lambda l, j: (l, 0, j)),
        out_shape=jax.ShapeDtypeStruct((depth, s, n6), F32),
        compiler_params=_cparams(("arbitrary", "arbitrary")),
        name="adaln_mod",
    )(c_all, w_mod, b_mod.reshape(depth, 1, n6))


def _row_tiling(n_tok, seq_len, tm_max):
    tm = min(tm_max, n_tok)
    if seq_len >= tm:
        assert seq_len % tm == 0
        return tm, 1, seq_len // tm
    assert tm % seq_len == 0
    return tm, tm // seq_len, 1


def _per_seq(x, nseg):
    tm, d = x.shape
    return x.reshape(nseg, tm // nseg, d)


def _final_norm_kernel(x_ref, g_ref, o_ref):
    x = x_ref[...]
    o_ref[...] = x * lax.rsqrt(jnp.mean(x * x, axis=-1, keepdims=True) + NORM_EPS) * g_ref[...]


def _final_norm(x, g):
    n, d = x.shape
    tm = min(512, n)
    return pl.pallas_call(
        _final_norm_kernel,
        grid=(n // tm,),
        in_specs=[pl.BlockSpec((tm, d), lambda i: (i, 0)), pl.BlockSpec((1, d), lambda i: (0, 0))],
        out_specs=pl.BlockSpec((tm, d), lambda i: (i, 0)),
        out_shape=jax.ShapeDtypeStruct((n, d), F32),
        compiler_params=_cparams(("arbitrary",)),
        name="final_norm",
    )(x, g.reshape(1, d))


def _norm_mod_rows(x_ref, g_ref, sc_ref, sh_ref, h_scr, nseg):
    tm = x_ref.shape[0]
    seg = tm // nseg
    rb = min(seg, NORM_ROWS)
    g = g_ref[...]

    def body(r, carry):
        rows = pl.ds(pl.multiple_of(r * rb, rb), rb)
        s = r // (seg // rb)
        x = x_ref[rows, :]
        y = x * lax.rsqrt(jnp.mean(x * x, axis=-1, keepdims=True) + NORM_EPS) * g
        h_scr[rows, :] = (y * (1.0 + sc_ref[s]) + sh_ref[s]).astype(BF16)
        return carry

    lax.fori_loop(0, tm // rb, body, 0)


def _mlp_up_kernel(x_ref, g_ref, sc_ref, sh_ref, w_ref, o_ref, h_scr, *, nseg):
    @pl.when(pl.program_id(1) == 0)
    def _():
        _norm_mod_rows(x_ref, g_ref, sc_ref, sh_ref, h_scr, nseg)

    y = jnp.maximum(jnp.dot(h_scr[...], w_ref[...], preferred_element_type=F32), 0.0)
    o_ref[...] = (y * y).astype(o_ref.dtype)


def _in_proj_kernel(x_ref, g_ref, sc_ref, sh_ref, w_ref, p_ref, gate_ref, h_scr, *, nseg, n_main):
    j = pl.program_id(1)

    @pl.when(j == 0)
    def _():
        _norm_mod_rows(x_ref, g_ref, sc_ref, sh_ref, h_scr, nseg)

    y = jnp.dot(h_scr[...], w_ref[...], preferred_element_type=F32)

    @pl.when(j < n_main)
    def _():
        p_ref[...] = y

    @pl.when(j >= n_main)
    def _():
        gate_ref[...] = (1.0 / (1.0 + jnp.exp(-y))).astype(gate_ref.dtype)


def _norm_proj_call(kern, x, g, mod6, k_scale, k_shift, w_stack, layer, seq_len, tn, out_specs, out_shape, name):
    n, d = x.shape
    m = w_stack.shape[2]
    tm, nseg, tps = _row_tiling(n, seq_len, ROW_TILE)

    def mod_spec(k):
        return pl.BlockSpec((None, nseg, 1, d), lambda i, j: (k, i // tps, 0, 0))

    return pl.pallas_call(
        functools.partial(kern, nseg=nseg),
        grid=(n // tm, m // tn),
        in_specs=[
            pl.BlockSpec((tm, d), lambda i, j: (i, 0)),
            pl.BlockSpec((1, d), lambda i, j: (0, 0)),
            mod_spec(k_scale),
            mod_spec(k_shift),
            pl.BlockSpec((None, d, tn), lambda i, j: (layer, 0, j)),
        ],
        out_specs=out_specs(tm),
        out_shape=out_shape,
        scratch_shapes=[pltpu.VMEM((tm, d), BF16)],
        compiler_params=_cparams(("arbitrary", "arbitrary")),
        name=name,
    )(x, g.reshape(1, d), mod6, mod6, w_stack)


def _mlp_up(x, g, mod6, w_stack, layer, seq_len):
    n = x.shape[0]
    m = w_stack.shape[2]
    tn = COL_TILE
    return _norm_proj_call(
        _mlp_up_kernel, x, g, mod6, 4, 3, w_stack, layer, seq_len, tn,
        lambda tm: pl.BlockSpec((tm, tn), lambda i, j: (i, j)),
        jax.ShapeDtypeStruct((n, m), BF16), "mlp_up")


def _in_proj(x, g, mod6, w_stack, layer, seq_len, main_cols):
    n = x.shape[0]
    m = w_stack.shape[2]
    tn = COL_TILE
    n_main = main_cols // tn
    out_specs = lambda tm: [
        pl.BlockSpec((tm, tn), lambda i, j: (i, jnp.minimum(j, n_main - 1))),
        pl.BlockSpec((tm, tn), lambda i, j: (i, jnp.maximum(j - n_main, 0))),
    ]
    out_shape = [jax.ShapeDtypeStruct((n, main_cols), F32), jax.ShapeDtypeStruct((n, m - main_cols), BF16)]
    return _norm_proj_call(
        functools.partial(_in_proj_kernel, n_main=n_main), x, g, mod6, 1, 0, w_stack, layer, seq_len, tn,
        out_specs, out_shape, "in_proj")


def _proj_res_kernel(a_ref, w_ref, x_ref, gate_ref, o_ref, *, nseg, nk):
    part = jnp.dot(a_ref[...], w_ref[...], preferred_element_type=F32)

    def finish(acc):
        x = x_ref[...]
        o_ref[...] = (_per_seq(x, nseg) + gate_ref[...] * _per_seq(acc, nseg)).reshape(x.shape)

    if nk == 1:
        finish(part)
        return
    k = pl.program_id(2)

    @pl.when(k == 0)
    def _():
        o_ref[...] = part

    @pl.when((k > 0) & (k < nk - 1))
    def _():
        o_ref[...] += part

    @pl.when(k == nk - 1)
    def _():
        finish(o_ref[...] + part)


def _proj_residual(a, w_stack, layer, x, mod6, k_gate, seq_len, name):
    n, kdim = a.shape
    d = w_stack.shape[2]
    tn = COL_TILE if kdim <= d else DOWN_TILE
    tk = kdim
    nk = kdim // tk
    tm, nseg, tps = _row_tiling(n, seq_len, ROW_TILE)
    return pl.pallas_call(
        functools.partial(_proj_res_kernel, nseg=nseg, nk=nk),
        grid=(n // tm, d // tn, nk),
        in_specs=[
            pl.BlockSpec((tm, tk), lambda i, j, k: (i, k)),
            pl.BlockSpec((None, tk, tn), lambda i, j, k: (layer, k, j)),
            pl.BlockSpec((tm, tn), lambda i, j, k: (i, j)),
            pl.BlockSpec((None, nseg, 1, tn), lambda i, j, k: (k_gate, i // tps, 0, j)),
        ],
        out_specs=pl.BlockSpec((tm, tn), lambda i, j, k: (i, j)),
        out_shape=jax.ShapeDtypeStruct((n, d), F32),
        compiler_params=_cparams(("arbitrary", "arbitrary", "arbitrary")),
        name=name,
    )(a, w_stack, x, mod6)


def _merge_kernel(ya_ref, yb_ref, yc_ref, wa_ref, wb_ref, wc_ref, ga_ref, gb_ref, gc_ref, o_ref):
    acc = None
    for y_ref, w_ref, g_ref in ((ya_ref, wa_ref, ga_ref), (yb_ref, wb_ref, gb_ref), (yc_ref, wc_ref, gc_ref)):
        term = g_ref[...].astype(F32) * jnp.dot(y_ref[...], w_ref[...], preferred_element_type=F32)
        acc = term if acc is None else acc + term
    o_ref[...] = acc.astype(o_ref.dtype)


def _merge(ya, yb, yc, w_stack, layer, gates, d):
    n, bw = ya.shape
    tn = MERGE_TILE
    tm = min(ROW_TILE, n)
    gsteps = d // tn
    y_spec = pl.BlockSpec((tm, bw), lambda i, j: (i, 0))

    def w_spec(k):
        return pl.BlockSpec((None, None, bw, tn), lambda i, j: (layer, k, 0, j))

    def g_spec(k):
        return pl.BlockSpec((tm, tn), lambda i, j: (i, k * gsteps + j))

    return pl.pallas_call(
        _merge_kernel,
        grid=(n // tm, d // tn),
        in_specs=[y_spec, y_spec, y_spec, w_spec(0), w_spec(1), w_spec(2), g_spec(0), g_spec(1), g_spec(2)],
        out_specs=pl.BlockSpec((tm, tn), lambda i, j: (i, j)),
        out_shape=jax.ShapeDtypeStruct((n, d), BF16),
        compiler_params=_cparams(("arbitrary", "arbitrary")),
        name="merge",
    )(ya, yb, yc, w_stack, w_stack, w_stack, gates, gates, gates)


def _hgrn_levels(c):
    lv, m = [], c // 2
    while m >= DIAG:
        lv.append(m)
        m //= 2
    return tuple(lv)


def _hgrn_select(c):
    t = np.arange(c)[:, None]
    j = np.arange(c)[None, :]
    mats = [j <= t]
    for m in _hgrn_levels(c):
        mats.append(j <= (t // (2 * m)) * (2 * m) + m - 1)
    mats.append(np.ones((c, c), bool))
    return jnp.asarray(np.concatenate(mats, axis=0).astype(np.float32), dtype=BF16)


def _hgrn_kernel(*refs, c, n_chunks, heads, has_init):
    if has_init:
        q_ref, f_ref, i_ref, g_ref, lbc_ref, gn_ref, sel_ref, s0_ref, ya_ref, s_ref, bpad, kpad = refs
    else:
        q_ref, f_ref, i_ref, g_ref, lbc_ref, gn_ref, sel_ref, ya_ref, s_ref, bpad, kpad = refs
    levels = _hgrn_levels(c)
    nl = len(levels)
    width = heads * A_DK
    head_lanes = [slice(hh * A_DK, (hh + 1) * A_DK) for hh in range(heads)]

    @pl.when(pl.program_id(1) == 0)
    def _():
        s_ref[...] = s0_ref[...] if has_init else jnp.zeros(s_ref.shape, F32)

    bpad[0:DIAG, :] = jnp.zeros((DIAG, width), F32)
    kpad[0:DIAG, :] = jnp.zeros((DIAG, width), F32)

    sel = sel_ref[...]
    log_lb = lbc_ref[0:1, :]
    log1m_lb = lbc_ref[1:2, :]
    one_m_lb = lbc_ref[2:3, :]
    row = lax.broadcasted_iota(jnp.int32, (c, width), 0)
    rr = lax.broadcasted_iota(jnp.int32, (c, c), 0)
    cc = lax.broadcasted_iota(jnp.int32, (c, c), 1)
    upper = [(row & m) != 0 for m in levels]
    same = [(rr ^ cc) < 2 * m for m in levels]
    dmat = jnp.where(((rr ^ cc) < DIAG) & (cc <= rr), rr - cc, -1)
    nt_dims = (((1,), (1,)), ((), ()))
    tn_dims = (((0,), (0,)), ((), ()))

    def chunk(ci, carry):
        rows = pl.ds(pl.multiple_of(ci * c, c), c)
        z = f_ref[rows, :]
        aq = q_ref[rows, :]

        log_sig = jnp.minimum(z, 0.0) - jnp.log(1.0 + jnp.exp(-jnp.abs(z)))
        bb = log1m_lb + log_sig
        log_f = jnp.maximum(log_lb, bb) + jnp.log(1.0 + jnp.exp(-jnp.abs(log_lb - bb)))
        k = one_m_lb / (1.0 + jnp.exp(z))
        q = _silu(aq)

        cs = _dot3(sel, log_f * LOG2E)
        b = cs[0:c]
        b_tot = cs[(nl + 1) * c:(nl + 2) * c]
        bpad[DIAG:DIAG + c, :] = b
        kpad[DIAG:DIAG + c, :] = k
        vb = i_ref[rows, :].astype(BF16)
        q_in = (q * jnp.exp2(b)).astype(BF16)
        k_out = (k * jnp.exp2(b_tot - b)).astype(BF16)

        att = [jnp.zeros((c, c), F32) for _ in range(heads)]
        for li in range(nl):
            e = jnp.exp2(-jnp.abs(b - cs[(li + 1) * c:(li + 2) * c]))
            qs = jnp.where(upper[li], q * e, 0.0).astype(BF16)
            ks = jnp.where(upper[li], 0.0, k * e).astype(BF16)
            for hh, lanes in enumerate(head_lanes):
                a_l = lax.dot_general(qs[:, lanes], ks[:, lanes], nt_dims, preferred_element_type=F32)
                att[hh] = att[hh] + jnp.where(same[li], a_l, 0.0)

        for dlt in range(DIAG):
            if dlt == 0:
                w = q * k
            else:
                ksh = kpad[DIAG - dlt:DIAG - dlt + c, :]
                bsh = bpad[DIAG - dlt:DIAG - dlt + c, :]
                w = q * ksh * jnp.exp2(b - bsh)
            for hh, lanes in enumerate(head_lanes):
                col = jnp.sum(w[:, lanes], axis=-1, keepdims=True)
                att[hh] = jnp.where(dmat == dlt, col, att[hh])

        decay_row = jnp.exp2(b_tot[0:1, :])
        outs = []
        for hh, lanes in enumerate(head_lanes):
            s_prev = s_ref[0, hh]
            o = jnp.dot(q_in[:, lanes], s_prev.astype(BF16), preferred_element_type=F32)
            o = o + jnp.dot(att[hh].astype(BF16), vb[:, lanes], preferred_element_type=F32)
            decay_col = jnp.broadcast_to(decay_row[:, lanes], (A_DK, A_DK)).T
            s_ref[0, hh] = decay_col * s_prev + lax.dot_general(
                k_out[:, lanes], vb[:, lanes], tn_dims, preferred_element_type=F32)
            outs.append(o * lax.rsqrt(jnp.mean(o * o, axis=-1, keepdims=True) + NORM_EPS))
        y = jnp.concatenate(outs, axis=1) * gn_ref[...] * _silu(g_ref[rows, :])
        ya_ref[rows, :] = y.astype(ya_ref.dtype)
        return carry

    lax.fori_loop(0, n_chunks, chunk, 0)


def _hgrn(p, lbc_stack, layer, gn, s0, bsz, seq_len, heads):
    c = min(seq_len, SCAN_CHUNK)
    t_blk = min(seq_len, 256)
    nt = seq_len // t_blk
    aw = heads * A_DK

    def col_spec(seg):
        return pl.BlockSpec((t_blk, aw), lambda b, t: (b * nt + t, seg))

    sel = _hgrn_select(c)
    in_specs = [col_spec(0), col_spec(1), col_spec(2), col_spec(3),
                pl.BlockSpec((None, SUBLANES, aw), lambda b, t: (layer, 0, 0)),
                pl.BlockSpec((1, aw), lambda b, t: (0, 0)),
                pl.BlockSpec(sel.shape, lambda b, t: (0, 0))]
    args = [p, p, p, p, lbc_stack, gn.reshape(1, aw), sel]
    state_spec = pl.BlockSpec((1, heads, A_DK, A_DK), lambda b, t: (b, 0, 0, 0))
    if s0 is not None:
        in_specs.append(pl.BlockSpec((None, 1, heads, A_DK, A_DK), lambda b, t: (layer, b, 0, 0, 0)))
        args.append(s0)
    return pl.pallas_call(
        functools.partial(_hgrn_kernel, c=c, n_chunks=t_blk // c, heads=heads, has_init=s0 is not None),
        grid=(bsz, nt),
        in_specs=in_specs,
        out_specs=[pl.BlockSpec((t_blk, aw), lambda b, t: (b * nt + t, 0)), state_spec],
        out_shape=[jax.ShapeDtypeStruct((bsz * seq_len, aw), BF16),
                   jax.ShapeDtypeStruct((bsz, heads, A_DK, A_DK), F32)],
        scratch_shapes=[pltpu.VMEM((c + DIAG, aw), F32), pltpu.VMEM((c + DIAG, aw), F32)],
        compiler_params=_cparams(("arbitrary", "arbitrary")),
        name="hgrn2",
    )(*args)


def _ssd_kernel(*refs, t, bw, has_init):
    if has_init:
        (z_ref, xs_ref, bc_ref, dt_ref, cw_ref, cb_ref, dtb_ref, alog_ref, dsk_ref, gn_ref, exp_ref, sel_ref,
         s0_ref, c0_ref, yb_ref, s_out_ref, conv_out_ref, xpad, st) = refs
    else:
        (z_ref, xs_ref, bc_ref, dt_ref, cw_ref, cb_ref, dtb_ref, alog_ref, dsk_ref, gn_ref, exp_ref, sel_ref,
         yb_ref, s_out_ref, conv_out_ref, xpad, st) = refs
    ti = pl.program_id(1)
    gw = bw // B_GROUPS
    n_bc = B_GROUPS * B_DSTATE
    pad = SUBLANES

    @pl.when(ti == 0)
    def _():
        if has_init:
            xpad[0:pad, :] = c0_ref[0]
            st[...] = s0_ref[0].reshape(bw, B_DSTATE).T
        else:
            xpad[0:pad, :] = jnp.zeros((pad, xpad.shape[1]), F32)
            st[...] = jnp.zeros(st.shape, F32)

    xpad[pad:pad + t, 0:bw] = xs_ref[...]
    xpad[pad:pad + t, bw:bw + 2 * n_bc] = bc_ref[...]
    conv = cb_ref[...]
    for j in range(CONV_W):
        lo = pad - (CONV_W - 1) + j
        conv = conv + xpad[lo:lo + t, :] * cw_ref[j:j + 1, :]
    xbc = _silu(conv)
    x = xbc[:, 0:bw]

    dt = _softplus(dt_ref[...] + dtb_ref[...])
    a = dt * (-jnp.exp(alog_ref[...]))
    cs = _dot3(sel_ref[...], a)
    a_cum = cs[0:t]
    a_tot = cs[t:2 * t]
    ex = _dot3_rhs(jnp.concatenate([dt, a_cum, a_tot - a_cum], axis=0), exp_ref[...])
    dt_e = ex[0:t]
    acum_e = ex[t:2 * t]
    dec_e = ex[2 * t:3 * t]
    atot_e = acum_e[t - 1:t, :]

    xdt = x * dt_e
    xw = (xdt * jnp.exp(dec_e)).astype(BF16)
    xdt_b = xdt.astype(BF16)
    a_cum_t = a_cum.T
    rr = lax.broadcasted_iota(jnp.int32, (t, t), 0)
    cc = lax.broadcasted_iota(jnp.int32, (t, t), 1)
    causal = cc <= rr
    lane = lax.broadcasted_iota(jnp.int32, (t, LANES), 1)
    heads_per_group = gw // B_HEADDIM
    pairs_per_group = gw // LANES

    y_groups = []
    for g in range(B_GROUPS):
        bg = xbc[:, bw + g * B_DSTATE:bw + (g + 1) * B_DSTATE].astype(BF16)
        cg = xbc[:, bw + n_bc + g * B_DSTATE:bw + n_bc + (g + 1) * B_DSTATE].astype(BF16)
        gs = slice(g * gw, (g + 1) * gw)
        st_g = st[:, gs]
        scores = lax.dot_general(cg, bg, (((1,), (1,)), ((), ())), preferred_element_type=F32)
        y_off = jnp.dot(cg, st_g.astype(BF16), preferred_element_type=F32) * jnp.exp(acum_e[:, gs])
        st[:, gs] = jnp.exp(atot_e[:, gs]) * st_g + lax.dot_general(
            bg, xw[:, gs], (((0,), (0,)), ((), ())), preferred_element_type=F32)
        y_pairs = []
        for pr in range(pairs_per_group):
            cols = slice(g * gw + pr * LANES, g * gw + (pr + 1) * LANES)
            xp = xdt_b[:, cols]
            acc = None
            for half in range(LANES // B_HEADDIM):
                h = g * heads_per_group + pr * (LANES // B_HEADDIM) + half
                diff = a_cum[:, h:h + 1] - a_cum_t[h:h + 1, :]
                lmat = jnp.where(causal, jnp.exp(jnp.where(causal, diff, 0.0)), 0.0)
                m = (scores * lmat).astype(BF16)
                in_half = (lane // B_HEADDIM) == half
                term = jnp.dot(m, jnp.where(in_half, xp, jnp.zeros_like(xp)), preferred_element_type=F32)
                acc = term if acc is None else acc + term
            y_pairs.append(acc)
        y_groups.append(jnp.concatenate(y_pairs, axis=1) + y_off)
    y = jnp.concatenate(y_groups, axis=1) + dsk_ref[...] * x
    y = y * _silu(z_ref[...])
    outs = []
    for g in range(B_GROUPS):
        yg = y[:, g * gw:(g + 1) * gw]
        outs.append(yg * lax.rsqrt(jnp.mean(yg * yg, axis=-1, keepdims=True) + NORM_EPS))
    yb_ref[...] = (jnp.concatenate(outs, axis=1) * gn_ref[...]).astype(yb_ref.dtype)

    tail = xpad[t:t + pad, :]
    xpad[0:pad, :] = tail

    @pl.when(ti == pl.num_programs(1) - 1)
    def _():
        conv_out_ref[0] = tail
        s_out_ref[0] = st[...].T.reshape(s_out_ref.shape[1:])


def _ssd(p, cols, prm, s0, c0, bsz, seq_len):
    bw = prm["bw"]
    heads = bw // B_HEADDIM
    n_bc = B_GROUPS * B_DSTATE
    cdim = bw + 2 * n_bc
    t = min(seq_len, 128)
    nt = seq_len // t
    tri = np.arange(t)[None, :] <= np.arange(t)[:, None]
    sel = jnp.asarray(np.concatenate([tri, np.ones((t, t), bool)], 0).astype(np.float32), dtype=BF16)
    expand = np.zeros((LANES, bw), np.float32)
    expand[np.arange(bw) // B_HEADDIM, np.arange(bw)] = 1.0
    expand = jnp.asarray(expand, dtype=BF16)

    def blk(width, off):
        return pl.BlockSpec((t, width), lambda b, ti: (b * nt + ti, off // width))

    def full(shape):
        return pl.BlockSpec(shape, lambda b, ti: (0,) * len(shape))

    in_specs = [blk(bw, cols["z"]), blk(bw, cols["xs"]), blk(2 * n_bc, cols["bc"]), blk(LANES, cols["dt"]),
                full((CONV_W, cdim)), full((1, cdim)), full((1, LANES)), full((1, LANES)),
                full((1, bw)), full((1, bw)), full(expand.shape), full(sel.shape)]
    args = [p, p, p, p, prm["conv_w"], prm["conv_b"], prm["dt_bias"], prm["a_log"], prm["d_skip"], prm["gn"],
            expand, sel]
    state_spec = pl.BlockSpec((1, heads, B_HEADDIM, B_DSTATE), lambda b, ti: (b, 0, 0, 0))
    conv_spec = pl.BlockSpec((1, SUBLANES, cdim), lambda b, ti: (b, 0, 0))
    if s0 is not None:
        layer = prm["layer"]
        in_specs += [pl.BlockSpec((None, 1, heads, B_HEADDIM, B_DSTATE), lambda b, ti: (layer, b, 0, 0, 0)),
                     pl.BlockSpec((None, 1, SUBLANES, cdim), lambda b, ti: (layer, b, 0, 0))]
        args += [s0, c0]
    return pl.pallas_call(
        functools.partial(_ssd_kernel, t=t, bw=bw, has_init=s0 is not None),
        grid=(bsz, nt),
        in_specs=in_specs,
        out_specs=[pl.BlockSpec((t, bw), lambda b, ti: (b * nt + ti, 0)), state_spec, conv_spec],
        out_shape=[jax.ShapeDtypeStruct((bsz * seq_len, bw), BF16),
                   jax.ShapeDtypeStruct((bsz, heads, B_HEADDIM, B_DSTATE), F32),
                   jax.ShapeDtypeStruct((bsz, SUBLANES, cdim), F32)],
        scratch_shapes=[pltpu.VMEM((t + SUBLANES, cdim), F32), pltpu.VMEM((B_DSTATE, bw), F32)],
        compiler_params=_cparams(("arbitrary", "arbitrary")),
        name="ssd",
    )(*args)


def _cmlp_kernel(u_ref, v_ref, lng_ref, lnb_ref, ws_ref, bst_ref, *out_refs, t, n_chunks, keep_v):
    yc_ref = out_refs[0]
    cw = u_ref.shape[1] // C_GROUPS
    rr = lax.broadcasted_iota(jnp.int32, (t, t), 0)
    cc = lax.broadcasted_iota(jnp.int32, (t, t), 1)
    wts = [jnp.where(cc <= rr, ws_ref[g, 0:t, 0:t], 0.0).astype(BF16) for g in range(C_GROUPS)]
    for ci in range(n_chunks):
        rows = slice(ci * t, (ci + 1) * t)
        u = _gelu(u_ref[rows, :])
        gv = _gelu(v_ref[rows, :])
        mu = jnp.mean(gv, axis=-1, keepdims=True)
        dv = gv - mu
        var = jnp.mean(dv * dv, axis=-1, keepdims=True)
        v = dv * lax.rsqrt(var + NORM_EPS) * lng_ref[...] + lnb_ref[...]
        if keep_v:
            out_refs[1][rows, :] = v
        vb = v.astype(BF16)
        for g in range(C_GROUPS):
            lanes = slice(g * cw, (g + 1) * cw)
            mixed = jnp.dot(wts[g], vb[:, lanes], preferred_element_type=F32) + bst_ref[0:t, g:g + 1]
            yc_ref[rows, lanes] = (u[:, lanes] * mixed).astype(yc_ref.dtype)


def _cmlp(p, col_u, col_v, prm, bsz, seq_len, keep_v):
    cw = prm["cw"]
    t = min(seq_len, CMLP_CHUNK)
    t_blk = min(seq_len, 4 * CMLP_CHUNK)
    n = bsz * seq_len

    def full(shape):
        return pl.BlockSpec(shape, lambda i: (0,) * len(shape))

    out_specs = [pl.BlockSpec((t_blk, cw), lambda i: (i, 0))]
    out_shape = [jax.ShapeDtypeStruct((n, cw), BF16)]
    if keep_v:
        out_specs.append(pl.BlockSpec((t_blk, cw), lambda i: (i, 0)))
        out_shape.append(jax.ShapeDtypeStruct((n, cw), F32))
    return pl.pallas_call(
        functools.partial(_cmlp_kernel, t=t, n_chunks=t_blk // t, keep_v=keep_v),
        grid=(n // t_blk,),
        in_specs=[pl.BlockSpec((t_blk, cw), lambda i: (i, col_u // cw)),
                  pl.BlockSpec((t_blk, cw), lambda i: (i, col_v // cw)),
                  full((1, cw)), full((1, cw)),
                  full((C_GROUPS, CMLP_CHUNK, CMLP_CHUNK)), full((CMLP_CHUNK, C_GROUPS))],
        out_specs=out_specs,
        out_shape=out_shape,
        compiler_params=_cparams(("arbitrary",)),
        name="cmlp",
    )(p, p, prm["ln_g"], prm["ln_b"], prm["ws"], prm["bs_t"])


def _run_trunk(x3, mod, st_hgrn, st_ssm, st_conv, keep_v, w):
    bsz, seq_len, d = x3.shape
    depth = mod.shape[0]
    aw = bw = cw = d // 2
    heads_a = aw // A_DK
    x = x3.reshape(bsz * seq_len, d)
    col = w["cols"]
    hgrn_out, ssm_out, conv_out, v_out = [], [], [], []
    for l in range(depth):
        mod6 = mod[l].reshape(bsz, 6, 1, d).transpose(1, 0, 2, 3)
        p, gates = _in_proj(x, w["norm1_g"][l], mod6, w["w_in"], l, seq_len, col["gate"])

        y_a, s_h = _hgrn(p, w["lbc"], l, w["hgrn_onorm_g"][l], st_hgrn, bsz, seq_len, heads_a)
        ssd_prm = dict(bw=bw, layer=l, conv_w=w["ssm_conv_w"][l], conv_b=w["ssm_conv_b"][l][None],
                       dt_bias=w["dt_bias_pad"][l][None], a_log=w["a_log_pad"][l][None],
                       d_skip=w["d_skip"][l][None], gn=w["ssm_onorm_g"][l][None])
        y_b, s_s, conv_tail = _ssd(p, col, ssd_prm, st_ssm, st_conv, bsz, seq_len)
        cm_prm = dict(cw=cw, ln_g=w["cmlp_ln_g"][l][None], ln_b=w["cmlp_ln_b"][l][None],
                      ws=w["cmlp_ws"][l], bs_t=w["cmlp_bs"][l].T)
        c_res = _cmlp(p, col["u"], col["v"], cm_prm, bsz, seq_len, keep_v)
        merged = _merge(y_a, y_b, c_res[0], w["w_branch"], l, gates, d)
        x = _proj_residual(merged, w["w_out"], l, x, mod6, 2, seq_len, "out_proj")
        hid = _mlp_up(x, w["norm2_g"][l], mod6, w["w_up"], l, seq_len)
        x = _proj_residual(hid, w["w_down"], l, x, mod6, 5, seq_len, "mlp_down")

        hgrn_out.append(s_h)
        ssm_out.append(s_s)
        conv_out.append(conv_tail[:, SUBLANES - (CONV_W - 1):, :])
        if keep_v:
            v_out.append(c_res[1].reshape(bsz, seq_len, cw))
    y = _final_norm(x, w["final_g"]).reshape(bsz, seq_len, d)
    return (y, jnp.stack(hgrn_out), jnp.stack(ssm_out), jnp.stack(conv_out),
            jnp.stack(v_out) if keep_v else None)


def kernel(x_prompt, x_sample, state_hgrn, state_ssm, state_conv, c_prompt, c_sample, norm1_g, norm2_g,
           w_mod, b_mod, w_in, hgrn_lb, hgrn_onorm_g, ssm_conv_w, ssm_conv_b, ssm_dt_bias, ssm_a_log, ssm_d,
           ssm_onorm_g, cmlp_ln_g, cmlp_ln_b, cmlp_ws, cmlp_bs, w_branch, w_out, w_up, w_down, final_g):
    d = x_prompt.shape[-1]
    depth = w_in.shape[0]
    aw = bw = cw = d // 2
    n_bc = B_GROUPS * B_DSTATE
    heads_b = bw // B_HEADDIM
    assert heads_b <= LANES and bw % LANES == 0 and DT_PAD % LANES == 0

    o_dt = 4 * aw + bw + bw + 2 * n_bc
    o_u = o_dt + heads_b
    w_in_r = jnp.concatenate(
        [w_in[:, :, :o_u].astype(BF16), jnp.zeros((depth, d, DT_PAD - heads_b), BF16),
         w_in[:, :, o_u:].astype(BF16)], axis=2)
    cols = dict(z=4 * aw, xs=4 * aw + bw, bc=4 * aw + 2 * bw, dt=o_dt)
    cols["u"] = o_dt + DT_PAD
    cols["v"] = cols["u"] + cw
    cols["gate"] = cols["v"] + cw
    assert cols["gate"] % COL_TILE == 0 and w_in_r.shape[2] == cols["gate"] + N_BRANCH * d

    pad_h = lambda a: jnp.pad(a.astype(F32), ((0, 0), (0, LANES - heads_b)))
    w = dict(
        cols=cols, w_in=w_in_r,
        norm1_g=norm1_g, norm2_g=norm2_g, final_g=final_g,
        lbc=_lb_consts(hgrn_lb), hgrn_onorm_g=hgrn_onorm_g,
        ssm_conv_w=ssm_conv_w, ssm_conv_b=ssm_conv_b,
        dt_bias_pad=pad_h(ssm_dt_bias), a_log_pad=pad_h(ssm_a_log),
        d_skip=jnp.repeat(ssm_d.astype(F32), B_HEADDIM, axis=1), ssm_onorm_g=ssm_onorm_g,
        cmlp_ln_g=cmlp_ln_g, cmlp_ln_b=cmlp_ln_b, cmlp_ws=cmlp_ws, cmlp_bs=cmlp_bs,
        w_branch=w_branch.astype(BF16).reshape(depth, N_BRANCH, aw, d), w_out=w_out.astype(BF16),
        w_up=w_up.astype(BF16), w_down=w_down.astype(BF16),
    )

    nb = x_prompt.shape[0]
    mod = _modulation(jnp.concatenate([c_prompt, c_sample], axis=0), w_mod, b_mod)
    y_p, hgrn_p, ssm_p, conv_p, _ = _run_trunk(x_prompt, mod[:, :nb], None, None, None, False, w)
    conv_pad = jnp.pad(state_conv, ((0, 0), (0, 0), (SUBLANES - (CONV_W - 1), 0), (0, 0)))
    y_s, hgrn_s, ssm_s, conv_s, v_s = _run_trunk(x_sample, mod[:, nb:], state_hgrn, state_ssm, conv_pad, True, w)
    return (y_p, y_s, hgrn_p, ssm_p, conv_p, hgrn_s, ssm_s, conv_s, v_s)
```

```python
import functools
import math

import numpy as np
import jax
import jax.numpy as jnp
from jax import lax
from jax.experimental import pallas as pl
from jax.experimental.pallas import tpu as pltpu

F32 = jnp.float32
BF16 = jnp.bfloat16

A_DK = 128
B_HEADDIM = 64
B_GROUPS = 2
B_DSTATE = 128
CONV_W = 4
C_GROUPS = 4
CMLP_CHUNK = 128
N_BRANCH = 3
SCAN_CHUNK = 64
NORM_EPS = 1e-6
LB_FLOOR = 1e-30
LOG2E = 1.4426950408889634

LANES = 128
SUBLANES = 8
VMEM_LIMIT = 56 * 1024 * 1024
DIAG = SUBLANES
DT_PAD = 512
ROW_TILE = 1024
COL_TILE = 1024
IN_TILE = 1536
MERGE_TILE = 512
DOWN_TILE = 256
EPILOGUE_COLS = 256
NORM_ROWS = 64


def _cparams(sem):
    return pltpu.CompilerParams(dimension_semantics=sem, vmem_limit_bytes=VMEM_LIMIT)


def _split3(x):
    hi = x.astype(BF16)
    r1 = x - hi.astype(F32)
    mid = r1.astype(BF16)
    lo = (r1 - mid.astype(F32)).astype(BF16)
    return hi, mid, lo


def _dot3(sel, x):
    hi, mid, lo = _split3(x)
    d = lambda p: jnp.dot(sel, p, preferred_element_type=F32)
    return d(hi) + d(mid) + d(lo)


def _dot3_rhs(x, sel):
    hi, mid, lo = _split3(x)
    d = lambda p: jnp.dot(p, sel, preferred_element_type=F32)
    return d(hi) + d(mid) + d(lo)


def _silu(x):
    return x / (1.0 + jnp.exp(-x))


def _softplus(x):
    return jnp.maximum(x, 0.0) + jnp.log1p(jnp.exp(-jnp.abs(x)))


def _gelu(x):
    return 0.5 * x * (1.0 + lax.erf(x * (1.0 / math.sqrt(2.0))))


def _lb_kernel(lb_ref, out_ref, *, depth):
    x = lb_ref[...]
    m = jnp.max(x, axis=0, keepdims=True)
    e = jnp.exp(x - m)
    p = e / jnp.sum(e, axis=0, keepdims=True)
    acc = jnp.zeros_like(p[0:1])
    zeros5 = jnp.zeros((SUBLANES - 3, x.shape[1]), F32)
    for l in range(depth):
        acc = acc + p[l:l + 1]
        lb = acc - p[0:1]
        out_ref[l] = jnp.concatenate(
            [jnp.log(jnp.maximum(lb, LB_FLOOR)), jnp.log1p(-lb), 1.0 - lb, zeros5], axis=0)


def _lb_consts(hgrn_lb):
    depth, aw = hgrn_lb.shape
    return pl.pallas_call(
        functools.partial(_lb_kernel, depth=depth),
        out_shape=jax.ShapeDtypeStruct((depth, SUBLANES, aw), F32),
        name="hgrn_lb",
    )(hgrn_lb.astype(F32))


def _mod_kernel(c_ref, w_ref, b_ref, o_ref):
    cs = _silu(c_ref[...]).astype(BF16)
    o_ref[...] = jnp.dot(cs, w_ref[...].astype(BF16), preferred_element_type=F32) + b_ref[...]


def _modulation(c_all, w_mod, b_mod):
    depth, d, n6 = w_mod.shape
    s = c_all.shape[0]
    tn = 1024
    return pl.pallas_call(
        _mod_kernel,
        grid=(depth, n6 // tn),
        in_specs=[
            pl.BlockSpec((s, d), lambda l, j: (0, 0)),
            pl.BlockSpec((None, d, tn), lambda l, j: (l, 0, j)),
            pl.BlockSpec((None, 1, tn), lambda l, j: (l, 0, j)),
        ],
        out_specs=pl.BlockSpec((None, s, tn), lambda l, j: (l, 0, j)),
        out_shape=jax.ShapeDtypeStruct((depth, s, n6), F32),
        compiler_params=_cparams(("arbitrary", "arbitrary")),
        name="adaln_mod",
    )(c_all, w_mod, b_mod.reshape(depth, 1, n6))


def _row_tiling(n_tok, seq_len, tm_max):
    tm = min(tm_max, n_tok)
    if seq_len >= tm:
        assert seq_len % tm == 0
        return tm, 1, seq_len // tm
    assert tm % seq_len == 0
    return tm, tm // seq_len, 1


def _per_seq(x, nseg):
    tm, d = x.shape
    return x.reshape(nseg, tm // nseg, d)


def _final_norm_kernel(x_ref, g_ref, o_ref):
    x = x_ref[...]
    o_ref[...] = x * lax.rsqrt(jnp.mean(x * x, axis=-1, keepdims=True) + NORM_EPS) * g_ref[...]


def _final_norm(x, g):
    n, d = x.shape
    tm = min(512, n)
    return pl.pallas_call(
        _final_norm_kernel,
        grid=(n // tm,),
        in_specs=[pl.BlockSpec((tm, d), lambda i: (i, 0)), pl.BlockSpec((1, d), lambda i: (0, 0))],
        out_specs=pl.BlockSpec((tm, d), lambda i: (i, 0)),
        out_shape=jax.ShapeDtypeStruct((n, d), F32),
        compiler_params=_cparams(("arbitrary",)),
        name="final_norm",
    )(x, g.reshape(1, d))


def _norm_mod_rows(x_ref, g_ref, sc_ref, sh_ref, h_scr, nseg):
    tm = x_ref.shape[0]
    seg = tm // nseg
    rb = min(seg, NORM_ROWS)
    g = g_ref[...]

    def body(r, carry):
        rows = pl.ds(pl.multiple_of(r * rb, rb), rb)
        s = r // (seg // rb)
        x = x_ref[rows, :]
        y = x * lax.rsqrt(jnp.mean(x * x, axis=-1, keepdims=True) + NORM_EPS) * g
        h_scr[rows, :] = (y * (1.0 + sc_ref[s]) + sh_ref[s]).astype(BF16)
        return carry

    lax.fori_loop(0, tm // rb, body, 0)


def _mlp_up_kernel(x_ref, g_ref, sc_ref, sh_ref, w_ref, o_ref, h_scr, *, nseg):
    @pl.when(pl.program_id(1) == 0)
    def _():
        _norm_mod_rows(x_ref, g_ref, sc_ref, sh_ref, h_scr, nseg)

    y = jnp.maximum(jnp.dot(h_scr[...], w_ref[...], preferred_element_type=F32), 0.0)
    o_ref[...] = (y * y).astype(o_ref.dtype)


def _in_proj_kernel(x_ref, g_ref, sc_ref, sh_ref, w_ref, p_ref, h_scr, *, nseg):
    @pl.when(pl.program_id(1) == 0)
    def _():
        _norm_mod_rows(x_ref, g_ref, sc_ref, sh_ref, h_scr, nseg)

    p_ref[...] = jnp.dot(h_scr[...], w_ref[...], preferred_element_type=F32)


def _norm_proj_call(kern, x, g, mod6, k_scale, k_shift, w_stack, layer, seq_len, tn, out_specs, out_shape, name):
    n, d = x.shape
    m = w_stack.shape[2]
    tm, nseg, tps = _row_tiling(n, seq_len, ROW_TILE)

    def mod_spec(k):
        return pl.BlockSpec((None, nseg, 1, d), lambda i, j: (k, i // tps, 0, 0))

    return pl.pallas_call(
        functools.partial(kern, nseg=nseg),
        grid=(n // tm, m // tn),
        in_specs=[
            pl.BlockSpec((tm, d), lambda i, j: (i, 0)),
            pl.BlockSpec((1, d), lambda i, j: (0, 0)),
            mod_spec(k_scale),
            mod_spec(k_shift),
            pl.BlockSpec((None, d, tn), lambda i, j: (layer, 0, j)),
        ],
        out_specs=out_specs(tm),
        out_shape=out_shape,
        scratch_shapes=[pltpu.VMEM((tm, d), BF16)],
        compiler_params=_cparams(("arbitrary", "arbitrary")),
        name=name,
    )(x, g.reshape(1, d), mod6, mod6, w_stack)


def _mlp_up(x, g, mod6, w_stack, layer, seq_len):
    n = x.shape[0]
    m = w_stack.shape[2]
    tn = COL_TILE
    return _norm_proj_call(
        _mlp_up_kernel, x, g, mod6, 4, 3, w_stack, layer, seq_len, tn,
        lambda tm: pl.BlockSpec((tm, tn), lambda i, j: (i, j)),
        jax.ShapeDtypeStruct((n, m), BF16), "mlp_up")


def _in_proj(x, g, mod6, w_stack, layer, seq_len):
    n = x.shape[0]
    m = w_stack.shape[2]
    tn = IN_TILE
    return _norm_proj_call(
        _in_proj_kernel, x, g, mod6, 1, 0, w_stack, layer, seq_len, tn,
        lambda tm: pl.BlockSpec((tm, tn), lambda i, j: (i, j)),
        jax.ShapeDtypeStruct((n, m), F32), "in_proj")


def _proj_res_kernel(a_ref, w_ref, x_ref, gate_ref, o_ref, *, nseg, nk):
    part = jnp.dot(a_ref[...], w_ref[...], preferred_element_type=F32)

    def finish(acc):
        x = x_ref[...]
        o_ref[...] = (_per_seq(x, nseg) + gate_ref[...] * _per_seq(acc, nseg)).reshape(x.shape)

    if nk == 1:
        finish(part)
        return
    k = pl.program_id(2)

    @pl.when(k == 0)
    def _():
        o_ref[...] = part

    @pl.when((k > 0) & (k < nk - 1))
    def _():
        o_ref[...] += part

    @pl.when(k == nk - 1)
    def _():
        finish(o_ref[...] + part)


def _proj_residual(a, w_stack, layer, x, mod6, k_gate, seq_len, name):
    n, kdim = a.shape
    d = w_stack.shape[2]
    tn = COL_TILE if kdim <= d else DOWN_TILE
    tk = kdim
    nk = kdim // tk
    tm, nseg, tps = _row_tiling(n, seq_len, ROW_TILE)
    return pl.pallas_call(
        functools.partial(_proj_res_kernel, nseg=nseg, nk=nk),
        grid=(n // tm, d // tn, nk),
        in_specs=[
            pl.BlockSpec((tm, tk), lambda i, j, k: (i, k)),
            pl.BlockSpec((None, tk, tn), lambda i, j, k: (layer, k, j)),
            pl.BlockSpec((tm, tn), lambda i, j, k: (i, j)),
            pl.BlockSpec((None, nseg, 1, tn), lambda i, j, k: (k_gate, i // tps, 0, j)),
        ],
        out_specs=pl.BlockSpec((tm, tn), lambda i, j, k: (i, j)),
        out_shape=jax.ShapeDtypeStruct((n, d), F32),
        compiler_params=_cparams(("arbitrary", "arbitrary", "arbitrary")),
        name=name,
    )(a, w_stack, x, mod6)


def _merge_kernel(ya_ref, yb_ref, yc_ref, wa_ref, wb_ref, wc_ref, ga_ref, gb_ref, gc_ref, o_ref):
    acc = None
    for y_ref, w_ref, g_ref in ((ya_ref, wa_ref, ga_ref), (yb_ref, wb_ref, gb_ref), (yc_ref, wc_ref, gc_ref)):
        gate = 1.0 / (1.0 + jnp.exp(-g_ref[...]))
        term = gate * jnp.dot(y_ref[...], w_ref[...], preferred_element_type=F32)
        acc = term if acc is None else acc + term
    o_ref[...] = acc.astype(o_ref.dtype)


def _merge(ya, yb, yc, w_stack, layer, p, gate_col0, d):
    n, bw = ya.shape
    tn = MERGE_TILE
    tm = min(ROW_TILE, n)
    gsteps = d // tn
    g0 = gate_col0 // tn
    y_spec = pl.BlockSpec((tm, bw), lambda i, j: (i, 0))

    def w_spec(k):
        return pl.BlockSpec((None, None, bw, tn), lambda i, j: (layer, k, 0, j))

    def g_spec(k):
        return pl.BlockSpec((tm, tn), lambda i, j: (i, g0 + k * gsteps + j))

    return pl.pallas_call(
        _merge_kernel,
        grid=(n // tm, d // tn),
        in_specs=[y_spec, y_spec, y_spec, w_spec(0), w_spec(1), w_spec(2), g_spec(0), g_spec(1), g_spec(2)],
        out_specs=pl.BlockSpec((tm, tn), lambda i, j: (i, j)),
        out_shape=jax.ShapeDtypeStruct((n, d), BF16),
        compiler_params=_cparams(("arbitrary", "arbitrary")),
        name="merge",
    )(ya, yb, yc, w_stack, w_stack, w_stack, p, p, p)


def _hgrn_levels(c):
    lv, m = [], c // 2
    while m >= DIAG:
        lv.append(m)
        m //= 2
    return tuple(lv)


def _hgrn_select(c):
    t = np.arange(c)[:, None]
    j = np.arange(c)[None, :]
    mats = [j <= t]
    for m in _hgrn_levels(c):
        mats.append(j <= (t // (2 * m)) * (2 * m) + m - 1)
    mats.append(np.ones((c, c), bool))
    return jnp.asarray(np.concatenate(mats, axis=0).astype(np.float32), dtype=BF16)


def _hgrn_kernel(*refs, c, n_chunks, heads, has_init):
    if has_init:
        q_ref, f_ref, i_ref, g_ref, lbc_ref, gn_ref, sel_ref, s0_ref, ya_ref, s_ref, bpad, kpad = refs
    else:
        q_ref, f_ref, i_ref, g_ref, lbc_ref, gn_ref, sel_ref, ya_ref, s_ref, bpad, kpad = refs

    @pl.when(pl.program_id(1) == 0)
    def _():
        s_ref[...] = s0_ref[...] if has_init else jnp.zeros(s_ref.shape, F32)

    _hgrn_chunks(q_ref, f_ref, i_ref, g_ref, lbc_ref, gn_ref, sel_ref, ya_ref, s_ref, bpad, kpad,
                 c=c, n_chunks=n_chunks, heads=heads, unroll=False)


def _hgrn_chunks(q_ref, f_ref, i_ref, g_ref, lbc_ref, gn_ref, sel_ref, ya_ref, s_ref, bpad, kpad,
                 *, c, n_chunks, heads, unroll, fillers=()):
    levels = _hgrn_levels(c)
    nl = len(levels)
    width = heads * A_DK
    head_lanes = [slice(hh * A_DK, (hh + 1) * A_DK) for hh in range(heads)]

    bpad[0:DIAG, :] = jnp.zeros((DIAG, width), F32)
    kpad[0:DIAG, :] = jnp.zeros((DIAG, width), F32)

    sel = sel_ref[...]
    log_lb = lbc_ref[0:1, :]
    log1m_lb = lbc_ref[1:2, :]
    one_m_lb = lbc_ref[2:3, :]
    row = lax.broadcasted_iota(jnp.int32, (c, width), 0)
    rr = lax.broadcasted_iota(jnp.int32, (c, c), 0)
    cc = lax.broadcasted_iota(jnp.int32, (c, c), 1)
    upper = [(row & m) != 0 for m in levels]
    same = [(rr ^ cc) < 2 * m for m in levels]
    dmat = jnp.where(((rr ^ cc) < DIAG) & (cc <= rr), rr - cc, -1)
    nt_dims = (((1,), (1,)), ((), ()))
    tn_dims = (((0,), (0,)), ((), ()))

    def chunk(ci):
        rows = slice(ci * c, (ci + 1) * c) if unroll else pl.ds(pl.multiple_of(ci * c, c), c)
        z = f_ref[rows, :]
        aq = q_ref[rows, :]

        log_sig = jnp.minimum(z, 0.0) - jnp.log(1.0 + jnp.exp(-jnp.abs(z)))
        bb = log1m_lb + log_sig
        log_f = jnp.maximum(log_lb, bb) + jnp.log(1.0 + jnp.exp(-jnp.abs(log_lb - bb)))
        k = one_m_lb / (1.0 + jnp.exp(z))
        q = _silu(aq)

        cs = _dot3(sel, log_f * LOG2E)
        b = cs[0:c]
        b_tot = cs[(nl + 1) * c:(nl + 2) * c]
        bpad[DIAG:DIAG + c, :] = b
        kpad[DIAG:DIAG + c, :] = k
        vb = i_ref[rows, :].astype(BF16)
        q_in = (q * jnp.exp2(b)).astype(BF16)
        k_out = (k * jnp.exp2(b_tot - b)).astype(BF16)
        yield

        att = [jnp.zeros((c, c), F32) for _ in range(heads)]
        for li in range(nl):
            e = jnp.exp2(-jnp.abs(b - cs[(li + 1) * c:(li + 2) * c]))
            qs = jnp.where(upper[li], q * e, 0.0).astype(BF16)
            ks = jnp.where(upper[li], 0.0, k * e).astype(BF16)
            for hh, lanes in enumerate(head_lanes):
                a_l = lax.dot_general(qs[:, lanes], ks[:, lanes], nt_dims, preferred_element_type=F32)
                att[hh] = att[hh] + jnp.where(same[li], a_l, 0.0)
        yield

        for dlt in range(DIAG):
            if dlt == 0:
                w = q * k
            else:
                ksh = kpad[DIAG - dlt:DIAG - dlt + c, :]
                bsh = bpad[DIAG - dlt:DIAG - dlt + c, :]
                w = q * ksh * jnp.exp2(b - bsh)
            for hh, lanes in enumerate(head_lanes):
                col = jnp.sum(w[:, lanes], axis=-1, keepdims=True)
                att[hh] = jnp.where(dmat == dlt, col, att[hh])
            if dlt in (DIAG // 2 - 1, DIAG - 1):
                yield

        decay_row = jnp.exp2(b_tot[0:1, :])
        outs = []
        for hh, lanes in enumerate(head_lanes):
            s_prev = s_ref[0, hh]
            o = jnp.dot(q_in[:, lanes], s_prev.astype(BF16), preferred_element_type=F32)
            o = o + jnp.dot(att[hh].astype(BF16), vb[:, lanes], preferred_element_type=F32)
            decay_col = jnp.broadcast_to(decay_row[:, lanes], (A_DK, A_DK)).T
            s_ref[0, hh] = decay_col * s_prev + lax.dot_general(
                k_out[:, lanes], vb[:, lanes], tn_dims, preferred_element_type=F32)
            outs.append(o * lax.rsqrt(jnp.mean(o * o, axis=-1, keepdims=True) + NORM_EPS))
        y = jnp.concatenate(outs, axis=1) * gn_ref[...] * _silu(g_ref[rows, :])
        ya_ref[rows, :] = y.astype(ya_ref.dtype)
        yield

    if not unroll:
        def body(ci, carry):
            for _ in chunk(ci):
                pass
            return carry

        lax.fori_loop(0, n_chunks, body, 0)
        return

    phases = [ph for ci in range(n_chunks) for ph in _phases_of(chunk(ci))]
    per_filler = -(-len(phases) // max(len(fillers), 1))
    pos = 0
    for fill in fillers:
        fill()
        for ph in phases[pos:pos + per_filler]:
            ph()
        pos += per_filler
    for ph in phases[pos:]:
        ph()


HGRN_PHASES = 5


def _phases_of(gen):
    return [functools.partial(next, gen, None) for _ in range(HGRN_PHASES)]


def _hgrn_specs(p, lbc_stack, layer, gn, heads, t_blk, row_block, out_rows):
    aw = heads * A_DK
    c = min(t_blk, SCAN_CHUNK)
    sel = _hgrn_select(c)

    def col_spec(seg):
        return pl.BlockSpec((t_blk, aw), lambda *g: (row_block(*g), seg))

    in_specs = [col_spec(0), col_spec(1), col_spec(2), col_spec(3),
                pl.BlockSpec((None, SUBLANES, aw), lambda *g: (layer, 0, 0)),
                pl.BlockSpec((1, aw), lambda *g: (0, 0)),
                pl.BlockSpec(sel.shape, lambda *g: (0, 0))]
    args = [p, p, p, p, lbc_stack, gn.reshape(1, aw), sel]
    scratch = [pltpu.VMEM((c + DIAG, aw), F32), pltpu.VMEM((c + DIAG, aw), F32)]
    return c, in_specs, args, pl.BlockSpec((t_blk, aw), lambda *g: (row_block(*g), 0)), scratch


def _mlp_up_hgrn_kernel(x_ref, g_ref, sc_ref, sh_ref, w_ref, q_ref, f_ref, i_ref, gg_ref, lbc_ref, gn_ref, sel_ref,
                        o_ref, ya_ref, s_ref, h_scr, bpad, kpad, *, nseg, c, n_chunks, heads, steps_per_seq):
    j = pl.program_id(1)
    step = pl.program_id(0) * pl.num_programs(1) + j

    @pl.when(j == 0)
    def _():
        _norm_mod_rows(x_ref, g_ref, sc_ref, sh_ref, h_scr, nseg)

    @pl.when(step % steps_per_seq == 0)
    def _():
        s_ref[...] = jnp.zeros(s_ref.shape, F32)

    def sub_dot(c0):
        cols = slice(c0, c0 + EPILOGUE_COLS)
        y = jnp.maximum(jnp.dot(h_scr[...], w_ref[:, cols], preferred_element_type=F32), 0.0)
        o_ref[:, cols] = (y * y).astype(o_ref.dtype)

    fillers = [functools.partial(sub_dot, c0) for c0 in range(0, w_ref.shape[1], EPILOGUE_COLS)]
    _hgrn_chunks(q_ref, f_ref, i_ref, gg_ref, lbc_ref, gn_ref, sel_ref, ya_ref, s_ref, bpad, kpad,
                 c=c, n_chunks=n_chunks, heads=heads, unroll=True, fillers=fillers)


def _mlp_up_hgrn(x, g, mod6, w_stack, layer, seq_len, p_other, lbc_stack, gn, heads):
    n, d = x.shape
    m = w_stack.shape[2]
    tn = COL_TILE
    tm, nseg, tps = _row_tiling(n, seq_len, ROW_TILE)
    ni, nj = n // tm, m // tn
    n_other = p_other.shape[0]
    t_step = n_other // (ni * nj)
    assert t_step * ni * nj == n_other and seq_len % t_step == 0 and t_step % SUBLANES == 0
    steps_per_seq = seq_len // t_step
    aw = heads * A_DK
    c, h_in_specs, h_args, ya_spec, h_scratch = _hgrn_specs(
        p_other, lbc_stack, layer, gn, heads, t_step, lambda i, j: i * nj + j, n_other)

    def mod_spec(k):
        return pl.BlockSpec((None, nseg, 1, d), lambda i, j: (k, i // tps, 0, 0))

    state_spec = pl.BlockSpec((1, heads, A_DK, A_DK), lambda i, j: ((i * nj + j) // steps_per_seq, 0, 0, 0))
    return pl.pallas_call(
        functools.partial(_mlp_up_hgrn_kernel, nseg=nseg, c=c, n_chunks=t_step // c, heads=heads,
                          steps_per_seq=steps_per_seq),
        grid=(ni, nj),
        in_specs=[
            pl.BlockSpec((tm, d), lambda i, j: (i, 0)),
            pl.BlockSpec((1, d), lambda i, j: (0, 0)),
            mod_spec(4),
            mod_spec(3),
            pl.BlockSpec((None, d, tn), lambda i, j: (layer, 0, j)),
        ] + h_in_specs,
        out_specs=[pl.BlockSpec((tm, tn), lambda i, j: (i, j)), ya_spec, state_spec],
        out_shape=[jax.ShapeDtypeStruct((n, m), BF16),
                   jax.ShapeDtypeStruct((n_other, aw), BF16),
                   jax.ShapeDtypeStruct((n_other // seq_len, heads, A_DK, A_DK), F32)],
        scratch_shapes=[pltpu.VMEM((tm, d), BF16)] + h_scratch,
        compiler_params=_cparams(("arbitrary", "arbitrary")),
        name="mlp_up_hgrn2",
    )(x, g.reshape(1, d), mod6, mod6, w_stack, *h_args)


def _hgrn(p, lbc_stack, layer, gn, s0, bsz, seq_len, heads):
    c = min(seq_len, SCAN_CHUNK)
    t_blk = min(seq_len, 256)
    nt = seq_len // t_blk
    aw = heads * A_DK

    def col_spec(seg):
        return pl.BlockSpec((t_blk, aw), lambda b, t: (b * nt + t, seg))

    sel = _hgrn_select(c)
    in_specs = [col_spec(0), col_spec(1), col_spec(2), col_spec(3),
                pl.BlockSpec((None, SUBLANES, aw), lambda b, t: (layer, 0, 0)),
                pl.BlockSpec((1, aw), lambda b, t: (0, 0)),
                pl.BlockSpec(sel.shape, lambda b, t: (0, 0))]
    args = [p, p, p, p, lbc_stack, gn.reshape(1, aw), sel]
    state_spec = pl.BlockSpec((1, heads, A_DK, A_DK), lambda b, t: (b, 0, 0, 0))
    if s0 is not None:
        in_specs.append(pl.BlockSpec((None, 1, heads, A_DK, A_DK), lambda b, t: (layer, b, 0, 0, 0)))
        args.append(s0)
    return pl.pallas_call(
        functools.partial(_hgrn_kernel, c=c, n_chunks=t_blk // c, heads=heads, has_init=s0 is not None),
        grid=(bsz, nt),
        in_specs=in_specs,
        out_specs=[pl.BlockSpec((t_blk, aw), lambda b, t: (b * nt + t, 0)), state_spec],
        out_shape=[jax.ShapeDtypeStruct((bsz * seq_len, aw), BF16),
                   jax.ShapeDtypeStruct((bsz, heads, A_DK, A_DK), F32)],
        scratch_shapes=[pltpu.VMEM((c + DIAG, aw), F32), pltpu.VMEM((c + DIAG, aw), F32)],
        compiler_params=_cparams(("arbitrary", "arbitrary")),
        name="hgrn2",
    )(*args)


def _ssd_kernel(*refs, t, bw, has_init):
    if has_init:
        (z_ref, xs_ref, bc_ref, dt_ref, cw_ref, cb_ref, dtb_ref, alog_ref, dsk_ref, gn_ref, exp_ref, sel_ref,
         s0_ref, c0_ref, yb_ref, s_out_ref, conv_out_ref, xpad, st) = refs
    else:
        (z_ref, xs_ref, bc_ref, dt_ref, cw_ref, cb_ref, dtb_ref, alog_ref, dsk_ref, gn_ref, exp_ref, sel_ref,
         yb_ref, s_out_ref, conv_out_ref, xpad, st) = refs
    ti = pl.program_id(1)
    gw = bw // B_GROUPS
    n_bc = B_GROUPS * B_DSTATE
    pad = SUBLANES

    @pl.when(ti == 0)
    def _():
        if has_init:
            xpad[0:pad, :] = c0_ref[0]
            st[...] = s0_ref[0].reshape(bw, B_DSTATE).T
        else:
            xpad[0:pad, :] = jnp.zeros((pad, xpad.shape[1]), F32)
            st[...] = jnp.zeros(st.shape, F32)

    xpad[pad:pad + t, 0:bw] = xs_ref[...]
    xpad[pad:pad + t, bw:bw + 2 * n_bc] = bc_ref[...]
    conv = cb_ref[...]
    for j in range(CONV_W):
        lo = pad - (CONV_W - 1) + j
        conv = conv + xpad[lo:lo + t, :] * cw_ref[j:j + 1, :]
    xbc = _silu(conv)
    x = xbc[:, 0:bw]

    dt = _softplus(dt_ref[...] + dtb_ref[...])
    a = dt * (-jnp.exp(alog_ref[...]))
    cs = _dot3(sel_ref[...], a)
    a_cum = cs[0:t]
    a_tot = cs[t:2 * t]
    ex = _dot3_rhs(jnp.concatenate([dt, a_cum, a_tot - a_cum], axis=0), exp_ref[...])
    dt_e = ex[0:t]
    acum_e = ex[t:2 * t]
    dec_e = ex[2 * t:3 * t]
    atot_e = acum_e[t - 1:t, :]

    xdt = x * dt_e
    xw = (xdt * jnp.exp(dec_e)).astype(BF16)
    xdt_b = xdt.astype(BF16)
    a_cum_t = a_cum.T
    rr = lax.broadcasted_iota(jnp.int32, (t, t), 0)
    cc = lax.broadcasted_iota(jnp.int32, (t, t), 1)
    causal = cc <= rr
    lane = lax.broadcasted_iota(jnp.int32, (t, LANES), 1)
    heads_per_group = gw // B_HEADDIM
    pairs_per_group = gw // LANES

    y_groups = []
    for g in range(B_GROUPS):
        bg = xbc[:, bw + g * B_DSTATE:bw + (g + 1) * B_DSTATE].astype(BF16)
        cg = xbc[:, bw + n_bc + g * B_DSTATE:bw + n_bc + (g + 1) * B_DSTATE].astype(BF16)
        gs = slice(g * gw, (g + 1) * gw)
        st_g = st[:, gs]
        scores = lax.dot_general(cg, bg, (((1,), (1,)), ((), ())), preferred_element_type=F32)
        y_off = jnp.dot(cg, st_g.astype(BF16), preferred_element_type=F32) * jnp.exp(acum_e[:, gs])
        st[:, gs] = jnp.exp(atot_e[:, gs]) * st_g + lax.dot_general(
            bg, xw[:, gs], (((0,), (0,)), ((), ())), preferred_element_type=F32)
        y_pairs = []
        for pr in range(pairs_per_group):
            cols = slice(g * gw + pr * LANES, g * gw + (pr + 1) * LANES)
            xp = xdt_b[:, cols]
            acc = None
            for half in range(LANES // B_HEADDIM):
                h = g * heads_per_group + pr * (LANES // B_HEADDIM) + half
                diff = a_cum[:, h:h + 1] - a_cum_t[h:h + 1, :]
                lmat = jnp.where(causal, jnp.exp(jnp.where(causal, diff, 0.0)), 0.0)
                m = (scores * lmat).astype(BF16)
                in_half = (lane // B_HEADDIM) == half
                term = jnp.dot(m, jnp.where(in_half, xp, jnp.zeros_like(xp)), preferred_element_type=F32)
                acc = term if acc is None else acc + term
            y_pairs.append(acc)
        y_groups.append(jnp.concatenate(y_pairs, axis=1) + y_off)
    y = jnp.concatenate(y_groups, axis=1) + dsk_ref[...] * x
    y = y * _silu(z_ref[...])
    outs = []
    for g in range(B_GROUPS):
        yg = y[:, g * gw:(g + 1) * gw]
        outs.append(yg * lax.rsqrt(jnp.mean(yg * yg, axis=-1, keepdims=True) + NORM_EPS))
    yb_ref[...] = (jnp.concatenate(outs, axis=1) * gn_ref[...]).astype(yb_ref.dtype)

    tail = xpad[t:t + pad, :]
    xpad[0:pad, :] = tail

    @pl.when(ti == pl.num_programs(1) - 1)
    def _():
        conv_out_ref[0] = tail
        s_out_ref[0] = st[...].T.reshape(s_out_ref.shape[1:])


def _ssd(p, cols, prm, s0, c0, bsz, seq_len):
    bw = prm["bw"]
    heads = bw // B_HEADDIM
    n_bc = B_GROUPS * B_DSTATE
    cdim = bw + 2 * n_bc
    t = min(seq_len, 128)
    nt = seq_len // t
    tri = np.arange(t)[None, :] <= np.arange(t)[:, None]
    sel = jnp.asarray(np.concatenate([tri, np.ones((t, t), bool)], 0).astype(np.float32), dtype=BF16)
    expand = np.zeros((LANES, bw), np.float32)
    expand[np.arange(bw) // B_HEADDIM, np.arange(bw)] = 1.0
    expand = jnp.asarray(expand, dtype=BF16)

    def blk(width, off):
        return pl.BlockSpec((t, width), lambda b, ti: (b * nt + ti, off // width))

    def full(shape):
        return pl.BlockSpec(shape, lambda b, ti: (0,) * len(shape))

    in_specs = [blk(bw, cols["z"]), blk(bw, cols["xs"]), blk(2 * n_bc, cols["bc"]), blk(LANES, cols["dt"]),
                full((CONV_W, cdim)), full((1, cdim)), full((1, LANES)), full((1, LANES)),
                full((1, bw)), full((1, bw)), full(expand.shape), full(sel.shape)]
    args = [p, p, p, p, prm["conv_w"], prm["conv_b"], prm["dt_bias"], prm["a_log"], prm["d_skip"], prm["gn"],
            expand, sel]
    state_spec = pl.BlockSpec((1, heads, B_HEADDIM, B_DSTATE), lambda b, ti: (b, 0, 0, 0))
    conv_spec = pl.BlockSpec((1, SUBLANES, cdim), lambda b, ti: (b, 0, 0))
    if s0 is not None:
        layer = prm["layer"]
        in_specs += [pl.BlockSpec((None, 1, heads, B_HEADDIM, B_DSTATE), lambda b, ti: (layer, b, 0, 0, 0)),
                     pl.BlockSpec((None, 1, SUBLANES, cdim), lambda b, ti: (layer, b, 0, 0))]
        args += [s0, c0]
    return pl.pallas_call(
        functools.partial(_ssd_kernel, t=t, bw=bw, has_init=s0 is not None),
        grid=(bsz, nt),
        in_specs=in_specs,
        out_specs=[pl.BlockSpec((t, bw), lambda b, ti: (b * nt + ti, 0)), state_spec, conv_spec],
        out_shape=[jax.ShapeDtypeStruct((bsz * seq_len, bw), BF16),
                   jax.ShapeDtypeStruct((bsz, heads, B_HEADDIM, B_DSTATE), F32),
                   jax.ShapeDtypeStruct((bsz, SUBLANES, cdim), F32)],
        scratch_shapes=[pltpu.VMEM((t + SUBLANES, cdim), F32), pltpu.VMEM((B_DSTATE, bw), F32)],
        compiler_params=_cparams(("arbitrary", "arbitrary")),
        name="ssd",
    )(*args)


def _cmlp_kernel(u_ref, v_ref, lng_ref, lnb_ref, ws_ref, bst_ref, *out_refs, t, n_chunks, keep_v):
    yc_ref = out_refs[0]
    cw = u_ref.shape[1] // C_GROUPS
    rr = lax.broadcasted_iota(jnp.int32, (t, t), 0)
    cc = lax.broadcasted_iota(jnp.int32, (t, t), 1)
    wts = [jnp.where(cc <= rr, ws_ref[g, 0:t, 0:t], 0.0).astype(BF16) for g in range(C_GROUPS)]
    for ci in range(n_chunks):
        rows = slice(ci * t, (ci + 1) * t)
        u = _gelu(u_ref[rows, :])
        gv = _gelu(v_ref[rows, :])
        mu = jnp.mean(gv, axis=-1, keepdims=True)
        dv = gv - mu
        var = jnp.mean(dv * dv, axis=-1, keepdims=True)
        v = dv * lax.rsqrt(var + NORM_EPS) * lng_ref[...] + lnb_ref[...]
        if keep_v:
            out_refs[1][rows, :] = v
        vb = v.astype(BF16)
        for g in range(C_GROUPS):
            lanes = slice(g * cw, (g + 1) * cw)
            mixed = jnp.dot(wts[g], vb[:, lanes], preferred_element_type=F32) + bst_ref[0:t, g:g + 1]
            yc_ref[rows, lanes] = (u[:, lanes] * mixed).astype(yc_ref.dtype)


def _cmlp(p, col_u, col_v, prm, bsz, seq_len, keep_v):
    cw = prm["cw"]
    t = min(seq_len, CMLP_CHUNK)
    t_blk = min(seq_len, 4 * CMLP_CHUNK)
    n = bsz * seq_len

    def full(shape):
        return pl.BlockSpec(shape, lambda i: (0,) * len(shape))

    out_specs = [pl.BlockSpec((t_blk, cw), lambda i: (i, 0))]
    out_shape = [jax.ShapeDtypeStruct((n, cw), BF16)]
    if keep_v:
        out_specs.append(pl.BlockSpec((t_blk, cw), lambda i: (i, 0)))
        out_shape.append(jax.ShapeDtypeStruct((n, cw), F32))
    return pl.pallas_call(
        functools.partial(_cmlp_kernel, t=t, n_chunks=t_blk // t, keep_v=keep_v),
        grid=(n // t_blk,),
        in_specs=[pl.BlockSpec((t_blk, cw), lambda i: (i, col_u // cw)),
                  pl.BlockSpec((t_blk, cw), lambda i: (i, col_v // cw)),
                  full((1, cw)), full((1, cw)),
                  full((C_GROUPS, CMLP_CHUNK, CMLP_CHUNK)), full((CMLP_CHUNK, C_GROUPS))],
        out_specs=out_specs,
        out_shape=out_shape,
        compiler_params=_cparams(("arbitrary",)),
        name="cmlp",
    )(p, p, prm["ln_g"], prm["ln_b"], prm["ws"], prm["bs_t"])


def _run_trunk(x3, mod, st_hgrn, st_ssm, st_conv, keep_v, w):
    bsz, seq_len, d = x3.shape
    depth = mod.shape[0]
    aw = bw = cw = d // 2
    heads_a = aw // A_DK
    x = x3.reshape(bsz * seq_len, d)
    col = w["cols"]
    hgrn_out, ssm_out, conv_out, v_out = [], [], [], []
    for l in range(depth):
        mod6 = mod[l].reshape(bsz, 6, 1, d).transpose(1, 0, 2, 3)
        p = _in_proj(x, w["norm1_g"][l], mod6, w["w_in"], l, seq_len)

        y_a, s_h = _hgrn(p, w["lbc"], l, w["hgrn_onorm_g"][l], st_hgrn, bsz, seq_len, heads_a)
        ssd_prm = dict(bw=bw, layer=l, conv_w=w["ssm_conv_w"][l], conv_b=w["ssm_conv_b"][l][None],
                       dt_bias=w["dt_bias_pad"][l][None], a_log=w["a_log_pad"][l][None],
                       d_skip=w["d_skip"][l][None], gn=w["ssm_onorm_g"][l][None])
        y_b, s_s, conv_tail = _ssd(p, col, ssd_prm, st_ssm, st_conv, bsz, seq_len)
        cm_prm = dict(cw=cw, ln_g=w["cmlp_ln_g"][l][None], ln_b=w["cmlp_ln_b"][l][None],
                      ws=w["cmlp_ws"][l], bs_t=w["cmlp_bs"][l].T)
        c_res = _cmlp(p, col["u"], col["v"], cm_prm, bsz, seq_len, keep_v)
        merged = _merge(y_a, y_b, c_res[0], w["w_branch"], l, p, col["gate"], d)
        x = _proj_residual(merged, w["w_out"], l, x, mod6, 2, seq_len, "out_proj")
        hid = _mlp_up(x, w["norm2_g"][l], mod6, w["w_up"], l, seq_len)
        x = _proj_residual(hid, w["w_down"], l, x, mod6, 5, seq_len, "mlp_down")

        hgrn_out.append(s_h)
        ssm_out.append(s_s)
        conv_out.append(conv_tail[:, SUBLANES - (CONV_W - 1):, :])
        if keep_v:
            v_out.append(c_res[1].reshape(bsz, seq_len, cw))
    y = _final_norm(x, w["final_g"]).reshape(bsz, seq_len, d)
    return (y, jnp.stack(hgrn_out), jnp.stack(ssm_out), jnp.stack(conv_out),
            jnp.stack(v_out) if keep_v else None)


def _run_trunk_halves(x3, mod, w):
    bsz, seq_len, d = x3.shape
    depth = mod.shape[0]
    hb = bsz // 2
    aw = bw = cw = d // 2
    heads_a = aw // A_DK
    col = w["cols"]
    xs = [x3[:hb].reshape(hb * seq_len, d), x3[hb:].reshape(hb * seq_len, d)]
    mods = [mod[:, :hb], mod[:, hb:]]
    hgrn_out, ssm_out, conv_out = [], [], []
    for l in range(depth):
        mod6 = [m[l].reshape(hb, 6, 1, d).transpose(1, 0, 2, 3) for m in mods]
        ssd_prm = dict(bw=bw, layer=l, conv_w=w["ssm_conv_w"][l], conv_b=w["ssm_conv_b"][l][None],
                       dt_bias=w["dt_bias_pad"][l][None], a_log=w["a_log_pad"][l][None],
                       d_skip=w["d_skip"][l][None], gn=w["ssm_onorm_g"][l][None])
        cm_prm = dict(cw=cw, ln_g=w["cmlp_ln_g"][l][None], ln_b=w["cmlp_ln_b"][l][None],
                      ws=w["cmlp_ws"][l], bs_t=w["cmlp_bs"][l].T)
        gn_a = w["hgrn_onorm_g"][l]
        proj = [_in_proj(xs[h], w["norm1_g"][l], mod6[h], w["w_in"], l, seq_len) for h in (0, 1)]

        def mix_merge(h, y_a):
            p = proj[h]
            y_b, s_s, conv_tail = _ssd(p, col, ssd_prm, None, None, hb, seq_len)
            y_c = _cmlp(p, col["u"], col["v"], cm_prm, hb, seq_len, False)[0]
            merged = _merge(y_a, y_b, y_c, w["w_branch"], l, p, col["gate"], d)
            return _proj_residual(merged, w["w_out"], l, xs[h], mod6[h], 2, seq_len, "out_proj"), s_s, conv_tail

        ya0, sh0 = _hgrn(proj[0], w["lbc"], l, gn_a, None, hb, seq_len, heads_a)
        x0, ss0, ct0 = mix_merge(0, ya0)
        hid0, ya1, sh1 = _mlp_up_hgrn(x0, w["norm2_g"][l], mod6[0], w["w_up"], l, seq_len,
                                      proj[1], w["lbc"], gn_a, heads_a)
        x0 = _proj_residual(hid0, w["w_down"], l, x0, mod6[0], 5, seq_len, "mlp_down")
        x1, ss1, ct1 = mix_merge(1, ya1)
        hid1 = _mlp_up(x1, w["norm2_g"][l], mod6[1], w["w_up"], l, seq_len)
        x1 = _proj_residual(hid1, w["w_down"], l, x1, mod6[1], 5, seq_len, "mlp_down")
        xs = [x0, x1]

        hgrn_out.append(jnp.concatenate([sh0, sh1], axis=0))
        ssm_out.append(jnp.concatenate([ss0, ss1], axis=0))
        conv_out.append(jnp.concatenate([ct0, ct1], axis=0)[:, SUBLANES - (CONV_W - 1):, :])
    y = jnp.concatenate([_final_norm(x, w["final_g"]) for x in xs], axis=0).reshape(bsz, seq_len, d)
    return (y, jnp.stack(hgrn_out), jnp.stack(ssm_out), jnp.stack(conv_out), None)


def kernel(x_prompt, x_sample, state_hgrn, state_ssm, state_conv, c_prompt, c_sample, norm1_g, norm2_g,
           w_mod, b_mod, w_in, hgrn_lb, hgrn_onorm_g, ssm_conv_w, ssm_conv_b, ssm_dt_bias, ssm_a_log, ssm_d,
           ssm_onorm_g, cmlp_ln_g, cmlp_ln_b, cmlp_ws, cmlp_bs, w_branch, w_out, w_up, w_down, final_g):
    d = x_prompt.shape[-1]
    depth = w_in.shape[0]
    aw = bw = cw = d // 2
    n_bc = B_GROUPS * B_DSTATE
    heads_b = bw // B_HEADDIM
    assert heads_b <= LANES and bw % LANES == 0 and DT_PAD % LANES == 0

    o_dt = 4 * aw + bw + bw + 2 * n_bc
    o_u = o_dt + heads_b
    w_in_r = jnp.concatenate(
        [w_in[:, :, :o_u].astype(BF16), jnp.zeros((depth, d, DT_PAD - heads_b), BF16),
         w_in[:, :, o_u:].astype(BF16)], axis=2)
    cols = dict(z=4 * aw, xs=4 * aw + bw, bc=4 * aw + 2 * bw, dt=o_dt)
    cols["u"] = o_dt + DT_PAD
    cols["v"] = cols["u"] + cw
    cols["gate"] = cols["v"] + cw
    assert cols["gate"] % MERGE_TILE == 0 and w_in_r.shape[2] % IN_TILE == 0

    pad_h = lambda a: jnp.pad(a.astype(F32), ((0, 0), (0, LANES - heads_b)))
    w = dict(
        cols=cols, w_in=w_in_r,
        norm1_g=norm1_g, norm2_g=norm2_g, final_g=final_g,
        lbc=_lb_consts(hgrn_lb), hgrn_onorm_g=hgrn_onorm_g,
        ssm_conv_w=ssm_conv_w, ssm_conv_b=ssm_conv_b,
        dt_bias_pad=pad_h(ssm_dt_bias), a_log_pad=pad_h(ssm_a_log),
        d_skip=jnp.repeat(ssm_d.astype(F32), B_HEADDIM, axis=1), ssm_onorm_g=ssm_onorm_g,
        cmlp_ln_g=cmlp_ln_g, cmlp_ln_b=cmlp_ln_b, cmlp_ws=cmlp_ws, cmlp_bs=cmlp_bs,
        w_branch=w_branch.astype(BF16).reshape(depth, N_BRANCH, aw, d), w_out=w_out.astype(BF16),
        w_up=w_up.astype(BF16), w_down=w_down.astype(BF16),
    )

    nb = x_prompt.shape[0]
    mod = _modulation(jnp.concatenate([c_prompt, c_sample], axis=0), w_mod, b_mod)
    if nb % 2 == 0:
        y_p, hgrn_p, ssm_p, conv_p, _ = _run_trunk_halves(x_prompt, mod[:, :nb], w)
    else:
        y_p, hgrn_p, ssm_p, conv_p, _ = _run_trunk(x_prompt, mod[:, :nb], None, None, None, False, w)
    conv_pad = jnp.pad(state_conv, ((0, 0), (0, 0), (SUBLANES - (CONV_W - 1), 0), (0, 0)))
    y_s, hgrn_s, ssm_s, conv_s, v_s = _run_trunk(x_sample, mod[:, nb:], state_hgrn, state_ssm, conv_pad, True, w)
    return (y_p, y_s, hgrn_p, ssm_p, conv_p, hgrn_s, ssm_s, conv_s, v_s)
```

```python
import functools
import math

import numpy as np
import jax
import jax.numpy as jnp
from jax import lax
from jax.experimental import pallas as pl
from jax.experimental.pallas import tpu as pltpu

F32 = jnp.float32
BF16 = jnp.bfloat16

A_DK = 128
B_HEADDIM = 64
B_GROUPS = 2
B_DSTATE = 128
CONV_W = 4
C_GROUPS = 4
CMLP_CHUNK = 128
N_BRANCH = 3
SCAN_CHUNK = 64
NORM_EPS = 1e-6
LB_FLOOR = 1e-30
LOG2E = 1.4426950408889634

LANES = 128
SUBLANES = 8
VMEM_LIMIT = 56 * 1024 * 1024
DIAG = SUBLANES
DT_PAD = 512
ROW_TILE = 1024
COL_TILE = 1024
IN_TILE = 1024
MERGE_TILE = 512
DOWN_TILE = 256
EPILOGUE_COLS = 256
DOWN_K_SPLIT = 4
NORM_ROWS = 64


def _cparams(sem):
    return pltpu.CompilerParams(dimension_semantics=sem, vmem_limit_bytes=VMEM_LIMIT)


def _split3(x):
    hi = x.astype(BF16)
    r1 = x - hi.astype(F32)
    mid = r1.astype(BF16)
    lo = (r1 - mid.astype(F32)).astype(BF16)
    return hi, mid, lo


def _dot3(sel, x):
    hi, mid, lo = _split3(x)
    d = lambda p: jnp.dot(sel, p, preferred_element_type=F32)
    return d(hi) + d(mid) + d(lo)


def _dot3_rhs(x, sel):
    hi, mid, lo = _split3(x)
    d = lambda p: jnp.dot(p, sel, preferred_element_type=F32)
    return d(hi) + d(mid) + d(lo)


def _silu(x):
    return x / (1.0 + jnp.exp(-x))


def _softplus(x):
    return jnp.maximum(x, 0.0) + jnp.log1p(jnp.exp(-jnp.abs(x)))


def _gelu(x):
    return 0.5 * x * (1.0 + lax.erf(x * (1.0 / math.sqrt(2.0))))


def _lb_kernel(lb_ref, out_ref, *, depth):
    x = lb_ref[...]
    m = jnp.max(x, axis=0, keepdims=True)
    e = jnp.exp(x - m)
    p = e / jnp.sum(e, axis=0, keepdims=True)
    acc = jnp.zeros_like(p[0:1])
    zeros5 = jnp.zeros((SUBLANES - 3, x.shape[1]), F32)
    for l in range(depth):
        acc = acc + p[l:l + 1]
        lb = acc - p[0:1]
        out_ref[l] = jnp.concatenate(
            [jnp.log(jnp.maximum(lb, LB_FLOOR)), jnp.log1p(-lb), 1.0 - lb, zeros5], axis=0)


def _lb_consts(hgrn_lb):
    depth, aw = hgrn_lb.shape
    return pl.pallas_call(
        functools.partial(_lb_kernel, depth=depth),
        out_shape=jax.ShapeDtypeStruct((depth, SUBLANES, aw), F32),
        name="hgrn_lb",
    )(hgrn_lb.astype(F32))


def _mod_kernel(c_ref, w_ref, b_ref, o_ref):
    cs = _silu(c_ref[...]).astype(BF16)
    o_ref[...] = jnp.dot(cs, w_ref[...].astype(BF16), preferred_element_type=F32) + b_ref[...]


def _modulation(c_all, w_mod, b_mod):
    depth, d, n6 = w_mod.shape
    s = c_all.shape[0]
    tn = 1024
    return pl.pallas_call(
        _mod_kernel,
        grid=(depth, n6 // tn),
        in_specs=[
            pl.BlockSpec((s, d), lambda l, j: (0, 0)),
            pl.BlockSpec((None, d, tn), lambda l, j: (l, 0, j)),
            pl.BlockSpec((None, 1, tn), lambda l, j: (l, 0, j)),
        ],
        out_specs=pl.BlockSpec((None, s, tn), lambda l, j: (l, 0, j)),
        out_shape=jax.ShapeDtypeStruct((depth, s, n6), F32),
        compiler_params=_cparams(("arbitrary", "arbitrary")),
        name="adaln_mod",
    )(c_all, w_mod, b_mod.reshape(depth, 1, n6))


def _row_tiling(n_tok, seq_len, tm_max):
    tm = min(tm_max, n_tok)
    if seq_len >= tm:
        assert seq_len % tm == 0
        return tm, 1, seq_len // tm
    assert tm % seq_len == 0
    return tm, tm // seq_len, 1


def _per_seq(x, nseg):
    tm, d = x.shape
    return x.reshape(nseg, tm // nseg, d)


def _final_norm_kernel(x_ref, g_ref, o_ref):
    x = x_ref[...]
    o_ref[...] = x * lax.rsqrt(jnp.mean(x * x, axis=-1, keepdims=True) + NORM_EPS) * g_ref[...]


def _final_norm(x, g):
    n, d = x.shape
    tm = min(512, n)
    return pl.pallas_call(
        _final_norm_kernel,
        grid=(n // tm,),
        in_specs=[pl.BlockSpec((tm, d), lambda i: (i, 0)), pl.BlockSpec((1, d), lambda i: (0, 0))],
        out_specs=pl.BlockSpec((tm, d), lambda i: (i, 0)),
        out_shape=jax.ShapeDtypeStruct((n, d), F32),
        compiler_params=_cparams(("arbitrary",)),
        name="final_norm",
    )(x, g.reshape(1, d))


def _final_norm_pair_kernel(xa_ref, xb_ref, g_ref, o_ref, *, nt):
    def norm(x):
        return x * lax.rsqrt(jnp.mean(x * x, axis=-1, keepdims=True) + NORM_EPS) * g_ref[...]

    @pl.when(pl.program_id(0) < nt)
    def _():
        o_ref[...] = norm(xa_ref[...])

    @pl.when(pl.program_id(0) >= nt)
    def _():
        o_ref[...] = norm(xb_ref[...])


def _final_norm_pair(xa, xb, g):
    n, d = xa.shape
    tm = min(512, n)
    nt = n // tm
    return pl.pallas_call(
        functools.partial(_final_norm_pair_kernel, nt=nt),
        grid=(2 * nt,),
        in_specs=[pl.BlockSpec((tm, d), lambda i: (jnp.minimum(i, nt - 1), 0)),
                  pl.BlockSpec((tm, d), lambda i: (jnp.maximum(i - nt, 0), 0)),
                  pl.BlockSpec((1, d), lambda i: (0, 0))],
        out_specs=pl.BlockSpec((tm, d), lambda i: (i, 0)),
        out_shape=jax.ShapeDtypeStruct((2 * n, d), F32),
        compiler_params=_cparams(("arbitrary",)),
        name="final_norm",
    )(xa, xb, g.reshape(1, d))


def _norm_mod_rows(x_ref, g_ref, sc_ref, sh_ref, h_scr, nseg):
    tm = x_ref.shape[0]
    seg = tm // nseg
    rb = min(seg, NORM_ROWS)
    g = g_ref[...]

    def body(r, carry):
        rows = pl.ds(pl.multiple_of(r * rb, rb), rb)
        s = r // (seg // rb)
        x = x_ref[rows, :]
        y = x * lax.rsqrt(jnp.mean(x * x, axis=-1, keepdims=True) + NORM_EPS) * g
        h_scr[rows, :] = (y * (1.0 + sc_ref[s]) + sh_ref[s]).astype(BF16)
        return carry

    lax.fori_loop(0, tm // rb, body, 0)


def _mlp_up_kernel(x_ref, g_ref, sc_ref, sh_ref, w_ref, o_ref, h_scr, *, nseg):
    @pl.when(pl.program_id(1) == 0)
    def _():
        _norm_mod_rows(x_ref, g_ref, sc_ref, sh_ref, h_scr, nseg)

    y = jnp.maximum(jnp.dot(h_scr[...], w_ref[...], preferred_element_type=F32), 0.0)
    o_ref[...] = (y * y).astype(o_ref.dtype)


def _in_proj_kernel(x_ref, g_ref, sc_ref, sh_ref, wh_ref, wt_ref, p_ref, h_scr, *, nseg, n_head):
    j = pl.program_id(1)

    @pl.when(j == 0)
    def _():
        _norm_mod_rows(x_ref, g_ref, sc_ref, sh_ref, h_scr, nseg)

    @pl.when(j < n_head)
    def _():
        p_ref[...] = jnp.dot(h_scr[...], wh_ref[...], preferred_element_type=F32)

    @pl.when(j >= n_head)
    def _():
        p_ref[...] = jnp.dot(h_scr[...], wt_ref[...], preferred_element_type=F32)


def _norm_proj_call(kern, x, g, mod6, k_scale, k_shift, w_stack, layer, seq_len, tn, out_specs, out_shape, name):
    n, d = x.shape
    m = w_stack.shape[2]
    tm, nseg, tps = _row_tiling(n, seq_len, ROW_TILE)

    def mod_spec(k):
        return pl.BlockSpec((None, nseg, 1, d), lambda i, j: (k, i // tps, 0, 0))

    return pl.pallas_call(
        functools.partial(kern, nseg=nseg),
        grid=(n // tm, m // tn),
        in_specs=[
            pl.BlockSpec((tm, d), lambda i, j: (i, 0)),
            pl.BlockSpec((1, d), lambda i, j: (0, 0)),
            mod_spec(k_scale),
            mod_spec(k_shift),
            pl.BlockSpec((None, d, tn), lambda i, j: (layer, 0, j)),
        ],
        out_specs=out_specs(tm),
        out_shape=out_shape,
        scratch_shapes=[pltpu.VMEM((tm, d), BF16)],
        compiler_params=_cparams(("arbitrary", "arbitrary")),
        name=name,
    )(x, g.reshape(1, d), mod6, mod6, w_stack)


def _mlp_up(x, g, mod6, w_stack, layer, seq_len):
    n = x.shape[0]
    m = w_stack.shape[2]
    tn = COL_TILE
    return _norm_proj_call(
        _mlp_up_kernel, x, g, mod6, 4, 3, w_stack, layer, seq_len, tn,
        lambda tm: pl.BlockSpec((tm, tn), lambda i, j: (i, j)),
        jax.ShapeDtypeStruct((n, m), BF16), "mlp_up")


def _in_proj(x, g, mod6, w_pair, layer, seq_len, rows=None):
    d = x.shape[1]
    row0, n = rows if rows is not None else (0, x.shape[0])
    w_head, w_tail = w_pair
    tn = IN_TILE
    n_head, n_tail = w_head.shape[2] // tn, w_tail.shape[2] // tn
    tm, nseg, tps = _row_tiling(n, seq_len, ROW_TILE)

    def mod_spec(k):
        return pl.BlockSpec((None, nseg, 1, d), lambda i, j: (k, i // tps, 0, 0))

    return pl.pallas_call(
        functools.partial(_in_proj_kernel, nseg=nseg, n_head=n_head),
        grid=(n // tm, n_head + n_tail),
        in_specs=[
            pl.BlockSpec((tm, d), lambda i, j: (i + row0 // tm, 0)),
            pl.BlockSpec((1, d), lambda i, j: (0, 0)),
            mod_spec(1),
            mod_spec(0),
            pl.BlockSpec((None, d, tn), lambda i, j: (layer, 0, jnp.minimum(j, n_head - 1))),
            pl.BlockSpec((None, d, tn), lambda i, j: (layer, 0, jnp.maximum(j - n_head, 0))),
        ],
        out_specs=pl.BlockSpec((tm, tn), lambda i, j: (i, j)),
        out_shape=jax.ShapeDtypeStruct((n, (n_head + n_tail) * tn), F32),
        scratch_shapes=[pltpu.VMEM((tm, d), BF16)],
        compiler_params=_cparams(("arbitrary", "arbitrary")),
        name="in_proj",
    )(x, g.reshape(1, d), mod6, mod6, w_head, w_tail)


def _proj_res_kernel(a_ref, w_ref, x_ref, gate_ref, o_ref, *, nseg, nk):
    part = jnp.dot(a_ref[...], w_ref[...], preferred_element_type=F32)

    def finish(acc):
        x = x_ref[...]
        o_ref[...] = (_per_seq(x, nseg) + gate_ref[...] * _per_seq(acc, nseg)).reshape(x.shape)

    if nk == 1:
        finish(part)
        return
    k = pl.program_id(2)

    @pl.when(k == 0)
    def _():
        o_ref[...] = part

    @pl.when((k > 0) & (k < nk - 1))
    def _():
        o_ref[...] += part

    @pl.when(k == nk - 1)
    def _():
        finish(o_ref[...] + part)


def _proj_residual(a, w_stack, layer, x, mod6, k_gate, seq_len, name, x_row0=0):
    n, kdim = a.shape
    d = w_stack.shape[2]
    tn = COL_TILE if kdim <= d else DOWN_TILE
    tk = kdim
    nk = kdim // tk
    tm, nseg, tps = _row_tiling(n, seq_len, ROW_TILE)
    return pl.pallas_call(
        functools.partial(_proj_res_kernel, nseg=nseg, nk=nk),
        grid=(n // tm, d // tn, nk),
        in_specs=[
            pl.BlockSpec((tm, tk), lambda i, j, k: (i, k)),
            pl.BlockSpec((None, tk, tn), lambda i, j, k: (layer, k, j)),
            pl.BlockSpec((tm, tn), lambda i, j, k: (i + x_row0 // tm, j)),
            pl.BlockSpec((None, nseg, 1, tn), lambda i, j, k: (k_gate, i // tps, 0, j)),
        ],
        out_specs=pl.BlockSpec((tm, tn), lambda i, j, k: (i, j)),
        out_shape=jax.ShapeDtypeStruct((n, d), F32),
        compiler_params=_cparams(("arbitrary", "arbitrary", "arbitrary")),
        name=name,
    )(a, w_stack, x, mod6)


def _merge_kernel(ya_ref, yb_ref, yc_ref, wa_ref, wb_ref, wc_ref, ga_ref, gb_ref, gc_ref, o_ref):
    acc = None
    for y_ref, w_ref, g_ref in ((ya_ref, wa_ref, ga_ref), (yb_ref, wb_ref, gb_ref), (yc_ref, wc_ref, gc_ref)):
        gate = 1.0 / (1.0 + jnp.exp(-g_ref[...]))
        term = gate * jnp.dot(y_ref[...], w_ref[...], preferred_element_type=F32)
        acc = term if acc is None else acc + term
    o_ref[...] = acc.astype(o_ref.dtype)


def _merge(ya, yb, yc, w_stack, layer, p, gate_col0, d):
    n, bw = ya.shape
    tn = MERGE_TILE
    tm = min(ROW_TILE, n)
    gsteps = d // tn
    g0 = gate_col0 // tn
    y_spec = pl.BlockSpec((tm, bw), lambda i, j: (i, 0))

    def w_spec(k):
        return pl.BlockSpec((None, None, bw, tn), lambda i, j: (layer, k, 0, j))

    def g_spec(k):
        return pl.BlockSpec((tm, tn), lambda i, j: (i, g0 + k * gsteps + j))

    return pl.pallas_call(
        _merge_kernel,
        grid=(n // tm, d // tn),
        in_specs=[y_spec, y_spec, y_spec, w_spec(0), w_spec(1), w_spec(2), g_spec(0), g_spec(1), g_spec(2)],
        out_specs=pl.BlockSpec((tm, tn), lambda i, j: (i, j)),
        out_shape=jax.ShapeDtypeStruct((n, d), BF16),
        compiler_params=_cparams(("arbitrary", "arbitrary")),
        name="merge",
    )(ya, yb, yc, w_stack, w_stack, w_stack, p, p, p)


def _hgrn_levels(c):
    lv, m = [], c // 2
    while m >= DIAG:
        lv.append(m)
        m //= 2
    return tuple(lv)


def _hgrn_select(c):
    tri = np.arange(c)[None, :] <= np.arange(c)[:, None]
    return jnp.asarray(tri.astype(np.float32), dtype=BF16)


def _hgrn_kernel(*refs, c, n_chunks, heads, has_init):
    if has_init:
        q_ref, f_ref, i_ref, g_ref, lbc_ref, gn_ref, sel_ref, s0_ref, ya_ref, s_ref, bpad, kpad = refs
    else:
        q_ref, f_ref, i_ref, g_ref, lbc_ref, gn_ref, sel_ref, ya_ref, s_ref, bpad, kpad = refs

    @pl.when(pl.program_id(1) == 0)
    def _():
        s_ref[...] = s0_ref[...] if has_init else jnp.zeros(s_ref.shape, F32)

    _hgrn_chunks(q_ref, f_ref, i_ref, g_ref, lbc_ref, gn_ref, sel_ref, ya_ref, s_ref, bpad, kpad,
                 c=c, n_chunks=n_chunks, heads=heads, unroll=False)


def _hgrn_chunks(q_ref, f_ref, i_ref, g_ref, lbc_ref, gn_ref, sel_ref, ya_ref, s_ref, bpad, kpad,
                 *, c, n_chunks, heads, unroll, fillers=()):
    levels = _hgrn_levels(c)
    nl = len(levels)
    width = heads * A_DK
    head_lanes = [slice(hh * A_DK, (hh + 1) * A_DK) for hh in range(heads)]

    bpad[0:DIAG, :] = jnp.zeros((DIAG, width), F32)
    kpad[0:DIAG, :] = jnp.zeros((DIAG, width), F32)

    sel = sel_ref[...]
    log_lb = lbc_ref[0:1, :]
    log1m_lb = lbc_ref[1:2, :]
    one_m_lb = lbc_ref[2:3, :]
    row = lax.broadcasted_iota(jnp.int32, (c, width), 0)
    rr = lax.broadcasted_iota(jnp.int32, (c, c), 0)
    cc = lax.broadcasted_iota(jnp.int32, (c, c), 1)
    upper = [(row & m) != 0 for m in levels]
    same = [(rr ^ cc) < 2 * m for m in levels]
    dmat = jnp.where(((rr ^ cc) < DIAG) & (cc <= rr), rr - cc, -1)
    nt_dims = (((1,), (1,)), ((), ()))
    tn_dims = (((0,), (0,)), ((), ()))

    def chunk(ci):
        rows = slice(ci * c, (ci + 1) * c) if unroll else pl.ds(pl.multiple_of(ci * c, c), c)
        z = f_ref[rows, :]
        aq = q_ref[rows, :]

        log_sig = jnp.minimum(z, 0.0) - jnp.log(1.0 + jnp.exp(-jnp.abs(z)))
        bb = log1m_lb + log_sig
        log_f = jnp.maximum(log_lb, bb) + jnp.log(1.0 + jnp.exp(-jnp.abs(log_lb - bb)))
        k = one_m_lb / (1.0 + jnp.exp(z))
        q = _silu(aq)

        b = _dot3(sel, log_f * LOG2E)
        bpad[DIAG:DIAG + c, :] = b
        kpad[DIAG:DIAG + c, :] = k
        b_tot = bpad[DIAG + c - 1:DIAG + c, :]
        vb = i_ref[rows, :].astype(BF16)
        q_in = (q * jnp.exp2(b)).astype(BF16)
        k_out = (k * jnp.exp2(b_tot - b)).astype(BF16)
        yield

        att = [jnp.zeros((c, c), F32) for _ in range(heads)]
        for li in range(nl):
            m = levels[li]
            ref_b = jnp.concatenate(
                [jnp.broadcast_to(bpad[DIAG + g0 + m - 1:DIAG + g0 + m, :], (2 * m, width))
                 for g0 in range(0, c, 2 * m)], axis=0)
            e = jnp.exp2(-jnp.abs(b - ref_b))
            qs = jnp.where(upper[li], q * e, 0.0).astype(BF16)
            ks = jnp.where(upper[li], 0.0, k * e).astype(BF16)
            for hh, lanes in enumerate(head_lanes):
                a_l = lax.dot_general(qs[:, lanes], ks[:, lanes], nt_dims, preferred_element_type=F32)
                att[hh] = att[hh] + jnp.where(same[li], a_l, 0.0)
        yield

        for dlt in range(DIAG):
            if dlt == 0:
                w = q * k
            else:
                ksh = kpad[DIAG - dlt:DIAG - dlt + c, :]
                bsh = bpad[DIAG - dlt:DIAG - dlt + c, :]
                w = q * ksh * jnp.exp2(b - bsh)
            for hh, lanes in enumerate(head_lanes):
                col = jnp.sum(w[:, lanes], axis=-1, keepdims=True)
                att[hh] = jnp.where(dmat == dlt, col, att[hh])
            if dlt in (DIAG // 2 - 1, DIAG - 1):
                yield

        decay_row = jnp.exp2(b_tot)
        outs = []
        for hh, lanes in enumerate(head_lanes):
            s_prev = s_ref[0, hh]
            o = jnp.dot(q_in[:, lanes], s_prev.astype(BF16), preferred_element_type=F32)
            o = o + jnp.dot(att[hh].astype(BF16), vb[:, lanes], preferred_element_type=F32)
            decay_col = jnp.broadcast_to(decay_row[:, lanes], (A_DK, A_DK)).T
            s_ref[0, hh] = decay_col * s_prev + lax.dot_general(
                k_out[:, lanes], vb[:, lanes], tn_dims, preferred_element_type=F32)
            outs.append(o * lax.rsqrt(jnp.mean(o * o, axis=-1, keepdims=True) + NORM_EPS))
        y = jnp.concatenate(outs, axis=1) * gn_ref[...] * _silu(g_ref[rows, :])
        ya_ref[rows, :] = y.astype(ya_ref.dtype)
        yield

    if not unroll:
        def body(ci, carry):
            for _ in chunk(ci):
                pass
            return carry

        lax.fori_loop(0, n_chunks, body, 0)
        return

    phases = [ph for ci in range(n_chunks) for ph in _phases_of(chunk(ci))]
    per_filler = -(-len(phases) // max(len(fillers), 1))
    pos = 0
    for fill in fillers:
        fill()
        for ph in phases[pos:pos + per_filler]:
            ph()
        pos += per_filler
    for ph in phases[pos:]:
        ph()


HGRN_PHASES = 5


def _phases_of(gen):
    return [functools.partial(next, gen, None) for _ in range(HGRN_PHASES)]


def _hgrn_specs(p, lbc_stack, layer, gn, heads, t_blk, row_block):
    aw = heads * A_DK
    c = min(t_blk, SCAN_CHUNK)
    sel = _hgrn_select(c)

    def col_spec(seg):
        return pl.BlockSpec((t_blk, aw), lambda *g: (row_block(*g), seg))

    in_specs = [col_spec(0), col_spec(1), col_spec(2), col_spec(3),
                pl.BlockSpec((None, SUBLANES, aw), lambda *g: (layer, 0, 0)),
                pl.BlockSpec((1, aw), lambda *g: (0, 0)),
                pl.BlockSpec(sel.shape, lambda *g: (0, 0))]
    args = [p, p, p, p, lbc_stack, gn.reshape(1, aw), sel]
    scratch = [pltpu.VMEM((c + DIAG, aw), F32), pltpu.VMEM((c + DIAG, aw), F32)]
    return c, in_specs, args, pl.BlockSpec((t_blk, aw), lambda *g: (row_block(*g), 0)), scratch


def _mlp_up_hgrn_kernel(x_ref, g_ref, sc_ref, sh_ref, w_ref, q_ref, f_ref, i_ref, gg_ref, lbc_ref, gn_ref, sel_ref,
                        o_ref, ya_ref, s_ref, h_scr, bpad, kpad, *, nseg, c, n_chunks, heads, steps_per_seq):
    j = pl.program_id(1)
    step = pl.program_id(0) * pl.num_programs(1) + j

    @pl.when(j == 0)
    def _():
        _norm_mod_rows(x_ref, g_ref, sc_ref, sh_ref, h_scr, nseg)

    @pl.when(step % steps_per_seq == 0)
    def _():
        s_ref[...] = jnp.zeros(s_ref.shape, F32)

    def sub_dot(c0):
        cols = slice(c0, c0 + EPILOGUE_COLS)
        y = jnp.maximum(jnp.dot(h_scr[...], w_ref[:, cols], preferred_element_type=F32), 0.0)
        o_ref[:, cols] = (y * y).astype(o_ref.dtype)

    fillers = [functools.partial(sub_dot, c0) for c0 in range(0, w_ref.shape[1], EPILOGUE_COLS)]
    _hgrn_chunks(q_ref, f_ref, i_ref, gg_ref, lbc_ref, gn_ref, sel_ref, ya_ref, s_ref, bpad, kpad,
                 c=c, n_chunks=n_chunks, heads=heads, unroll=True, fillers=fillers)


def _mlp_up_hgrn(x, g, mod6, w_stack, layer, seq_len, p_other, lbc_stack, gn, heads):
    n, d = x.shape
    m = w_stack.shape[2]
    tn = COL_TILE
    tm, nseg, tps = _row_tiling(n, seq_len, ROW_TILE)
    ni, nj = n // tm, m // tn
    n_other = p_other.shape[0]
    t_step = n_other // (ni * nj)
    assert t_step * ni * nj == n_other and seq_len % t_step == 0 and t_step % SUBLANES == 0
    steps_per_seq = seq_len // t_step
    aw = heads * A_DK
    c, h_in_specs, h_args, ya_spec, h_scratch = _hgrn_specs(
        p_other, lbc_stack, layer, gn, heads, t_step, lambda i, j: i * nj + j)

    def mod_spec(k):
        return pl.BlockSpec((None, nseg, 1, d), lambda i, j: (k, i // tps, 0, 0))

    state_spec = pl.BlockSpec((1, heads, A_DK, A_DK), lambda i, j: ((i * nj + j) // steps_per_seq, 0, 0, 0))
    return pl.pallas_call(
        functools.partial(_mlp_up_hgrn_kernel, nseg=nseg, c=c, n_chunks=t_step // c, heads=heads,
                          steps_per_seq=steps_per_seq),
        grid=(ni, nj),
        in_specs=[
            pl.BlockSpec((tm, d), lambda i, j: (i, 0)),
            pl.BlockSpec((1, d), lambda i, j: (0, 0)),
            mod_spec(4),
            mod_spec(3),
            pl.BlockSpec((None, d, tn), lambda i, j: (layer, 0, j)),
        ] + h_in_specs,
        out_specs=[pl.BlockSpec((tm, tn), lambda i, j: (i, j)), ya_spec, state_spec],
        out_shape=[jax.ShapeDtypeStruct((n, m), BF16),
                   jax.ShapeDtypeStruct((n_other, aw), BF16),
                   jax.ShapeDtypeStruct((n_other // seq_len, heads, A_DK, A_DK), F32)],
        scratch_shapes=[pltpu.VMEM((tm, d), BF16)] + h_scratch,
        compiler_params=_cparams(("arbitrary", "arbitrary")),
        name="mlp_up_hgrn2",
    )(x, g.reshape(1, d), mod6, mod6, w_stack, *h_args)


def _hgrn(p, lbc_stack, layer, gn, s0, bsz, seq_len, heads):
    c = min(seq_len, SCAN_CHUNK)
    t_blk = min(seq_len, 256)
    nt = seq_len // t_blk
    aw = heads * A_DK

    def col_spec(seg):
        return pl.BlockSpec((t_blk, aw), lambda b, t: (b * nt + t, seg))

    sel = _hgrn_select(c)
    in_specs = [col_spec(0), col_spec(1), col_spec(2), col_spec(3),
                pl.BlockSpec((None, SUBLANES, aw), lambda b, t: (layer, 0, 0)),
                pl.BlockSpec((1, aw), lambda b, t: (0, 0)),
                pl.BlockSpec(sel.shape, lambda b, t: (0, 0))]
    args = [p, p, p, p, lbc_stack, gn.reshape(1, aw), sel]
    state_spec = pl.BlockSpec((1, heads, A_DK, A_DK), lambda b, t: (b, 0, 0, 0))
    if s0 is not None:
        in_specs.append(pl.BlockSpec((None, 1, heads, A_DK, A_DK), lambda b, t: (layer, b, 0, 0, 0)))
        args.append(s0)
    return pl.pallas_call(
        functools.partial(_hgrn_kernel, c=c, n_chunks=t_blk // c, heads=heads, has_init=s0 is not None),
        grid=(bsz, nt),
        in_specs=in_specs,
        out_specs=[pl.BlockSpec((t_blk, aw), lambda b, t: (b * nt + t, 0)), state_spec],
        out_shape=[jax.ShapeDtypeStruct((bsz * seq_len, aw), BF16),
                   jax.ShapeDtypeStruct((bsz, heads, A_DK, A_DK), F32)],
        scratch_shapes=[pltpu.VMEM((c + DIAG, aw), F32), pltpu.VMEM((c + DIAG, aw), F32)],
        compiler_params=_cparams(("arbitrary", "arbitrary")),
        name="hgrn2",
    )(*args)


def _ssd_kernel(*refs, t, bw, has_init):
    if has_init:
        (z_ref, xs_ref, bc_ref, dt_ref, cw_ref, cb_ref, dtb_ref, alog_ref, dsk_ref, gn_ref, exp_ref, sel_ref,
         s0_ref, c0_ref, yb_ref, s_out_ref, conv_out_ref, xpad, st) = refs
    else:
        (z_ref, xs_ref, bc_ref, dt_ref, cw_ref, cb_ref, dtb_ref, alog_ref, dsk_ref, gn_ref, exp_ref, sel_ref,
         yb_ref, s_out_ref, conv_out_ref, xpad, st) = refs
    ti = pl.program_id(1)
    gw = bw // B_GROUPS
    n_bc = B_GROUPS * B_DSTATE
    pad = SUBLANES

    @pl.when(ti == 0)
    def _():
        if has_init:
            xpad[0:pad, :] = c0_ref[0]
            st[...] = s0_ref[0].reshape(bw, B_DSTATE).T
        else:
            xpad[0:pad, :] = jnp.zeros((pad, xpad.shape[1]), F32)
            st[...] = jnp.zeros(st.shape, F32)

    xpad[pad:pad + t, 0:bw] = xs_ref[...]
    xpad[pad:pad + t, bw:bw + 2 * n_bc] = bc_ref[...]
    conv = cb_ref[...]
    for j in range(CONV_W):
        lo = pad - (CONV_W - 1) + j
        conv = conv + xpad[lo:lo + t, :] * cw_ref[j:j + 1, :]
    xbc = _silu(conv)
    x = xbc[:, 0:bw]

    dt = _softplus(dt_ref[...] + dtb_ref[...])
    a = dt * (-jnp.exp(alog_ref[...]))
    cs = _dot3(sel_ref[...], a)
    a_cum = cs[0:t]
    a_tot = cs[t:2 * t]
    ex = _dot3_rhs(jnp.concatenate([dt, a_cum, a_tot - a_cum], axis=0), exp_ref[...])
    dt_e = ex[0:t]
    acum_e = ex[t:2 * t]
    dec_e = ex[2 * t:3 * t]
    atot_e = acum_e[t - 1:t, :]

    xdt = x * dt_e
    xw = (xdt * jnp.exp(dec_e)).astype(BF16)
    xdt_b = xdt.astype(BF16)
    a_cum_t = a_cum.T
    rr = lax.broadcasted_iota(jnp.int32, (t, t), 0)
    cc = lax.broadcasted_iota(jnp.int32, (t, t), 1)
    causal = cc <= rr
    lane = lax.broadcasted_iota(jnp.int32, (t, LANES), 1)
    heads_per_group = gw // B_HEADDIM
    pairs_per_group = gw // LANES

    y_groups = []
    for g in range(B_GROUPS):
        bg = xbc[:, bw + g * B_DSTATE:bw + (g + 1) * B_DSTATE].astype(BF16)
        cg = xbc[:, bw + n_bc + g * B_DSTATE:bw + n_bc + (g + 1) * B_DSTATE].astype(BF16)
        gs = slice(g * gw, (g + 1) * gw)
        st_g = st[:, gs]
        scores = lax.dot_general(cg, bg, (((1,), (1,)), ((), ())), preferred_element_type=F32)
        y_off = jnp.dot(cg, st_g.astype(BF16), preferred_element_type=F32) * jnp.exp(acum_e[:, gs])
        st[:, gs] = jnp.exp(atot_e[:, gs]) * st_g + lax.dot_general(
            bg, xw[:, gs], (((0,), (0,)), ((), ())), preferred_element_type=F32)
        y_pairs = []
        for pr in range(pairs_per_group):
            cols = slice(g * gw + pr * LANES, g * gw + (pr + 1) * LANES)
            xp = xdt_b[:, cols]
            acc = None
            for half in range(LANES // B_HEADDIM):
                h = g * heads_per_group + pr * (LANES // B_HEADDIM) + half
                diff = a_cum[:, h:h + 1] - a_cum_t[h:h + 1, :]
                lmat = jnp.where(causal, jnp.exp(jnp.where(causal, diff, 0.0)), 0.0)
                m = (scores * lmat).astype(BF16)
                in_half = (lane // B_HEADDIM) == half
                term = jnp.dot(m, jnp.where(in_half, xp, jnp.zeros_like(xp)), preferred_element_type=F32)
                acc = term if acc is None else acc + term
            y_pairs.append(acc)
        y_groups.append(jnp.concatenate(y_pairs, axis=1) + y_off)
    y = jnp.concatenate(y_groups, axis=1) + dsk_ref[...] * x
    y = y * _silu(z_ref[...])
    outs = []
    for g in range(B_GROUPS):
        yg = y[:, g * gw:(g + 1) * gw]
        outs.append(yg * lax.rsqrt(jnp.mean(yg * yg, axis=-1, keepdims=True) + NORM_EPS))
    yb_ref[...] = (jnp.concatenate(outs, axis=1) * gn_ref[...]).astype(yb_ref.dtype)

    tail = xpad[t:t + pad, :]
    xpad[0:pad, :] = tail

    @pl.when(ti == pl.num_programs(1) - 1)
    def _():
        conv_out_ref[0] = tail
        s_out_ref[0] = st[...].T.reshape(s_out_ref.shape[1:])


def _ssd(p, cols, prm, s0, c0, bsz, seq_len):
    bw = prm["bw"]
    heads = bw // B_HEADDIM
    n_bc = B_GROUPS * B_DSTATE
    cdim = bw + 2 * n_bc
    t = min(seq_len, 128)
    nt = seq_len // t
    tri = np.arange(t)[None, :] <= np.arange(t)[:, None]
    sel = jnp.asarray(np.concatenate([tri, np.ones((t, t), bool)], 0).astype(np.float32), dtype=BF16)
    expand = np.zeros((LANES, bw), np.float32)
    expand[np.arange(bw) // B_HEADDIM, np.arange(bw)] = 1.0
    expand = jnp.asarray(expand, dtype=BF16)

    def blk(width, off):
        return pl.BlockSpec((t, width), lambda b, ti: (b * nt + ti, off // width))

    def full(shape):
        return pl.BlockSpec(shape, lambda b, ti: (0,) * len(shape))

    in_specs = [blk(bw, cols["z"]), blk(bw, cols["xs"]), blk(2 * n_bc, cols["bc"]), blk(LANES, cols["dt"]),
                full((CONV_W, cdim)), full((1, cdim)), full((1, LANES)), full((1, LANES)),
                full((1, bw)), full((1, bw)), full(expand.shape), full(sel.shape)]
    args = [p, p, p, p, prm["conv_w"], prm["conv_b"], prm["dt_bias"], prm["a_log"], prm["d_skip"], prm["gn"],
            expand, sel]
    state_spec = pl.BlockSpec((1, heads, B_HEADDIM, B_DSTATE), lambda b, ti: (b, 0, 0, 0))
    conv_spec = pl.BlockSpec((1, SUBLANES, cdim), lambda b, ti: (b, 0, 0))
    if s0 is not None:
        layer = prm["layer"]
        in_specs += [pl.BlockSpec((None, 1, heads, B_HEADDIM, B_DSTATE), lambda b, ti: (layer, b, 0, 0, 0)),
                     pl.BlockSpec((None, 1, SUBLANES, cdim), lambda b, ti: (layer, b, 0, 0))]
        args += [s0, c0]
    return pl.pallas_call(
        functools.partial(_ssd_kernel, t=t, bw=bw, has_init=s0 is not None),
        grid=(bsz, nt),
        in_specs=in_specs,
        out_specs=[pl.BlockSpec((t, bw), lambda b, ti: (b * nt + ti, 0)), state_spec, conv_spec],
        out_shape=[jax.ShapeDtypeStruct((bsz * seq_len, bw), BF16),
                   jax.ShapeDtypeStruct((bsz, heads, B_HEADDIM, B_DSTATE), F32),
                   jax.ShapeDtypeStruct((bsz, SUBLANES, cdim), F32)],
        scratch_shapes=[pltpu.VMEM((t + SUBLANES, cdim), F32), pltpu.VMEM((B_DSTATE, bw), F32)],
        compiler_params=_cparams(("arbitrary", "arbitrary")),
        name="ssd",
    )(*args)


def _cmlp_kernel(u_ref, v_ref, lng_ref, lnb_ref, ws_ref, bst_ref, *out_refs, t, n_chunks, keep_v):
    yc_ref = out_refs[0]
    cw = u_ref.shape[1] // C_GROUPS
    rr = lax.broadcasted_iota(jnp.int32, (t, t), 0)
    cc = lax.broadcasted_iota(jnp.int32, (t, t), 1)
    wts = [jnp.where(cc <= rr, ws_ref[g, 0:t, 0:t], 0.0).astype(BF16) for g in range(C_GROUPS)]
    for ci in range(n_chunks):
        rows = slice(ci * t, (ci + 1) * t)
        u = _gelu(u_ref[rows, :])
        gv = _gelu(v_ref[rows, :])
        mu = jnp.mean(gv, axis=-1, keepdims=True)
        dv = gv - mu
        var = jnp.mean(dv * dv, axis=-1, keepdims=True)
        v = dv * lax.rsqrt(var + NORM_EPS) * lng_ref[...] + lnb_ref[...]
        if keep_v:
            out_refs[1][rows, :] = v
        vb = v.astype(BF16)
        for g in range(C_GROUPS):
            lanes = slice(g * cw, (g + 1) * cw)
            mixed = jnp.dot(wts[g], vb[:, lanes], preferred_element_type=F32) + bst_ref[0:t, g:g + 1]
            yc_ref[rows, lanes] = (u[:, lanes] * mixed).astype(yc_ref.dtype)


def _cmlp(p, col_u, col_v, prm, bsz, seq_len, keep_v):
    cw = prm["cw"]
    t = min(seq_len, CMLP_CHUNK)
    t_blk = min(seq_len, 4 * CMLP_CHUNK)
    n = bsz * seq_len

    def full(shape):
        return pl.BlockSpec(shape, lambda i: (0,) * len(shape))

    out_specs = [pl.BlockSpec((t_blk, cw), lambda i: (i, 0))]
    out_shape = [jax.ShapeDtypeStruct((n, cw), BF16)]
    if keep_v:
        out_specs.append(pl.BlockSpec((t_blk, cw), lambda i: (i, 0)))
        out_shape.append(jax.ShapeDtypeStruct((n, cw), F32))
    return pl.pallas_call(
        functools.partial(_cmlp_kernel, t=t, n_chunks=t_blk // t, keep_v=keep_v),
        grid=(n // t_blk,),
        in_specs=[pl.BlockSpec((t_blk, cw), lambda i: (i, col_u // cw)),
                  pl.BlockSpec((t_blk, cw), lambda i: (i, col_v // cw)),
                  full((1, cw)), full((1, cw)),
                  full((C_GROUPS, CMLP_CHUNK, CMLP_CHUNK)), full((CMLP_CHUNK, C_GROUPS))],
        out_specs=out_specs,
        out_shape=out_shape,
        compiler_params=_cparams(("arbitrary",)),
        name="cmlp",
    )(p, p, prm["ln_g"], prm["ln_b"], prm["ws"], prm["bs_t"])


def _run_trunk(x3, mod, st_hgrn, st_ssm, st_conv, keep_v, w):
    bsz, seq_len, d = x3.shape
    depth = mod.shape[0]
    aw = bw = cw = d // 2
    heads_a = aw // A_DK
    x = x3.reshape(bsz * seq_len, d)
    col = w["cols"]
    hgrn_out, ssm_out, conv_out, v_out = [], [], [], []
    for l in range(depth):
        mod6 = mod[l].reshape(bsz, 6, 1, d).transpose(1, 0, 2, 3)
        p = _in_proj(x, w["norm1_g"][l], mod6, w["w_in"], l, seq_len)

        y_a, s_h = _hgrn(p, w["lbc"], l, w["hgrn_onorm_g"][l], st_hgrn, bsz, seq_len, heads_a)
        ssd_prm = dict(bw=bw, layer=l, conv_w=w["ssm_conv_w"][l], conv_b=w["ssm_conv_b"][l][None],
                       dt_bias=w["dt_bias_pad"][l][None], a_log=w["a_log_pad"][l][None],
                       d_skip=w["d_skip"][l][None], gn=w["ssm_onorm_g"][l][None])
        y_b, s_s, conv_tail = _ssd(p, col, ssd_prm, st_ssm, st_conv, bsz, seq_len)
        cm_prm = dict(cw=cw, ln_g=w["cmlp_ln_g"][l][None], ln_b=w["cmlp_ln_b"][l][None],
                      ws=w["cmlp_ws"][l], bs_t=w["cmlp_bs"][l].T)
        c_res = _cmlp(p, col["u"], col["v"], cm_prm, bsz, seq_len, keep_v)
        merged = _merge(y_a, y_b, c_res[0], w["w_branch"], l, p, col["gate"], d)
        x = _proj_residual(merged, w["w_out"], l, x, mod6, 2, seq_len, "out_proj")
        hid = _mlp_up(x, w["norm2_g"][l], mod6, w["w_up"], l, seq_len)
        x = _proj_residual(hid, w["w_down"], l, x, mod6, 5, seq_len, "mlp_down")

        hgrn_out.append(s_h)
        ssm_out.append(s_s)
        conv_out.append(conv_tail[:, SUBLANES - (CONV_W - 1):, :])
        if keep_v:
            v_out.append(c_res[1].reshape(bsz, seq_len, cw))
    y = _final_norm(x, w["final_g"]).reshape(bsz, seq_len, d)
    return (y, jnp.stack(hgrn_out), jnp.stack(ssm_out), jnp.stack(conv_out),
            jnp.stack(v_out) if keep_v else None)


def _mlp_down_hgrn_kernel(a_ref, w_ref, x_ref, gate_ref, q_ref, f_ref, i_ref, gg_ref, lbc_ref, gn_ref, sel_ref,
                          o_ref, ya_ref, s_ref, acc_scr, bpad, kpad, *, nseg, c, n_chunks, heads, steps_per_seq):
    step = pl.program_id(0) * pl.num_programs(1) + pl.program_id(1)

    @pl.when(step % steps_per_seq == 0)
    def _():
        s_ref[...] = jnp.zeros(s_ref.shape, F32)

    kdim = a_ref.shape[1]
    kc = kdim // DOWN_K_SPLIT

    def sub_dot(ki):
        rows = slice(ki * kc, (ki + 1) * kc)
        part = jnp.dot(a_ref[:, rows], w_ref[rows, :], preferred_element_type=F32)
        if ki == 0:
            acc_scr[...] = part
        elif ki < DOWN_K_SPLIT - 1:
            acc_scr[...] += part
        else:
            x = x_ref[...]
            acc = acc_scr[...] + part
            o_ref[...] = (_per_seq(x, nseg) + gate_ref[...] * _per_seq(acc, nseg)).reshape(x.shape)

    fillers = [functools.partial(sub_dot, ki) for ki in range(DOWN_K_SPLIT)]
    _hgrn_chunks(q_ref, f_ref, i_ref, gg_ref, lbc_ref, gn_ref, sel_ref, ya_ref, s_ref, bpad, kpad,
                 c=c, n_chunks=n_chunks, heads=heads, unroll=True, fillers=fillers)


def _mlp_down_hgrn(a, w_stack, layer, x, mod6, k_gate, seq_len, p_other, lbc_stack, lbc_layer, gn, heads):
    n, kdim = a.shape
    d = w_stack.shape[2]
    tn = DOWN_TILE
    tm, nseg, tps = _row_tiling(n, seq_len, ROW_TILE)
    ni, nj = n // tm, d // tn
    n_other = p_other.shape[0]
    t_step = n_other // (ni * nj)
    assert t_step * ni * nj == n_other and seq_len % t_step == 0 and t_step % SUBLANES == 0
    assert kdim % DOWN_K_SPLIT == 0
    steps_per_seq = seq_len // t_step
    aw = heads * A_DK
    c, h_in_specs, h_args, ya_spec, h_scratch = _hgrn_specs(
        p_other, lbc_stack, lbc_layer, gn, heads, t_step, lambda i, j: i * nj + j)
    state_spec = pl.BlockSpec((1, heads, A_DK, A_DK), lambda i, j: ((i * nj + j) // steps_per_seq, 0, 0, 0))
    return pl.pallas_call(
        functools.partial(_mlp_down_hgrn_kernel, nseg=nseg, c=c, n_chunks=t_step // c, heads=heads,
                          steps_per_seq=steps_per_seq),
        grid=(ni, nj),
        in_specs=[
            pl.BlockSpec((tm, kdim), lambda i, j: (i, 0)),
            pl.BlockSpec((None, kdim, tn), lambda i, j: (layer, 0, j)),
            pl.BlockSpec((tm, tn), lambda i, j: (i, j)),
            pl.BlockSpec((None, nseg, 1, tn), lambda i, j: (k_gate, i // tps, 0, j)),
        ] + h_in_specs,
        out_specs=[pl.BlockSpec((tm, tn), lambda i, j: (i, j)), ya_spec, state_spec],
        out_shape=[jax.ShapeDtypeStruct((n, d), F32),
                   jax.ShapeDtypeStruct((n_other, aw), BF16),
                   jax.ShapeDtypeStruct((n_other // seq_len, heads, A_DK, A_DK), F32)],
        scratch_shapes=[pltpu.VMEM((tm, tn), F32)] + h_scratch,
        compiler_params=_cparams(("arbitrary", "arbitrary")),
        name="mlp_down_hgrn2",
    )(a, w_stack, x, mod6, *h_args)


def _run_trunk_halves(x3, mod, w):
    bsz, seq_len, d = x3.shape
    depth = mod.shape[0]
    hb = bsz // 2
    aw = bw = cw = d // 2
    heads_a = aw // A_DK
    col = w["cols"]
    n_half = hb * seq_len
    x_full = x3.reshape(bsz * seq_len, d)
    xs = [x_full, x_full]
    row0 = [0, n_half]
    mods = [mod[:, :hb], mod[:, hb:]]
    hgrn_out, ssm_out, conv_out = [], [], []
    pending = None
    for l in range(depth):
        mod6 = [m[l].reshape(hb, 6, 1, d).transpose(1, 0, 2, 3) for m in mods]
        ssd_prm = dict(bw=bw, layer=l, conv_w=w["ssm_conv_w"][l], conv_b=w["ssm_conv_b"][l][None],
                       dt_bias=w["dt_bias_pad"][l][None], a_log=w["a_log_pad"][l][None],
                       d_skip=w["d_skip"][l][None], gn=w["ssm_onorm_g"][l][None])
        cm_prm = dict(cw=cw, ln_g=w["cmlp_ln_g"][l][None], ln_b=w["cmlp_ln_b"][l][None],
                      ws=w["cmlp_ws"][l], bs_t=w["cmlp_bs"][l].T)
        gn_a = w["hgrn_onorm_g"][l]

        def mix_merge(h, p, y_a):
            y_b, s_s, conv_tail = _ssd(p, col, ssd_prm, None, None, hb, seq_len)
            y_c = _cmlp(p, col["u"], col["v"], cm_prm, hb, seq_len, False)[0]
            merged = _merge(y_a, y_b, y_c, w["w_branch"], l, p, col["gate"], d)
            x_new = _proj_residual(merged, w["w_out"], l, xs[h], mod6[h], 2, seq_len, "out_proj", row0[h])
            row0[h] = 0
            return x_new, s_s, conv_tail

        p0 = _in_proj(xs[0], w["norm1_g"][l], mod6[0], w["w_in"], l, seq_len, (row0[0], n_half))
        if pending is None:
            ya0, sh0 = _hgrn(p0, w["lbc"], l, gn_a, None, hb, seq_len, heads_a)
        else:
            hid1, x1_mid, mod6_prev = pending
            xs[1], ya0, sh0 = _mlp_down_hgrn(hid1, w["w_down"], l - 1, x1_mid, mod6_prev, 5, seq_len,
                                              p0, w["lbc"], l, gn_a, heads_a)
        p1 = _in_proj(xs[1], w["norm1_g"][l], mod6[1], w["w_in"], l, seq_len, (row0[1], n_half))
        x0, ss0, ct0 = mix_merge(0, p0, ya0)
        hid0, ya1, sh1 = _mlp_up_hgrn(x0, w["norm2_g"][l], mod6[0], w["w_up"], l, seq_len,
                                      p1, w["lbc"], gn_a, heads_a)
        xs[0] = _proj_residual(hid0, w["w_down"], l, x0, mod6[0], 5, seq_len, "mlp_down")
        x1_mid, ss1, ct1 = mix_merge(1, p1, ya1)
        hid1 = _mlp_up(x1_mid, w["norm2_g"][l], mod6[1], w["w_up"], l, seq_len)
        pending = (hid1, x1_mid, mod6[1])

        hgrn_out.append(jnp.concatenate([sh0, sh1], axis=0))
        ssm_out.append(jnp.concatenate([ss0, ss1], axis=0))
        conv_out.append(jnp.concatenate([ct0, ct1], axis=0)[:, SUBLANES - (CONV_W - 1):, :])
    hid1, x1_mid, mod6_prev = pending
    xs[1] = _proj_residual(hid1, w["w_down"], depth - 1, x1_mid, mod6_prev, 5, seq_len, "mlp_down")
    y = _final_norm_pair(xs[0], xs[1], w["final_g"]).reshape(bsz, seq_len, d)
    return (y, jnp.stack(hgrn_out), jnp.stack(ssm_out), jnp.stack(conv_out), None)


def kernel(x_prompt, x_sample, state_hgrn, state_ssm, state_conv, c_prompt, c_sample, norm1_g, norm2_g,
           w_mod, b_mod, w_in, hgrn_lb, hgrn_onorm_g, ssm_conv_w, ssm_conv_b, ssm_dt_bias, ssm_a_log, ssm_d,
           ssm_onorm_g, cmlp_ln_g, cmlp_ln_b, cmlp_ws, cmlp_bs, w_branch, w_out, w_up, w_down, final_g):
    d = x_prompt.shape[-1]
    depth = w_in.shape[0]
    aw = bw = cw = d // 2
    n_bc = B_GROUPS * B_DSTATE
    heads_b = bw // B_HEADDIM
    assert heads_b <= LANES and bw % LANES == 0 and DT_PAD % LANES == 0

    o_dt = 4 * aw + bw + bw + 2 * n_bc
    o_u = o_dt + heads_b
    w_in_r = (jnp.pad(w_in[:, :, :o_u].astype(BF16), ((0, 0), (0, 0), (0, DT_PAD - heads_b))),
              w_in[:, :, o_u:].astype(BF16))
    cols = dict(z=4 * aw, xs=4 * aw + bw, bc=4 * aw + 2 * bw, dt=o_dt)
    cols["u"] = o_dt + DT_PAD
    cols["v"] = cols["u"] + cw
    cols["gate"] = cols["v"] + cw
    assert cols["gate"] % MERGE_TILE == 0 and all(part.shape[2] % IN_TILE == 0 for part in w_in_r)

    pad_h = lambda a: jnp.pad(a.astype(F32), ((0, 0), (0, LANES - heads_b)))
    w = dict(
        cols=cols, w_in=w_in_r,
        norm1_g=norm1_g, norm2_g=norm2_g, final_g=final_g,
        lbc=_lb_consts(hgrn_lb), hgrn_onorm_g=hgrn_onorm_g,
        ssm_conv_w=ssm_conv_w, ssm_conv_b=ssm_conv_b,
        dt_bias_pad=pad_h(ssm_dt_bias), a_log_pad=pad_h(ssm_a_log),
        d_skip=jnp.repeat(ssm_d.astype(F32), B_HEADDIM, axis=1), ssm_onorm_g=ssm_onorm_g,
        cmlp_ln_g=cmlp_ln_g, cmlp_ln_b=cmlp_ln_b, cmlp_ws=cmlp_ws, cmlp_bs=cmlp_bs,
        w_branch=w_branch.astype(BF16).reshape(depth, N_BRANCH, aw, d), w_out=w_out.astype(BF16),
        w_up=w_up.astype(BF16), w_down=w_down.astype(BF16),
    )

    nb = x_prompt.shape[0]
    mod = _modulation(jnp.concatenate([c_prompt, c_sample], axis=0), w_mod, b_mod)
    if nb % 2 == 0:
        y_p, hgrn_p, ssm_p, conv_p, _ = _run_trunk_halves(x_prompt, mod[:, :nb], w)
    else:
        y_p, hgrn_p, ssm_p, conv_p, _ = _run_trunk(x_prompt, mod[:, :nb], None, None, None, False, w)
    conv_pad = jnp.pad(state_conv, ((0, 0), (0, 0), (SUBLANES - (CONV_W - 1), 0), (0, 0)))
    y_s, hgrn_s, ssm_s, conv_s, v_s = _run_trunk(x_sample, mod[:, nb:], state_hgrn, state_ssm, conv_pad, True, w)
    return (y_p, y_s, hgrn_p, ssm_p, conv_p, hgrn_s, ssm_s, conv_s, v_s)
```

```python
import functools
import math

import numpy as np
import jax
import jax.numpy as jnp
from jax import lax
from jax.experimental import pallas as pl
from jax.experimental.pallas import tpu as pltpu

F32 = jnp.float32
BF16 = jnp.bfloat16

A_DK = 128
B_HEADDIM = 64
B_GROUPS = 2
B_DSTATE = 128
CONV_W = 4
C_GROUPS = 4
CMLP_CHUNK = 128
N_BRANCH = 3
SCAN_CHUNK = 64
NORM_EPS = 1e-6
LB_FLOOR = 1e-30
LOG2E = 1.4426950408889634

LANES = 128
SUBLANES = 8
VMEM_LIMIT = 56 * 1024 * 1024
DIAG = SUBLANES
DT_PAD = 512
ROW_TILE = 1024
COL_TILE = 1024
IN_TILE = 1536
MERGE_TILE = 512
DOWN_TILE = 256
EPILOGUE_COLS = 256
DOWN_K_SPLIT = 4
NORM_ROWS = 64


def _cparams(sem):
    return pltpu.CompilerParams(dimension_semantics=sem, vmem_limit_bytes=VMEM_LIMIT)


def _split3(x):
    hi = x.astype(BF16)
    r1 = x - hi.astype(F32)
    mid = r1.astype(BF16)
    lo = (r1 - mid.astype(F32)).astype(BF16)
    return hi, mid, lo


def _dot3(sel, x):
    hi, mid, lo = _split3(x)
    d = lambda p: jnp.dot(sel, p, preferred_element_type=F32)
    return d(hi) + d(mid) + d(lo)


def _dot3_rhs(x, sel):
    hi, mid, lo = _split3(x)
    d = lambda p: jnp.dot(p, sel, preferred_element_type=F32)
    return d(hi) + d(mid) + d(lo)


def _silu(x):
    return x / (1.0 + jnp.exp(-x))


def _softplus(x):
    return jnp.maximum(x, 0.0) + jnp.log1p(jnp.exp(-jnp.abs(x)))


def _gelu(x):
    return 0.5 * x * (1.0 + lax.erf(x * (1.0 / math.sqrt(2.0))))


def _lb_kernel(lb_ref, out_ref, *, depth):
    x = lb_ref[...]
    m = jnp.max(x, axis=0, keepdims=True)
    e = jnp.exp(x - m)
    p = e / jnp.sum(e, axis=0, keepdims=True)
    acc = jnp.zeros_like(p[0:1])
    zeros5 = jnp.zeros((SUBLANES - 3, x.shape[1]), F32)
    for l in range(depth):
        acc = acc + p[l:l + 1]
        lb = acc - p[0:1]
        out_ref[l] = jnp.concatenate(
            [jnp.log(jnp.maximum(lb, LB_FLOOR)), jnp.log1p(-lb), 1.0 - lb, zeros5], axis=0)


def _lb_consts(hgrn_lb):
    depth, aw = hgrn_lb.shape
    return pl.pallas_call(
        functools.partial(_lb_kernel, depth=depth),
        out_shape=jax.ShapeDtypeStruct((depth, SUBLANES, aw), F32),
        name="hgrn_lb",
    )(hgrn_lb.astype(F32))


def _mod_kernel(c_ref, w_ref, b_ref, o_ref):
    cs = _silu(c_ref[...]).astype(BF16)
    o_ref[...] = jnp.dot(cs, w_ref[...].astype(BF16), preferred_element_type=F32) + b_ref[...]


def _modulation(c_all, w_mod, b_mod):
    depth, d, n6 = w_mod.shape
    s = c_all.shape[0]
    tn = 1024
    return pl.pallas_call(
        _mod_kernel,
        grid=(depth, n6 // tn),
        in_specs=[
            pl.BlockSpec((s, d), lambda l, j: (0, 0)),
            pl.BlockSpec((None, d, tn), lambda l, j: (l, 0, j)),
            pl.BlockSpec((None, 1, tn), lambda l, j: (l, 0, j)),
        ],
        out_specs=pl.BlockSpec((None, s, tn), lambda l, j: (l, 0, j)),
        out_shape=jax.ShapeDtypeStruct((depth, s, n6), F32),
        compiler_params=_cparams(("arbitrary", "arbitrary")),
        name="adaln_mod",
    )(c_all, w_mod, b_mod.reshape(depth, 1, n6))


def _row_tiling(n_tok, seq_len, tm_max):
    tm = min(tm_max, n_tok)
    if seq_len >= tm:
        assert seq_len % tm == 0
        return tm, 1, seq_len // tm
    assert tm % seq_len == 0
    return tm, tm // seq_len, 1


def _per_seq(x, nseg):
    tm, d = x.shape
    return x.reshape(nseg, tm // nseg, d)


def _final_norm_kernel(x_ref, g_ref, o_ref):
    x = x_ref[...]
    o_ref[...] = x * lax.rsqrt(jnp.mean(x * x, axis=-1, keepdims=True) + NORM_EPS) * g_ref[...]


def _final_norm(x, g):
    n, d = x.shape
    tm = min(512, n)
    return pl.pallas_call(
        _final_norm_kernel,
        grid=(n // tm,),
        in_specs=[pl.BlockSpec((tm, d), lambda i: (i, 0)), pl.BlockSpec((1, d), lambda i: (0, 0))],
        out_specs=pl.BlockSpec((tm, d), lambda i: (i, 0)),
        out_shape=jax.ShapeDtypeStruct((n, d), F32),
        compiler_params=_cparams(("arbitrary",)),
        name="final_norm",
    )(x, g.reshape(1, d))


def _final_norm_pair_kernel(xa_ref, xb_ref, g_ref, o_ref, *, nt):
    def norm(x):
        return x * lax.rsqrt(jnp.mean(x * x, axis=-1, keepdims=True) + NORM_EPS) * g_ref[...]

    @pl.when(pl.program_id(0) < nt)
    def _():
        o_ref[...] = norm(xa_ref[...])

    @pl.when(pl.program_id(0) >= nt)
    def _():
        o_ref[...] = norm(xb_ref[...])


def _final_norm_pair(xa, xb, g):
    n, d = xa.shape
    tm = min(512, n)
    nt = n // tm
    return pl.pallas_call(
        functools.partial(_final_norm_pair_kernel, nt=nt),
        grid=(2 * nt,),
        in_specs=[pl.BlockSpec((tm, d), lambda i: (jnp.minimum(i, nt - 1), 0)),
                  pl.BlockSpec((tm, d), lambda i: (jnp.maximum(i - nt, 0), 0)),
                  pl.BlockSpec((1, d), lambda i: (0, 0))],
        out_specs=pl.BlockSpec((tm, d), lambda i: (i, 0)),
        out_shape=jax.ShapeDtypeStruct((2 * n, d), F32),
        compiler_params=_cparams(("arbitrary",)),
        name="final_norm",
    )(xa, xb, g.reshape(1, d))


def _norm_mod_rows(x_ref, g_ref, sc_ref, sh_ref, h_scr, nseg):
    tm = x_ref.shape[0]
    seg = tm // nseg
    rb = min(seg, NORM_ROWS)
    g = g_ref[...]

    def body(r, carry):
        rows = pl.ds(pl.multiple_of(r * rb, rb), rb)
        s = r // (seg // rb)
        x = x_ref[rows, :]
        y = x * lax.rsqrt(jnp.mean(x * x, axis=-1, keepdims=True) + NORM_EPS) * g
        h_scr[rows, :] = (y * (1.0 + sc_ref[s]) + sh_ref[s]).astype(BF16)
        return carry

    lax.fori_loop(0, tm // rb, body, 0)


def _mlp_up_kernel(x_ref, g_ref, sc_ref, sh_ref, w_ref, o_ref, h_scr, *, nseg):
    @pl.when(pl.program_id(1) == 0)
    def _():
        _norm_mod_rows(x_ref, g_ref, sc_ref, sh_ref, h_scr, nseg)

    y = jnp.maximum(jnp.dot(h_scr[...], w_ref[...], preferred_element_type=F32), 0.0)
    o_ref[...] = (y * y).astype(o_ref.dtype)


def _in_proj_kernel(x_ref, g_ref, sc_ref, sh_ref, w_ref, p_ref, h_scr, *, nseg):
    @pl.when(pl.program_id(1) == 0)
    def _():
        _norm_mod_rows(x_ref, g_ref, sc_ref, sh_ref, h_scr, nseg)

    p_ref[...] = jnp.dot(h_scr[...], w_ref[...], preferred_element_type=F32)


def _norm_proj_call(kern, x, g, mod6, k_scale, k_shift, w_stack, layer, seq_len, tn, out_specs, out_shape, name):
    n, d = x.shape
    m = w_stack.shape[2]
    tm, nseg, tps = _row_tiling(n, seq_len, ROW_TILE)

    def mod_spec(k):
        return pl.BlockSpec((None, nseg, 1, d), lambda i, j: (k, i // tps, 0, 0))

    return pl.pallas_call(
        functools.partial(kern, nseg=nseg),
        grid=(n // tm, m // tn),
        in_specs=[
            pl.BlockSpec((tm, d), lambda i, j: (i, 0)),
            pl.BlockSpec((1, d), lambda i, j: (0, 0)),
            mod_spec(k_scale),
            mod_spec(k_shift),
            pl.BlockSpec((None, d, tn), lambda i, j: (layer, 0, j)),
        ],
        out_specs=out_specs(tm),
        out_shape=out_shape,
        scratch_shapes=[pltpu.VMEM((tm, d), BF16)],
        compiler_params=_cparams(("arbitrary", "arbitrary")),
        name=name,
    )(x, g.reshape(1, d), mod6, mod6, w_stack)


def _mlp_up(x, g, mod6, w_stack, layer, seq_len):
    n = x.shape[0]
    m = w_stack.shape[2]
    tn = COL_TILE
    return _norm_proj_call(
        _mlp_up_kernel, x, g, mod6, 4, 3, w_stack, layer, seq_len, tn,
        lambda tm: pl.BlockSpec((tm, tn), lambda i, j: (i, j)),
        jax.ShapeDtypeStruct((n, m), BF16), "mlp_up")


def _in_proj(x, g, mod6, w_stack, layer, seq_len, rows=None):
    d = x.shape[1]
    row0, n = rows if rows is not None else (0, x.shape[0])
    m = w_stack.shape[2]
    tn = IN_TILE
    tm, nseg, tps = _row_tiling(n, seq_len, ROW_TILE)

    def mod_spec(k):
        return pl.BlockSpec((None, nseg, 1, d), lambda i, j: (k, i // tps, 0, 0))

    return pl.pallas_call(
        functools.partial(_in_proj_kernel, nseg=nseg),
        grid=(n // tm, m // tn),
        in_specs=[
            pl.BlockSpec((tm, d), lambda i, j: (i + row0 // tm, 0)),
            pl.BlockSpec((1, d), lambda i, j: (0, 0)),
            mod_spec(1),
            mod_spec(0),
            pl.BlockSpec((None, d, tn), lambda i, j: (layer, 0, j)),
        ],
        out_specs=pl.BlockSpec((tm, tn), lambda i, j: (i, j)),
        out_shape=jax.ShapeDtypeStruct((n, m), F32),
        scratch_shapes=[pltpu.VMEM((tm, d), BF16)],
        compiler_params=_cparams(("arbitrary", "arbitrary")),
        name="in_proj",
    )(x, g.reshape(1, d), mod6, mod6, w_stack)


def _proj_res_kernel(a_ref, w_ref, x_ref, gate_ref, o_ref, *, nseg, nk):
    part = jnp.dot(a_ref[...], w_ref[...], preferred_element_type=F32)

    def finish(acc):
        x = x_ref[...]
        o_ref[...] = (_per_seq(x, nseg) + gate_ref[...] * _per_seq(acc, nseg)).reshape(x.shape)

    if nk == 1:
        finish(part)
        return
    k = pl.program_id(2)

    @pl.when(k == 0)
    def _():
        o_ref[...] = part

    @pl.when((k > 0) & (k < nk - 1))
    def _():
        o_ref[...] += part

    @pl.when(k == nk - 1)
    def _():
        finish(o_ref[...] + part)


def _proj_residual(a, w_stack, layer, x, mod6, k_gate, seq_len, name, x_row0=0):
    n, kdim = a.shape
    d = w_stack.shape[2]
    tn = COL_TILE if kdim <= d else DOWN_TILE
    tk = kdim
    nk = kdim // tk
    tm, nseg, tps = _row_tiling(n, seq_len, ROW_TILE)
    return pl.pallas_call(
        functools.partial(_proj_res_kernel, nseg=nseg, nk=nk),
        grid=(n // tm, d // tn, nk),
        in_specs=[
            pl.BlockSpec((tm, tk), lambda i, j, k: (i, k)),
            pl.BlockSpec((None, tk, tn), lambda i, j, k: (layer, k, j)),
            pl.BlockSpec((tm, tn), lambda i, j, k: (i + x_row0 // tm, j)),
            pl.BlockSpec((None, nseg, 1, tn), lambda i, j, k: (k_gate, i // tps, 0, j)),
        ],
        out_specs=pl.BlockSpec((tm, tn), lambda i, j, k: (i, j)),
        out_shape=jax.ShapeDtypeStruct((n, d), F32),
        compiler_params=_cparams(("arbitrary", "arbitrary", "arbitrary")),
        name=name,
    )(a, w_stack, x, mod6)


def _merge_kernel(ya_ref, yb_ref, yc_ref, wa_ref, wb_ref, wc_ref, ga_ref, gb_ref, gc_ref, o_ref):
    acc = None
    for y_ref, w_ref, g_ref in ((ya_ref, wa_ref, ga_ref), (yb_ref, wb_ref, gb_ref), (yc_ref, wc_ref, gc_ref)):
        gate = 1.0 / (1.0 + jnp.exp(-g_ref[...]))
        term = gate * jnp.dot(y_ref[...], w_ref[...], preferred_element_type=F32)
        acc = term if acc is None else acc + term
    o_ref[...] = acc.astype(o_ref.dtype)


def _merge(ya, yb, yc, w_stack, layer, p, gate_col0, d):
    n, bw = ya.shape
    tn = MERGE_TILE
    tm = min(ROW_TILE, n)
    gsteps = d // tn
    g0 = gate_col0 // tn
    y_spec = pl.BlockSpec((tm, bw), lambda i, j: (i, 0))

    def w_spec(k):
        return pl.BlockSpec((None, None, bw, tn), lambda i, j: (layer, k, 0, j))

    def g_spec(k):
        return pl.BlockSpec((tm, tn), lambda i, j: (i, g0 + k * gsteps + j))

    return pl.pallas_call(
        _merge_kernel,
        grid=(n // tm, d // tn),
        in_specs=[y_spec, y_spec, y_spec, w_spec(0), w_spec(1), w_spec(2), g_spec(0), g_spec(1), g_spec(2)],
        out_specs=pl.BlockSpec((tm, tn), lambda i, j: (i, j)),
        out_shape=jax.ShapeDtypeStruct((n, d), BF16),
        compiler_params=_cparams(("arbitrary", "arbitrary")),
        name="merge",
    )(ya, yb, yc, w_stack, w_stack, w_stack, p, p, p)


def _hgrn_levels(c):
    lv, m = [], c // 2
    while m >= DIAG:
        lv.append(m)
        m //= 2
    return tuple(lv)


def _hgrn_select(c):
    tri = np.arange(c)[None, :] <= np.arange(c)[:, None]
    return jnp.asarray(tri.astype(np.float32), dtype=BF16)


def _hgrn_kernel(*refs, c, n_chunks, heads, has_init):
    if has_init:
        q_ref, f_ref, i_ref, g_ref, lbc_ref, gn_ref, sel_ref, s0_ref, ya_ref, s_ref, bpad, kpad = refs
    else:
        q_ref, f_ref, i_ref, g_ref, lbc_ref, gn_ref, sel_ref, ya_ref, s_ref, bpad, kpad = refs

    @pl.when(pl.program_id(1) == 0)
    def _():
        s_ref[...] = s0_ref[...] if has_init else jnp.zeros(s_ref.shape, F32)

    _hgrn_chunks(q_ref, f_ref, i_ref, g_ref, lbc_ref, gn_ref, sel_ref, ya_ref, s_ref, bpad, kpad,
                 c=c, n_chunks=n_chunks, heads=heads, unroll=False)


def _hgrn_chunks(q_ref, f_ref, i_ref, g_ref, lbc_ref, gn_ref, sel_ref, ya_ref, s_ref, bpad, kpad,
                 *, c, n_chunks, heads, unroll, fillers=()):
    levels = _hgrn_levels(c)
    nl = len(levels)
    width = heads * A_DK
    head_lanes = [slice(hh * A_DK, (hh + 1) * A_DK) for hh in range(heads)]

    bpad[0:DIAG, :] = jnp.zeros((DIAG, width), F32)
    kpad[0:DIAG, :] = jnp.zeros((DIAG, width), F32)

    sel = sel_ref[...]
    log_lb = lbc_ref[0:1, :]
    log1m_lb = lbc_ref[1:2, :]
    one_m_lb = lbc_ref[2:3, :]
    row = lax.broadcasted_iota(jnp.int32, (c, width), 0)
    rr = lax.broadcasted_iota(jnp.int32, (c, c), 0)
    cc = lax.broadcasted_iota(jnp.int32, (c, c), 1)
    upper = [(row & m) != 0 for m in levels]
    same = [(rr ^ cc) < 2 * m for m in levels]
    dmat = jnp.where(((rr ^ cc) < DIAG) & (cc <= rr), rr - cc, -1)
    nt_dims = (((1,), (1,)), ((), ()))
    tn_dims = (((0,), (0,)), ((), ()))

    def chunk(ci):
        rows = slice(ci * c, (ci + 1) * c) if unroll else pl.ds(pl.multiple_of(ci * c, c), c)
        z = f_ref[rows, :]
        aq = q_ref[rows, :]

        log_sig = jnp.minimum(z, 0.0) - jnp.log(1.0 + jnp.exp(-jnp.abs(z)))
        bb = log1m_lb + log_sig
        log_f = jnp.maximum(log_lb, bb) + jnp.log(1.0 + jnp.exp(-jnp.abs(log_lb - bb)))
        k = one_m_lb / (1.0 + jnp.exp(z))
        q = _silu(aq)

        b = _dot3(sel, log_f * LOG2E)
        bpad[DIAG:DIAG + c, :] = b
        kpad[DIAG:DIAG + c, :] = k
        b_tot = bpad[DIAG + c - 1:DIAG + c, :]
        vb = i_ref[rows, :].astype(BF16)
        q_in = (q * jnp.exp2(b)).astype(BF16)
        k_out = (k * jnp.exp2(b_tot - b)).astype(BF16)
        yield

        att = [jnp.zeros((c, c), F32) for _ in range(heads)]
        for li in range(nl):
            m = levels[li]
            ref_b = jnp.concatenate(
                [jnp.broadcast_to(bpad[DIAG + g0 + m - 1:DIAG + g0 + m, :], (2 * m, width))
                 for g0 in range(0, c, 2 * m)], axis=0)
            e = jnp.exp2(-jnp.abs(b - ref_b))
            qs = jnp.where(upper[li], q * e, 0.0).astype(BF16)
            ks = jnp.where(upper[li], 0.0, k * e).astype(BF16)
            for hh, lanes in enumerate(head_lanes):
                a_l = lax.dot_general(qs[:, lanes], ks[:, lanes], nt_dims, preferred_element_type=F32)
                att[hh] = att[hh] + jnp.where(same[li], a_l, 0.0)
        yield

        for dlt in range(DIAG):
            if dlt == 0:
                w = q * k
            else:
                ksh = kpad[DIAG - dlt:DIAG - dlt + c, :]
                bsh = bpad[DIAG - dlt:DIAG - dlt + c, :]
                w = q * ksh * jnp.exp2(b - bsh)
            for hh, lanes in enumerate(head_lanes):
                col = jnp.sum(w[:, lanes], axis=-1, keepdims=True)
                att[hh] = jnp.where(dmat == dlt, col, att[hh])
            if dlt in (DIAG // 2 - 1, DIAG - 1):
                yield

        decay_row = jnp.exp2(b_tot)
        outs = []
        for hh, lanes in enumerate(head_lanes):
            s_prev = s_ref[0, hh]
            o = jnp.dot(q_in[:, lanes], s_prev.astype(BF16), preferred_element_type=F32)
            o = o + jnp.dot(att[hh].astype(BF16), vb[:, lanes], preferred_element_type=F32)
            decay_col = jnp.broadcast_to(decay_row[:, lanes], (A_DK, A_DK)).T
            s_ref[0, hh] = decay_col * s_prev + lax.dot_general(
                k_out[:, lanes], vb[:, lanes], tn_dims, preferred_element_type=F32)
            outs.append(o * lax.rsqrt(jnp.mean(o * o, axis=-1, keepdims=True) + NORM_EPS))
        y = jnp.concatenate(outs, axis=1) * gn_ref[...] * _silu(g_ref[rows, :])
        ya_ref[rows, :] = y.astype(ya_ref.dtype)
        yield

    if not unroll:
        def body(ci, carry):
            for _ in chunk(ci):
                pass
            return carry

        lax.fori_loop(0, n_chunks, body, 0)
        return

    phases = [ph for ci in range(n_chunks) for ph in _phases_of(chunk(ci))]
    per_filler = -(-len(phases) // max(len(fillers), 1))
    pos = 0
    for fill in fillers:
        fill()
        for ph in phases[pos:pos + per_filler]:
            ph()
        pos += per_filler
    for ph in phases[pos:]:
        ph()


HGRN_PHASES = 5


def _phases_of(gen):
    return [functools.partial(next, gen, None) for _ in range(HGRN_PHASES)]


def _hgrn_specs(p, lbc_stack, layer, gn, heads, t_blk, row_block):
    aw = heads * A_DK
    c = min(t_blk, SCAN_CHUNK)
    sel = _hgrn_select(c)

    def col_spec(seg):
        return pl.BlockSpec((t_blk, aw), lambda *g: (row_block(*g), seg))

    in_specs = [col_spec(0), col_spec(1), col_spec(2), col_spec(3),
                pl.BlockSpec((None, SUBLANES, aw), lambda *g: (layer, 0, 0)),
                pl.BlockSpec((1, aw), lambda *g: (0, 0)),
                pl.BlockSpec(sel.shape, lambda *g: (0, 0))]
    args = [p, p, p, p, lbc_stack, gn.reshape(1, aw), sel]
    scratch = [pltpu.VMEM((c + DIAG, aw), F32), pltpu.VMEM((c + DIAG, aw), F32)]
    return c, in_specs, args, pl.BlockSpec((t_blk, aw), lambda *g: (row_block(*g), 0)), scratch


def _mlp_up_hgrn_kernel(x_ref, g_ref, sc_ref, sh_ref, w_ref, q_ref, f_ref, i_ref, gg_ref, lbc_ref, gn_ref, sel_ref,
                        o_ref, ya_ref, s_ref, h_scr, bpad, kpad, *, nseg, c, n_chunks, heads, steps_per_seq):
    j = pl.program_id(1)
    step = pl.program_id(0) * pl.num_programs(1) + j

    @pl.when(j == 0)
    def _():
        _norm_mod_rows(x_ref, g_ref, sc_ref, sh_ref, h_scr, nseg)

    @pl.when(step % steps_per_seq == 0)
    def _():
        s_ref[...] = jnp.zeros(s_ref.shape, F32)

    def sub_dot(c0):
        cols = slice(c0, c0 + EPILOGUE_COLS)
        y = jnp.maximum(jnp.dot(h_scr[...], w_ref[:, cols], preferred_element_type=F32), 0.0)
        o_ref[:, cols] = (y * y).astype(o_ref.dtype)

    fillers = [functools.partial(sub_dot, c0) for c0 in range(0, w_ref.shape[1], EPILOGUE_COLS)]
    _hgrn_chunks(q_ref, f_ref, i_ref, gg_ref, lbc_ref, gn_ref, sel_ref, ya_ref, s_ref, bpad, kpad,
                 c=c, n_chunks=n_chunks, heads=heads, unroll=True, fillers=fillers)


def _mlp_up_hgrn(x, g, mod6, w_stack, layer, seq_len, p_other, lbc_stack, gn, heads):
    n, d = x.shape
    m = w_stack.shape[2]
    tn = COL_TILE
    tm, nseg, tps = _row_tiling(n, seq_len, ROW_TILE)
    ni, nj = n // tm, m // tn
    n_other = p_other.shape[0]
    t_step = n_other // (ni * nj)
    assert t_step * ni * nj == n_other and seq_len % t_step == 0 and t_step % SUBLANES == 0
    steps_per_seq = seq_len // t_step
    aw = heads * A_DK
    c, h_in_specs, h_args, ya_spec, h_scratch = _hgrn_specs(
        p_other, lbc_stack, layer, gn, heads, t_step, lambda i, j: i * nj + j)

    def mod_spec(k):
        return pl.BlockSpec((None, nseg, 1, d), lambda i, j: (k, i // tps, 0, 0))

    state_spec = pl.BlockSpec((1, heads, A_DK, A_DK), lambda i, j: ((i * nj + j) // steps_per_seq, 0, 0, 0))
    return pl.pallas_call(
        functools.partial(_mlp_up_hgrn_kernel, nseg=nseg, c=c, n_chunks=t_step // c, heads=heads,
                          steps_per_seq=steps_per_seq),
        grid=(ni, nj),
        in_specs=[
            pl.BlockSpec((tm, d), lambda i, j: (i, 0)),
            pl.BlockSpec((1, d), lambda i, j: (0, 0)),
            mod_spec(4),
            mod_spec(3),
            pl.BlockSpec((None, d, tn), lambda i, j: (layer, 0, j)),
        ] + h_in_specs,
        out_specs=[pl.BlockSpec((tm, tn), lambda i, j: (i, j)), ya_spec, state_spec],
        out_shape=[jax.ShapeDtypeStruct((n, m), BF16),
                   jax.ShapeDtypeStruct((n_other, aw), BF16),
                   jax.ShapeDtypeStruct((n_other // seq_len, heads, A_DK, A_DK), F32)],
        scratch_shapes=[pltpu.VMEM((tm, d), BF16)] + h_scratch,
        compiler_params=_cparams(("arbitrary", "arbitrary")),
        name="mlp_up_hgrn2",
    )(x, g.reshape(1, d), mod6, mod6, w_stack, *h_args)


def _hgrn(p, lbc_stack, layer, gn, s0, bsz, seq_len, heads):
    c = min(seq_len, SCAN_CHUNK)
    t_blk = min(seq_len, 256)
    nt = seq_len // t_blk
    aw = heads * A_DK

    def col_spec(seg):
        return pl.BlockSpec((t_blk, aw), lambda b, t: (b * nt + t, seg))

    sel = _hgrn_select(c)
    in_specs = [col_spec(0), col_spec(1), col_spec(2), col_spec(3),
                pl.BlockSpec((None, SUBLANES, aw), lambda b, t: (layer, 0, 0)),
                pl.BlockSpec((1, aw), lambda b, t: (0, 0)),
                pl.BlockSpec(sel.shape, lambda b, t: (0, 0))]
    args = [p, p, p, p, lbc_stack, gn.reshape(1, aw), sel]
    state_spec = pl.BlockSpec((1, heads, A_DK, A_DK), lambda b, t: (b, 0, 0, 0))
    if s0 is not None:
        in_specs.append(pl.BlockSpec((None, 1, heads, A_DK, A_DK), lambda b, t: (layer, b, 0, 0, 0)))
        args.append(s0)
    return pl.pallas_call(
        functools.partial(_hgrn_kernel, c=c, n_chunks=t_blk // c, heads=heads, has_init=s0 is not None),
        grid=(bsz, nt),
        in_specs=in_specs,
        out_specs=[pl.BlockSpec((t_blk, aw), lambda b, t: (b * nt + t, 0)), state_spec],
        out_shape=[jax.ShapeDtypeStruct((bsz * seq_len, aw), BF16),
                   jax.ShapeDtypeStruct((bsz, heads, A_DK, A_DK), F32)],
        scratch_shapes=[pltpu.VMEM((c + DIAG, aw), F32), pltpu.VMEM((c + DIAG, aw), F32)],
        compiler_params=_cparams(("arbitrary", "arbitrary")),
        name="hgrn2",
    )(*args)


def _ssd_kernel(*refs, t, bw, has_init):
    if has_init:
        (z_ref, xs_ref, bc_ref, dt_ref, cw_ref, cb_ref, dtb_ref, alog_ref, dsk_ref, gn_ref, exp_ref, sel_ref,
         s0_ref, c0_ref, yb_ref, s_out_ref, conv_out_ref, xpad, st) = refs
    else:
        (z_ref, xs_ref, bc_ref, dt_ref, cw_ref, cb_ref, dtb_ref, alog_ref, dsk_ref, gn_ref, exp_ref, sel_ref,
         yb_ref, s_out_ref, conv_out_ref, xpad, st) = refs
    ti = pl.program_id(1)
    gw = bw // B_GROUPS
    n_bc = B_GROUPS * B_DSTATE
    pad = SUBLANES

    @pl.when(ti == 0)
    def _():
        if has_init:
            xpad[0:pad, :] = c0_ref[0]
            st[...] = s0_ref[0].reshape(bw, B_DSTATE).T
        else:
            xpad[0:pad, :] = jnp.zeros((pad, xpad.shape[1]), F32)
            st[...] = jnp.zeros(st.shape, F32)

    cur = jnp.concatenate([xs_ref[...], bc_ref[...]], axis=1)
    prev = xpad[...]
    row8 = lax.broadcasted_iota(jnp.int32, (pad, cur.shape[1]), 0)
    conv = cb_ref[...]
    for j in range(CONV_W):
        s = CONV_W - 1 - j
        if s == 0:
            tap = cur
        else:
            rolled = pltpu.roll(cur, s, 0)
            head = jnp.where(row8 < s, pltpu.roll(prev, s, 0), rolled[0:pad])
            tap = jnp.concatenate([head, rolled[pad:]], axis=0)
        conv = conv + tap * cw_ref[j:j + 1, :]
    xbc = _silu(conv)
    x = xbc[:, 0:bw]

    dt = _softplus(dt_ref[...] + dtb_ref[...])
    a = dt * (-jnp.exp(alog_ref[...]))
    cs = _dot3(sel_ref[...], a)
    a_cum = cs[0:t]
    a_tot = cs[t:2 * t]
    ex = _dot3_rhs(jnp.concatenate([dt, a_cum, a_tot - a_cum], axis=0), exp_ref[...])
    dt_e = ex[0:t]
    acum_e = ex[t:2 * t]
    dec_e = ex[2 * t:3 * t]
    atot_e = acum_e[t - 1:t, :]

    xdt = x * dt_e
    xw = (xdt * jnp.exp(dec_e)).astype(BF16)
    xdt_b = xdt.astype(BF16)
    a_cum_t = a_cum.T
    rr = lax.broadcasted_iota(jnp.int32, (t, t), 0)
    cc = lax.broadcasted_iota(jnp.int32, (t, t), 1)
    causal = cc <= rr
    lane = lax.broadcasted_iota(jnp.int32, (t, LANES), 1)
    heads_per_group = gw // B_HEADDIM
    pairs_per_group = gw // LANES

    y_groups = []
    for g in range(B_GROUPS):
        bg = xbc[:, bw + g * B_DSTATE:bw + (g + 1) * B_DSTATE].astype(BF16)
        cg = xbc[:, bw + n_bc + g * B_DSTATE:bw + n_bc + (g + 1) * B_DSTATE].astype(BF16)
        gs = slice(g * gw, (g + 1) * gw)
        st_g = st[:, gs]
        scores = lax.dot_general(cg, bg, (((1,), (1,)), ((), ())), preferred_element_type=F32)
        y_off = jnp.dot(cg, st_g.astype(BF16), preferred_element_type=F32) * jnp.exp(acum_e[:, gs])
        st[:, gs] = jnp.exp(atot_e[:, gs]) * st_g + lax.dot_general(
            bg, xw[:, gs], (((0,), (0,)), ((), ())), preferred_element_type=F32)
        y_pairs = []
        for pr in range(pairs_per_group):
            cols = slice(g * gw + pr * LANES, g * gw + (pr + 1) * LANES)
            xp = xdt_b[:, cols]
            acc = None
            for half in range(LANES // B_HEADDIM):
                h = g * heads_per_group + pr * (LANES // B_HEADDIM) + half
                diff = a_cum[:, h:h + 1] - a_cum_t[h:h + 1, :]
                lmat = jnp.where(causal, jnp.exp(jnp.where(causal, diff, 0.0)), 0.0)
                m = (scores * lmat).astype(BF16)
                in_half = (lane // B_HEADDIM) == half
                term = jnp.dot(m, jnp.where(in_half, xp, jnp.zeros_like(xp)), preferred_element_type=F32)
                acc = term if acc is None else acc + term
            y_pairs.append(acc)
        y_groups.append(jnp.concatenate(y_pairs, axis=1) + y_off)
    y = jnp.concatenate(y_groups, axis=1) + dsk_ref[...] * x
    y = y * _silu(z_ref[...])
    outs = []
    for g in range(B_GROUPS):
        yg = y[:, g * gw:(g + 1) * gw]
        outs.append(yg * lax.rsqrt(jnp.mean(yg * yg, axis=-1, keepdims=True) + NORM_EPS))
    yb_ref[...] = (jnp.concatenate(outs, axis=1) * gn_ref[...]).astype(yb_ref.dtype)

    tail = cur[t - pad:t, :]
    xpad[...] = tail

    @pl.when(ti == pl.num_programs(1) - 1)
    def _():
        conv_out_ref[0] = tail
        s_out_ref[0] = st[...].T.reshape(s_out_ref.shape[1:])


def _ssd(p, cols, prm, s0, c0, bsz, seq_len):
    bw = prm["bw"]
    heads = bw // B_HEADDIM
    n_bc = B_GROUPS * B_DSTATE
    cdim = bw + 2 * n_bc
    t = min(seq_len, 128)
    nt = seq_len // t
    tri = np.arange(t)[None, :] <= np.arange(t)[:, None]
    sel = jnp.asarray(np.concatenate([tri, np.ones((t, t), bool)], 0).astype(np.float32), dtype=BF16)
    expand = np.zeros((LANES, bw), np.float32)
    expand[np.arange(bw) // B_HEADDIM, np.arange(bw)] = 1.0
    expand = jnp.asarray(expand, dtype=BF16)

    def blk(width, off):
        return pl.BlockSpec((t, width), lambda b, ti: (b * nt + ti, off // width))

    def full(shape):
        return pl.BlockSpec(shape, lambda b, ti: (0,) * len(shape))

    in_specs = [blk(bw, cols["z"]), blk(bw, cols["xs"]), blk(2 * n_bc, cols["bc"]), blk(LANES, cols["dt"]),
                full((CONV_W, cdim)), full((1, cdim)), full((1, LANES)), full((1, LANES)),
                full((1, bw)), full((1, bw)), full(expand.shape), full(sel.shape)]
    args = [p, p, p, p, prm["conv_w"], prm["conv_b"], prm["dt_bias"], prm["a_log"], prm["d_skip"], prm["gn"],
            expand, sel]
    state_spec = pl.BlockSpec((1, heads, B_HEADDIM, B_DSTATE), lambda b, ti: (b, 0, 0, 0))
    conv_spec = pl.BlockSpec((1, SUBLANES, cdim), lambda b, ti: (b, 0, 0))
    if s0 is not None:
        layer = prm["layer"]
        in_specs += [pl.BlockSpec((None, 1, heads, B_HEADDIM, B_DSTATE), lambda b, ti: (layer, b, 0, 0, 0)),
                     pl.BlockSpec((None, 1, SUBLANES, cdim), lambda b, ti: (layer, b, 0, 0))]
        args += [s0, c0]
    return pl.pallas_call(
        functools.partial(_ssd_kernel, t=t, bw=bw, has_init=s0 is not None),
        grid=(bsz, nt),
        in_specs=in_specs,
        out_specs=[pl.BlockSpec((t, bw), lambda b, ti: (b * nt + ti, 0)), state_spec, conv_spec],
        out_shape=[jax.ShapeDtypeStruct((bsz * seq_len, bw), BF16),
                   jax.ShapeDtypeStruct((bsz, heads, B_HEADDIM, B_DSTATE), F32),
                   jax.ShapeDtypeStruct((bsz, SUBLANES, cdim), F32)],
        scratch_shapes=[pltpu.VMEM((SUBLANES, cdim), F32), pltpu.VMEM((B_DSTATE, bw), F32)],
        compiler_params=_cparams(("arbitrary", "arbitrary")),
        name="ssd",
    )(*args)


def _cmlp_kernel(u_ref, v_ref, lng_ref, lnb_ref, ws_ref, bst_ref, *out_refs, t, n_chunks, keep_v):
    yc_ref = out_refs[0]
    cw = u_ref.shape[1] // C_GROUPS
    rr = lax.broadcasted_iota(jnp.int32, (t, t), 0)
    cc = lax.broadcasted_iota(jnp.int32, (t, t), 1)
    wts = [jnp.where(cc <= rr, ws_ref[g, 0:t, 0:t], 0.0).astype(BF16) for g in range(C_GROUPS)]
    for ci in range(n_chunks):
        rows = slice(ci * t, (ci + 1) * t)
        u = _gelu(u_ref[rows, :])
        gv = _gelu(v_ref[rows, :])
        mu = jnp.mean(gv, axis=-1, keepdims=True)
        dv = gv - mu
        var = jnp.mean(dv * dv, axis=-1, keepdims=True)
        v = dv * lax.rsqrt(var + NORM_EPS) * lng_ref[...] + lnb_ref[...]
        if keep_v:
            out_refs[1][rows, :] = v
        vb = v.astype(BF16)
        for g in range(C_GROUPS):
            lanes = slice(g * cw, (g + 1) * cw)
            mixed = jnp.dot(wts[g], vb[:, lanes], preferred_element_type=F32) + bst_ref[0:t, g:g + 1]
            yc_ref[rows, lanes] = (u[:, lanes] * mixed).astype(yc_ref.dtype)


def _cmlp(p, col_u, col_v, prm, bsz, seq_len, keep_v):
    cw = prm["cw"]
    t = min(seq_len, CMLP_CHUNK)
    t_blk = min(seq_len, 4 * CMLP_CHUNK)
    n = bsz * seq_len

    def full(shape):
        return pl.BlockSpec(shape, lambda i: (0,) * len(shape))

    out_specs = [pl.BlockSpec((t_blk, cw), lambda i: (i, 0))]
    out_shape = [jax.ShapeDtypeStruct((n, cw), BF16)]
    if keep_v:
        out_specs.append(pl.BlockSpec((t_blk, cw), lambda i: (i, 0)))
        out_shape.append(jax.ShapeDtypeStruct((n, cw), F32))
    return pl.pallas_call(
        functools.partial(_cmlp_kernel, t=t, n_chunks=t_blk // t, keep_v=keep_v),
        grid=(n // t_blk,),
        in_specs=[pl.BlockSpec((t_blk, cw), lambda i: (i, col_u // cw)),
                  pl.BlockSpec((t_blk, cw), lambda i: (i, col_v // cw)),
                  full((1, cw)), full((1, cw)),
                  full((C_GROUPS, CMLP_CHUNK, CMLP_CHUNK)), full((CMLP_CHUNK, C_GROUPS))],
        out_specs=out_specs,
        out_shape=out_shape,
        compiler_params=_cparams(("arbitrary",)),
        name="cmlp",
    )(p, p, prm["ln_g"], prm["ln_b"], prm["ws"], prm["bs_t"])


def _run_trunk(x3, mod, st_hgrn, st_ssm, st_conv, keep_v, w):
    bsz, seq_len, d = x3.shape
    depth = mod.shape[0]
    aw = bw = cw = d // 2
    heads_a = aw // A_DK
    x = x3.reshape(bsz * seq_len, d)
    col = w["cols"]
    hgrn_out, ssm_out, conv_out, v_out = [], [], [], []
    for l in range(depth):
        mod6 = mod[l].reshape(bsz, 6, 1, d).transpose(1, 0, 2, 3)
        p = _in_proj(x, w["norm1_g"][l], mod6, w["w_in"], l, seq_len)

        y_a, s_h = _hgrn(p, w["lbc"], l, w["hgrn_onorm_g"][l], st_hgrn, bsz, seq_len, heads_a)
        ssd_prm = dict(bw=bw, layer=l, conv_w=w["ssm_conv_w"][l], conv_b=w["ssm_conv_b"][l][None],
                       dt_bias=w["dt_bias_pad"][l][None], a_log=w["a_log_pad"][l][None],
                       d_skip=w["d_skip"][l][None], gn=w["ssm_onorm_g"][l][None])
        y_b, s_s, conv_tail = _ssd(p, col, ssd_prm, st_ssm, st_conv, bsz, seq_len)
        cm_prm = dict(cw=cw, ln_g=w["cmlp_ln_g"][l][None], ln_b=w["cmlp_ln_b"][l][None],
                      ws=w["cmlp_ws"][l], bs_t=w["cmlp_bs"][l].T)
        c_res = _cmlp(p, col["u"], col["v"], cm_prm, bsz, seq_len, keep_v)
        merged = _merge(y_a, y_b, c_res[0], w["w_branch"], l, p, col["gate"], d)
        x = _proj_residual(merged, w["w_out"], l, x, mod6, 2, seq_len, "out_proj")
        hid = _mlp_up(x, w["norm2_g"][l], mod6, w["w_up"], l, seq_len)
        x = _proj_residual(hid, w["w_down"], l, x, mod6, 5, seq_len, "mlp_down")

        hgrn_out.append(s_h)
        ssm_out.append(s_s)
        conv_out.append(conv_tail[:, SUBLANES - (CONV_W - 1):, :])
        if keep_v:
            v_out.append(c_res[1].reshape(bsz, seq_len, cw))
    y = _final_norm(x, w["final_g"]).reshape(bsz, seq_len, d)
    return (y, jnp.stack(hgrn_out), jnp.stack(ssm_out), jnp.stack(conv_out),
            jnp.stack(v_out) if keep_v else None)


def _mlp_down_hgrn_kernel(a_ref, w_ref, x_ref, gate_ref, q_ref, f_ref, i_ref, gg_ref, lbc_ref, gn_ref, sel_ref,
                          o_ref, ya_ref, s_ref, acc_scr, bpad, kpad, *, nseg, c, n_chunks, heads, steps_per_seq):
    step = pl.program_id(0) * pl.num_programs(1) + pl.program_id(1)

    @pl.when(step % steps_per_seq == 0)
    def _():
        s_ref[...] = jnp.zeros(s_ref.shape, F32)

    kdim = a_ref.shape[1]
    kc = kdim // DOWN_K_SPLIT

    def sub_dot(ki):
        rows = slice(ki * kc, (ki + 1) * kc)
        part = jnp.dot(a_ref[:, rows], w_ref[rows, :], preferred_element_type=F32)
        if ki == 0:
            acc_scr[...] = part
        elif ki < DOWN_K_SPLIT - 1:
            acc_scr[...] += part
        else:
            x = x_ref[...]
            acc = acc_scr[...] + part
            o_ref[...] = (_per_seq(x, nseg) + gate_ref[...] * _per_seq(acc, nseg)).reshape(x.shape)

    fillers = [functools.partial(sub_dot, ki) for ki in range(DOWN_K_SPLIT)]
    _hgrn_chunks(q_ref, f_ref, i_ref, gg_ref, lbc_ref, gn_ref, sel_ref, ya_ref, s_ref, bpad, kpad,
                 c=c, n_chunks=n_chunks, heads=heads, unroll=True, fillers=fillers)


def _mlp_down_hgrn(a, w_stack, layer, x, mod6, k_gate, seq_len, p_other, lbc_stack, lbc_layer, gn, heads):
    n, kdim = a.shape
    d = w_stack.shape[2]
    tn = DOWN_TILE
    tm, nseg, tps = _row_tiling(n, seq_len, ROW_TILE)
    ni, nj = n // tm, d // tn
    n_other = p_other.shape[0]
    t_step = n_other // (ni * nj)
    assert t_step * ni * nj == n_other and seq_len % t_step == 0 and t_step % SUBLANES == 0
    assert kdim % DOWN_K_SPLIT == 0
    steps_per_seq = seq_len // t_step
    aw = heads * A_DK
    c, h_in_specs, h_args, ya_spec, h_scratch = _hgrn_specs(
        p_other, lbc_stack, lbc_layer, gn, heads, t_step, lambda i, j: i * nj + j)
    state_spec = pl.BlockSpec((1, heads, A_DK, A_DK), lambda i, j: ((i * nj + j) // steps_per_seq, 0, 0, 0))
    return pl.pallas_call(
        functools.partial(_mlp_down_hgrn_kernel, nseg=nseg, c=c, n_chunks=t_step // c, heads=heads,
                          steps_per_seq=steps_per_seq),
        grid=(ni, nj),
        in_specs=[
            pl.BlockSpec((tm, kdim), lambda i, j: (i, 0)),
            pl.BlockSpec((None, kdim, tn), lambda i, j: (layer, 0, j)),
            pl.BlockSpec((tm, tn), lambda i, j: (i, j)),
            pl.BlockSpec((None, nseg, 1, tn), lambda i, j: (k_gate, i // tps, 0, j)),
        ] + h_in_specs,
        out_specs=[pl.BlockSpec((tm, tn), lambda i, j: (i, j)), ya_spec, state_spec],
        out_shape=[jax.ShapeDtypeStruct((n, d), F32),
                   jax.ShapeDtypeStruct((n_other, aw), BF16),
                   jax.ShapeDtypeStruct((n_other // seq_len, heads, A_DK, A_DK), F32)],
        scratch_shapes=[pltpu.VMEM((tm, tn), F32)] + h_scratch,
        compiler_params=_cparams(("arbitrary", "arbitrary")),
        name="mlp_down_hgrn2",
    )(a, w_stack, x, mod6, *h_args)


def _run_trunk_halves(x3, mod, w):
    bsz, seq_len, d = x3.shape
    depth = mod.shape[0]
    hb = bsz // 2
    aw = bw = cw = d // 2
    heads_a = aw // A_DK
    col = w["cols"]
    n_half = hb * seq_len
    x_full = x3.reshape(bsz * seq_len, d)
    xs = [x_full, x_full]
    row0 = [0, n_half]
    mods = [mod[:, :hb], mod[:, hb:]]
    hgrn_out, ssm_out, conv_out = [], [], []
    pending = None
    for l in range(depth):
        mod6 = [m[l].reshape(hb, 6, 1, d).transpose(1, 0, 2, 3) for m in mods]
        ssd_prm = dict(bw=bw, layer=l, conv_w=w["ssm_conv_w"][l], conv_b=w["ssm_conv_b"][l][None],
                       dt_bias=w["dt_bias_pad"][l][None], a_log=w["a_log_pad"][l][None],
                       d_skip=w["d_skip"][l][None], gn=w["ssm_onorm_g"][l][None])
        cm_prm = dict(cw=cw, ln_g=w["cmlp_ln_g"][l][None], ln_b=w["cmlp_ln_b"][l][None],
                      ws=w["cmlp_ws"][l], bs_t=w["cmlp_bs"][l].T)
        gn_a = w["hgrn_onorm_g"][l]

        def mix_merge(h, p, y_a):
            y_b, s_s, conv_tail = _ssd(p, col, ssd_prm, None, None, hb, seq_len)
            y_c = _cmlp(p, col["u"], col["v"], cm_prm, hb, seq_len, False)[0]
            merged = _merge(y_a, y_b, y_c, w["w_branch"], l, p, col["gate"], d)
            x_new = _proj_residual(merged, w["w_out"], l, xs[h], mod6[h], 2, seq_len, "out_proj", row0[h])
            row0[h] = 0
            return x_new, s_s, conv_tail

        p0 = _in_proj(xs[0], w["norm1_g"][l], mod6[0], w["w_in"], l, seq_len, (row0[0], n_half))
        if pending is None:
            ya0, sh0 = _hgrn(p0, w["lbc"], l, gn_a, None, hb, seq_len, heads_a)
        else:
            hid1, x1_mid, mod6_prev = pending
            xs[1], ya0, sh0 = _mlp_down_hgrn(hid1, w["w_down"], l - 1, x1_mid, mod6_prev, 5, seq_len,
                                              p0, w["lbc"], l, gn_a, heads_a)
        p1 = _in_proj(xs[1], w["norm1_g"][l], mod6[1], w["w_in"], l, seq_len, (row0[1], n_half))
        x0, ss0, ct0 = mix_merge(0, p0, ya0)
        hid0, ya1, sh1 = _mlp_up_hgrn(x0, w["norm2_g"][l], mod6[0], w["w_up"], l, seq_len,
                                      p1, w["lbc"], gn_a, heads_a)
        xs[0] = _proj_residual(hid0, w["w_down"], l, x0, mod6[0], 5, seq_len, "mlp_down")
        x1_mid, ss1, ct1 = mix_merge(1, p1, ya1)
        hid1 = _mlp_up(x1_mid, w["norm2_g"][l], mod6[1], w["w_up"], l, seq_len)
        pending = (hid1, x1_mid, mod6[1])

        hgrn_out.append(jnp.concatenate([sh0, sh1], axis=0))
        ssm_out.append(jnp.concatenate([ss0, ss1], axis=0))
        conv_out.append(jnp.concatenate([ct0, ct1], axis=0)[:, SUBLANES - (CONV_W - 1):, :])
    hid1, x1_mid, mod6_prev = pending
    xs[1] = _proj_residual(hid1, w["w_down"], depth - 1, x1_mid, mod6_prev, 5, seq_len, "mlp_down")
    y = _final_norm_pair(xs[0], xs[1], w["final_g"]).reshape(bsz, seq_len, d)
    return (y, jnp.stack(hgrn_out), jnp.stack(ssm_out), jnp.stack(conv_out), None)


def kernel(x_prompt, x_sample, state_hgrn, state_ssm, state_conv, c_prompt, c_sample, norm1_g, norm2_g,
           w_mod, b_mod, w_in, hgrn_lb, hgrn_onorm_g, ssm_conv_w, ssm_conv_b, ssm_dt_bias, ssm_a_log, ssm_d,
           ssm_onorm_g, cmlp_ln_g, cmlp_ln_b, cmlp_ws, cmlp_bs, w_branch, w_out, w_up, w_down, final_g):
    d = x_prompt.shape[-1]
    depth = w_in.shape[0]
    aw = bw = cw = d // 2
    n_bc = B_GROUPS * B_DSTATE
    heads_b = bw // B_HEADDIM
    assert heads_b <= LANES and bw % LANES == 0 and DT_PAD % LANES == 0

    o_dt = 4 * aw + bw + bw + 2 * n_bc
    o_u = o_dt + heads_b
    w_in_r = jnp.concatenate(
        [w_in[:, :, :o_u].astype(BF16), jnp.zeros((depth, d, DT_PAD - heads_b), BF16),
         w_in[:, :, o_u:].astype(BF16)], axis=2)
    cols = dict(z=4 * aw, xs=4 * aw + bw, bc=4 * aw + 2 * bw, dt=o_dt)
    cols["u"] = o_dt + DT_PAD
    cols["v"] = cols["u"] + cw
    cols["gate"] = cols["v"] + cw
    assert cols["gate"] % MERGE_TILE == 0 and w_in_r.shape[2] % IN_TILE == 0

    pad_h = lambda a: jnp.pad(a.astype(F32), ((0, 0), (0, LANES - heads_b)))
    w = dict(
        cols=cols, w_in=w_in_r,
        norm1_g=norm1_g, norm2_g=norm2_g, final_g=final_g,
        lbc=_lb_consts(hgrn_lb), hgrn_onorm_g=hgrn_onorm_g,
        ssm_conv_w=ssm_conv_w, ssm_conv_b=ssm_conv_b,
        dt_bias_pad=pad_h(ssm_dt_bias), a_log_pad=pad_h(ssm_a_log),
        d_skip=jnp.repeat(ssm_d.astype(F32), B_HEADDIM, axis=1), ssm_onorm_g=ssm_onorm_g,
        cmlp_ln_g=cmlp_ln_g, cmlp_ln_b=cmlp_ln_b, cmlp_ws=cmlp_ws, cmlp_bs=cmlp_bs,
        w_branch=w_branch.astype(BF16).reshape(depth, N_BRANCH, aw, d), w_out=w_out.astype(BF16),
        w_up=w_up.astype(BF16), w_down=w_down.astype(BF16),
    )

    nb = x_prompt.shape[0]
    mod = _modulation(jnp.concatenate([c_prompt, c_sample], axis=0), w_mod, b_mod)
    if nb % 2 == 0:
        y_p, hgrn_p, ssm_p, conv_p, _ = _run_trunk_halves(x_prompt, mod[:, :nb], w)
    else:
        y_p, hgrn_p, ssm_p, conv_p, _ = _run_trunk(x_prompt, mod[:, :nb], None, None, None, False, w)
    conv_pad = jnp.pad(state_conv, ((0, 0), (0, 0), (SUBLANES - (CONV_W - 1), 0), (0, 0)))
    y_s, hgrn_s, ssm_s, conv_s, v_s = _run_trunk(x_sample, mod[:, nb:], state_hgrn, state_ssm, conv_pad, True, w)
    return (y_p, y_s, hgrn_p, ssm_p, conv_p, hgrn_s, ssm_s, conv_s, v_s)
```

```python
import functools
import math

import numpy as np
import jax
import jax.numpy as jnp
from jax import lax
from jax.experimental import pallas as pl
from jax.experimental.pallas import tpu as pltpu

F32 = jnp.float32
BF16 = jnp.bfloat16

A_DK = 128
B_HEADDIM = 64
B_GROUPS = 2
B_DSTATE = 128
CONV_W = 4
C_GROUPS = 4
CMLP_CHUNK = 128
N_BRANCH = 3
SCAN_CHUNK = 64
NORM_EPS = 1e-6
LB_FLOOR = 1e-30
LOG2E = 1.4426950408889634

LANES = 128
SUBLANES = 8
VMEM_LIMIT = 56 * 1024 * 1024
DIAG = SUBLANES
DT_PAD = 512
ROW_TILE = 1024
COL_TILE = 1024
IN_TILE = 1536
MERGE_TILE = 512
DOWN_TILE = 256
EPILOGUE_COLS = 256
DOWN_K_SPLIT = 4
NORM_ROWS = 128


def _cparams(sem):
    return pltpu.CompilerParams(dimension_semantics=sem, vmem_limit_bytes=VMEM_LIMIT)


def _split3(x):
    hi = x.astype(BF16)
    r1 = x - hi.astype(F32)
    mid = r1.astype(BF16)
    lo = (r1 - mid.astype(F32)).astype(BF16)
    return hi, mid, lo


def _dot3(sel, x):
    hi, mid, lo = _split3(x)
    d = lambda p: jnp.dot(sel, p, preferred_element_type=F32)
    return d(hi) + d(mid) + d(lo)


def _dot3_rhs(x, sel):
    hi, mid, lo = _split3(x)
    d = lambda p: jnp.dot(p, sel, preferred_element_type=F32)
    return d(hi) + d(mid) + d(lo)


def _silu(x):
    return x / (1.0 + jnp.exp(-x))


def _softplus(x):
    return jnp.maximum(x, 0.0) + jnp.log1p(jnp.exp(-jnp.abs(x)))


def _gelu(x):
    return 0.5 * x * (1.0 + lax.erf(x * (1.0 / math.sqrt(2.0))))


def _lb_kernel(lb_ref, out_ref, *, depth):
    x = lb_ref[...]
    m = jnp.max(x, axis=0, keepdims=True)
    e = jnp.exp(x - m)
    p = e / jnp.sum(e, axis=0, keepdims=True)
    acc = jnp.zeros_like(p[0:1])
    zeros5 = jnp.zeros((SUBLANES - 3, x.shape[1]), F32)
    for l in range(depth):
        acc = acc + p[l:l + 1]
        lb = acc - p[0:1]
        out_ref[l] = jnp.concatenate(
            [jnp.log(jnp.maximum(lb, LB_FLOOR)), jnp.log1p(-lb), 1.0 - lb, zeros5], axis=0)


def _lb_consts(hgrn_lb):
    depth, aw = hgrn_lb.shape
    return pl.pallas_call(
        functools.partial(_lb_kernel, depth=depth),
        out_shape=jax.ShapeDtypeStruct((depth, SUBLANES, aw), F32),
        name="hgrn_lb",
    )(hgrn_lb.astype(F32))


def _mod_kernel(c_ref, w_ref, b_ref, o_ref):
    cs = _silu(c_ref[...]).astype(BF16)
    o_ref[...] = jnp.dot(cs, w_ref[...].astype(BF16), preferred_element_type=F32) + b_ref[...]


def _modulation(c_all, w_mod, b_mod):
    depth, d, n6 = w_mod.shape
    s = c_all.shape[0]
    tn = 1024
    return pl.pallas_call(
        _mod_kernel,
        grid=(depth, n6 // tn),
        in_specs=[
            pl.BlockSpec((s, d), lambda l, j: (0, 0)),
            pl.BlockSpec((None, d, tn), lambda l, j: (l, 0, j)),
            pl.BlockSpec((None, 1, tn), lambda l, j: (l, 0, j)),
        ],
        out_specs=pl.BlockSpec((None, s, tn), lambda l, j: (l, 0, j)),
        out_shape=jax.ShapeDtypeStruct((depth, s, n6), F32),
        compiler_params=_cparams(("arbitrary", "arbitrary")),
        name="adaln_mod",
    )(c_all, w_mod, b_mod.reshape(depth, 1, n6))


def _row_tiling(n_tok, seq_len, tm_max):
    tm = min(tm_max, n_tok)
    if seq_len >= tm:
        assert seq_len % tm == 0
        return tm, 1, seq_len // tm
    assert tm % seq_len == 0
    return tm, tm // seq_len, 1


def _per_seq(x, nseg):
    tm, d = x.shape
    return x.reshape(nseg, tm // nseg, d)


def _final_norm_kernel(x_ref, g_ref, o_ref):
    x = x_ref[...]
    o_ref[...] = x * lax.rsqrt(jnp.mean(x * x, axis=-1, keepdims=True) + NORM_EPS) * g_ref[...]


def _final_norm(x, g):
    n, d = x.shape
    tm = min(512, n)
    return pl.pallas_call(
        _final_norm_kernel,
        grid=(n // tm,),
        in_specs=[pl.BlockSpec((tm, d), lambda i: (i, 0)), pl.BlockSpec((1, d), lambda i: (0, 0))],
        out_specs=pl.BlockSpec((tm, d), lambda i: (i, 0)),
        out_shape=jax.ShapeDtypeStruct((n, d), F32),
        compiler_params=_cparams(("arbitrary",)),
        name="final_norm",
    )(x, g.reshape(1, d))


def _final_norm_pair_kernel(xa_ref, xb_ref, g_ref, o_ref, *, nt):
    def norm(x):
        return x * lax.rsqrt(jnp.mean(x * x, axis=-1, keepdims=True) + NORM_EPS) * g_ref[...]

    @pl.when(pl.program_id(0) < nt)
    def _():
        o_ref[...] = norm(xa_ref[...])

    @pl.when(pl.program_id(0) >= nt)
    def _():
        o_ref[...] = norm(xb_ref[...])


def _final_norm_pair(xa, xb, g):
    n, d = xa.shape
    tm = min(512, n)
    nt = n // tm
    return pl.pallas_call(
        functools.partial(_final_norm_pair_kernel, nt=nt),
        grid=(2 * nt,),
        in_specs=[pl.BlockSpec((tm, d), lambda i: (jnp.minimum(i, nt - 1), 0)),
                  pl.BlockSpec((tm, d), lambda i: (jnp.maximum(i - nt, 0), 0)),
                  pl.BlockSpec((1, d), lambda i: (0, 0))],
        out_specs=pl.BlockSpec((tm, d), lambda i: (i, 0)),
        out_shape=jax.ShapeDtypeStruct((2 * n, d), F32),
        compiler_params=_cparams(("arbitrary",)),
        name="final_norm",
    )(xa, xb, g.reshape(1, d))


def _norm_mod_rows(x_ref, g_ref, sc_ref, sh_ref, h_scr, nseg):
    tm = x_ref.shape[0]
    seg = tm // nseg
    rb = min(seg, NORM_ROWS)
    g = g_ref[...]

    def body(r, carry):
        rows = pl.ds(pl.multiple_of(r * rb, rb), rb)
        s = r // (seg // rb)
        x = x_ref[rows, :]
        gain = g * (1.0 + sc_ref[s])
        inv = lax.rsqrt(jnp.mean(x * x, axis=-1, keepdims=True) + NORM_EPS)
        h_scr[rows, :] = (x * inv * gain + sh_ref[s]).astype(BF16)
        return carry

    lax.fori_loop(0, tm // rb, body, 0, unroll=2 if (tm // rb) % 2 == 0 else 1)


def _mlp_up_kernel(x_ref, g_ref, sc_ref, sh_ref, w_ref, o_ref, h_scr, *, nseg):
    @pl.when(pl.program_id(1) == 0)
    def _():
        _norm_mod_rows(x_ref, g_ref, sc_ref, sh_ref, h_scr, nseg)

    y = jnp.maximum(jnp.dot(h_scr[...], w_ref[...], preferred_element_type=F32), 0.0)
    o_ref[...] = (y * y).astype(o_ref.dtype)


def _in_proj_kernel(x_ref, g_ref, sc_ref, sh_ref, w_ref, p_ref, gate_ref, h_scr, *, nseg, n_main):
    j = pl.program_id(1)

    @pl.when(j == 0)
    def _():
        _norm_mod_rows(x_ref, g_ref, sc_ref, sh_ref, h_scr, nseg)

    @pl.when(j < n_main)
    def _():
        p_ref[...] = jnp.dot(h_scr[...], w_ref[...], preferred_element_type=F32)

    @pl.when(j >= n_main)
    def _():
        gate_ref[...] = jnp.dot(h_scr[...], w_ref[...], preferred_element_type=F32).astype(gate_ref.dtype)


def _norm_proj_call(kern, x, g, mod6, k_scale, k_shift, w_stack, layer, seq_len, tn, out_specs, out_shape, name):
    n, d = x.shape
    m = w_stack.shape[2]
    tm, nseg, tps = _row_tiling(n, seq_len, ROW_TILE)

    def mod_spec(k):
        return pl.BlockSpec((None, nseg, 1, d), lambda i, j: (k, i // tps, 0, 0))

    return pl.pallas_call(
        functools.partial(kern, nseg=nseg),
        grid=(n // tm, m // tn),
        in_specs=[
            pl.BlockSpec((tm, d), lambda i, j: (i, 0)),
            pl.BlockSpec((1, d), lambda i, j: (0, 0)),
            mod_spec(k_scale),
            mod_spec(k_shift),
            pl.BlockSpec((None, d, tn), lambda i, j: (layer, 0, j)),
        ],
        out_specs=out_specs(tm),
        out_shape=out_shape,
        scratch_shapes=[pltpu.VMEM((tm, d), BF16)],
        compiler_params=_cparams(("arbitrary", "arbitrary")),
        name=name,
    )(x, g.reshape(1, d), mod6, mod6, w_stack)


def _mlp_up(x, g, mod6, w_stack, layer, seq_len):
    n = x.shape[0]
    m = w_stack.shape[2]
    tn = COL_TILE
    return _norm_proj_call(
        _mlp_up_kernel, x, g, mod6, 4, 3, w_stack, layer, seq_len, tn,
        lambda tm: pl.BlockSpec((tm, tn), lambda i, j: (i, j)),
        jax.ShapeDtypeStruct((n, m), BF16), "mlp_up")


def _in_proj(x, g, mod6, w_stack, layer, seq_len, main_cols, rows=None):
    d = x.shape[1]
    row0, n = rows if rows is not None else (0, x.shape[0])
    m = w_stack.shape[2]
    tn = IN_TILE
    n_main = main_cols // tn
    tm, nseg, tps = _row_tiling(n, seq_len, ROW_TILE)

    def mod_spec(k):
        return pl.BlockSpec((None, nseg, 1, d), lambda i, j: (k, i // tps, 0, 0))

    return pl.pallas_call(
        functools.partial(_in_proj_kernel, nseg=nseg, n_main=n_main),
        grid=(n // tm, m // tn),
        in_specs=[
            pl.BlockSpec((tm, d), lambda i, j: (i + row0 // tm, 0)),
            pl.BlockSpec((1, d), lambda i, j: (0, 0)),
            mod_spec(1),
            mod_spec(0),
            pl.BlockSpec((None, d, tn), lambda i, j: (layer, 0, j)),
        ],
        out_specs=[pl.BlockSpec((tm, tn), lambda i, j: (i, jnp.minimum(j, n_main - 1))),
                   pl.BlockSpec((tm, tn), lambda i, j: (i, jnp.maximum(j - n_main, 0)))],
        out_shape=[jax.ShapeDtypeStruct((n, main_cols), F32), jax.ShapeDtypeStruct((n, m - main_cols), BF16)],
        scratch_shapes=[pltpu.VMEM((tm, d), BF16)],
        compiler_params=_cparams(("arbitrary", "arbitrary")),
        name="in_proj",
    )(x, g.reshape(1, d), mod6, mod6, w_stack)


def _proj_res_kernel(a_ref, w_ref, x_ref, gate_ref, o_ref, *, nseg, nk):
    part = jnp.dot(a_ref[...], w_ref[...], preferred_element_type=F32)

    def finish(acc):
        x = x_ref[...]
        o_ref[...] = (_per_seq(x, nseg) + gate_ref[...] * _per_seq(acc, nseg)).reshape(x.shape)

    if nk == 1:
        finish(part)
        return
    k = pl.program_id(2)

    @pl.when(k == 0)
    def _():
        o_ref[...] = part

    @pl.when((k > 0) & (k < nk - 1))
    def _():
        o_ref[...] += part

    @pl.when(k == nk - 1)
    def _():
        finish(o_ref[...] + part)


def _proj_residual(a, w_stack, layer, x, mod6, k_gate, seq_len, name, x_row0=0):
    n, kdim = a.shape
    d = w_stack.shape[2]
    tn = COL_TILE if kdim <= d else DOWN_TILE
    tk = kdim
    nk = kdim // tk
    tm, nseg, tps = _row_tiling(n, seq_len, ROW_TILE)
    return pl.pallas_call(
        functools.partial(_proj_res_kernel, nseg=nseg, nk=nk),
        grid=(n // tm, d // tn, nk),
        in_specs=[
            pl.BlockSpec((tm, tk), lambda i, j, k: (i, k)),
            pl.BlockSpec((None, tk, tn), lambda i, j, k: (layer, k, j)),
            pl.BlockSpec((tm, tn), lambda i, j, k: (i + x_row0 // tm, j)),
            pl.BlockSpec((None, nseg, 1, tn), lambda i, j, k: (k_gate, i // tps, 0, j)),
        ],
        out_specs=pl.BlockSpec((tm, tn), lambda i, j, k: (i, j)),
        out_shape=jax.ShapeDtypeStruct((n, d), F32),
        compiler_params=_cparams(("arbitrary", "arbitrary", "arbitrary")),
        name=name,
    )(a, w_stack, x, mod6)


def _merge_kernel(ya_ref, yb_ref, yc_ref, wa_ref, wb_ref, wc_ref, ga_ref, gb_ref, gc_ref, o_ref):
    acc = None
    for y_ref, w_ref, g_ref in ((ya_ref, wa_ref, ga_ref), (yb_ref, wb_ref, gb_ref), (yc_ref, wc_ref, gc_ref)):
        gate = 1.0 / (1.0 + jnp.exp(-g_ref[...].astype(F32)))
        term = gate * jnp.dot(y_ref[...], w_ref[...], preferred_element_type=F32)
        acc = term if acc is None else acc + term
    o_ref[...] = acc.astype(o_ref.dtype)


def _merge(ya, yb, yc, w_stack, layer, gates, d):
    n, bw = ya.shape
    tn = MERGE_TILE
    tm = min(ROW_TILE, n)
    gsteps = d // tn
    y_spec = pl.BlockSpec((tm, bw), lambda i, j: (i, 0))

    def w_spec(k):
        return pl.BlockSpec((None, None, bw, tn), lambda i, j: (layer, k, 0, j))

    def g_spec(k):
        return pl.BlockSpec((tm, tn), lambda i, j: (i, k * gsteps + j))

    return pl.pallas_call(
        _merge_kernel,
        grid=(n // tm, d // tn),
        in_specs=[y_spec, y_spec, y_spec, w_spec(0), w_spec(1), w_spec(2), g_spec(0), g_spec(1), g_spec(2)],
        out_specs=pl.BlockSpec((tm, tn), lambda i, j: (i, j)),
        out_shape=jax.ShapeDtypeStruct((n, d), BF16),
        compiler_params=_cparams(("arbitrary", "arbitrary")),
        name="merge",
    )(ya, yb, yc, w_stack, w_stack, w_stack, gates, gates, gates)


def _hgrn_levels(c):
    lv, m = [], c // 2
    while m >= DIAG:
        lv.append(m)
        m //= 2
    return tuple(lv)


def _hgrn_select(c):
    tri = np.arange(c)[None, :] <= np.arange(c)[:, None]
    return jnp.asarray(tri.astype(np.float32), dtype=BF16)


def _hgrn_kernel(*refs, c, n_chunks, heads, has_init):
    if has_init:
        q_ref, f_ref, i_ref, g_ref, lbc_ref, gn_ref, sel_ref, s0_ref, ya_ref, s_ref, bpad, kpad = refs
    else:
        q_ref, f_ref, i_ref, g_ref, lbc_ref, gn_ref, sel_ref, ya_ref, s_ref, bpad, kpad = refs

    @pl.when(pl.program_id(1) == 0)
    def _():
        s_ref[...] = s0_ref[...] if has_init else jnp.zeros(s_ref.shape, F32)

    _hgrn_chunks(q_ref, f_ref, i_ref, g_ref, lbc_ref, gn_ref, sel_ref, ya_ref, s_ref, bpad, kpad,
                 c=c, n_chunks=n_chunks, heads=heads, unroll=False)


def _hgrn_chunks(q_ref, f_ref, i_ref, g_ref, lbc_ref, gn_ref, sel_ref, ya_ref, s_ref, bpad, kpad,
                 *, c, n_chunks, heads, unroll, fillers=()):
    levels = _hgrn_levels(c)
    nl = len(levels)
    width = heads * A_DK
    head_lanes = [slice(hh * A_DK, (hh + 1) * A_DK) for hh in range(heads)]

    bpad[0:DIAG, :] = jnp.zeros((DIAG, width), F32)
    kpad[0:DIAG, :] = jnp.zeros((DIAG, width), F32)

    sel = sel_ref[...]
    log_lb = lbc_ref[0:1, :]
    log1m_lb = lbc_ref[1:2, :]
    one_m_lb = lbc_ref[2:3, :]
    row = lax.broadcasted_iota(jnp.int32, (c, width), 0)
    rr = lax.broadcasted_iota(jnp.int32, (c, c), 0)
    cc = lax.broadcasted_iota(jnp.int32, (c, c), 1)
    upper = [(row & m) != 0 for m in levels]
    same = [(rr ^ cc) < 2 * m for m in levels]
    dmat = jnp.where(((rr ^ cc) < DIAG) & (cc <= rr), rr - cc, -1)
    nt_dims = (((1,), (1,)), ((), ()))
    tn_dims = (((0,), (0,)), ((), ()))

    def chunk(ci):
        rows = slice(ci * c, (ci + 1) * c) if unroll else pl.ds(pl.multiple_of(ci * c, c), c)
        z = f_ref[rows, :]
        aq = q_ref[rows, :]

        log_sig = jnp.minimum(z, 0.0) - jnp.log(1.0 + jnp.exp(-jnp.abs(z)))
        bb = log1m_lb + log_sig
        log_f = jnp.maximum(log_lb, bb) + jnp.log(1.0 + jnp.exp(-jnp.abs(log_lb - bb)))
        k = one_m_lb / (1.0 + jnp.exp(z))
        q = _silu(aq)

        b = _dot3(sel, log_f * LOG2E)
        bpad[DIAG:DIAG + c, :] = b
        kpad[DIAG:DIAG + c, :] = k
        b_tot = bpad[DIAG + c - 1:DIAG + c, :]
        vb = i_ref[rows, :].astype(BF16)
        q_in = (q * jnp.exp2(b)).astype(BF16)
        k_out = (k * jnp.exp2(b_tot - b)).astype(BF16)
        yield

        att = [jnp.zeros((c, c), F32) for _ in range(heads)]
        for li in range(nl):
            m = levels[li]
            ref_b = jnp.concatenate(
                [jnp.broadcast_to(bpad[DIAG + g0 + m - 1:DIAG + g0 + m, :], (2 * m, width))
                 for g0 in range(0, c, 2 * m)], axis=0)
            e = jnp.exp2(-jnp.abs(b - ref_b))
            qs = jnp.where(upper[li], q * e, 0.0).astype(BF16)
            ks = jnp.where(upper[li], 0.0, k * e).astype(BF16)
            for hh, lanes in enumerate(head_lanes):
                a_l = lax.dot_general(qs[:, lanes], ks[:, lanes], nt_dims, preferred_element_type=F32)
                att[hh] = att[hh] + jnp.where(same[li], a_l, 0.0)
        yield

        for dlt in range(DIAG):
            if dlt == 0:
                w = q * k
            else:
                ksh = kpad[DIAG - dlt:DIAG - dlt + c, :]
                bsh = bpad[DIAG - dlt:DIAG - dlt + c, :]
                w = q * ksh * jnp.exp2(b - bsh)
            for hh, lanes in enumerate(head_lanes):
                col = jnp.sum(w[:, lanes], axis=-1, keepdims=True)
                att[hh] = jnp.where(dmat == dlt, col, att[hh])
            if dlt in (DIAG // 2 - 1, DIAG - 1):
                yield

        decay_row = jnp.exp2(b_tot)
        outs = []
        for hh, lanes in enumerate(head_lanes):
            s_prev = s_ref[0, hh]
            o = jnp.dot(q_in[:, lanes], s_prev.astype(BF16), preferred_element_type=F32)
            o = o + jnp.dot(att[hh].astype(BF16), vb[:, lanes], preferred_element_type=F32)
            decay_col = jnp.broadcast_to(decay_row[:, lanes], (A_DK, A_DK)).T
            s_ref[0, hh] = decay_col * s_prev + lax.dot_general(
                k_out[:, lanes], vb[:, lanes], tn_dims, preferred_element_type=F32)
            outs.append(o * lax.rsqrt(jnp.mean(o * o, axis=-1, keepdims=True) + NORM_EPS))
        y = jnp.concatenate(outs, axis=1) * gn_ref[...] * _silu(g_ref[rows, :])
        ya_ref[rows, :] = y.astype(ya_ref.dtype)
        yield

    if not unroll:
        def body(ci, carry):
            for _ in chunk(ci):
                pass
            return carry

        lax.fori_loop(0, n_chunks, body, 0)
        return

    phases = [ph for ci in range(n_chunks) for ph in _phases_of(chunk(ci))]
    per_filler = -(-len(phases) // max(len(fillers), 1))
    pos = 0
    for fill in fillers:
        fill()
        for ph in phases[pos:pos + per_filler]:
            ph()
        pos += per_filler
    for ph in phases[pos:]:
        ph()


HGRN_PHASES = 5


def _phases_of(gen):
    return [functools.partial(next, gen, None) for _ in range(HGRN_PHASES)]


def _hgrn_specs(p, lbc_stack, layer, gn, heads, t_blk, row_block):
    aw = heads * A_DK
    c = min(t_blk, SCAN_CHUNK)
    sel = _hgrn_select(c)

    def col_spec(seg):
        return pl.BlockSpec((t_blk, aw), lambda *g: (row_block(*g), seg))

    in_specs = [col_spec(0), col_spec(1), col_spec(2), col_spec(3),
                pl.BlockSpec((None, SUBLANES, aw), lambda *g: (layer, 0, 0)),
                pl.BlockSpec((1, aw), lambda *g: (0, 0)),
                pl.BlockSpec(sel.shape, lambda *g: (0, 0))]
    args = [p, p, p, p, lbc_stack, gn.reshape(1, aw), sel]
    scratch = [pltpu.VMEM((c + DIAG, aw), F32), pltpu.VMEM((c + DIAG, aw), F32)]
    return c, in_specs, args, pl.BlockSpec((t_blk, aw), lambda *g: (row_block(*g), 0)), scratch


def _mlp_up_hgrn_kernel(x_ref, g_ref, sc_ref, sh_ref, w_ref, q_ref, f_ref, i_ref, gg_ref, lbc_ref, gn_ref, sel_ref,
                        o_ref, ya_ref, s_ref, h_scr, bpad, kpad, *, nseg, c, n_chunks, heads, steps_per_seq):
    j = pl.program_id(1)
    step = pl.program_id(0) * pl.num_programs(1) + j

    @pl.when(j == 0)
    def _():
        _norm_mod_rows(x_ref, g_ref, sc_ref, sh_ref, h_scr, nseg)

    @pl.when(step % steps_per_seq == 0)
    def _():
        s_ref[...] = jnp.zeros(s_ref.shape, F32)

    def sub_dot(c0):
        cols = slice(c0, c0 + EPILOGUE_COLS)
        y = jnp.maximum(jnp.dot(h_scr[...], w_ref[:, cols], preferred_element_type=F32), 0.0)
        o_ref[:, cols] = (y * y).astype(o_ref.dtype)

    fillers = [functools.partial(sub_dot, c0) for c0 in range(0, w_ref.shape[1], EPILOGUE_COLS)]
    _hgrn_chunks(q_ref, f_ref, i_ref, gg_ref, lbc_ref, gn_ref, sel_ref, ya_ref, s_ref, bpad, kpad,
                 c=c, n_chunks=n_chunks, heads=heads, unroll=True, fillers=fillers)


def _mlp_up_hgrn(x, g, mod6, w_stack, layer, seq_len, p_other, lbc_stack, gn, heads):
    n, d = x.shape
    m = w_stack.shape[2]
    tn = COL_TILE
    tm, nseg, tps = _row_tiling(n, seq_len, ROW_TILE)
    ni, nj = n // tm, m // tn
    n_other = p_other.shape[0]
    t_step = n_other // (ni * nj)
    assert t_step * ni * nj == n_other and seq_len % t_step == 0 and t_step % SUBLANES == 0
    steps_per_seq = seq_len // t_step
    aw = heads * A_DK
    c, h_in_specs, h_args, ya_spec, h_scratch = _hgrn_specs(
        p_other, lbc_stack, layer, gn, heads, t_step, lambda i, j: i * nj + j)

    def mod_spec(k):
        return pl.BlockSpec((None, nseg, 1, d), lambda i, j: (k, i // tps, 0, 0))

    state_spec = pl.BlockSpec((1, heads, A_DK, A_DK), lambda i, j: ((i * nj + j) // steps_per_seq, 0, 0, 0))
    return pl.pallas_call(
        functools.partial(_mlp_up_hgrn_kernel, nseg=nseg, c=c, n_chunks=t_step // c, heads=heads,
                          steps_per_seq=steps_per_seq),
        grid=(ni, nj),
        in_specs=[
            pl.BlockSpec((tm, d), lambda i, j: (i, 0)),
            pl.BlockSpec((1, d), lambda i, j: (0, 0)),
            mod_spec(4),
            mod_spec(3),
            pl.BlockSpec((None, d, tn), lambda i, j: (layer, 0, j)),
        ] + h_in_specs,
        out_specs=[pl.BlockSpec((tm, tn), lambda i, j: (i, j)), ya_spec, state_spec],
        out_shape=[jax.ShapeDtypeStruct((n, m), BF16),
                   jax.ShapeDtypeStruct((n_other, aw), BF16),
                   jax.ShapeDtypeStruct((n_other // seq_len, heads, A_DK, A_DK), F32)],
        scratch_shapes=[pltpu.VMEM((tm, d), BF16)] + h_scratch,
        compiler_params=_cparams(("arbitrary", "arbitrary")),
        name="mlp_up_hgrn2",
    )(x, g.reshape(1, d), mod6, mod6, w_stack, *h_args)


def _hgrn(p, lbc_stack, layer, gn, s0, bsz, seq_len, heads):
    c = min(seq_len, SCAN_CHUNK)
    t_blk = min(seq_len, 256)
    nt = seq_len // t_blk
    aw = heads * A_DK

    def col_spec(seg):
        return pl.BlockSpec((t_blk, aw), lambda b, t: (b * nt + t, seg))

    sel = _hgrn_select(c)
    in_specs = [col_spec(0), col_spec(1), col_spec(2), col_spec(3),
                pl.BlockSpec((None, SUBLANES, aw), lambda b, t: (layer, 0, 0)),
                pl.BlockSpec((1, aw), lambda b, t: (0, 0)),
                pl.BlockSpec(sel.shape, lambda b, t: (0, 0))]
    args = [p, p, p, p, lbc_stack, gn.reshape(1, aw), sel]
    state_spec = pl.BlockSpec((1, heads, A_DK, A_DK), lambda b, t: (b, 0, 0, 0))
    if s0 is not None:
        in_specs.append(pl.BlockSpec((None, 1, heads, A_DK, A_DK), lambda b, t: (layer, b, 0, 0, 0)))
        args.append(s0)
    return pl.pallas_call(
        functools.partial(_hgrn_kernel, c=c, n_chunks=t_blk // c, heads=heads, has_init=s0 is not None),
        grid=(bsz, nt),
        in_specs=in_specs,
        out_specs=[pl.BlockSpec((t_blk, aw), lambda b, t: (b * nt + t, 0)), state_spec],
        out_shape=[jax.ShapeDtypeStruct((bsz * seq_len, aw), BF16),
                   jax.ShapeDtypeStruct((bsz, heads, A_DK, A_DK), F32)],
        scratch_shapes=[pltpu.VMEM((c + DIAG, aw), F32), pltpu.VMEM((c + DIAG, aw), F32)],
        compiler_params=_cparams(("arbitrary", "arbitrary")),
        name="hgrn2",
    )(*args)


def _ssd_kernel(*refs, t, bw, has_init):
    if has_init:
        (z_ref, xs_ref, bc_ref, dt_ref, cw_ref, cb_ref, dtb_ref, alog_ref, dsk_ref, gn_ref, exp_ref, sel_ref,
         s0_ref, c0_ref, yb_ref, s_out_ref, conv_out_ref, xpad, st) = refs
    else:
        (z_ref, xs_ref, bc_ref, dt_ref, cw_ref, cb_ref, dtb_ref, alog_ref, dsk_ref, gn_ref, exp_ref, sel_ref,
         yb_ref, s_out_ref, conv_out_ref, xpad, st) = refs
    ti = pl.program_id(1)
    gw = bw // B_GROUPS
    n_bc = B_GROUPS * B_DSTATE
    pad = SUBLANES

    @pl.when(ti == 0)
    def _():
        if has_init:
            xpad[0:pad, :] = c0_ref[0]
            st[...] = s0_ref[0].reshape(bw, B_DSTATE).T
        else:
            xpad[0:pad, :] = jnp.zeros((pad, xpad.shape[1]), F32)
            st[...] = jnp.zeros(st.shape, F32)

    cur = jnp.concatenate([xs_ref[...], bc_ref[...]], axis=1)
    prev = xpad[...]
    row8 = lax.broadcasted_iota(jnp.int32, (pad, cur.shape[1]), 0)
    conv = cb_ref[...]
    for j in range(CONV_W):
        s = CONV_W - 1 - j
        if s == 0:
            tap = cur
        else:
            rolled = pltpu.roll(cur, s, 0)
            head = jnp.where(row8 < s, pltpu.roll(prev, s, 0), rolled[0:pad])
            tap = jnp.concatenate([head, rolled[pad:]], axis=0)
        conv = conv + tap * cw_ref[j:j + 1, :]
    xbc = _silu(conv)
    x = xbc[:, 0:bw]

    dt = _softplus(dt_ref[...] + dtb_ref[...])
    a = dt * (-jnp.exp(alog_ref[...]))
    cs = _dot3(sel_ref[...], a)
    a_cum = cs[0:t]
    a_tot = cs[t:2 * t]
    ex = _dot3_rhs(jnp.concatenate([dt, a_cum, a_tot - a_cum], axis=0), exp_ref[...])
    dt_e = ex[0:t]
    acum_e = ex[t:2 * t]
    dec_e = ex[2 * t:3 * t]
    atot_e = acum_e[t - 1:t, :]

    xdt = x * dt_e
    xw = (xdt * jnp.exp(dec_e)).astype(BF16)
    xdt_b = xdt.astype(BF16)
    a_cum_t = a_cum.T
    rr = lax.broadcasted_iota(jnp.int32, (t, t), 0)
    cc = lax.broadcasted_iota(jnp.int32, (t, t), 1)
    causal = cc <= rr
    lane = lax.broadcasted_iota(jnp.int32, (t, LANES), 1)
    heads_per_group = gw // B_HEADDIM
    pairs_per_group = gw // LANES

    y_groups = []
    for g in range(B_GROUPS):
        bg = xbc[:, bw + g * B_DSTATE:bw + (g + 1) * B_DSTATE].astype(BF16)
        cg = xbc[:, bw + n_bc + g * B_DSTATE:bw + n_bc + (g + 1) * B_DSTATE].astype(BF16)
        gs = slice(g * gw, (g + 1) * gw)
        st_g = st[:, gs]
        scores = lax.dot_general(cg, bg, (((1,), (1,)), ((), ())), preferred_element_type=F32)
        y_off = jnp.dot(cg, st_g.astype(BF16), preferred_element_type=F32) * jnp.exp(acum_e[:, gs])
        st[:, gs] = jnp.exp(atot_e[:, gs]) * st_g + lax.dot_general(
            bg, xw[:, gs], (((0,), (0,)), ((), ())), preferred_element_type=F32)
        y_pairs = []
        for pr in range(pairs_per_group):
            cols = slice(g * gw + pr * LANES, g * gw + (pr + 1) * LANES)
            xp = xdt_b[:, cols]
            acc = None
            for half in range(LANES // B_HEADDIM):
                h = g * heads_per_group + pr * (LANES // B_HEADDIM) + half
                diff = a_cum[:, h:h + 1] - a_cum_t[h:h + 1, :]
                lmat = jnp.where(causal, jnp.exp(jnp.where(causal, diff, 0.0)), 0.0)
                m = (scores * lmat).astype(BF16)
                in_half = (lane // B_HEADDIM) == half
                term = jnp.dot(m, jnp.where(in_half, xp, jnp.zeros_like(xp)), preferred_element_type=F32)
                acc = term if acc is None else acc + term
            y_pairs.append(acc)
        y_groups.append(jnp.concatenate(y_pairs, axis=1) + y_off)
    y = jnp.concatenate(y_groups, axis=1) + dsk_ref[...] * x
    y = y * _silu(z_ref[...])
    outs = []
    for g in range(B_GROUPS):
        yg = y[:, g * gw:(g + 1) * gw]
        outs.append(yg * lax.rsqrt(jnp.mean(yg * yg, axis=-1, keepdims=True) + NORM_EPS))
    yb_ref[...] = (jnp.concatenate(outs, axis=1) * gn_ref[...]).astype(yb_ref.dtype)

    tail = cur[t - pad:t, :]
    xpad[...] = tail

    @pl.when(ti == pl.num_programs(1) - 1)
    def _():
        conv_out_ref[0] = tail
        s_out_ref[0] = st[...].T.reshape(s_out_ref.shape[1:])


def _ssd(p, cols, prm, s0, c0, bsz, seq_len):
    bw = prm["bw"]
    heads = bw // B_HEADDIM
    n_bc = B_GROUPS * B_DSTATE
    cdim = bw + 2 * n_bc
    t = min(seq_len, 128)
    nt = seq_len // t
    tri = np.arange(t)[None, :] <= np.arange(t)[:, None]
    sel = jnp.asarray(np.concatenate([tri, np.ones((t, t), bool)], 0).astype(np.float32), dtype=BF16)
    expand = np.zeros((LANES, bw), np.float32)
    expand[np.arange(bw) // B_HEADDIM, np.arange(bw)] = 1.0
    expand = jnp.asarray(expand, dtype=BF16)

    def blk(width, off):
        return pl.BlockSpec((t, width), lambda b, ti: (b * nt + ti, off // width))

    def full(shape):
        return pl.BlockSpec(shape, lambda b, ti: (0,) * len(shape))

    in_specs = [blk(bw, cols["z"]), blk(bw, cols["xs"]), blk(2 * n_bc, cols["bc"]), blk(LANES, cols["dt"]),
                full((CONV_W, cdim)), full((1, cdim)), full((1, LANES)), full((1, LANES)),
                full((1, bw)), full((1, bw)), full(expand.shape), full(sel.shape)]
    args = [p, p, p, p, prm["conv_w"], prm["conv_b"], prm["dt_bias"], prm["a_log"], prm["d_skip"], prm["gn"],
            expand, sel]
    state_spec = pl.BlockSpec((1, heads, B_HEADDIM, B_DSTATE), lambda b, ti: (b, 0, 0, 0))
    conv_spec = pl.BlockSpec((1, SUBLANES, cdim), lambda b, ti: (b, 0, 0))
    if s0 is not None:
        layer = prm["layer"]
        in_specs += [pl.BlockSpec((None, 1, heads, B_HEADDIM, B_DSTATE), lambda b, ti: (layer, b, 0, 0, 0)),
                     pl.BlockSpec((None, 1, SUBLANES, cdim), lambda b, ti: (layer, b, 0, 0))]
        args += [s0, c0]
    return pl.pallas_call(
        functools.partial(_ssd_kernel, t=t, bw=bw, has_init=s0 is not None),
        grid=(bsz, nt),
        in_specs=in_specs,
        out_specs=[pl.BlockSpec((t, bw), lambda b, ti: (b * nt + ti, 0)), state_spec, conv_spec],
        out_shape=[jax.ShapeDtypeStruct((bsz * seq_len, bw), BF16),
                   jax.ShapeDtypeStruct((bsz, heads, B_HEADDIM, B_DSTATE), F32),
                   jax.ShapeDtypeStruct((bsz, SUBLANES, cdim), F32)],
        scratch_shapes=[pltpu.VMEM((SUBLANES, cdim), F32), pltpu.VMEM((B_DSTATE, bw), F32)],
        compiler_params=_cparams(("arbitrary", "arbitrary")),
        name="ssd",
    )(*args)


def _cmlp_kernel(u_ref, v_ref, lng_ref, lnb_ref, ws_ref, bst_ref, *out_refs, t, n_chunks, keep_v):
    yc_ref = out_refs[0]
    cw = u_ref.shape[1] // C_GROUPS
    rr = lax.broadcasted_iota(jnp.int32, (t, t), 0)
    cc = lax.broadcasted_iota(jnp.int32, (t, t), 1)
    wts = [jnp.where(cc <= rr, ws_ref[g, 0:t, 0:t], 0.0).astype(BF16) for g in range(C_GROUPS)]
    for ci in range(n_chunks):
        rows = slice(ci * t, (ci + 1) * t)
        u = _gelu(u_ref[rows, :])
        gv = _gelu(v_ref[rows, :])
        mu = jnp.mean(gv, axis=-1, keepdims=True)
        dv = gv - mu
        var = jnp.mean(dv * dv, axis=-1, keepdims=True)
        v = dv * lax.rsqrt(var + NORM_EPS) * lng_ref[...] + lnb_ref[...]
        if keep_v:
            out_refs[1][rows, :] = v
        vb = v.astype(BF16)
        for g in range(C_GROUPS):
            lanes = slice(g * cw, (g + 1) * cw)
            mixed = jnp.dot(wts[g], vb[:, lanes], preferred_element_type=F32) + bst_ref[0:t, g:g + 1]
            yc_ref[rows, lanes] = (u[:, lanes] * mixed).astype(yc_ref.dtype)


def _cmlp(p, col_u, col_v, prm, bsz, seq_len, keep_v):
    cw = prm["cw"]
    t = min(seq_len, CMLP_CHUNK)
    t_blk = min(seq_len, 4 * CMLP_CHUNK)
    n = bsz * seq_len

    def full(shape):
        return pl.BlockSpec(shape, lambda i: (0,) * len(shape))

    out_specs = [pl.BlockSpec((t_blk, cw), lambda i: (i, 0))]
    out_shape = [jax.ShapeDtypeStruct((n, cw), BF16)]
    if keep_v:
        out_specs.append(pl.BlockSpec((t_blk, cw), lambda i: (i, 0)))
        out_shape.append(jax.ShapeDtypeStruct((n, cw), F32))
    return pl.pallas_call(
        functools.partial(_cmlp_kernel, t=t, n_chunks=t_blk // t, keep_v=keep_v),
        grid=(n // t_blk,),
        in_specs=[pl.BlockSpec((t_blk, cw), lambda i: (i, col_u // cw)),
                  pl.BlockSpec((t_blk, cw), lambda i: (i, col_v // cw)),
                  full((1, cw)), full((1, cw)),
                  full((C_GROUPS, CMLP_CHUNK, CMLP_CHUNK)), full((CMLP_CHUNK, C_GROUPS))],
        out_specs=out_specs,
        out_shape=out_shape,
        compiler_params=_cparams(("arbitrary",)),
        name="cmlp",
    )(p, p, prm["ln_g"], prm["ln_b"], prm["ws"], prm["bs_t"])


def _run_trunk(x3, mod, st_hgrn, st_ssm, st_conv, keep_v, w):
    bsz, seq_len, d = x3.shape
    depth = mod.shape[0]
    aw = bw = cw = d // 2
    heads_a = aw // A_DK
    x = x3.reshape(bsz * seq_len, d)
    col = w["cols"]
    hgrn_out, ssm_out, conv_out, v_out = [], [], [], []
    for l in range(depth):
        mod6 = mod[l].reshape(bsz, 6, 1, d).transpose(1, 0, 2, 3)
        p, gates = _in_proj(x, w["norm1_g"][l], mod6, w["w_in"], l, seq_len, col["gate"])

        y_a, s_h = _hgrn(p, w["lbc"], l, w["hgrn_onorm_g"][l], st_hgrn, bsz, seq_len, heads_a)
        ssd_prm = dict(bw=bw, layer=l, conv_w=w["ssm_conv_w"][l], conv_b=w["ssm_conv_b"][l][None],
                       dt_bias=w["dt_bias_pad"][l][None], a_log=w["a_log_pad"][l][None],
                       d_skip=w["d_skip"][l][None], gn=w["ssm_onorm_g"][l][None])
        y_b, s_s, conv_tail = _ssd(p, col, ssd_prm, st_ssm, st_conv, bsz, seq_len)
        cm_prm = dict(cw=cw, ln_g=w["cmlp_ln_g"][l][None], ln_b=w["cmlp_ln_b"][l][None],
                      ws=w["cmlp_ws"][l], bs_t=w["cmlp_bs"][l].T)
        c_res = _cmlp(p, col["u"], col["v"], cm_prm, bsz, seq_len, keep_v)
        merged = _merge(y_a, y_b, c_res[0], w["w_branch"], l, gates, d)
        x = _proj_residual(merged, w["w_out"], l, x, mod6, 2, seq_len, "out_proj")
        hid = _mlp_up(x, w["norm2_g"][l], mod6, w["w_up"], l, seq_len)
        x = _proj_residual(hid, w["w_down"], l, x, mod6, 5, seq_len, "mlp_down")

        hgrn_out.append(s_h)
        ssm_out.append(s_s)
        conv_out.append(conv_tail[:, SUBLANES - (CONV_W - 1):, :])
        if keep_v:
            v_out.append(c_res[1].reshape(bsz, seq_len, cw))
    y = _final_norm(x, w["final_g"]).reshape(bsz, seq_len, d)
    return (y, jnp.stack(hgrn_out), jnp.stack(ssm_out), jnp.stack(conv_out),
            jnp.stack(v_out) if keep_v else None)


def _mlp_down_hgrn_kernel(a_ref, w_ref, x_ref, gate_ref, q_ref, f_ref, i_ref, gg_ref, lbc_ref, gn_ref, sel_ref,
                          o_ref, ya_ref, s_ref, acc_scr, bpad, kpad, *, nseg, c, n_chunks, heads, steps_per_seq):
    step = pl.program_id(0) * pl.num_programs(1) + pl.program_id(1)

    @pl.when(step % steps_per_seq == 0)
    def _():
        s_ref[...] = jnp.zeros(s_ref.shape, F32)

    kdim = a_ref.shape[1]
    kc = kdim // DOWN_K_SPLIT

    def sub_dot(ki):
        rows = slice(ki * kc, (ki + 1) * kc)
        part = jnp.dot(a_ref[:, rows], w_ref[rows, :], preferred_element_type=F32)
        if ki == 0:
            acc_scr[...] = part
        elif ki < DOWN_K_SPLIT - 1:
            acc_scr[...] += part
        else:
            x = x_ref[...]
            acc = acc_scr[...] + part
            o_ref[...] = (_per_seq(x, nseg) + gate_ref[...] * _per_seq(acc, nseg)).reshape(x.shape)

    fillers = [functools.partial(sub_dot, ki) for ki in range(DOWN_K_SPLIT)]
    _hgrn_chunks(q_ref, f_ref, i_ref, gg_ref, lbc_ref, gn_ref, sel_ref, ya_ref, s_ref, bpad, kpad,
                 c=c, n_chunks=n_chunks, heads=heads, unroll=True, fillers=fillers)


def _mlp_down_hgrn(a, w_stack, layer, x, mod6, k_gate, seq_len, p_other, lbc_stack, lbc_layer, gn, heads):
    n, kdim = a.shape
    d = w_stack.shape[2]
    tn = DOWN_TILE
    tm, nseg, tps = _row_tiling(n, seq_len, ROW_TILE)
    ni, nj = n // tm, d // tn
    n_other = p_other.shape[0]
    t_step = n_other // (ni * nj)
    assert t_step * ni * nj == n_other and seq_len % t_step == 0 and t_step % SUBLANES == 0
    assert kdim % DOWN_K_SPLIT == 0
    steps_per_seq = seq_len // t_step
    aw = heads * A_DK
    c, h_in_specs, h_args, ya_spec, h_scratch = _hgrn_specs(
        p_other, lbc_stack, lbc_layer, gn, heads, t_step, lambda i, j: i * nj + j)
    state_spec = pl.BlockSpec((1, heads, A_DK, A_DK), lambda i, j: ((i * nj + j) // steps_per_seq, 0, 0, 0))
    return pl.pallas_call(
        functools.partial(_mlp_down_hgrn_kernel, nseg=nseg, c=c, n_chunks=t_step // c, heads=heads,
                          steps_per_seq=steps_per_seq),
        grid=(ni, nj),
        in_specs=[
            pl.BlockSpec((tm, kdim), lambda i, j: (i, 0)),
            pl.BlockSpec((None, kdim, tn), lambda i, j: (layer, 0, j)),
            pl.BlockSpec((tm, tn), lambda i, j: (i, j)),
            pl.BlockSpec((None, nseg, 1, tn), lambda i, j: (k_gate, i // tps, 0, j)),
        ] + h_in_specs,
        out_specs=[pl.BlockSpec((tm, tn), lambda i, j: (i, j)), ya_spec, state_spec],
        out_shape=[jax.ShapeDtypeStruct((n, d), F32),
                   jax.ShapeDtypeStruct((n_other, aw), BF16),
                   jax.ShapeDtypeStruct((n_other // seq_len, heads, A_DK, A_DK), F32)],
        scratch_shapes=[pltpu.VMEM((tm, tn), F32)] + h_scratch,
        compiler_params=_cparams(("arbitrary", "arbitrary")),
        name="mlp_down_hgrn2",
    )(a, w_stack, x, mod6, *h_args)


def _run_trunk_halves(x3, mod, w):
    bsz, seq_len, d = x3.shape
    depth = mod.shape[0]
    hb = bsz // 2
    aw = bw = cw = d // 2
    heads_a = aw // A_DK
    col = w["cols"]
    n_half = hb * seq_len
    x_full = x3.reshape(bsz * seq_len, d)
    xs = [x_full, x_full]
    row0 = [0, n_half]
    mods = [mod[:, :hb], mod[:, hb:]]
    hgrn_out, ssm_out, conv_out = [], [], []
    pending = None
    for l in range(depth):
        mod6 = [m[l].reshape(hb, 6, 1, d).transpose(1, 0, 2, 3) for m in mods]
        ssd_prm = dict(bw=bw, layer=l, conv_w=w["ssm_conv_w"][l], conv_b=w["ssm_conv_b"][l][None],
                       dt_bias=w["dt_bias_pad"][l][None], a_log=w["a_log_pad"][l][None],
                       d_skip=w["d_skip"][l][None], gn=w["ssm_onorm_g"][l][None])
        cm_prm = dict(cw=cw, ln_g=w["cmlp_ln_g"][l][None], ln_b=w["cmlp_ln_b"][l][None],
                      ws=w["cmlp_ws"][l], bs_t=w["cmlp_bs"][l].T)
        gn_a = w["hgrn_onorm_g"][l]

        def mix_merge(h, p, gates, y_a):
            y_b, s_s, conv_tail = _ssd(p, col, ssd_prm, None, None, hb, seq_len)
            y_c = _cmlp(p, col["u"], col["v"], cm_prm, hb, seq_len, False)[0]
            merged = _merge(y_a, y_b, y_c, w["w_branch"], l, gates, d)
            x_new = _proj_residual(merged, w["w_out"], l, xs[h], mod6[h], 2, seq_len, "out_proj", row0[h])
            row0[h] = 0
            return x_new, s_s, conv_tail

        p0, g0 = _in_proj(xs[0], w["norm1_g"][l], mod6[0], w["w_in"], l, seq_len, col["gate"], (row0[0], n_half))
        if pending is None:
            ya0, sh0 = _hgrn(p0, w["lbc"], l, gn_a, None, hb, seq_len, heads_a)
        else:
            hid1, x1_mid, mod6_prev = pending
            xs[1], ya0, sh0 = _mlp_down_hgrn(hid1, w["w_down"], l - 1, x1_mid, mod6_prev, 5, seq_len,
                                              p0, w["lbc"], l, gn_a, heads_a)
        p1, g1 = _in_proj(xs[1], w["norm1_g"][l], mod6[1], w["w_in"], l, seq_len, col["gate"], (row0[1], n_half))
        x0, ss0, ct0 = mix_merge(0, p0, g0, ya0)
        hid0, ya1, sh1 = _mlp_up_hgrn(x0, w["norm2_g"][l], mod6[0], w["w_up"], l, seq_len,
                                      p1, w["lbc"], gn_a, heads_a)
        xs[0] = _proj_residual(hid0, w["w_down"], l, x0, mod6[0], 5, seq_len, "mlp_down")
        x1_mid, ss1, ct1 = mix_merge(1, p1, g1, ya1)
        hid1 = _mlp_up(x1_mid, w["norm2_g"][l], mod6[1], w["w_up"], l, seq_len)
        pending = (hid1, x1_mid, mod6[1])

        hgrn_out.append(jnp.concatenate([sh0, sh1], axis=0))
        ssm_out.append(jnp.concatenate([ss0, ss1], axis=0))
        conv_out.append(jnp.concatenate([ct0, ct1], axis=0)[:, SUBLANES - (CONV_W - 1):, :])
    hid1, x1_mid, mod6_prev = pending
    xs[1] = _proj_residual(hid1, w["w_down"], depth - 1, x1_mid, mod6_prev, 5, seq_len, "mlp_down")
    y = _final_norm_pair(xs[0], xs[1], w["final_g"]).reshape(bsz, seq_len, d)
    return (y, jnp.stack(hgrn_out), jnp.stack(ssm_out), jnp.stack(conv_out), None)


def kernel(x_prompt, x_sample, state_hgrn, state_ssm, state_conv, c_prompt, c_sample, norm1_g, norm2_g,
           w_mod, b_mod, w_in, hgrn_lb, hgrn_onorm_g, ssm_conv_w, ssm_conv_b, ssm_dt_bias, ssm_a_log, ssm_d,
           ssm_onorm_g, cmlp_ln_g, cmlp_ln_b, cmlp_ws, cmlp_bs, w_branch, w_out, w_up, w_down, final_g):
    d = x_prompt.shape[-1]
    depth = w_in.shape[0]
    aw = bw = cw = d // 2
    n_bc = B_GROUPS * B_DSTATE
    heads_b = bw // B_HEADDIM
    assert heads_b <= LANES and bw % LANES == 0 and DT_PAD % LANES == 0

    o_dt = 4 * aw + bw + bw + 2 * n_bc
    o_u = o_dt + heads_b
    w_in_r = jnp.concatenate(
        [w_in[:, :, :o_u].astype(BF16), jnp.zeros((depth, d, DT_PAD - heads_b), BF16),
         w_in[:, :, o_u:].astype(BF16)], axis=2)
    cols = dict(z=4 * aw, xs=4 * aw + bw, bc=4 * aw + 2 * bw, dt=o_dt)
    cols["u"] = o_dt + DT_PAD
    cols["v"] = cols["u"] + cw
    cols["gate"] = cols["v"] + cw
    assert cols["gate"] % IN_TILE == 0 and w_in_r.shape[2] % IN_TILE == 0

    pad_h = lambda a: jnp.pad(a.astype(F32), ((0, 0), (0, LANES - heads_b)))
    w = dict(
        cols=cols, w_in=w_in_r,
        norm1_g=norm1_g, norm2_g=norm2_g, final_g=final_g,
        lbc=_lb_consts(hgrn_lb), hgrn_onorm_g=hgrn_onorm_g,
        ssm_conv_w=ssm_conv_w, ssm_conv_b=ssm_conv_b,
        dt_bias_pad=pad_h(ssm_dt_bias), a_log_pad=pad_h(ssm_a_log),
        d_skip=jnp.repeat(ssm_d.astype(F32), B_HEADDIM, axis=1), ssm_onorm_g=ssm_onorm_g,
        cmlp_ln_g=cmlp_ln_g, cmlp_ln_b=cmlp_ln_b, cmlp_ws=cmlp_ws, cmlp_bs=cmlp_bs,
        w_branch=w_branch.astype(BF16).reshape(depth, N_BRANCH, aw, d), w_out=w_out.astype(BF16),
        w_up=w_up.astype(BF16), w_down=w_down.astype(BF16),
    )

    nb = x_prompt.shape[0]
    mod = _modulation(jnp.concatenate([c_prompt, c_sample], axis=0), w_mod, b_mod)
    if nb % 2 == 0:
        y_p, hgrn_p, ssm_p, conv_p, _ = _run_trunk_halves(x_prompt, mod[:, :nb], w)
    else:
        y_p, hgrn_p, ssm_p, conv_p, _ = _run_trunk(x_prompt, mod[:, :nb], None, None, None, False, w)
    conv_pad = jnp.pad(state_conv, ((0, 0), (0, 0), (SUBLANES - (CONV_W - 1), 0), (0, 0)))
    y_s, hgrn_s, ssm_s, conv_s, v_s = _run_trunk(x_sample, mod[:, nb:], state_hgrn, state_ssm, conv_pad, True, w)
    return (y_p, y_s, hgrn_p, ssm_p, conv_p, hgrn_s, ssm_s, conv_s, v_s)
```

```python
import functools
import math

import numpy as np
import jax
import jax.numpy as jnp
from jax import lax
from jax.experimental import pallas as pl
from jax.experimental.pallas import tpu as pltpu

F32 = jnp.float32
BF16 = jnp.bfloat16

A_DK = 128
B_HEADDIM = 64
B_GROUPS = 2
B_DSTATE = 128
CONV_W = 4
C_GROUPS = 4
CMLP_CHUNK = 128
N_BRANCH = 3
SCAN_CHUNK = 64
NORM_EPS = 1e-6
LB_FLOOR = 1e-30
LOG2E = 1.4426950408889634

LANES = 128
SUBLANES = 8
VMEM_LIMIT = 56 * 1024 * 1024
DIAG = SUBLANES
DT_PAD = 512
ROW_TILE = 1024
COL_TILE = 1024
IN_TILE = 1536
MERGE_TILE = 512
DOWN_TILE = 256
EPILOGUE_COLS = 256
SSD_ROWS = 128
DOWN_K_SPLIT = 4
NORM_ROWS = 128


def _cparams(sem):
    return pltpu.CompilerParams(dimension_semantics=sem, vmem_limit_bytes=VMEM_LIMIT)


def _split3(x):
    hi = x.astype(BF16)
    r1 = x - hi.astype(F32)
    mid = r1.astype(BF16)
    lo = (r1 - mid.astype(F32)).astype(BF16)
    return hi, mid, lo


def _dot3(sel, x):
    hi, mid, lo = _split3(x)
    d = lambda p: jnp.dot(sel, p, preferred_element_type=F32)
    return d(hi) + d(mid) + d(lo)


def _dot3_rhs(x, sel):
    hi, mid, lo = _split3(x)
    d = lambda p: jnp.dot(p, sel, preferred_element_type=F32)
    return d(hi) + d(mid) + d(lo)


def _silu(x):
    return x / (1.0 + jnp.exp(-x))


def _softplus(x):
    return jnp.maximum(x, 0.0) + jnp.log1p(jnp.exp(-jnp.abs(x)))


def _gelu(x):
    return 0.5 * x * (1.0 + lax.erf(x * (1.0 / math.sqrt(2.0))))


def _lb_kernel(lb_ref, out_ref, *, depth):
    x = lb_ref[...]
    m = jnp.max(x, axis=0, keepdims=True)
    e = jnp.exp(x - m)
    p = e / jnp.sum(e, axis=0, keepdims=True)
    acc = jnp.zeros_like(p[0:1])
    zeros5 = jnp.zeros((SUBLANES - 3, x.shape[1]), F32)
    for l in range(depth):
        acc = acc + p[l:l + 1]
        lb = acc - p[0:1]
        out_ref[l] = jnp.concatenate(
            [jnp.log(jnp.maximum(lb, LB_FLOOR)), jnp.log1p(-lb), 1.0 - lb, zeros5], axis=0)


def _lb_consts(hgrn_lb):
    depth, aw = hgrn_lb.shape
    return pl.pallas_call(
        functools.partial(_lb_kernel, depth=depth),
        out_shape=jax.ShapeDtypeStruct((depth, SUBLANES, aw), F32),
        name="hgrn_lb",
    )(hgrn_lb.astype(F32))


def _mod_kernel(c_ref, w_ref, b_ref, o_ref):
    cs = _silu(c_ref[...]).astype(BF16)
    o_ref[...] = jnp.dot(cs, w_ref[...].astype(BF16), preferred_element_type=F32) + b_ref[...]


def _modulation(c_all, w_mod, b_mod):
    depth, d, n6 = w_mod.shape
    s = c_all.shape[0]
    tn = 1024
    return pl.pallas_call(
        _mod_kernel,
        grid=(depth, n6 // tn),
        in_specs=[
            pl.BlockSpec((s, d), lambda l, j: (0, 0)),
            pl.BlockSpec((None, d, tn), lambda l, j: (l, 0, j)),
            pl.BlockSpec((None, 1, tn), lambda l, j: (l, 0, j)),
        ],
        out_specs=pl.BlockSpec((None, s, tn), lambda l, j: (l, 0, j)),
        out_shape=jax.ShapeDtypeStruct((depth, s, n6), F32),
        compiler_params=_cparams(("arbitrary", "arbitrary")),
        name="adaln_mod",
    )(c_all, w_mod, b_mod.reshape(depth, 1, n6))


def _row_tiling(n_tok, seq_len, tm_max):
    tm = min(tm_max, n_tok)
    if seq_len >= tm:
        assert seq_len % tm == 0
        return tm, 1, seq_len // tm
    assert tm % seq_len == 0
    return tm, tm // seq_len, 1


def _per_seq(x, nseg):
    tm, d = x.shape
    return x.reshape(nseg, tm // nseg, d)


def _final_norm_kernel(x_ref, g_ref, o_ref):
    x = x_ref[...]
    o_ref[...] = x * lax.rsqrt(jnp.mean(x * x, axis=-1, keepdims=True) + NORM_EPS) * g_ref[...]


def _final_norm(x, g):
    n, d = x.shape
    tm = min(512, n)
    return pl.pallas_call(
        _final_norm_kernel,
        grid=(n // tm,),
        in_specs=[pl.BlockSpec((tm, d), lambda i: (i, 0)), pl.BlockSpec((1, d), lambda i: (0, 0))],
        out_specs=pl.BlockSpec((tm, d), lambda i: (i, 0)),
        out_shape=jax.ShapeDtypeStruct((n, d), F32),
        compiler_params=_cparams(("arbitrary",)),
        name="final_norm",
    )(x, g.reshape(1, d))


def _final_norm_pair_kernel(xa_ref, xb_ref, g_ref, o_ref, *, nt):
    def norm(x):
        return x * lax.rsqrt(jnp.mean(x * x, axis=-1, keepdims=True) + NORM_EPS) * g_ref[...]

    @pl.when(pl.program_id(0) < nt)
    def _():
        o_ref[...] = norm(xa_ref[...])

    @pl.when(pl.program_id(0) >= nt)
    def _():
        o_ref[...] = norm(xb_ref[...])


def _final_norm_pair(xa, xb, g):
    n, d = xa.shape
    tm = min(512, n)
    nt = n // tm
    return pl.pallas_call(
        functools.partial(_final_norm_pair_kernel, nt=nt),
        grid=(2 * nt,),
        in_specs=[pl.BlockSpec((tm, d), lambda i: (jnp.minimum(i, nt - 1), 0)),
                  pl.BlockSpec((tm, d), lambda i: (jnp.maximum(i - nt, 0), 0)),
                  pl.BlockSpec((1, d), lambda i: (0, 0))],
        out_specs=pl.BlockSpec((tm, d), lambda i: (i, 0)),
        out_shape=jax.ShapeDtypeStruct((2 * n, d), F32),
        compiler_params=_cparams(("arbitrary",)),
        name="final_norm",
    )(xa, xb, g.reshape(1, d))


def _norm_mod_rows(x_ref, g_ref, sc_ref, sh_ref, h_scr, nseg):
    tm = x_ref.shape[0]
    seg = tm // nseg
    rb = min(seg, NORM_ROWS)
    g = g_ref[...]

    def body(r, carry):
        rows = pl.ds(pl.multiple_of(r * rb, rb), rb)
        s = r // (seg // rb)
        x = x_ref[rows, :]
        gain = g * (1.0 + sc_ref[s])
        inv = lax.rsqrt(jnp.mean(x * x, axis=-1, keepdims=True) + NORM_EPS)
        h_scr[rows, :] = (x * inv * gain + sh_ref[s]).astype(BF16)
        return carry

    lax.fori_loop(0, tm // rb, body, 0, unroll=2 if (tm // rb) % 2 == 0 else 1)


def _mlp_up_kernel(x_ref, g_ref, sc_ref, sh_ref, w_ref, o_ref, h_scr, *, nseg):
    @pl.when(pl.program_id(1) == 0)
    def _():
        _norm_mod_rows(x_ref, g_ref, sc_ref, sh_ref, h_scr, nseg)

    y = jnp.maximum(jnp.dot(h_scr[...], w_ref[...], preferred_element_type=F32), 0.0)
    o_ref[...] = (y * y).astype(o_ref.dtype)


def _in_proj_kernel(x_ref, g_ref, sc_ref, sh_ref, w_ref, p_ref, gate_ref, h_scr, *, nseg, n_main):
    j = pl.program_id(1)

    @pl.when(j == 0)
    def _():
        _norm_mod_rows(x_ref, g_ref, sc_ref, sh_ref, h_scr, nseg)

    @pl.when(j < n_main)
    def _():
        p_ref[...] = jnp.dot(h_scr[...], w_ref[...], preferred_element_type=F32)

    @pl.when(j >= n_main)
    def _():
        gate_ref[...] = jnp.dot(h_scr[...], w_ref[...], preferred_element_type=F32).astype(gate_ref.dtype)


def _norm_proj_call(kern, x, g, mod6, k_scale, k_shift, w_stack, layer, seq_len, tn, out_specs, out_shape, name):
    n, d = x.shape
    m = w_stack.shape[2]
    tm, nseg, tps = _row_tiling(n, seq_len, ROW_TILE)

    def mod_spec(k):
        return pl.BlockSpec((None, nseg, 1, d), lambda i, j: (k, i // tps, 0, 0))

    return pl.pallas_call(
        functools.partial(kern, nseg=nseg),
        grid=(n // tm, m // tn),
        in_specs=[
            pl.BlockSpec((tm, d), lambda i, j: (i, 0)),
            pl.BlockSpec((1, d), lambda i, j: (0, 0)),
            mod_spec(k_scale),
            mod_spec(k_shift),
            pl.BlockSpec((None, d, tn), lambda i, j: (layer, 0, j)),
        ],
        out_specs=out_specs(tm),
        out_shape=out_shape,
        scratch_shapes=[pltpu.VMEM((tm, d), BF16)],
        compiler_params=_cparams(("arbitrary", "arbitrary")),
        name=name,
    )(x, g.reshape(1, d), mod6, mod6, w_stack)


def _mlp_up(x, g, mod6, w_stack, layer, seq_len):
    n = x.shape[0]
    m = w_stack.shape[2]
    tn = COL_TILE
    return _norm_proj_call(
        _mlp_up_kernel, x, g, mod6, 4, 3, w_stack, layer, seq_len, tn,
        lambda tm: pl.BlockSpec((tm, tn), lambda i, j: (i, j)),
        jax.ShapeDtypeStruct((n, m), BF16), "mlp_up")


def _in_proj(x, g, mod6, w_stack, layer, seq_len, main_cols, rows=None):
    d = x.shape[1]
    row0, n = rows if rows is not None else (0, x.shape[0])
    m = w_stack.shape[2]
    tn = IN_TILE
    n_main = main_cols // tn
    tm, nseg, tps = _row_tiling(n, seq_len, ROW_TILE)

    def mod_spec(k):
        return pl.BlockSpec((None, nseg, 1, d), lambda i, j: (k, i // tps, 0, 0))

    return pl.pallas_call(
        functools.partial(_in_proj_kernel, nseg=nseg, n_main=n_main),
        grid=(n // tm, m // tn),
        in_specs=[
            pl.BlockSpec((tm, d), lambda i, j: (i + row0 // tm, 0)),
            pl.BlockSpec((1, d), lambda i, j: (0, 0)),
            mod_spec(1),
            mod_spec(0),
            pl.BlockSpec((None, d, tn), lambda i, j: (layer, 0, j)),
        ],
        out_specs=[pl.BlockSpec((tm, tn), lambda i, j: (i, jnp.minimum(j, n_main - 1))),
                   pl.BlockSpec((tm, tn), lambda i, j: (i, jnp.maximum(j - n_main, 0)))],
        out_shape=[jax.ShapeDtypeStruct((n, main_cols), F32), jax.ShapeDtypeStruct((n, m - main_cols), BF16)],
        scratch_shapes=[pltpu.VMEM((tm, d), BF16)],
        compiler_params=_cparams(("arbitrary", "arbitrary")),
        name="in_proj",
    )(x, g.reshape(1, d), mod6, mod6, w_stack)


def _proj_res_kernel(a_ref, w_ref, x_ref, gate_ref, o_ref, *, nseg, nk):
    part = jnp.dot(a_ref[...], w_ref[...], preferred_element_type=F32)

    def finish(acc):
        x = x_ref[...]
        o_ref[...] = (_per_seq(x, nseg) + gate_ref[...] * _per_seq(acc, nseg)).reshape(x.shape)

    if nk == 1:
        finish(part)
        return
    k = pl.program_id(2)

    @pl.when(k == 0)
    def _():
        o_ref[...] = part

    @pl.when((k > 0) & (k < nk - 1))
    def _():
        o_ref[...] += part

    @pl.when(k == nk - 1)
    def _():
        finish(o_ref[...] + part)


def _proj_residual(a, w_stack, layer, x, mod6, k_gate, seq_len, name, x_row0=0):
    n, kdim = a.shape
    d = w_stack.shape[2]
    tn = COL_TILE if kdim <= d else DOWN_TILE
    tk = kdim
    nk = kdim // tk
    tm, nseg, tps = _row_tiling(n, seq_len, ROW_TILE)
    return pl.pallas_call(
        functools.partial(_proj_res_kernel, nseg=nseg, nk=nk),
        grid=(n // tm, d // tn, nk),
        in_specs=[
            pl.BlockSpec((tm, tk), lambda i, j, k: (i, k)),
            pl.BlockSpec((None, tk, tn), lambda i, j, k: (layer, k, j)),
            pl.BlockSpec((tm, tn), lambda i, j, k: (i + x_row0 // tm, j)),
            pl.BlockSpec((None, nseg, 1, tn), lambda i, j, k: (k_gate, i // tps, 0, j)),
        ],
        out_specs=pl.BlockSpec((tm, tn), lambda i, j, k: (i, j)),
        out_shape=jax.ShapeDtypeStruct((n, d), F32),
        compiler_params=_cparams(("arbitrary", "arbitrary", "arbitrary")),
        name=name,
    )(a, w_stack, x, mod6)


def _merge_kernel(ya_ref, yb_ref, yc_ref, wa_ref, wb_ref, wc_ref, ga_ref, gb_ref, gc_ref, o_ref):
    acc = None
    for y_ref, w_ref, g_ref in ((ya_ref, wa_ref, ga_ref), (yb_ref, wb_ref, gb_ref), (yc_ref, wc_ref, gc_ref)):
        gate = 1.0 / (1.0 + jnp.exp(-g_ref[...].astype(F32)))
        term = gate * jnp.dot(y_ref[...], w_ref[...], preferred_element_type=F32)
        acc = term if acc is None else acc + term
    o_ref[...] = acc.astype(o_ref.dtype)


def _merge(ya, yb, yc, w_stack, layer, gates, d):
    n, bw = ya.shape
    tn = MERGE_TILE
    tm = min(ROW_TILE, n)
    gsteps = d // tn
    y_spec = pl.BlockSpec((tm, bw), lambda i, j: (i, 0))

    def w_spec(k):
        return pl.BlockSpec((None, None, bw, tn), lambda i, j: (layer, k, 0, j))

    def g_spec(k):
        return pl.BlockSpec((tm, tn), lambda i, j: (i, k * gsteps + j))

    return pl.pallas_call(
        _merge_kernel,
        grid=(n // tm, d // tn),
        in_specs=[y_spec, y_spec, y_spec, w_spec(0), w_spec(1), w_spec(2), g_spec(0), g_spec(1), g_spec(2)],
        out_specs=pl.BlockSpec((tm, tn), lambda i, j: (i, j)),
        out_shape=jax.ShapeDtypeStruct((n, d), BF16),
        compiler_params=_cparams(("arbitrary", "arbitrary")),
        name="merge",
    )(ya, yb, yc, w_stack, w_stack, w_stack, gates, gates, gates)


def _hgrn_levels(c):
    lv, m = [], c // 2
    while m >= DIAG:
        lv.append(m)
        m //= 2
    return tuple(lv)


def _hgrn_select(c):
    tri = np.arange(c)[None, :] <= np.arange(c)[:, None]
    return jnp.asarray(tri.astype(np.float32), dtype=BF16)


def _hgrn_kernel(*refs, c, n_chunks, heads, has_init):
    if has_init:
        q_ref, f_ref, i_ref, g_ref, lbc_ref, gn_ref, sel_ref, s0_ref, ya_ref, s_ref, bpad, kpad = refs
    else:
        q_ref, f_ref, i_ref, g_ref, lbc_ref, gn_ref, sel_ref, ya_ref, s_ref, bpad, kpad = refs

    @pl.when(pl.program_id(1) == 0)
    def _():
        s_ref[...] = s0_ref[...] if has_init else jnp.zeros(s_ref.shape, F32)

    _hgrn_chunks(q_ref, f_ref, i_ref, g_ref, lbc_ref, gn_ref, sel_ref, ya_ref, s_ref, bpad, kpad,
                 c=c, n_chunks=n_chunks, heads=heads, unroll=False)


def _hgrn_chunks(q_ref, f_ref, i_ref, g_ref, lbc_ref, gn_ref, sel_ref, ya_ref, s_ref, bpad, kpad,
                 *, c, n_chunks, heads, unroll, fillers=()):
    levels = _hgrn_levels(c)
    nl = len(levels)
    width = heads * A_DK
    head_lanes = [slice(hh * A_DK, (hh + 1) * A_DK) for hh in range(heads)]

    bpad[0:DIAG, :] = jnp.zeros((DIAG, width), F32)
    kpad[0:DIAG, :] = jnp.zeros((DIAG, width), F32)

    sel = sel_ref[...]
    log_lb = lbc_ref[0:1, :]
    log1m_lb = lbc_ref[1:2, :]
    one_m_lb = lbc_ref[2:3, :]
    row = lax.broadcasted_iota(jnp.int32, (c, width), 0)
    rr = lax.broadcasted_iota(jnp.int32, (c, c), 0)
    cc = lax.broadcasted_iota(jnp.int32, (c, c), 1)
    upper = [(row & m) != 0 for m in levels]
    same = [(rr ^ cc) < 2 * m for m in levels]
    dmat = jnp.where(((rr ^ cc) < DIAG) & (cc <= rr), rr - cc, -1)
    nt_dims = (((1,), (1,)), ((), ()))
    tn_dims = (((0,), (0,)), ((), ()))

    def chunk(ci):
        rows = slice(ci * c, (ci + 1) * c) if unroll else pl.ds(pl.multiple_of(ci * c, c), c)
        z = f_ref[rows, :]
        aq = q_ref[rows, :]

        log_sig = jnp.minimum(z, 0.0) - jnp.log(1.0 + jnp.exp(-jnp.abs(z)))
        bb = log1m_lb + log_sig
        log_f = jnp.maximum(log_lb, bb) + jnp.log(1.0 + jnp.exp(-jnp.abs(log_lb - bb)))
        k = one_m_lb / (1.0 + jnp.exp(z))
        q = _silu(aq)

        b = _dot3(sel, log_f * LOG2E)
        bpad[DIAG:DIAG + c, :] = b
        kpad[DIAG:DIAG + c, :] = k
        b_tot = bpad[DIAG + c - 1:DIAG + c, :]
        vb = i_ref[rows, :].astype(BF16)
        q_in = (q * jnp.exp2(b)).astype(BF16)
        k_out = (k * jnp.exp2(b_tot - b)).astype(BF16)
        yield

        att = [jnp.zeros((c, c), F32) for _ in range(heads)]
        for li in range(nl):
            m = levels[li]
            ref_b = jnp.concatenate(
                [jnp.broadcast_to(bpad[DIAG + g0 + m - 1:DIAG + g0 + m, :], (2 * m, width))
                 for g0 in range(0, c, 2 * m)], axis=0)
            e = jnp.exp2(-jnp.abs(b - ref_b))
            qs = jnp.where(upper[li], q * e, 0.0).astype(BF16)
            ks = jnp.where(upper[li], 0.0, k * e).astype(BF16)
            for hh, lanes in enumerate(head_lanes):
                a_l = lax.dot_general(qs[:, lanes], ks[:, lanes], nt_dims, preferred_element_type=F32)
                att[hh] = att[hh] + jnp.where(same[li], a_l, 0.0)
        yield

        for dlt in range(DIAG):
            if dlt == 0:
                w = q * k
            else:
                ksh = kpad[DIAG - dlt:DIAG - dlt + c, :]
                bsh = bpad[DIAG - dlt:DIAG - dlt + c, :]
                w = q * ksh * jnp.exp2(b - bsh)
            for hh, lanes in enumerate(head_lanes):
                col = jnp.sum(w[:, lanes], axis=-1, keepdims=True)
                att[hh] = jnp.where(dmat == dlt, col, att[hh])
            if dlt in (DIAG // 2 - 1, DIAG - 1):
                yield

        decay_row = jnp.exp2(b_tot)
        outs = []
        for hh, lanes in enumerate(head_lanes):
            s_prev = s_ref[0, hh]
            o = jnp.dot(q_in[:, lanes], s_prev.astype(BF16), preferred_element_type=F32)
            o = o + jnp.dot(att[hh].astype(BF16), vb[:, lanes], preferred_element_type=F32)
            decay_col = jnp.broadcast_to(decay_row[:, lanes], (A_DK, A_DK)).T
            s_ref[0, hh] = decay_col * s_prev + lax.dot_general(
                k_out[:, lanes], vb[:, lanes], tn_dims, preferred_element_type=F32)
            outs.append(o * lax.rsqrt(jnp.mean(o * o, axis=-1, keepdims=True) + NORM_EPS))
        y = jnp.concatenate(outs, axis=1) * gn_ref[...] * _silu(g_ref[rows, :])
        ya_ref[rows, :] = y.astype(ya_ref.dtype)
        yield

    if not unroll:
        def body(ci, carry):
            for _ in chunk(ci):
                pass
            return carry

        lax.fori_loop(0, n_chunks, body, 0)
        return

    _emit_interleaved(fillers, [ph for ci in range(n_chunks) for ph in _phases_of(chunk(ci))])


HGRN_PHASES = 5


def _phases_of(gen):
    return [functools.partial(next, gen, None) for _ in range(HGRN_PHASES)]


def _hgrn_specs(p, lbc_stack, layer, gn, heads, t_blk, row_block):
    aw = heads * A_DK
    c = min(t_blk, SCAN_CHUNK)
    sel = _hgrn_select(c)

    def col_spec(seg):
        return pl.BlockSpec((t_blk, aw), lambda *g: (row_block(*g), seg))

    in_specs = [col_spec(0), col_spec(1), col_spec(2), col_spec(3),
                pl.BlockSpec((None, SUBLANES, aw), lambda *g: (layer, 0, 0)),
                pl.BlockSpec((1, aw), lambda *g: (0, 0)),
                pl.BlockSpec(sel.shape, lambda *g: (0, 0))]
    args = [p, p, p, p, lbc_stack, gn.reshape(1, aw), sel]
    scratch = [pltpu.VMEM((c + DIAG, aw), F32), pltpu.VMEM((c + DIAG, aw), F32)]
    return c, in_specs, args, pl.BlockSpec((t_blk, aw), lambda *g: (row_block(*g), 0)), scratch


def _mlp_up_hgrn_kernel(x_ref, g_ref, sc_ref, sh_ref, w_ref, q_ref, f_ref, i_ref, gg_ref, lbc_ref, gn_ref, sel_ref,
                        o_ref, ya_ref, s_ref, h_scr, bpad, kpad, *, nseg, c, n_chunks, heads, steps_per_seq):
    j = pl.program_id(1)
    step = pl.program_id(0) * pl.num_programs(1) + j

    @pl.when(j == 0)
    def _():
        _norm_mod_rows(x_ref, g_ref, sc_ref, sh_ref, h_scr, nseg)

    @pl.when(step % steps_per_seq == 0)
    def _():
        s_ref[...] = jnp.zeros(s_ref.shape, F32)

    def sub_dot(c0):
        cols = slice(c0, c0 + EPILOGUE_COLS)
        y = jnp.maximum(jnp.dot(h_scr[...], w_ref[:, cols], preferred_element_type=F32), 0.0)
        o_ref[:, cols] = (y * y).astype(o_ref.dtype)

    fillers = [functools.partial(sub_dot, c0) for c0 in range(0, w_ref.shape[1], EPILOGUE_COLS)]
    _hgrn_chunks(q_ref, f_ref, i_ref, gg_ref, lbc_ref, gn_ref, sel_ref, ya_ref, s_ref, bpad, kpad,
                 c=c, n_chunks=n_chunks, heads=heads, unroll=True, fillers=fillers)


def _mlp_up_hgrn(x, g, mod6, w_stack, layer, seq_len, p_other, lbc_stack, gn, heads):
    n, d = x.shape
    m = w_stack.shape[2]
    tn = COL_TILE
    tm, nseg, tps = _row_tiling(n, seq_len, ROW_TILE)
    ni, nj = n // tm, m // tn
    n_other = p_other.shape[0]
    t_step = n_other // (ni * nj)
    assert t_step * ni * nj == n_other and seq_len % t_step == 0 and t_step % SUBLANES == 0
    steps_per_seq = seq_len // t_step
    aw = heads * A_DK
    c, h_in_specs, h_args, ya_spec, h_scratch = _hgrn_specs(
        p_other, lbc_stack, layer, gn, heads, t_step, lambda i, j: i * nj + j)

    def mod_spec(k):
        return pl.BlockSpec((None, nseg, 1, d), lambda i, j: (k, i // tps, 0, 0))

    state_spec = pl.BlockSpec((1, heads, A_DK, A_DK), lambda i, j: ((i * nj + j) // steps_per_seq, 0, 0, 0))
    return pl.pallas_call(
        functools.partial(_mlp_up_hgrn_kernel, nseg=nseg, c=c, n_chunks=t_step // c, heads=heads,
                          steps_per_seq=steps_per_seq),
        grid=(ni, nj),
        in_specs=[
            pl.BlockSpec((tm, d), lambda i, j: (i, 0)),
            pl.BlockSpec((1, d), lambda i, j: (0, 0)),
            mod_spec(4),
            mod_spec(3),
            pl.BlockSpec((None, d, tn), lambda i, j: (layer, 0, j)),
        ] + h_in_specs,
        out_specs=[pl.BlockSpec((tm, tn), lambda i, j: (i, j)), ya_spec, state_spec],
        out_shape=[jax.ShapeDtypeStruct((n, m), BF16),
                   jax.ShapeDtypeStruct((n_other, aw), BF16),
                   jax.ShapeDtypeStruct((n_other // seq_len, heads, A_DK, A_DK), F32)],
        scratch_shapes=[pltpu.VMEM((tm, d), BF16)] + h_scratch,
        compiler_params=_cparams(("arbitrary", "arbitrary")),
        name="mlp_up_hgrn2",
    )(x, g.reshape(1, d), mod6, mod6, w_stack, *h_args)


def _hgrn(p, lbc_stack, layer, gn, s0, bsz, seq_len, heads):
    c = min(seq_len, SCAN_CHUNK)
    t_blk = min(seq_len, 256)
    nt = seq_len // t_blk
    aw = heads * A_DK

    def col_spec(seg):
        return pl.BlockSpec((t_blk, aw), lambda b, t: (b * nt + t, seg))

    sel = _hgrn_select(c)
    in_specs = [col_spec(0), col_spec(1), col_spec(2), col_spec(3),
                pl.BlockSpec((None, SUBLANES, aw), lambda b, t: (layer, 0, 0)),
                pl.BlockSpec((1, aw), lambda b, t: (0, 0)),
                pl.BlockSpec(sel.shape, lambda b, t: (0, 0))]
    args = [p, p, p, p, lbc_stack, gn.reshape(1, aw), sel]
    state_spec = pl.BlockSpec((1, heads, A_DK, A_DK), lambda b, t: (b, 0, 0, 0))
    if s0 is not None:
        in_specs.append(pl.BlockSpec((None, 1, heads, A_DK, A_DK), lambda b, t: (layer, b, 0, 0, 0)))
        args.append(s0)
    return pl.pallas_call(
        functools.partial(_hgrn_kernel, c=c, n_chunks=t_blk // c, heads=heads, has_init=s0 is not None),
        grid=(bsz, nt),
        in_specs=in_specs,
        out_specs=[pl.BlockSpec((t_blk, aw), lambda b, t: (b * nt + t, 0)), state_spec],
        out_shape=[jax.ShapeDtypeStruct((bsz * seq_len, aw), BF16),
                   jax.ShapeDtypeStruct((bsz, heads, A_DK, A_DK), F32)],
        scratch_shapes=[pltpu.VMEM((c + DIAG, aw), F32), pltpu.VMEM((c + DIAG, aw), F32)],
        compiler_params=_cparams(("arbitrary", "arbitrary")),
        name="hgrn2",
    )(*args)


def _ssd_kernel(*refs, t, bw, has_init):
    n_in = SSD_N_INPUTS
    s0_ref, c0_ref = (refs[n_in], refs[n_in + 1]) if has_init else (None, None)
    rest = refs[n_in + 2:] if has_init else refs[n_in:]
    yb_ref, s_out_ref, conv_out_ref, xpad, st = rest
    ti = pl.program_id(1)
    _ssd_reset(s0_ref, c0_ref, xpad, st, ti == 0)
    for _ in _ssd_phases(*refs[:n_in], yb_ref, xpad, st, t=t, bw=bw):
        pass
    _ssd_flush(conv_out_ref, s_out_ref, xpad, st, ti == pl.num_programs(1) - 1)


SSD_N_INPUTS = 12
SSD_PHASES = 9


def _ssd_phases(z_ref, xs_ref, bc_ref, dt_ref, cw_ref, cb_ref, dtb_ref, alog_ref, dsk_ref, gn_ref, exp_ref, sel_ref,
                yb_ref, xpad, st, *, t, bw):
    gw = bw // B_GROUPS
    n_bc = B_GROUPS * B_DSTATE
    pad = SUBLANES

    cur = jnp.concatenate([xs_ref[...], bc_ref[...]], axis=1)
    prev = xpad[...]
    row8 = lax.broadcasted_iota(jnp.int32, (pad, cur.shape[1]), 0)
    conv = cb_ref[...]
    for j in range(CONV_W):
        s = CONV_W - 1 - j
        if s == 0:
            tap = cur
        else:
            rolled = pltpu.roll(cur, s, 0)
            head = jnp.where(row8 < s, pltpu.roll(prev, s, 0), rolled[0:pad])
            tap = jnp.concatenate([head, rolled[pad:]], axis=0)
        conv = conv + tap * cw_ref[j:j + 1, :]
    xbc = _silu(conv)
    x = xbc[:, 0:bw]
    yield

    dt = _softplus(dt_ref[...] + dtb_ref[...])
    a = dt * (-jnp.exp(alog_ref[...]))
    cs = _dot3(sel_ref[...], a)
    a_cum = cs[0:t]
    a_tot = cs[t:2 * t]
    ex = _dot3_rhs(jnp.concatenate([dt, a_cum, a_tot - a_cum], axis=0), exp_ref[...])
    dt_e = ex[0:t]
    acum_e = ex[t:2 * t]
    dec_e = ex[2 * t:3 * t]
    atot_e = acum_e[t - 1:t, :]

    xdt = x * dt_e
    xw = (xdt * jnp.exp(dec_e)).astype(BF16)
    xdt_b = xdt.astype(BF16)
    a_cum_t = a_cum.T
    rr = lax.broadcasted_iota(jnp.int32, (t, t), 0)
    cc = lax.broadcasted_iota(jnp.int32, (t, t), 1)
    causal = cc <= rr
    lane = lax.broadcasted_iota(jnp.int32, (t, LANES), 1)
    heads_per_group = gw // B_HEADDIM
    pairs_per_group = gw // LANES
    yield

    y_groups = []
    for g in range(B_GROUPS):
        bg = xbc[:, bw + g * B_DSTATE:bw + (g + 1) * B_DSTATE].astype(BF16)
        cg = xbc[:, bw + n_bc + g * B_DSTATE:bw + n_bc + (g + 1) * B_DSTATE].astype(BF16)
        gs = slice(g * gw, (g + 1) * gw)
        st_g = st[:, gs]
        scores = lax.dot_general(cg, bg, (((1,), (1,)), ((), ())), preferred_element_type=F32)
        y_off = jnp.dot(cg, st_g.astype(BF16), preferred_element_type=F32) * jnp.exp(acum_e[:, gs])
        st[:, gs] = jnp.exp(atot_e[:, gs]) * st_g + lax.dot_general(
            bg, xw[:, gs], (((0,), (0,)), ((), ())), preferred_element_type=F32)
        yield
        y_pairs = []
        for pr in range(pairs_per_group):
            cols = slice(g * gw + pr * LANES, g * gw + (pr + 1) * LANES)
            xp = xdt_b[:, cols]
            acc = None
            for half in range(LANES // B_HEADDIM):
                h = g * heads_per_group + pr * (LANES // B_HEADDIM) + half
                diff = a_cum[:, h:h + 1] - a_cum_t[h:h + 1, :]
                lmat = jnp.where(causal, jnp.exp(jnp.where(causal, diff, 0.0)), 0.0)
                m = (scores * lmat).astype(BF16)
                in_half = (lane // B_HEADDIM) == half
                term = jnp.dot(m, jnp.where(in_half, xp, jnp.zeros_like(xp)), preferred_element_type=F32)
                acc = term if acc is None else acc + term
            y_pairs.append(acc)
            if pr % 2 == 1:
                yield
        y_groups.append(jnp.concatenate(y_pairs, axis=1) + y_off)
    y = jnp.concatenate(y_groups, axis=1) + dsk_ref[...] * x
    y = y * _silu(z_ref[...])
    outs = []
    for g in range(B_GROUPS):
        yg = y[:, g * gw:(g + 1) * gw]
        outs.append(yg * lax.rsqrt(jnp.mean(yg * yg, axis=-1, keepdims=True) + NORM_EPS))
    yb_ref[...] = (jnp.concatenate(outs, axis=1) * gn_ref[...]).astype(yb_ref.dtype)

    xpad[...] = cur[t - pad:t, :]
    yield


def _ssd_reset(s0_ref, c0_ref, xpad, st, first):
    @pl.when(first)
    def _():
        if s0_ref is not None:
            xpad[...] = c0_ref[0]
            st[...] = s0_ref[0].reshape(st.shape[1], st.shape[0]).T
        else:
            xpad[...] = jnp.zeros(xpad.shape, F32)
            st[...] = jnp.zeros(st.shape, F32)


def _ssd_flush(conv_out_ref, s_out_ref, xpad, st, last):
    @pl.when(last)
    def _():
        conv_out_ref[0] = xpad[...]
        s_out_ref[0] = st[...].T.reshape(s_out_ref.shape[1:])


def _ssd_operands(p, cols, prm, t, n_rows, n_seq, row_block, seq_block):
    bw = prm["bw"]
    heads = bw // B_HEADDIM
    n_bc = B_GROUPS * B_DSTATE
    cdim = bw + 2 * n_bc
    tri = np.arange(t)[None, :] <= np.arange(t)[:, None]
    sel = jnp.asarray(np.concatenate([tri, np.ones((t, t), bool)], 0).astype(np.float32), dtype=BF16)
    expand = np.zeros((LANES, bw), np.float32)
    expand[np.arange(bw) // B_HEADDIM, np.arange(bw)] = 1.0
    expand = jnp.asarray(expand, dtype=BF16)

    def blk(width, off):
        return pl.BlockSpec((t, width), lambda *g: (row_block(*g), off // width))

    def full(shape):
        return pl.BlockSpec(shape, lambda *g: (0,) * len(shape))

    in_specs = [blk(bw, cols["z"]), blk(bw, cols["xs"]), blk(2 * n_bc, cols["bc"]), blk(LANES, cols["dt"]),
                full((CONV_W, cdim)), full((1, cdim)), full((1, LANES)), full((1, LANES)),
                full((1, bw)), full((1, bw)), full(expand.shape), full(sel.shape)]
    args = [p, p, p, p, prm["conv_w"], prm["conv_b"], prm["dt_bias"], prm["a_log"], prm["d_skip"], prm["gn"],
            expand, sel]
    assert len(args) == SSD_N_INPUTS
    out_specs = [pl.BlockSpec((t, bw), lambda *g: (row_block(*g), 0)),
                 pl.BlockSpec((1, heads, B_HEADDIM, B_DSTATE), lambda *g: (seq_block(*g), 0, 0, 0)),
                 pl.BlockSpec((1, SUBLANES, cdim), lambda *g: (seq_block(*g), 0, 0))]
    out_shape = [jax.ShapeDtypeStruct((n_rows, bw), BF16),
                 jax.ShapeDtypeStruct((n_seq, heads, B_HEADDIM, B_DSTATE), F32),
                 jax.ShapeDtypeStruct((n_seq, SUBLANES, cdim), F32)]
    scratch = [pltpu.VMEM((SUBLANES, cdim), F32), pltpu.VMEM((B_DSTATE, bw), F32)]
    return in_specs, args, out_specs, out_shape, scratch


def _ssd(p, cols, prm, s0, c0, bsz, seq_len):
    bw = prm["bw"]
    heads = bw // B_HEADDIM
    cdim = bw + 2 * B_GROUPS * B_DSTATE
    t = min(seq_len, SSD_ROWS)
    nt = seq_len // t
    in_specs, args, out_specs, out_shape, scratch = _ssd_operands(
        p, cols, prm, t, bsz * seq_len, bsz, lambda b, ti: b * nt + ti, lambda b, ti: b)
    if s0 is not None:
        layer = prm["layer"]
        in_specs += [pl.BlockSpec((None, 1, heads, B_HEADDIM, B_DSTATE), lambda b, ti: (layer, b, 0, 0, 0)),
                     pl.BlockSpec((None, 1, SUBLANES, cdim), lambda b, ti: (layer, b, 0, 0))]
        args += [s0, c0]
    return pl.pallas_call(
        functools.partial(_ssd_kernel, t=t, bw=bw, has_init=s0 is not None),
        grid=(bsz, nt),
        in_specs=in_specs,
        out_specs=out_specs,
        out_shape=out_shape,
        scratch_shapes=scratch,
        compiler_params=_cparams(("arbitrary", "arbitrary")),
        name="ssd",
    )(*args)


def _down_sub_dots(a_ref, w_ref, x_ref, gate_ref, o_ref, acc_scr, nseg):
    kc = a_ref.shape[1] // DOWN_K_SPLIT

    def sub_dot(ki):
        rows = slice(ki * kc, (ki + 1) * kc)
        part = jnp.dot(a_ref[:, rows], w_ref[rows, :], preferred_element_type=F32)
        if ki == 0:
            acc_scr[...] = part
        elif ki < DOWN_K_SPLIT - 1:
            acc_scr[...] += part
        else:
            x = x_ref[...]
            acc = acc_scr[...] + part
            o_ref[...] = (_per_seq(x, nseg) + gate_ref[...] * _per_seq(acc, nseg)).reshape(x.shape)

    return [functools.partial(sub_dot, ki) for ki in range(DOWN_K_SPLIT)]


def _emit_interleaved(fillers, phases):
    per_filler = -(-len(phases) // max(len(fillers), 1))
    pos = 0
    for fill in fillers:
        fill()
        for ph in phases[pos:pos + per_filler]:
            ph()
        pos += per_filler
    for ph in phases[pos:]:
        ph()


def _mlp_down_ssd_kernel(a_ref, w_ref, x_ref, gate_ref, *refs, nseg, t, bw, steps_per_seq):
    ssd_in = refs[:SSD_N_INPUTS]
    o_ref, yb_ref, s_out_ref, conv_out_ref, acc_scr, xpad, st = refs[SSD_N_INPUTS:]
    step = pl.program_id(0) * pl.num_programs(1) + pl.program_id(1)
    pos = step % steps_per_seq
    _ssd_reset(None, None, xpad, st, pos == 0)
    gen = _ssd_phases(*ssd_in, yb_ref, xpad, st, t=t, bw=bw)
    phases = [functools.partial(next, gen, None) for _ in range(SSD_PHASES)]
    _emit_interleaved(_down_sub_dots(a_ref, w_ref, x_ref, gate_ref, o_ref, acc_scr, nseg), phases)
    _ssd_flush(conv_out_ref, s_out_ref, xpad, st, pos == steps_per_seq - 1)


def _mlp_down_ssd(a, w_stack, layer, x, mod6, k_gate, seq_len, p_other, cols, prm):
    n, kdim = a.shape
    d = w_stack.shape[2]
    tn = DOWN_TILE
    tm, nseg, tps = _row_tiling(n, seq_len, ROW_TILE)
    ni, nj = n // tm, d // tn
    n_other = p_other.shape[0]
    t = n_other // (ni * nj)
    assert t * ni * nj == n_other and seq_len % t == 0 and t % SUBLANES == 0 and kdim % DOWN_K_SPLIT == 0
    steps_per_seq = seq_len // t
    s_in_specs, s_args, s_out_specs, s_out_shape, s_scratch = _ssd_operands(
        p_other, cols, prm, t, n_other, n_other // seq_len,
        lambda i, j: i * nj + j, lambda i, j: (i * nj + j) // steps_per_seq)
    return pl.pallas_call(
        functools.partial(_mlp_down_ssd_kernel, nseg=nseg, t=t, bw=prm["bw"], steps_per_seq=steps_per_seq),
        grid=(ni, nj),
        in_specs=[
            pl.BlockSpec((tm, kdim), lambda i, j: (i, 0)),
            pl.BlockSpec((None, kdim, tn), lambda i, j: (layer, 0, j)),
            pl.BlockSpec((tm, tn), lambda i, j: (i, j)),
            pl.BlockSpec((None, nseg, 1, tn), lambda i, j: (k_gate, i // tps, 0, j)),
        ] + s_in_specs,
        out_specs=[pl.BlockSpec((tm, tn), lambda i, j: (i, j))] + s_out_specs,
        out_shape=[jax.ShapeDtypeStruct((n, d), F32)] + s_out_shape,
        scratch_shapes=[pltpu.VMEM((tm, tn), F32)] + s_scratch,
        compiler_params=_cparams(("arbitrary", "arbitrary")),
        name="mlp_down_ssd",
    )(a, w_stack, x, mod6, *s_args)


def _cmlp_kernel(u_ref, v_ref, lng_ref, lnb_ref, ws_ref, bst_ref, *out_refs, t, n_chunks, keep_v):
    yc_ref = out_refs[0]
    cw = u_ref.shape[1] // C_GROUPS
    rr = lax.broadcasted_iota(jnp.int32, (t, t), 0)
    cc = lax.broadcasted_iota(jnp.int32, (t, t), 1)
    wts = [jnp.where(cc <= rr, ws_ref[g, 0:t, 0:t], 0.0).astype(BF16) for g in range(C_GROUPS)]
    for ci in range(n_chunks):
        rows = slice(ci * t, (ci + 1) * t)
        u = _gelu(u_ref[rows, :])
        gv = _gelu(v_ref[rows, :])
        mu = jnp.mean(gv, axis=-1, keepdims=True)
        dv = gv - mu
        var = jnp.mean(dv * dv, axis=-1, keepdims=True)
        v = dv * lax.rsqrt(var + NORM_EPS) * lng_ref[...] + lnb_ref[...]
        if keep_v:
            out_refs[1][rows, :] = v
        vb = v.astype(BF16)
        for g in range(C_GROUPS):
            lanes = slice(g * cw, (g + 1) * cw)
            mixed = jnp.dot(wts[g], vb[:, lanes], preferred_element_type=F32) + bst_ref[0:t, g:g + 1]
            yc_ref[rows, lanes] = (u[:, lanes] * mixed).astype(yc_ref.dtype)


def _cmlp(p, col_u, col_v, prm, bsz, seq_len, keep_v):
    cw = prm["cw"]
    t = min(seq_len, CMLP_CHUNK)
    t_blk = min(seq_len, 4 * CMLP_CHUNK)
    n = bsz * seq_len

    def full(shape):
        return pl.BlockSpec(shape, lambda i: (0,) * len(shape))

    out_specs = [pl.BlockSpec((t_blk, cw), lambda i: (i, 0))]
    out_shape = [jax.ShapeDtypeStruct((n, cw), BF16)]
    if keep_v:
        out_specs.append(pl.BlockSpec((t_blk, cw), lambda i: (i, 0)))
        out_shape.append(jax.ShapeDtypeStruct((n, cw), F32))
    return pl.pallas_call(
        functools.partial(_cmlp_kernel, t=t, n_chunks=t_blk // t, keep_v=keep_v),
        grid=(n // t_blk,),
        in_specs=[pl.BlockSpec((t_blk, cw), lambda i: (i, col_u // cw)),
                  pl.BlockSpec((t_blk, cw), lambda i: (i, col_v // cw)),
                  full((1, cw)), full((1, cw)),
                  full((C_GROUPS, CMLP_CHUNK, CMLP_CHUNK)), full((CMLP_CHUNK, C_GROUPS))],
        out_specs=out_specs,
        out_shape=out_shape,
        compiler_params=_cparams(("arbitrary",)),
        name="cmlp",
    )(p, p, prm["ln_g"], prm["ln_b"], prm["ws"], prm["bs_t"])


def _run_trunk(x3, mod, st_hgrn, st_ssm, st_conv, keep_v, w):
    bsz, seq_len, d = x3.shape
    depth = mod.shape[0]
    aw = bw = cw = d // 2
    heads_a = aw // A_DK
    x = x3.reshape(bsz * seq_len, d)
    col = w["cols"]
    hgrn_out, ssm_out, conv_out, v_out = [], [], [], []
    for l in range(depth):
        mod6 = mod[l].reshape(bsz, 6, 1, d).transpose(1, 0, 2, 3)
        p, gates = _in_proj(x, w["norm1_g"][l], mod6, w["w_in"], l, seq_len, col["gate"])

        y_a, s_h = _hgrn(p, w["lbc"], l, w["hgrn_onorm_g"][l], st_hgrn, bsz, seq_len, heads_a)
        ssd_prm = dict(bw=bw, layer=l, conv_w=w["ssm_conv_w"][l], conv_b=w["ssm_conv_b"][l][None],
                       dt_bias=w["dt_bias_pad"][l][None], a_log=w["a_log_pad"][l][None],
                       d_skip=w["d_skip"][l][None], gn=w["ssm_onorm_g"][l][None])
        y_b, s_s, conv_tail = _ssd(p, col, ssd_prm, st_ssm, st_conv, bsz, seq_len)
        cm_prm = dict(cw=cw, ln_g=w["cmlp_ln_g"][l][None], ln_b=w["cmlp_ln_b"][l][None],
                      ws=w["cmlp_ws"][l], bs_t=w["cmlp_bs"][l].T)
        c_res = _cmlp(p, col["u"], col["v"], cm_prm, bsz, seq_len, keep_v)
        merged = _merge(y_a, y_b, c_res[0], w["w_branch"], l, gates, d)
        x = _proj_residual(merged, w["w_out"], l, x, mod6, 2, seq_len, "out_proj")
        hid = _mlp_up(x, w["norm2_g"][l], mod6, w["w_up"], l, seq_len)
        x = _proj_residual(hid, w["w_down"], l, x, mod6, 5, seq_len, "mlp_down")

        hgrn_out.append(s_h)
        ssm_out.append(s_s)
        conv_out.append(conv_tail[:, SUBLANES - (CONV_W - 1):, :])
        if keep_v:
            v_out.append(c_res[1].reshape(bsz, seq_len, cw))
    y = _final_norm(x, w["final_g"]).reshape(bsz, seq_len, d)
    return (y, jnp.stack(hgrn_out), jnp.stack(ssm_out), jnp.stack(conv_out),
            jnp.stack(v_out) if keep_v else None)


def _mlp_down_hgrn_kernel(a_ref, w_ref, x_ref, gate_ref, q_ref, f_ref, i_ref, gg_ref, lbc_ref, gn_ref, sel_ref,
                          o_ref, ya_ref, s_ref, acc_scr, bpad, kpad, *, nseg, c, n_chunks, heads, steps_per_seq):
    step = pl.program_id(0) * pl.num_programs(1) + pl.program_id(1)

    @pl.when(step % steps_per_seq == 0)
    def _():
        s_ref[...] = jnp.zeros(s_ref.shape, F32)

    fillers = _down_sub_dots(a_ref, w_ref, x_ref, gate_ref, o_ref, acc_scr, nseg)
    _hgrn_chunks(q_ref, f_ref, i_ref, gg_ref, lbc_ref, gn_ref, sel_ref, ya_ref, s_ref, bpad, kpad,
                 c=c, n_chunks=n_chunks, heads=heads, unroll=True, fillers=fillers)


def _mlp_down_hgrn(a, w_stack, layer, x, mod6, k_gate, seq_len, p_other, lbc_stack, lbc_layer, gn, heads):
    n, kdim = a.shape
    d = w_stack.shape[2]
    tn = DOWN_TILE
    tm, nseg, tps = _row_tiling(n, seq_len, ROW_TILE)
    ni, nj = n // tm, d // tn
    n_other = p_other.shape[0]
    t_step = n_other // (ni * nj)
    assert t_step * ni * nj == n_other and seq_len % t_step == 0 and t_step % SUBLANES == 0
    assert kdim % DOWN_K_SPLIT == 0
    steps_per_seq = seq_len // t_step
    aw = heads * A_DK
    c, h_in_specs, h_args, ya_spec, h_scratch = _hgrn_specs(
        p_other, lbc_stack, lbc_layer, gn, heads, t_step, lambda i, j: i * nj + j)
    state_spec = pl.BlockSpec((1, heads, A_DK, A_DK), lambda i, j: ((i * nj + j) // steps_per_seq, 0, 0, 0))
    return pl.pallas_call(
        functools.partial(_mlp_down_hgrn_kernel, nseg=nseg, c=c, n_chunks=t_step // c, heads=heads,
                          steps_per_seq=steps_per_seq),
        grid=(ni, nj),
        in_specs=[
            pl.BlockSpec((tm, kdim), lambda i, j: (i, 0)),
            pl.BlockSpec((None, kdim, tn), lambda i, j: (layer, 0, j)),
            pl.BlockSpec((tm, tn), lambda i, j: (i, j)),
            pl.BlockSpec((None, nseg, 1, tn), lambda i, j: (k_gate, i // tps, 0, j)),
        ] + h_in_specs,
        out_specs=[pl.BlockSpec((tm, tn), lambda i, j: (i, j)), ya_spec, state_spec],
        out_shape=[jax.ShapeDtypeStruct((n, d), F32),
                   jax.ShapeDtypeStruct((n_other, aw), BF16),
                   jax.ShapeDtypeStruct((n_other // seq_len, heads, A_DK, A_DK), F32)],
        scratch_shapes=[pltpu.VMEM((tm, tn), F32)] + h_scratch,
        compiler_params=_cparams(("arbitrary", "arbitrary")),
        name="mlp_down_hgrn2",
    )(a, w_stack, x, mod6, *h_args)


def _run_trunk_halves(x3, mod, w):
    bsz, seq_len, d = x3.shape
    depth = mod.shape[0]
    hb = bsz // 2
    aw = bw = cw = d // 2
    heads_a = aw // A_DK
    col = w["cols"]
    n_half = hb * seq_len
    x_full = x3.reshape(bsz * seq_len, d)
    xs = [x_full, x_full]
    row0 = [0, n_half]
    mods = [mod[:, :hb], mod[:, hb:]]
    hgrn_out, ssm_out, conv_out = [], [], []
    pending = None
    for l in range(depth):
        mod6 = [m[l].reshape(hb, 6, 1, d).transpose(1, 0, 2, 3) for m in mods]
        ssd_prm = dict(bw=bw, layer=l, conv_w=w["ssm_conv_w"][l], conv_b=w["ssm_conv_b"][l][None],
                       dt_bias=w["dt_bias_pad"][l][None], a_log=w["a_log_pad"][l][None],
                       d_skip=w["d_skip"][l][None], gn=w["ssm_onorm_g"][l][None])
        cm_prm = dict(cw=cw, ln_g=w["cmlp_ln_g"][l][None], ln_b=w["cmlp_ln_b"][l][None],
                      ws=w["cmlp_ws"][l], bs_t=w["cmlp_bs"][l].T)
        gn_a = w["hgrn_onorm_g"][l]

        def mix_merge(h, p, gates, y_a, ssd_res=None):
            y_b, s_s, conv_tail = ssd_res or _ssd(p, col, ssd_prm, None, None, hb, seq_len)
            y_c = _cmlp(p, col["u"], col["v"], cm_prm, hb, seq_len, False)[0]
            merged = _merge(y_a, y_b, y_c, w["w_branch"], l, gates, d)
            x_new = _proj_residual(merged, w["w_out"], l, xs[h], mod6[h], 2, seq_len, "out_proj", row0[h])
            row0[h] = 0
            return x_new, s_s, conv_tail

        p0, g0 = _in_proj(xs[0], w["norm1_g"][l], mod6[0], w["w_in"], l, seq_len, col["gate"], (row0[0], n_half))
        if pending is None:
            ya0, sh0 = _hgrn(p0, w["lbc"], l, gn_a, None, hb, seq_len, heads_a)
        else:
            hid1, x1_mid, mod6_prev = pending
            xs[1], ya0, sh0 = _mlp_down_hgrn(hid1, w["w_down"], l - 1, x1_mid, mod6_prev, 5, seq_len,
                                              p0, w["lbc"], l, gn_a, heads_a)
        p1, g1 = _in_proj(xs[1], w["norm1_g"][l], mod6[1], w["w_in"], l, seq_len, col["gate"], (row0[1], n_half))
        x0, ss0, ct0 = mix_merge(0, p0, g0, ya0)
        hid0, ya1, sh1 = _mlp_up_hgrn(x0, w["norm2_g"][l], mod6[0], w["w_up"], l, seq_len,
                                      p1, w["lbc"], gn_a, heads_a)
        xs[0], *ssd1 = _mlp_down_ssd(hid0, w["w_down"], l, x0, mod6[0], 5, seq_len, p1, col, ssd_prm)
        x1_mid, ss1, ct1 = mix_merge(1, p1, g1, ya1, tuple(ssd1))
        hid1 = _mlp_up(x1_mid, w["norm2_g"][l], mod6[1], w["w_up"], l, seq_len)
        pending = (hid1, x1_mid, mod6[1])

        hgrn_out.append(jnp.concatenate([sh0, sh1], axis=0))
        ssm_out.append(jnp.concatenate([ss0, ss1], axis=0))
        conv_out.append(jnp.concatenate([ct0, ct1], axis=0)[:, SUBLANES - (CONV_W - 1):, :])
    hid1, x1_mid, mod6_prev = pending
    xs[1] = _proj_residual(hid1, w["w_down"], depth - 1, x1_mid, mod6_prev, 5, seq_len, "mlp_down")
    y = _final_norm_pair(xs[0], xs[1], w["final_g"]).reshape(bsz, seq_len, d)
    return (y, jnp.stack(hgrn_out), jnp.stack(ssm_out), jnp.stack(conv_out), None)


def kernel(x_prompt, x_sample, state_hgrn, state_ssm, state_conv, c_prompt, c_sample, norm1_g, norm2_g,
           w_mod, b_mod, w_in, hgrn_lb, hgrn_onorm_g, ssm_conv_w, ssm_conv_b, ssm_dt_bias, ssm_a_log, ssm_d,
           ssm_onorm_g, cmlp_ln_g, cmlp_ln_b, cmlp_ws, cmlp_bs, w_branch, w_out, w_up, w_down, final_g):
    d = x_prompt.shape[-1]
    depth = w_in.shape[0]
    aw = bw = cw = d // 2
    n_bc = B_GROUPS * B_DSTATE
    heads_b = bw // B_HEADDIM
    assert heads_b <= LANES and bw % LANES == 0 and DT_PAD % LANES == 0

    o_dt = 4 * aw + bw + bw + 2 * n_bc
    o_u = o_dt + heads_b
    w_in_r = jnp.concatenate(
        [w_in[:, :, :o_u].astype(BF16), jnp.zeros((depth, d, DT_PAD - heads_b), BF16),
         w_in[:, :, o_u:].astype(BF16)], axis=2)
    cols = dict(z=4 * aw, xs=4 * aw + bw, bc=4 * aw + 2 * bw, dt=o_dt)
    cols["u"] = o_dt + DT_PAD
    cols["v"] = cols["u"] + cw
    cols["gate"] = cols["v"] + cw
    assert cols["gate"] % IN_TILE == 0 and w_in_r.shape[2] % IN_TILE == 0

    pad_h = lambda a: jnp.pad(a.astype(F32), ((0, 0), (0, LANES - heads_b)))
    w = dict(
        cols=cols, w_in=w_in_r,
        norm1_g=norm1_g, norm2_g=norm2_g, final_g=final_g,
        lbc=_lb_consts(hgrn_lb), hgrn_onorm_g=hgrn_onorm_g,
        ssm_conv_w=ssm_conv_w, ssm_conv_b=ssm_conv_b,
        dt_bias_pad=pad_h(ssm_dt_bias), a_log_pad=pad_h(ssm_a_log),
        d_skip=jnp.repeat(ssm_d.astype(F32), B_HEADDIM, axis=1), ssm_onorm_g=ssm_onorm_g,
        cmlp_ln_g=cmlp_ln_g, cmlp_ln_b=cmlp_ln_b, cmlp_ws=cmlp_ws, cmlp_bs=cmlp_bs,
        w_branch=w_branch.astype(BF16).reshape(depth, N_BRANCH, aw, d), w_out=w_out.astype(BF16),
        w_up=w_up.astype(BF16), w_down=w_down.astype(BF16),
    )

    nb = x_prompt.shape[0]
    mod = _modulation(jnp.concatenate([c_prompt, c_sample], axis=0), w_mod, b_mod)
    if nb % 2 == 0:
        y_p, hgrn_p, ssm_p, conv_p, _ = _run_trunk_halves(x_prompt, mod[:, :nb], w)
    else:
        y_p, hgrn_p, ssm_p, conv_p, _ = _run_trunk(x_prompt, mod[:, :nb], None, None, None, False, w)
    conv_pad = jnp.pad(state_conv, ((0, 0), (0, 0), (SUBLANES - (CONV_W - 1), 0), (0, 0)))
    y_s, hgrn_s, ssm_s, conv_s, v_s = _run_trunk(x_sample, mod[:, nb:], state_hgrn, state_ssm, conv_pad, True, w)
    return (y_p, y_s, hgrn_p, ssm_p, conv_p, hgrn_s, ssm_s, conv_s, v_s)
```

```python
import functools
import math

import numpy as np
import jax
import jax.numpy as jnp
from jax import lax
from jax.experimental import pallas as pl
from jax.experimental.pallas import tpu as pltpu

F32 = jnp.float32
BF16 = jnp.bfloat16

A_DK = 128
B_HEADDIM = 64
B_GROUPS = 2
B_DSTATE = 128
CONV_W = 4
C_GROUPS = 4
CMLP_CHUNK = 128
N_BRANCH = 3
SCAN_CHUNK = 64
NORM_EPS = 1e-6
LB_FLOOR = 1e-30
LOG2E = 1.4426950408889634

LANES = 128
SUBLANES = 8
VMEM_LIMIT = 56 * 1024 * 1024
DIAG = SUBLANES
DT_PAD = 512
ROW_TILE = 1024
COL_TILE = 1024
IN_TILE = 1536
MERGE_TILE = 512
DOWN_TILE = 256
EPILOGUE_COLS = 256
SSD_ROWS = 128
DOWN_K_SPLIT = 4
NORM_ROWS = 128


def _cparams(sem):
    return pltpu.CompilerParams(dimension_semantics=sem, vmem_limit_bytes=VMEM_LIMIT)


def _split3(x):
    hi = x.astype(BF16)
    r1 = x - hi.astype(F32)
    mid = r1.astype(BF16)
    lo = (r1 - mid.astype(F32)).astype(BF16)
    return hi, mid, lo


def _dot3(sel, x):
    hi, mid, lo = _split3(x)
    d = lambda p: jnp.dot(sel, p, preferred_element_type=F32)
    return d(hi) + d(mid) + d(lo)


def _dot3_rhs(x, sel):
    hi, mid, lo = _split3(x)
    d = lambda p: jnp.dot(p, sel, preferred_element_type=F32)
    return d(hi) + d(mid) + d(lo)


def _silu(x):
    return x / (1.0 + jnp.exp(-x))


def _softplus(x):
    return jnp.maximum(x, 0.0) + jnp.log1p(jnp.exp(-jnp.abs(x)))


def _gelu(x):
    return 0.5 * x * (1.0 + lax.erf(x * (1.0 / math.sqrt(2.0))))


def _lb_kernel(lb_ref, out_ref, *, depth):
    x = lb_ref[...]
    m = jnp.max(x, axis=0, keepdims=True)
    e = jnp.exp(x - m)
    p = e / jnp.sum(e, axis=0, keepdims=True)
    acc = jnp.zeros_like(p[0:1])
    zeros5 = jnp.zeros((SUBLANES - 3, x.shape[1]), F32)
    for l in range(depth):
        acc = acc + p[l:l + 1]
        lb = acc - p[0:1]
        out_ref[l] = jnp.concatenate(
            [jnp.log(jnp.maximum(lb, LB_FLOOR)), jnp.log1p(-lb), 1.0 - lb, zeros5], axis=0)


def _lb_consts(hgrn_lb):
    depth, aw = hgrn_lb.shape
    return pl.pallas_call(
        functools.partial(_lb_kernel, depth=depth),
        out_shape=jax.ShapeDtypeStruct((depth, SUBLANES, aw), F32),
        name="hgrn_lb",
    )(hgrn_lb.astype(F32))


def _mod_kernel(c_ref, w_ref, b_ref, o_ref):
    cs = _silu(c_ref[...]).astype(BF16)
    o_ref[...] = jnp.dot(cs, w_ref[...].astype(BF16), preferred_element_type=F32) + b_ref[...]


def _modulation(c_all, w_mod, b_mod):
    depth, d, n6 = w_mod.shape
    s = c_all.shape[0]
    tn = 1024
    return pl.pallas_call(
        _mod_kernel,
        grid=(depth, n6 // tn),
        in_specs=[
            pl.BlockSpec((s, d), lambda l, j: (0, 0)),
            pl.BlockSpec((None, d, tn), lambda l, j: (l, 0, j)),
            pl.BlockSpec((None, 1, tn), lambda l, j: (l, 0, j)),
        ],
        out_specs=pl.BlockSpec((None, s, tn), lambda l, j: (l, 0, j)),
        out_shape=jax.ShapeDtypeStruct((depth, s, n6), F32),
        compiler_params=_cparams(("arbitrary", "arbitrary")),
        name="adaln_mod",
    )(c_all, w_mod, b_mod.reshape(depth, 1, n6))


def _row_tiling(n_tok, seq_len, tm_max):
    tm = min(tm_max, n_tok)
    if seq_len >= tm:
        assert seq_len % tm == 0
        return tm, 1, seq_len // tm
    assert tm % seq_len == 0
    return tm, tm // seq_len, 1


def _per_seq(x, nseg):
    tm, d = x.shape
    return x.reshape(nseg, tm // nseg, d)


def _final_norm_kernel(x_ref, g_ref, o_ref):
    x = x_ref[...]
    o_ref[...] = x * lax.rsqrt(jnp.mean(x * x, axis=-1, keepdims=True) + NORM_EPS) * g_ref[...]


def _final_norm(x, g):
    n, d = x.shape
    tm = min(512, n)
    return pl.pallas_call(
        _final_norm_kernel,
        grid=(n // tm,),
        in_specs=[pl.BlockSpec((tm, d), lambda i: (i, 0)), pl.BlockSpec((1, d), lambda i: (0, 0))],
        out_specs=pl.BlockSpec((tm, d), lambda i: (i, 0)),
        out_shape=jax.ShapeDtypeStruct((n, d), F32),
        compiler_params=_cparams(("arbitrary",)),
        name="final_norm",
    )(x, g.reshape(1, d))


def _final_norm_pair_kernel(xa_ref, xb_ref, g_ref, o_ref, *, nt):
    def norm(x):
        return x * lax.rsqrt(jnp.mean(x * x, axis=-1, keepdims=True) + NORM_EPS) * g_ref[...]

    @pl.when(pl.program_id(0) < nt)
    def _():
        o_ref[...] = norm(xa_ref[...])

    @pl.when(pl.program_id(0) >= nt)
    def _():
        o_ref[...] = norm(xb_ref[...])


def _final_norm_pair(xa, xb, g):
    n, d = xa.shape
    tm = min(512, n)
    nt = n // tm
    return pl.pallas_call(
        functools.partial(_final_norm_pair_kernel, nt=nt),
        grid=(2 * nt,),
        in_specs=[pl.BlockSpec((tm, d), lambda i: (jnp.minimum(i, nt - 1), 0)),
                  pl.BlockSpec((tm, d), lambda i: (jnp.maximum(i - nt, 0), 0)),
                  pl.BlockSpec((1, d), lambda i: (0, 0))],
        out_specs=pl.BlockSpec((tm, d), lambda i: (i, 0)),
        out_shape=jax.ShapeDtypeStruct((2 * n, d), F32),
        compiler_params=_cparams(("arbitrary",)),
        name="final_norm",
    )(xa, xb, g.reshape(1, d))


def _norm_mod_rows(x_ref, g_ref, sc_ref, sh_ref, h_scr, nseg):
    tm = x_ref.shape[0]
    seg = tm // nseg
    rb = min(seg, NORM_ROWS)
    g = g_ref[...]

    def body(r, carry):
        rows = pl.ds(pl.multiple_of(r * rb, rb), rb)
        s = r // (seg // rb)
        x = x_ref[rows, :]
        gain = g * (1.0 + sc_ref[s])
        inv = lax.rsqrt(jnp.mean(x * x, axis=-1, keepdims=True) + NORM_EPS)
        h_scr[rows, :] = (x * inv * gain + sh_ref[s]).astype(BF16)
        return carry

    lax.fori_loop(0, tm // rb, body, 0, unroll=2 if (tm // rb) % 2 == 0 else 1)


def _mlp_up_kernel(x_ref, g_ref, sc_ref, sh_ref, w_ref, o_ref, h_scr, *, nseg):
    @pl.when(pl.program_id(1) == 0)
    def _():
        _norm_mod_rows(x_ref, g_ref, sc_ref, sh_ref, h_scr, nseg)

    y = jnp.maximum(jnp.dot(h_scr[...], w_ref[...], preferred_element_type=F32), 0.0)
    o_ref[...] = (y * y).astype(o_ref.dtype)


def _in_proj_kernel(x_ref, g_ref, sc_ref, sh_ref, w_ref, p_ref, gate_ref, h_scr, *, nseg, n_main):
    j = pl.program_id(1)

    @pl.when(j == 0)
    def _():
        _norm_mod_rows(x_ref, g_ref, sc_ref, sh_ref, h_scr, nseg)

    def project():
        return lax.dot_general(h_scr[...], w_ref[...], (((1,), (1,)), ((), ())), preferred_element_type=F32)

    @pl.when(j < n_main)
    def _():
        p_ref[...] = project()

    @pl.when(j >= n_main)
    def _():
        gate_ref[...] = project().astype(gate_ref.dtype)


def _norm_proj_call(kern, x, g, mod6, k_scale, k_shift, w_stack, layer, seq_len, tn, out_specs, out_shape, name):
    n, d = x.shape
    m = w_stack.shape[2]
    tm, nseg, tps = _row_tiling(n, seq_len, ROW_TILE)

    def mod_spec(k):
        return pl.BlockSpec((None, nseg, 1, d), lambda i, j: (k, i // tps, 0, 0))

    return pl.pallas_call(
        functools.partial(kern, nseg=nseg),
        grid=(n // tm, m // tn),
        in_specs=[
            pl.BlockSpec((tm, d), lambda i, j: (i, 0)),
            pl.BlockSpec((1, d), lambda i, j: (0, 0)),
            mod_spec(k_scale),
            mod_spec(k_shift),
            pl.BlockSpec((None, d, tn), lambda i, j: (layer, 0, j)),
        ],
        out_specs=out_specs(tm),
        out_shape=out_shape,
        scratch_shapes=[pltpu.VMEM((tm, d), BF16)],
        compiler_params=_cparams(("arbitrary", "arbitrary")),
        name=name,
    )(x, g.reshape(1, d), mod6, mod6, w_stack)


def _mlp_up(x, g, mod6, w_stack, layer, seq_len):
    n = x.shape[0]
    m = w_stack.shape[2]
    tn = COL_TILE
    return _norm_proj_call(
        _mlp_up_kernel, x, g, mod6, 4, 3, w_stack, layer, seq_len, tn,
        lambda tm: pl.BlockSpec((tm, tn), lambda i, j: (i, j)),
        jax.ShapeDtypeStruct((n, m), BF16), "mlp_up")


def _in_proj(x, g, mod6, w_stack, layer, seq_len, main_cols, rows=None):
    d = x.shape[1]
    row0, n = rows if rows is not None else (0, x.shape[0])
    m = w_stack.shape[1]
    tn = IN_TILE
    n_main = main_cols // tn
    tm, nseg, tps = _row_tiling(n, seq_len, ROW_TILE)

    def mod_spec(k):
        return pl.BlockSpec((None, nseg, 1, d), lambda i, j: (k, i // tps, 0, 0))

    return pl.pallas_call(
        functools.partial(_in_proj_kernel, nseg=nseg, n_main=n_main),
        grid=(n // tm, m // tn),
        in_specs=[
            pl.BlockSpec((tm, d), lambda i, j: (i + row0 // tm, 0)),
            pl.BlockSpec((1, d), lambda i, j: (0, 0)),
            mod_spec(1),
            mod_spec(0),
            pl.BlockSpec((None, tn, d), lambda i, j: (layer, j, 0)),
        ],
        out_specs=[pl.BlockSpec((tm, tn), lambda i, j: (i, jnp.minimum(j, n_main - 1))),
                   pl.BlockSpec((tm, tn), lambda i, j: (i, jnp.maximum(j - n_main, 0)))],
        out_shape=[jax.ShapeDtypeStruct((n, main_cols), F32), jax.ShapeDtypeStruct((n, m - main_cols), BF16)],
        scratch_shapes=[pltpu.VMEM((tm, d), BF16)],
        compiler_params=_cparams(("arbitrary", "arbitrary")),
        name="in_proj",
    )(x, g.reshape(1, d), mod6, mod6, w_stack)


def _proj_res_kernel(a_ref, w_ref, x_ref, gate_ref, o_ref, *, nseg, nk):
    part = jnp.dot(a_ref[...], w_ref[...], preferred_element_type=F32)

    def finish(acc):
        x = x_ref[...]
        o_ref[...] = (_per_seq(x, nseg) + gate_ref[...] * _per_seq(acc, nseg)).reshape(x.shape)

    if nk == 1:
        finish(part)
        return
    k = pl.program_id(2)

    @pl.when(k == 0)
    def _():
        o_ref[...] = part

    @pl.when((k > 0) & (k < nk - 1))
    def _():
        o_ref[...] += part

    @pl.when(k == nk - 1)
    def _():
        finish(o_ref[...] + part)


def _proj_residual(a, w_stack, layer, x, mod6, k_gate, seq_len, name, x_row0=0):
    n, kdim = a.shape
    d = w_stack.shape[2]
    tn = COL_TILE if kdim <= d else DOWN_TILE
    tk = kdim
    nk = kdim // tk
    tm, nseg, tps = _row_tiling(n, seq_len, ROW_TILE)
    return pl.pallas_call(
        functools.partial(_proj_res_kernel, nseg=nseg, nk=nk),
        grid=(n // tm, d // tn, nk),
        in_specs=[
            pl.BlockSpec((tm, tk), lambda i, j, k: (i, k)),
            pl.BlockSpec((None, tk, tn), lambda i, j, k: (layer, k, j)),
            pl.BlockSpec((tm, tn), lambda i, j, k: (i + x_row0 // tm, j)),
            pl.BlockSpec((None, nseg, 1, tn), lambda i, j, k: (k_gate, i // tps, 0, j)),
        ],
        out_specs=pl.BlockSpec((tm, tn), lambda i, j, k: (i, j)),
        out_shape=jax.ShapeDtypeStruct((n, d), F32),
        compiler_params=_cparams(("arbitrary", "arbitrary", "arbitrary")),
        name=name,
    )(a, w_stack, x, mod6)


def _merge_kernel(ya_ref, yb_ref, yc_ref, wa_ref, wb_ref, wc_ref, ga_ref, gb_ref, gc_ref, o_ref):
    acc = None
    for y_ref, w_ref, g_ref in ((ya_ref, wa_ref, ga_ref), (yb_ref, wb_ref, gb_ref), (yc_ref, wc_ref, gc_ref)):
        gate = 1.0 / (1.0 + jnp.exp(-g_ref[...].astype(F32)))
        term = gate * jnp.dot(y_ref[...], w_ref[...], preferred_element_type=F32)
        acc = term if acc is None else acc + term
    o_ref[...] = acc.astype(o_ref.dtype)


def _merge(ya, yb, yc, w_stack, layer, gates, d):
    n, bw = ya.shape
    tn = MERGE_TILE
    tm = min(ROW_TILE, n)
    gsteps = d // tn
    y_spec = pl.BlockSpec((tm, bw), lambda i, j: (i, 0))

    def w_spec(k):
        return pl.BlockSpec((None, None, bw, tn), lambda i, j: (layer, k, 0, j))

    def g_spec(k):
        return pl.BlockSpec((tm, tn), lambda i, j: (i, k * gsteps + j))

    return pl.pallas_call(
        _merge_kernel,
        grid=(n // tm, d // tn),
        in_specs=[y_spec, y_spec, y_spec, w_spec(0), w_spec(1), w_spec(2), g_spec(0), g_spec(1), g_spec(2)],
        out_specs=pl.BlockSpec((tm, tn), lambda i, j: (i, j)),
        out_shape=jax.ShapeDtypeStruct((n, d), BF16),
        compiler_params=_cparams(("arbitrary", "arbitrary")),
        name="merge",
    )(ya, yb, yc, w_stack, w_stack, w_stack, gates, gates, gates)


def _hgrn_levels(c):
    lv, m = [], c // 2
    while m >= DIAG:
        lv.append(m)
        m //= 2
    return tuple(lv)


def _hgrn_select(c):
    tri = np.arange(c)[None, :] <= np.arange(c)[:, None]
    return jnp.asarray(tri.astype(np.float32), dtype=BF16)


def _hgrn_kernel(*refs, c, n_chunks, heads, has_init):
    if has_init:
        q_ref, f_ref, i_ref, g_ref, lbc_ref, gn_ref, sel_ref, s0_ref, ya_ref, s_ref, bpad, kpad = refs
    else:
        q_ref, f_ref, i_ref, g_ref, lbc_ref, gn_ref, sel_ref, ya_ref, s_ref, bpad, kpad = refs

    @pl.when(pl.program_id(1) == 0)
    def _():
        s_ref[...] = s0_ref[...] if has_init else jnp.zeros(s_ref.shape, F32)

    _hgrn_chunks(q_ref, f_ref, i_ref, g_ref, lbc_ref, gn_ref, sel_ref, ya_ref, s_ref, bpad, kpad,
                 c=c, n_chunks=n_chunks, heads=heads, unroll=False)


def _hgrn_chunks(q_ref, f_ref, i_ref, g_ref, lbc_ref, gn_ref, sel_ref, ya_ref, s_ref, bpad, kpad,
                 *, c, n_chunks, heads, unroll, fillers=()):
    levels = _hgrn_levels(c)
    nl = len(levels)
    width = heads * A_DK
    head_lanes = [slice(hh * A_DK, (hh + 1) * A_DK) for hh in range(heads)]

    bpad[0:DIAG, :] = jnp.zeros((DIAG, width), F32)
    kpad[0:DIAG, :] = jnp.zeros((DIAG, width), F32)

    sel = sel_ref[...]
    log_lb = lbc_ref[0:1, :]
    log1m_lb = lbc_ref[1:2, :]
    one_m_lb = lbc_ref[2:3, :]
    row = lax.broadcasted_iota(jnp.int32, (c, width), 0)
    rr = lax.broadcasted_iota(jnp.int32, (c, c), 0)
    cc = lax.broadcasted_iota(jnp.int32, (c, c), 1)
    upper = [(row & m) != 0 for m in levels]
    same = [(rr ^ cc) < 2 * m for m in levels]
    dmat = jnp.where(((rr ^ cc) < DIAG) & (cc <= rr), rr - cc, -1)
    nt_dims = (((1,), (1,)), ((), ()))
    tn_dims = (((0,), (0,)), ((), ()))

    def chunk(ci):
        rows = slice(ci * c, (ci + 1) * c) if unroll else pl.ds(pl.multiple_of(ci * c, c), c)
        z = f_ref[rows, :]
        aq = q_ref[rows, :]

        log_sig = jnp.minimum(z, 0.0) - jnp.log(1.0 + jnp.exp(-jnp.abs(z)))
        bb = log1m_lb + log_sig
        log_f = jnp.maximum(log_lb, bb) + jnp.log(1.0 + jnp.exp(-jnp.abs(log_lb - bb)))
        k = one_m_lb / (1.0 + jnp.exp(z))
        q = _silu(aq)

        b = _dot3(sel, log_f * LOG2E)
        bpad[DIAG:DIAG + c, :] = b
        kpad[DIAG:DIAG + c, :] = k
        b_tot = bpad[DIAG + c - 1:DIAG + c, :]
        vb = i_ref[rows, :].astype(BF16)
        q_in = (q * jnp.exp2(b)).astype(BF16)
        k_out = (k * jnp.exp2(b_tot - b)).astype(BF16)
        yield

        att = [jnp.zeros((c, c), F32) for _ in range(heads)]
        for li in range(nl):
            m = levels[li]
            ref_b = jnp.concatenate(
                [jnp.broadcast_to(bpad[DIAG + g0 + m - 1:DIAG + g0 + m, :], (2 * m, width))
                 for g0 in range(0, c, 2 * m)], axis=0)
            e = jnp.exp2(-jnp.abs(b - ref_b))
            qs = jnp.where(upper[li], q * e, 0.0).astype(BF16)
            ks = jnp.where(upper[li], 0.0, k * e).astype(BF16)
            for hh, lanes in enumerate(head_lanes):
                a_l = lax.dot_general(qs[:, lanes], ks[:, lanes], nt_dims, preferred_element_type=F32)
                att[hh] = att[hh] + jnp.where(same[li], a_l, 0.0)
        yield

        for dlt in range(DIAG):
            if dlt == 0:
                w = q * k
            else:
                ksh = kpad[DIAG - dlt:DIAG - dlt + c, :]
                bsh = bpad[DIAG - dlt:DIAG - dlt + c, :]
                w = q * ksh * jnp.exp2(b - bsh)
            for hh, lanes in enumerate(head_lanes):
                col = jnp.sum(w[:, lanes], axis=-1, keepdims=True)
                att[hh] = jnp.where(dmat == dlt, col, att[hh])
            if dlt in (DIAG // 2 - 1, DIAG - 1):
                yield

        decay_row = jnp.exp2(b_tot)
        outs = []
        for hh, lanes in enumerate(head_lanes):
            s_prev = s_ref[0, hh]
            o = jnp.dot(q_in[:, lanes], s_prev.astype(BF16), preferred_element_type=F32)
            o = o + jnp.dot(att[hh].astype(BF16), vb[:, lanes], preferred_element_type=F32)
            decay_col = jnp.broadcast_to(decay_row[:, lanes], (A_DK, A_DK)).T
            s_ref[0, hh] = decay_col * s_prev + lax.dot_general(
                k_out[:, lanes], vb[:, lanes], tn_dims, preferred_element_type=F32)
            outs.append(o * lax.rsqrt(jnp.mean(o * o, axis=-1, keepdims=True) + NORM_EPS))
        y = jnp.concatenate(outs, axis=1) * gn_ref[...] * _silu(g_ref[rows, :])
        ya_ref[rows, :] = y.astype(ya_ref.dtype)
        yield

    if not unroll:
        def body(ci, carry):
            for _ in chunk(ci):
                pass
            return carry

        lax.fori_loop(0, n_chunks, body, 0)
        return

    _emit_interleaved(fillers, [ph for ci in range(n_chunks) for ph in _phases_of(chunk(ci))])


HGRN_PHASES = 5


def _phases_of(gen):
    return [functools.partial(next, gen, None) for _ in range(HGRN_PHASES)]


def _hgrn_specs(p, lbc_stack, layer, gn, heads, t_blk, row_block):
    aw = heads * A_DK
    c = min(t_blk, SCAN_CHUNK)
    sel = _hgrn_select(c)

    def col_spec(seg):
        return pl.BlockSpec((t_blk, aw), lambda *g: (row_block(*g), seg))

    in_specs = [col_spec(0), col_spec(1), col_spec(2), col_spec(3),
                pl.BlockSpec((None, SUBLANES, aw), lambda *g: (layer, 0, 0)),
                pl.BlockSpec((1, aw), lambda *g: (0, 0)),
                pl.BlockSpec(sel.shape, lambda *g: (0, 0))]
    args = [p, p, p, p, lbc_stack, gn.reshape(1, aw), sel]
    scratch = [pltpu.VMEM((c + DIAG, aw), F32), pltpu.VMEM((c + DIAG, aw), F32)]
    return c, in_specs, args, pl.BlockSpec((t_blk, aw), lambda *g: (row_block(*g), 0)), scratch


def _mlp_up_hgrn_kernel(x_ref, g_ref, sc_ref, sh_ref, w_ref, q_ref, f_ref, i_ref, gg_ref, lbc_ref, gn_ref, sel_ref,
                        o_ref, ya_ref, s_ref, h_scr, bpad, kpad, *, nseg, c, n_chunks, heads, steps_per_seq):
    j = pl.program_id(1)
    step = pl.program_id(0) * pl.num_programs(1) + j

    @pl.when(j == 0)
    def _():
        _norm_mod_rows(x_ref, g_ref, sc_ref, sh_ref, h_scr, nseg)

    @pl.when(step % steps_per_seq == 0)
    def _():
        s_ref[...] = jnp.zeros(s_ref.shape, F32)

    def sub_dot(c0):
        cols = slice(c0, c0 + EPILOGUE_COLS)
        y = jnp.maximum(jnp.dot(h_scr[...], w_ref[:, cols], preferred_element_type=F32), 0.0)
        o_ref[:, cols] = (y * y).astype(o_ref.dtype)

    fillers = [functools.partial(sub_dot, c0) for c0 in range(0, w_ref.shape[1], EPILOGUE_COLS)]
    _hgrn_chunks(q_ref, f_ref, i_ref, gg_ref, lbc_ref, gn_ref, sel_ref, ya_ref, s_ref, bpad, kpad,
                 c=c, n_chunks=n_chunks, heads=heads, unroll=True, fillers=fillers)


def _mlp_up_hgrn(x, g, mod6, w_stack, layer, seq_len, p_other, lbc_stack, gn, heads):
    n, d = x.shape
    m = w_stack.shape[2]
    tn = COL_TILE
    tm, nseg, tps = _row_tiling(n, seq_len, ROW_TILE)
    ni, nj = n // tm, m // tn
    n_other = p_other.shape[0]
    t_step = n_other // (ni * nj)
    assert t_step * ni * nj == n_other and seq_len % t_step == 0 and t_step % SUBLANES == 0
    steps_per_seq = seq_len // t_step
    aw = heads * A_DK
    c, h_in_specs, h_args, ya_spec, h_scratch = _hgrn_specs(
        p_other, lbc_stack, layer, gn, heads, t_step, lambda i, j: i * nj + j)

    def mod_spec(k):
        return pl.BlockSpec((None, nseg, 1, d), lambda i, j: (k, i // tps, 0, 0))

    state_spec = pl.BlockSpec((1, heads, A_DK, A_DK), lambda i, j: ((i * nj + j) // steps_per_seq, 0, 0, 0))
    return pl.pallas_call(
        functools.partial(_mlp_up_hgrn_kernel, nseg=nseg, c=c, n_chunks=t_step // c, heads=heads,
                          steps_per_seq=steps_per_seq),
        grid=(ni, nj),
        in_specs=[
            pl.BlockSpec((tm, d), lambda i, j: (i, 0)),
            pl.BlockSpec((1, d), lambda i, j: (0, 0)),
            mod_spec(4),
            mod_spec(3),
            pl.BlockSpec((None, d, tn), lambda i, j: (layer, 0, j)),
        ] + h_in_specs,
        out_specs=[pl.BlockSpec((tm, tn), lambda i, j: (i, j)), ya_spec, state_spec],
        out_shape=[jax.ShapeDtypeStruct((n, m), BF16),
                   jax.ShapeDtypeStruct((n_other, aw), BF16),
                   jax.ShapeDtypeStruct((n_other // seq_len, heads, A_DK, A_DK), F32)],
        scratch_shapes=[pltpu.VMEM((tm, d), BF16)] + h_scratch,
        compiler_params=_cparams(("arbitrary", "arbitrary")),
        name="mlp_up_hgrn2",
    )(x, g.reshape(1, d), mod6, mod6, w_stack, *h_args)


def _hgrn(p, lbc_stack, layer, gn, s0, bsz, seq_len, heads):
    c = min(seq_len, SCAN_CHUNK)
    t_blk = min(seq_len, 256)
    nt = seq_len // t_blk
    aw = heads * A_DK

    def col_spec(seg):
        return pl.BlockSpec((t_blk, aw), lambda b, t: (b * nt + t, seg))

    sel = _hgrn_select(c)
    in_specs = [col_spec(0), col_spec(1), col_spec(2), col_spec(3),
                pl.BlockSpec((None, SUBLANES, aw), lambda b, t: (layer, 0, 0)),
                pl.BlockSpec((1, aw), lambda b, t: (0, 0)),
                pl.BlockSpec(sel.shape, lambda b, t: (0, 0))]
    args = [p, p, p, p, lbc_stack, gn.reshape(1, aw), sel]
    state_spec = pl.BlockSpec((1, heads, A_DK, A_DK), lambda b, t: (b, 0, 0, 0))
    if s0 is not None:
        in_specs.append(pl.BlockSpec((None, 1, heads, A_DK, A_DK), lambda b, t: (layer, b, 0, 0, 0)))
        args.append(s0)
    return pl.pallas_call(
        functools.partial(_hgrn_kernel, c=c, n_chunks=t_blk // c, heads=heads, has_init=s0 is not None),
        grid=(bsz, nt),
        in_specs=in_specs,
        out_specs=[pl.BlockSpec((t_blk, aw), lambda b, t: (b * nt + t, 0)), state_spec],
        out_shape=[jax.ShapeDtypeStruct((bsz * seq_len, aw), BF16),
                   jax.ShapeDtypeStruct((bsz, heads, A_DK, A_DK), F32)],
        scratch_shapes=[pltpu.VMEM((c + DIAG, aw), F32), pltpu.VMEM((c + DIAG, aw), F32)],
        compiler_params=_cparams(("arbitrary", "arbitrary")),
        name="hgrn2",
    )(*args)


def _ssd_kernel(*refs, t, bw, has_init):
    n_in = SSD_N_INPUTS
    s0_ref, c0_ref = (refs[n_in], refs[n_in + 1]) if has_init else (None, None)
    rest = refs[n_in + 2:] if has_init else refs[n_in:]
    yb_ref, s_out_ref, conv_out_ref, xpad, st = rest
    ti = pl.program_id(1)
    _ssd_reset(s0_ref, c0_ref, xpad, st, ti == 0)
    for _ in _ssd_phases(*refs[:n_in], yb_ref, xpad, st, t=t, bw=bw):
        pass
    _ssd_flush(conv_out_ref, s_out_ref, xpad, st, ti == pl.num_programs(1) - 1)


SSD_N_INPUTS = 12
SSD_PHASES = 9


def _ssd_phases(z_ref, xs_ref, bc_ref, dt_ref, cw_ref, cb_ref, dtb_ref, alog_ref, dsk_ref, gn_ref, exp_ref, sel_ref,
                yb_ref, xpad, st, *, t, bw):
    gw = bw // B_GROUPS
    n_bc = B_GROUPS * B_DSTATE
    pad = SUBLANES

    cur = jnp.concatenate([xs_ref[...], bc_ref[...]], axis=1)
    prev = xpad[...]
    row8 = lax.broadcasted_iota(jnp.int32, (pad, cur.shape[1]), 0)
    conv = cb_ref[...]
    for j in range(CONV_W):
        s = CONV_W - 1 - j
        if s == 0:
            tap = cur
        else:
            rolled = pltpu.roll(cur, s, 0)
            head = jnp.where(row8 < s, pltpu.roll(prev, s, 0), rolled[0:pad])
            tap = jnp.concatenate([head, rolled[pad:]], axis=0)
        conv = conv + tap * cw_ref[j:j + 1, :]
    xbc = _silu(conv)
    x = xbc[:, 0:bw]
    yield

    dt = _softplus(dt_ref[...] + dtb_ref[...])
    a = dt * (-jnp.exp(alog_ref[...]))
    cs = _dot3(sel_ref[...], a)
    a_cum = cs[0:t]
    a_tot = cs[t:2 * t]
    ex = _dot3_rhs(jnp.concatenate([dt, a_cum, a_tot - a_cum], axis=0), exp_ref[...])
    dt_e = ex[0:t]
    acum_e = ex[t:2 * t]
    dec_e = ex[2 * t:3 * t]
    atot_e = acum_e[t - 1:t, :]

    xdt = x * dt_e
    xw = (xdt * jnp.exp(dec_e)).astype(BF16)
    xdt_b = xdt.astype(BF16)
    a_cum_t = a_cum.T
    rr = lax.broadcasted_iota(jnp.int32, (t, t), 0)
    cc = lax.broadcasted_iota(jnp.int32, (t, t), 1)
    causal = cc <= rr
    lane = lax.broadcasted_iota(jnp.int32, (t, LANES), 1)
    heads_per_group = gw // B_HEADDIM
    pairs_per_group = gw // LANES
    yield

    y_groups = []
    for g in range(B_GROUPS):
        bg = xbc[:, bw + g * B_DSTATE:bw + (g + 1) * B_DSTATE].astype(BF16)
        cg = xbc[:, bw + n_bc + g * B_DSTATE:bw + n_bc + (g + 1) * B_DSTATE].astype(BF16)
        gs = slice(g * gw, (g + 1) * gw)
        st_g = st[:, gs]
        scores = lax.dot_general(cg, bg, (((1,), (1,)), ((), ())), preferred_element_type=F32)
        y_off = jnp.dot(cg, st_g.astype(BF16), preferred_element_type=F32) * jnp.exp(acum_e[:, gs])
        st[:, gs] = jnp.exp(atot_e[:, gs]) * st_g + lax.dot_general(
            bg, xw[:, gs], (((0,), (0,)), ((), ())), preferred_element_type=F32)
        yield
        y_pairs = []
        for pr in range(pairs_per_group):
            cols = slice(g * gw + pr * LANES, g * gw + (pr + 1) * LANES)
            xp = xdt_b[:, cols]
            acc = None
            for half in range(LANES // B_HEADDIM):
                h = g * heads_per_group + pr * (LANES // B_HEADDIM) + half
                diff = a_cum[:, h:h + 1] - a_cum_t[h:h + 1, :]
                lmat = jnp.where(causal, jnp.exp(jnp.where(causal, diff, 0.0)), 0.0)
                m = (scores * lmat).astype(BF16)
                in_half = (lane // B_HEADDIM) == half
                term = jnp.dot(m, jnp.where(in_half, xp, jnp.zeros_like(xp)), preferred_element_type=F32)
                acc = term if acc is None else acc + term
            y_pairs.append(acc)
            if pr % 2 == 1:
                yield
        y_groups.append(jnp.concatenate(y_pairs, axis=1) + y_off)
    y = jnp.concatenate(y_groups, axis=1) + dsk_ref[...] * x
    y = y * _silu(z_ref[...])
    outs = []
    for g in range(B_GROUPS):
        yg = y[:, g * gw:(g + 1) * gw]
        outs.append(yg * lax.rsqrt(jnp.mean(yg * yg, axis=-1, keepdims=True) + NORM_EPS))
    yb_ref[...] = (jnp.concatenate(outs, axis=1) * gn_ref[...]).astype(yb_ref.dtype)

    xpad[...] = cur[t - pad:t, :]
    yield


def _ssd_reset(s0_ref, c0_ref, xpad, st, first):
    @pl.when(first)
    def _():
        if s0_ref is not None:
            xpad[...] = c0_ref[0]
            st[...] = s0_ref[0].reshape(st.shape[1], st.shape[0]).T
        else:
            xpad[...] = jnp.zeros(xpad.shape, F32)
            st[...] = jnp.zeros(st.shape, F32)


def _ssd_flush(conv_out_ref, s_out_ref, xpad, st, last):
    @pl.when(last)
    def _():
        conv_out_ref[0] = xpad[...]
        s_out_ref[0] = st[...].T.reshape(s_out_ref.shape[1:])


def _ssd_operands(p, cols, prm, t, n_rows, n_seq, row_block, seq_block):
    bw = prm["bw"]
    heads = bw // B_HEADDIM
    n_bc = B_GROUPS * B_DSTATE
    cdim = bw + 2 * n_bc
    tri = np.arange(t)[None, :] <= np.arange(t)[:, None]
    sel = jnp.asarray(np.concatenate([tri, np.ones((t, t), bool)], 0).astype(np.float32), dtype=BF16)
    expand = np.zeros((LANES, bw), np.float32)
    expand[np.arange(bw) // B_HEADDIM, np.arange(bw)] = 1.0
    expand = jnp.asarray(expand, dtype=BF16)

    def blk(width, off):
        return pl.BlockSpec((t, width), lambda *g: (row_block(*g), off // width))

    def full(shape):
        return pl.BlockSpec(shape, lambda *g: (0,) * len(shape))

    in_specs = [blk(bw, cols["z"]), blk(bw, cols["xs"]), blk(2 * n_bc, cols["bc"]), blk(LANES, cols["dt"]),
                full((CONV_W, cdim)), full((1, cdim)), full((1, LANES)), full((1, LANES)),
                full((1, bw)), full((1, bw)), full(expand.shape), full(sel.shape)]
    args = [p, p, p, p, prm["conv_w"], prm["conv_b"], prm["dt_bias"], prm["a_log"], prm["d_skip"], prm["gn"],
            expand, sel]
    assert len(args) == SSD_N_INPUTS
    out_specs = [pl.BlockSpec((t, bw), lambda *g: (row_block(*g), 0)),
                 pl.BlockSpec((1, heads, B_HEADDIM, B_DSTATE), lambda *g: (seq_block(*g), 0, 0, 0)),
                 pl.BlockSpec((1, SUBLANES, cdim), lambda *g: (seq_block(*g), 0, 0))]
    out_shape = [jax.ShapeDtypeStruct((n_rows, bw), BF16),
                 jax.ShapeDtypeStruct((n_seq, heads, B_HEADDIM, B_DSTATE), F32),
                 jax.ShapeDtypeStruct((n_seq, SUBLANES, cdim), F32)]
    scratch = [pltpu.VMEM((SUBLANES, cdim), F32), pltpu.VMEM((B_DSTATE, bw), F32)]
    return in_specs, args, out_specs, out_shape, scratch


def _ssd(p, cols, prm, s0, c0, bsz, seq_len):
    bw = prm["bw"]
    heads = bw // B_HEADDIM
    cdim = bw + 2 * B_GROUPS * B_DSTATE
    t = min(seq_len, SSD_ROWS)
    nt = seq_len // t
    in_specs, args, out_specs, out_shape, scratch = _ssd_operands(
        p, cols, prm, t, bsz * seq_len, bsz, lambda b, ti: b * nt + ti, lambda b, ti: b)
    if s0 is not None:
        layer = prm["layer"]
        in_specs += [pl.BlockSpec((None, 1, heads, B_HEADDIM, B_DSTATE), lambda b, ti: (layer, b, 0, 0, 0)),
                     pl.BlockSpec((None, 1, SUBLANES, cdim), lambda b, ti: (layer, b, 0, 0))]
        args += [s0, c0]
    return pl.pallas_call(
        functools.partial(_ssd_kernel, t=t, bw=bw, has_init=s0 is not None),
        grid=(bsz, nt),
        in_specs=in_specs,
        out_specs=out_specs,
        out_shape=out_shape,
        scratch_shapes=scratch,
        compiler_params=_cparams(("arbitrary", "arbitrary")),
        name="ssd",
    )(*args)


def _down_sub_dots(a_ref, w_ref, x_ref, gate_ref, o_ref, acc_scr, nseg):
    kc = a_ref.shape[1] // DOWN_K_SPLIT

    def sub_dot(ki):
        rows = slice(ki * kc, (ki + 1) * kc)
        part = jnp.dot(a_ref[:, rows], w_ref[rows, :], preferred_element_type=F32)
        if ki == 0:
            acc_scr[...] = part
        elif ki < DOWN_K_SPLIT - 1:
            acc_scr[...] += part
        else:
            x = x_ref[...]
            acc = acc_scr[...] + part
            o_ref[...] = (_per_seq(x, nseg) + gate_ref[...] * _per_seq(acc, nseg)).reshape(x.shape)

    return [functools.partial(sub_dot, ki) for ki in range(DOWN_K_SPLIT)]


def _emit_interleaved(fillers, phases):
    per_filler = -(-len(phases) // max(len(fillers), 1))
    pos = 0
    for fill in fillers:
        fill()
        for ph in phases[pos:pos + per_filler]:
            ph()
        pos += per_filler
    for ph in phases[pos:]:
        ph()


def _mlp_down_ssd_kernel(a_ref, w_ref, x_ref, gate_ref, *refs, nseg, t, bw, steps_per_seq):
    ssd_in = refs[:SSD_N_INPUTS]
    o_ref, yb_ref, s_out_ref, conv_out_ref, acc_scr, xpad, st = refs[SSD_N_INPUTS:]
    step = pl.program_id(0) * pl.num_programs(1) + pl.program_id(1)
    pos = step % steps_per_seq
    _ssd_reset(None, None, xpad, st, pos == 0)
    gen = _ssd_phases(*ssd_in, yb_ref, xpad, st, t=t, bw=bw)
    phases = [functools.partial(next, gen, None) for _ in range(SSD_PHASES)]
    _emit_interleaved(_down_sub_dots(a_ref, w_ref, x_ref, gate_ref, o_ref, acc_scr, nseg), phases)
    _ssd_flush(conv_out_ref, s_out_ref, xpad, st, pos == steps_per_seq - 1)


def _mlp_down_ssd(a, w_stack, layer, x, mod6, k_gate, seq_len, p_other, cols, prm):
    n, kdim = a.shape
    d = w_stack.shape[2]
    tn = DOWN_TILE
    tm, nseg, tps = _row_tiling(n, seq_len, ROW_TILE)
    ni, nj = n // tm, d // tn
    n_other = p_other.shape[0]
    t = n_other // (ni * nj)
    assert t * ni * nj == n_other and seq_len % t == 0 and t % SUBLANES == 0 and kdim % DOWN_K_SPLIT == 0
    steps_per_seq = seq_len // t
    s_in_specs, s_args, s_out_specs, s_out_shape, s_scratch = _ssd_operands(
        p_other, cols, prm, t, n_other, n_other // seq_len,
        lambda i, j: i * nj + j, lambda i, j: (i * nj + j) // steps_per_seq)
    return pl.pallas_call(
        functools.partial(_mlp_down_ssd_kernel, nseg=nseg, t=t, bw=prm["bw"], steps_per_seq=steps_per_seq),
        grid=(ni, nj),
        in_specs=[
            pl.BlockSpec((tm, kdim), lambda i, j: (i, 0)),
            pl.BlockSpec((None, kdim, tn), lambda i, j: (layer, 0, j)),
            pl.BlockSpec((tm, tn), lambda i, j: (i, j)),
            pl.BlockSpec((None, nseg, 1, tn), lambda i, j: (k_gate, i // tps, 0, j)),
        ] + s_in_specs,
        out_specs=[pl.BlockSpec((tm, tn), lambda i, j: (i, j))] + s_out_specs,
        out_shape=[jax.ShapeDtypeStruct((n, d), F32)] + s_out_shape,
        scratch_shapes=[pltpu.VMEM((tm, tn), F32)] + s_scratch,
        compiler_params=_cparams(("arbitrary", "arbitrary")),
        name="mlp_down_ssd",
    )(a, w_stack, x, mod6, *s_args)


def _cmlp_kernel(u_ref, v_ref, lng_ref, lnb_ref, ws_ref, bst_ref, *out_refs, t, n_chunks, keep_v):
    yc_ref = out_refs[0]
    cw = u_ref.shape[1] // C_GROUPS
    rr = lax.broadcasted_iota(jnp.int32, (t, t), 0)
    cc = lax.broadcasted_iota(jnp.int32, (t, t), 1)
    wts = [jnp.where(cc <= rr, ws_ref[g, 0:t, 0:t], 0.0).astype(BF16) for g in range(C_GROUPS)]
    for ci in range(n_chunks):
        rows = slice(ci * t, (ci + 1) * t)
        u = _gelu(u_ref[rows, :])
        gv = _gelu(v_ref[rows, :])
        mu = jnp.mean(gv, axis=-1, keepdims=True)
        dv = gv - mu
        var = jnp.mean(dv * dv, axis=-1, keepdims=True)
        v = dv * lax.rsqrt(var + NORM_EPS) * lng_ref[...] + lnb_ref[...]
        if keep_v:
            out_refs[1][rows, :] = v
        vb = v.astype(BF16)
        for g in range(C_GROUPS):
            lanes = slice(g * cw, (g + 1) * cw)
            mixed = jnp.dot(wts[g], vb[:, lanes], preferred_element_type=F32) + bst_ref[0:t, g:g + 1]
            yc_ref[rows, lanes] = (u[:, lanes] * mixed).astype(yc_ref.dtype)


def _cmlp(p, col_u, col_v, prm, bsz, seq_len, keep_v):
    cw = prm["cw"]
    t = min(seq_len, CMLP_CHUNK)
    t_blk = min(seq_len, 4 * CMLP_CHUNK)
    n = bsz * seq_len

    def full(shape):
        return pl.BlockSpec(shape, lambda i: (0,) * len(shape))

    out_specs = [pl.BlockSpec((t_blk, cw), lambda i: (i, 0))]
    out_shape = [jax.ShapeDtypeStruct((n, cw), BF16)]
    if keep_v:
        out_specs.append(pl.BlockSpec((t_blk, cw), lambda i: (i, 0)))
        out_shape.append(jax.ShapeDtypeStruct((n, cw), F32))
    return pl.pallas_call(
        functools.partial(_cmlp_kernel, t=t, n_chunks=t_blk // t, keep_v=keep_v),
        grid=(n // t_blk,),
        in_specs=[pl.BlockSpec((t_blk, cw), lambda i: (i, col_u // cw)),
                  pl.BlockSpec((t_blk, cw), lambda i: (i, col_v // cw)),
                  full((1, cw)), full((1, cw)),
                  full((C_GROUPS, CMLP_CHUNK, CMLP_CHUNK)), full((CMLP_CHUNK, C_GROUPS))],
        out_specs=out_specs,
        out_shape=out_shape,
        compiler_params=_cparams(("arbitrary",)),
        name="cmlp",
    )(p, p, prm["ln_g"], prm["ln_b"], prm["ws"], prm["bs_t"])


def _run_trunk(x3, mod, st_hgrn, st_ssm, st_conv, keep_v, w):
    bsz, seq_len, d = x3.shape
    depth = mod.shape[0]
    aw = bw = cw = d // 2
    heads_a = aw // A_DK
    x = x3.reshape(bsz * seq_len, d)
    col = w["cols"]
    hgrn_out, ssm_out, conv_out, v_out = [], [], [], []
    for l in range(depth):
        mod6 = mod[l].reshape(bsz, 6, 1, d).transpose(1, 0, 2, 3)
        p, gates = _in_proj(x, w["norm1_g"][l], mod6, w["w_in"], l, seq_len, col["gate"])

        y_a, s_h = _hgrn(p, w["lbc"], l, w["hgrn_onorm_g"][l], st_hgrn, bsz, seq_len, heads_a)
        ssd_prm = dict(bw=bw, layer=l, conv_w=w["ssm_conv_w"][l], conv_b=w["ssm_conv_b"][l][None],
                       dt_bias=w["dt_bias_pad"][l][None], a_log=w["a_log_pad"][l][None],
                       d_skip=w["d_skip"][l][None], gn=w["ssm_onorm_g"][l][None])
        y_b, s_s, conv_tail = _ssd(p, col, ssd_prm, st_ssm, st_conv, bsz, seq_len)
        cm_prm = dict(cw=cw, ln_g=w["cmlp_ln_g"][l][None], ln_b=w["cmlp_ln_b"][l][None],
                      ws=w["cmlp_ws"][l], bs_t=w["cmlp_bs"][l].T)
        c_res = _cmlp(p, col["u"], col["v"], cm_prm, bsz, seq_len, keep_v)
        merged = _merge(y_a, y_b, c_res[0], w["w_branch"], l, gates, d)
        x = _proj_residual(merged, w["w_out"], l, x, mod6, 2, seq_len, "out_proj")
        hid = _mlp_up(x, w["norm2_g"][l], mod6, w["w_up"], l, seq_len)
        x = _proj_residual(hid, w["w_down"], l, x, mod6, 5, seq_len, "mlp_down")

        hgrn_out.append(s_h)
        ssm_out.append(s_s)
        conv_out.append(conv_tail[:, SUBLANES - (CONV_W - 1):, :])
        if keep_v:
            v_out.append(c_res[1].reshape(bsz, seq_len, cw))
    y = _final_norm(x, w["final_g"]).reshape(bsz, seq_len, d)
    return (y, jnp.stack(hgrn_out), jnp.stack(ssm_out), jnp.stack(conv_out),
            jnp.stack(v_out) if keep_v else None)


def _mlp_down_hgrn_kernel(a_ref, w_ref, x_ref, gate_ref, q_ref, f_ref, i_ref, gg_ref, lbc_ref, gn_ref, sel_ref,
                          o_ref, ya_ref, s_ref, acc_scr, bpad, kpad, *, nseg, c, n_chunks, heads, steps_per_seq):
    step = pl.program_id(0) * pl.num_programs(1) + pl.program_id(1)

    @pl.when(step % steps_per_seq == 0)
    def _():
        s_ref[...] = jnp.zeros(s_ref.shape, F32)

    fillers = _down_sub_dots(a_ref, w_ref, x_ref, gate_ref, o_ref, acc_scr, nseg)
    _hgrn_chunks(q_ref, f_ref, i_ref, gg_ref, lbc_ref, gn_ref, sel_ref, ya_ref, s_ref, bpad, kpad,
                 c=c, n_chunks=n_chunks, heads=heads, unroll=True, fillers=fillers)


def _mlp_down_hgrn(a, w_stack, layer, x, mod6, k_gate, seq_len, p_other, lbc_stack, lbc_layer, gn, heads):
    n, kdim = a.shape
    d = w_stack.shape[2]
    tn = DOWN_TILE
    tm, nseg, tps = _row_tiling(n, seq_len, ROW_TILE)
    ni, nj = n // tm, d // tn
    n_other = p_other.shape[0]
    t_step = n_other // (ni * nj)
    assert t_step * ni * nj == n_other and seq_len % t_step == 0 and t_step % SUBLANES == 0
    assert kdim % DOWN_K_SPLIT == 0
    steps_per_seq = seq_len // t_step
    aw = heads * A_DK
    c, h_in_specs, h_args, ya_spec, h_scratch = _hgrn_specs(
        p_other, lbc_stack, lbc_layer, gn, heads, t_step, lambda i, j: i * nj + j)
    state_spec = pl.BlockSpec((1, heads, A_DK, A_DK), lambda i, j: ((i * nj + j) // steps_per_seq, 0, 0, 0))
    return pl.pallas_call(
        functools.partial(_mlp_down_hgrn_kernel, nseg=nseg, c=c, n_chunks=t_step // c, heads=heads,
                          steps_per_seq=steps_per_seq),
        grid=(ni, nj),
        in_specs=[
            pl.BlockSpec((tm, kdim), lambda i, j: (i, 0)),
            pl.BlockSpec((None, kdim, tn), lambda i, j: (layer, 0, j)),
            pl.BlockSpec((tm, tn), lambda i, j: (i, j)),
            pl.BlockSpec((None, nseg, 1, tn), lambda i, j: (k_gate, i // tps, 0, j)),
        ] + h_in_specs,
        out_specs=[pl.BlockSpec((tm, tn), lambda i, j: (i, j)), ya_spec, state_spec],
        out_shape=[jax.ShapeDtypeStruct((n, d), F32),
                   jax.ShapeDtypeStruct((n_other, aw), BF16),
                   jax.ShapeDtypeStruct((n_other // seq_len, heads, A_DK, A_DK), F32)],
        scratch_shapes=[pltpu.VMEM((tm, tn), F32)] + h_scratch,
        compiler_params=_cparams(("arbitrary", "arbitrary")),
        name="mlp_down_hgrn2",
    )(a, w_stack, x, mod6, *h_args)


def _run_trunk_halves(x3, mod, w):
    bsz, seq_len, d = x3.shape
    depth = mod.shape[0]
    hb = bsz // 2
    aw = bw = cw = d // 2
    heads_a = aw // A_DK
    col = w["cols"]
    n_half = hb * seq_len
    x_full = x3.reshape(bsz * seq_len, d)
    xs = [x_full, x_full]
    row0 = [0, n_half]
    mods = [mod[:, :hb], mod[:, hb:]]
    hgrn_out, ssm_out, conv_out = [], [], []
    pending = None
    for l in range(depth):
        mod6 = [m[l].reshape(hb, 6, 1, d).transpose(1, 0, 2, 3) for m in mods]
        ssd_prm = dict(bw=bw, layer=l, conv_w=w["ssm_conv_w"][l], conv_b=w["ssm_conv_b"][l][None],
                       dt_bias=w["dt_bias_pad"][l][None], a_log=w["a_log_pad"][l][None],
                       d_skip=w["d_skip"][l][None], gn=w["ssm_onorm_g"][l][None])
        cm_prm = dict(cw=cw, ln_g=w["cmlp_ln_g"][l][None], ln_b=w["cmlp_ln_b"][l][None],
                      ws=w["cmlp_ws"][l], bs_t=w["cmlp_bs"][l].T)
        gn_a = w["hgrn_onorm_g"][l]

        def mix_merge(h, p, gates, y_a, ssd_res=None):
            y_b, s_s, conv_tail = ssd_res or _ssd(p, col, ssd_prm, None, None, hb, seq_len)
            y_c = _cmlp(p, col["u"], col["v"], cm_prm, hb, seq_len, False)[0]
            merged = _merge(y_a, y_b, y_c, w["w_branch"], l, gates, d)
            x_new = _proj_residual(merged, w["w_out"], l, xs[h], mod6[h], 2, seq_len, "out_proj", row0[h])
            row0[h] = 0
            return x_new, s_s, conv_tail

        p0, g0 = _in_proj(xs[0], w["norm1_g"][l], mod6[0], w["w_in"], l, seq_len, col["gate"], (row0[0], n_half))
        if pending is None:
            ya0, sh0 = _hgrn(p0, w["lbc"], l, gn_a, None, hb, seq_len, heads_a)
        else:
            hid1, x1_mid, mod6_prev = pending
            xs[1], ya0, sh0 = _mlp_down_hgrn(hid1, w["w_down"], l - 1, x1_mid, mod6_prev, 5, seq_len,
                                              p0, w["lbc"], l, gn_a, heads_a)
        p1, g1 = _in_proj(xs[1], w["norm1_g"][l], mod6[1], w["w_in"], l, seq_len, col["gate"], (row0[1], n_half))
        x0, ss0, ct0 = mix_merge(0, p0, g0, ya0)
        hid0, ya1, sh1 = _mlp_up_hgrn(x0, w["norm2_g"][l], mod6[0], w["w_up"], l, seq_len,
                                      p1, w["lbc"], gn_a, heads_a)
        xs[0], *ssd1 = _mlp_down_ssd(hid0, w["w_down"], l, x0, mod6[0], 5, seq_len, p1, col, ssd_prm)
        x1_mid, ss1, ct1 = mix_merge(1, p1, g1, ya1, tuple(ssd1))
        hid1 = _mlp_up(x1_mid, w["norm2_g"][l], mod6[1], w["w_up"], l, seq_len)
        pending = (hid1, x1_mid, mod6[1])

        hgrn_out.append(jnp.concatenate([sh0, sh1], axis=0))
        ssm_out.append(jnp.concatenate([ss0, ss1], axis=0))
        conv_out.append(jnp.concatenate([ct0, ct1], axis=0)[:, SUBLANES - (CONV_W - 1):, :])
    hid1, x1_mid, mod6_prev = pending
    xs[1] = _proj_residual(hid1, w["w_down"], depth - 1, x1_mid, mod6_prev, 5, seq_len, "mlp_down")
    y = _final_norm_pair(xs[0], xs[1], w["final_g"]).reshape(bsz, seq_len, d)
    return (y, jnp.stack(hgrn_out), jnp.stack(ssm_out), jnp.stack(conv_out), None)


def kernel(x_prompt, x_sample, state_hgrn, state_ssm, state_conv, c_prompt, c_sample, norm1_g, norm2_g,
           w_mod, b_mod, w_in, hgrn_lb, hgrn_onorm_g, ssm_conv_w, ssm_conv_b, ssm_dt_bias, ssm_a_log, ssm_d,
           ssm_onorm_g, cmlp_ln_g, cmlp_ln_b, cmlp_ws, cmlp_bs, w_branch, w_out, w_up, w_down, final_g):
    d = x_prompt.shape[-1]
    depth = w_in.shape[0]
    aw = bw = cw = d // 2
    n_bc = B_GROUPS * B_DSTATE
    heads_b = bw // B_HEADDIM
    assert heads_b <= LANES and bw % LANES == 0 and DT_PAD % LANES == 0

    o_dt = 4 * aw + bw + bw + 2 * n_bc
    o_u = o_dt + heads_b
    w_in_t = jnp.swapaxes(w_in, 1, 2).astype(BF16)
    w_in_r = jnp.concatenate(
        [w_in_t[:, :o_u], jnp.zeros((depth, DT_PAD - heads_b, d), BF16), w_in_t[:, o_u:]], axis=1)
    cols = dict(z=4 * aw, xs=4 * aw + bw, bc=4 * aw + 2 * bw, dt=o_dt)
    cols["u"] = o_dt + DT_PAD
    cols["v"] = cols["u"] + cw
    cols["gate"] = cols["v"] + cw
    assert cols["gate"] % IN_TILE == 0 and w_in_r.shape[1] % IN_TILE == 0

    pad_h = lambda a: jnp.pad(a.astype(F32), ((0, 0), (0, LANES - heads_b)))
    w = dict(
        cols=cols, w_in=w_in_r,
        norm1_g=norm1_g, norm2_g=norm2_g, final_g=final_g,
        lbc=_lb_consts(hgrn_lb), hgrn_onorm_g=hgrn_onorm_g,
        ssm_conv_w=ssm_conv_w, ssm_conv_b=ssm_conv_b,
        dt_bias_pad=pad_h(ssm_dt_bias), a_log_pad=pad_h(ssm_a_log),
        d_skip=jnp.repeat(ssm_d.astype(F32), B_HEADDIM, axis=1), ssm_onorm_g=ssm_onorm_g,
        cmlp_ln_g=cmlp_ln_g, cmlp_ln_b=cmlp_ln_b, cmlp_ws=cmlp_ws, cmlp_bs=cmlp_bs,
        w_branch=w_branch.astype(BF16).reshape(depth, N_BRANCH, aw, d), w_out=w_out.astype(BF16),
        w_up=w_up.astype(BF16), w_down=w_down.astype(BF16),
    )

    nb = x_prompt.shape[0]
    mod = _modulation(jnp.concatenate([c_prompt, c_sample], axis=0), w_mod, b_mod)
    if nb % 2 == 0:
        y_p, hgrn_p, ssm_p, conv_p, _ = _run_trunk_halves(x_prompt, mod[:, :nb], w)
    else:
        y_p, hgrn_p, ssm_p, conv_p, _ = _run_trunk(x_prompt, mod[:, :nb], None, None, None, False, w)
    conv_pad = jnp.pad(state_conv, ((0, 0), (0, 0), (SUBLANES - (CONV_W - 1), 0), (0, 0)))
    y_s, hgrn_s, ssm_s, conv_s, v_s = _run_trunk(x_sample, mod[:, nb:], state_hgrn, state_ssm, conv_pad, True, w)
    return (y_p, y_s, hgrn_p, ssm_p, conv_p, hgrn_s, ssm_s, conv_s, v_s)
```

```python
import functools
import math

import numpy as np
import jax
import jax.numpy as jnp
from jax import lax
from jax.experimental import pallas as pl
from jax.experimental.pallas import tpu as pltpu

F32 = jnp.float32
BF16 = jnp.bfloat16

A_DK = 128
B_HEADDIM = 64
B_GROUPS = 2
B_DSTATE = 128
CONV_W = 4
C_GROUPS = 4
CMLP_CHUNK = 128
N_BRANCH = 3
SCAN_CHUNK = 64
NORM_EPS = 1e-6
LB_FLOOR = 1e-30
LOG2E = 1.4426950408889634

LANES = 128
SUBLANES = 8
VMEM_LIMIT = 56 * 1024 * 1024
DIAG = SUBLANES
DT_PAD = 512
ROW_TILE = 1024
COL_TILE = 1024
IN_TILE = 1536
MERGE_TILE = 512
DOWN_TILE = 256
EPILOGUE_COLS = 256
SSD_ROWS = 128
UP_K_SPLIT = 1
DOWN_K_SPLIT = 16
NORM_ROWS = 128


def _cparams(sem):
    return pltpu.CompilerParams(dimension_semantics=sem, vmem_limit_bytes=VMEM_LIMIT)


def _split3(x):
    hi = x.astype(BF16)
    r1 = x - hi.astype(F32)
    mid = r1.astype(BF16)
    lo = (r1 - mid.astype(F32)).astype(BF16)
    return hi, mid, lo


def _dot3(sel, x):
    hi, mid, lo = _split3(x)
    d = lambda p: jnp.dot(sel, p, preferred_element_type=F32)
    return d(hi) + d(mid) + d(lo)


def _dot3_rhs(x, sel):
    hi, mid, lo = _split3(x)
    d = lambda p: jnp.dot(p, sel, preferred_element_type=F32)
    return d(hi) + d(mid) + d(lo)


def _silu(x):
    return x / (1.0 + jnp.exp(-x))


def _softplus(x):
    return jnp.maximum(x, 0.0) + jnp.log1p(jnp.exp(-jnp.abs(x)))


def _gelu(x):
    return 0.5 * x * (1.0 + lax.erf(x * (1.0 / math.sqrt(2.0))))


def _lb_kernel(lb_ref, out_ref, *, depth):
    x = lb_ref[...]
    m = jnp.max(x, axis=0, keepdims=True)
    e = jnp.exp(x - m)
    p = e / jnp.sum(e, axis=0, keepdims=True)
    acc = jnp.zeros_like(p[0:1])
    zeros5 = jnp.zeros((SUBLANES - 3, x.shape[1]), F32)
    for l in range(depth):
        acc = acc + p[l:l + 1]
        lb = acc - p[0:1]
        out_ref[l] = jnp.concatenate(
            [jnp.log(jnp.maximum(lb, LB_FLOOR)), jnp.log1p(-lb), 1.0 - lb, zeros5], axis=0)


def _lb_consts(hgrn_lb):
    depth, aw = hgrn_lb.shape
    return pl.pallas_call(
        functools.partial(_lb_kernel, depth=depth),
        out_shape=jax.ShapeDtypeStruct((depth, SUBLANES, aw), F32),
        name="hgrn_lb",
    )(hgrn_lb.astype(F32))


def _mod_kernel(c_ref, w_ref, b_ref, o_ref):
    cs = _silu(c_ref[...]).astype(BF16)
    o_ref[...] = jnp.dot(cs, w_ref[...].astype(BF16), preferred_element_type=F32) + b_ref[...]


def _modulation(c_all, w_mod, b_mod):
    depth, d, n6 = w_mod.shape
    s = c_all.shape[0]
    tn = 1024
    return pl.pallas_call(
        _mod_kernel,
        grid=(depth, n6 // tn),
        in_specs=[
            pl.BlockSpec((s, d), lambda l, j: (0, 0)),
            pl.BlockSpec((None, d, tn), lambda l, j: (l, 0, j)),
            pl.BlockSpec((None, 1, tn), lambda l, j: (l, 0, j)),
        ],
        out_specs=pl.BlockSpec((None, s, tn), lambda l, j: (l, 0, j)),
        out_shape=jax.ShapeDtypeStruct((depth, s, n6), F32),
        compiler_params=_cparams(("arbitrary", "arbitrary")),
        name="adaln_mod",
    )(c_all, w_mod, b_mod.reshape(depth, 1, n6))


def _row_tiling(n_tok, seq_len, tm_max):
    tm = min(tm_max, n_tok)
    if seq_len >= tm:
        assert seq_len % tm == 0
        return tm, 1, seq_len // tm
    assert tm % seq_len == 0
    return tm, tm // seq_len, 1


def _per_seq(x, nseg):
    tm, d = x.shape
    return x.reshape(nseg, tm // nseg, d)


def _final_norm_kernel(x_ref, g_ref, o_ref):
    x = x_ref[...]
    o_ref[...] = x * lax.rsqrt(jnp.mean(x * x, axis=-1, keepdims=True) + NORM_EPS) * g_ref[...]


def _final_norm(x, g):
    n, d = x.shape
    tm = min(512, n)
    return pl.pallas_call(
        _final_norm_kernel,
        grid=(n // tm,),
        in_specs=[pl.BlockSpec((tm, d), lambda i: (i, 0)), pl.BlockSpec((1, d), lambda i: (0, 0))],
        out_specs=pl.BlockSpec((tm, d), lambda i: (i, 0)),
        out_shape=jax.ShapeDtypeStruct((n, d), F32),
        compiler_params=_cparams(("arbitrary",)),
        name="final_norm",
    )(x, g.reshape(1, d))


def _final_norm_pair_kernel(xa_ref, xb_ref, g_ref, o_ref, *, nt):
    def norm(x):
        return x * lax.rsqrt(jnp.mean(x * x, axis=-1, keepdims=True) + NORM_EPS) * g_ref[...]

    @pl.when(pl.program_id(0) < nt)
    def _():
        o_ref[...] = norm(xa_ref[...])

    @pl.when(pl.program_id(0) >= nt)
    def _():
        o_ref[...] = norm(xb_ref[...])


def _final_norm_pair(xa, xb, g):
    n, d = xa.shape
    tm = min(512, n)
    nt = n // tm
    return pl.pallas_call(
        functools.partial(_final_norm_pair_kernel, nt=nt),
        grid=(2 * nt,),
        in_specs=[pl.BlockSpec((tm, d), lambda i: (jnp.minimum(i, nt - 1), 0)),
                  pl.BlockSpec((tm, d), lambda i: (jnp.maximum(i - nt, 0), 0)),
                  pl.BlockSpec((1, d), lambda i: (0, 0))],
        out_specs=pl.BlockSpec((tm, d), lambda i: (i, 0)),
        out_shape=jax.ShapeDtypeStruct((2 * n, d), F32),
        compiler_params=_cparams(("arbitrary",)),
        name="final_norm",
    )(xa, xb, g.reshape(1, d))


def _norm_mod_rows(x_ref, g_ref, sc_ref, sh_ref, h_scr, nseg):
    tm = x_ref.shape[0]
    seg = tm // nseg
    rb = min(seg, NORM_ROWS)
    g = g_ref[...]

    def body(r, carry):
        rows = pl.ds(pl.multiple_of(r * rb, rb), rb)
        s = r // (seg // rb)
        x = x_ref[rows, :]
        gain = g * (1.0 + sc_ref[s])
        inv = lax.rsqrt(jnp.mean(x * x, axis=-1, keepdims=True) + NORM_EPS)
        h_scr[rows, :] = (x * inv * gain + sh_ref[s]).astype(BF16)
        return carry

    lax.fori_loop(0, tm // rb, body, 0, unroll=2 if (tm // rb) % 2 == 0 else 1)


def _mlp_up_kernel(x_ref, g_ref, sc_ref, sh_ref, w_ref, o_ref, h_scr, *, nseg):
    @pl.when(pl.program_id(1) == 0)
    def _():
        _norm_mod_rows(x_ref, g_ref, sc_ref, sh_ref, h_scr, nseg)

    y = jnp.maximum(jnp.dot(h_scr[...], w_ref[...], preferred_element_type=F32), 0.0)
    o_ref[...] = (y * y).astype(o_ref.dtype)


def _in_proj_kernel(x_ref, g_ref, sc_ref, sh_ref, w_ref, p_ref, gate_ref, h_scr, *, nseg, n_main):
    j = pl.program_id(1)

    @pl.when(j == 0)
    def _():
        _norm_mod_rows(x_ref, g_ref, sc_ref, sh_ref, h_scr, nseg)

    def project():
        return lax.dot_general(h_scr[...], w_ref[...], (((1,), (1,)), ((), ())), preferred_element_type=F32)

    @pl.when(j < n_main)
    def _():
        p_ref[...] = project()

    @pl.when(j >= n_main)
    def _():
        gate_ref[...] = project().astype(gate_ref.dtype)


def _norm_proj_call(kern, x, g, mod6, k_scale, k_shift, w_stack, layer, seq_len, tn, out_specs, out_shape, name):
    n, d = x.shape
    m = w_stack.shape[2]
    tm, nseg, tps = _row_tiling(n, seq_len, ROW_TILE)

    def mod_spec(k):
        return pl.BlockSpec((None, nseg, 1, d), lambda i, j: (k, i // tps, 0, 0))

    return pl.pallas_call(
        functools.partial(kern, nseg=nseg),
        grid=(n // tm, m // tn),
        in_specs=[
            pl.BlockSpec((tm, d), lambda i, j: (i, 0)),
            pl.BlockSpec((1, d), lambda i, j: (0, 0)),
            mod_spec(k_scale),
            mod_spec(k_shift),
            pl.BlockSpec((None, d, tn), lambda i, j: (layer, 0, j)),
        ],
        out_specs=out_specs(tm),
        out_shape=out_shape,
        scratch_shapes=[pltpu.VMEM((tm, d), BF16)],
        compiler_params=_cparams(("arbitrary", "arbitrary")),
        name=name,
    )(x, g.reshape(1, d), mod6, mod6, w_stack)


def _mlp_up(x, g, mod6, w_stack, layer, seq_len):
    n = x.shape[0]
    m = w_stack.shape[2]
    tn = COL_TILE
    return _norm_proj_call(
        _mlp_up_kernel, x, g, mod6, 4, 3, w_stack, layer, seq_len, tn,
        lambda tm: pl.BlockSpec((tm, tn), lambda i, j: (i, j)),
        jax.ShapeDtypeStruct((n, m), BF16), "mlp_up")


def _in_proj(x, g, mod6, w_stack, layer, seq_len, main_cols, rows=None):
    d = x.shape[1]
    row0, n = rows if rows is not None else (0, x.shape[0])
    m = w_stack.shape[1]
    tn = IN_TILE
    n_main = main_cols // tn
    tm, nseg, tps = _row_tiling(n, seq_len, ROW_TILE)

    def mod_spec(k):
        return pl.BlockSpec((None, nseg, 1, d), lambda i, j: (k, i // tps, 0, 0))

    return pl.pallas_call(
        functools.partial(_in_proj_kernel, nseg=nseg, n_main=n_main),
        grid=(n // tm, m // tn),
        in_specs=[
            pl.BlockSpec((tm, d), lambda i, j: (i + row0 // tm, 0)),
            pl.BlockSpec((1, d), lambda i, j: (0, 0)),
            mod_spec(1),
            mod_spec(0),
            pl.BlockSpec((None, tn, d), lambda i, j: (layer, j, 0)),
        ],
        out_specs=[pl.BlockSpec((tm, tn), lambda i, j: (i, jnp.minimum(j, n_main - 1))),
                   pl.BlockSpec((tm, tn), lambda i, j: (i, jnp.maximum(j - n_main, 0)))],
        out_shape=[jax.ShapeDtypeStruct((n, main_cols), F32), jax.ShapeDtypeStruct((n, m - main_cols), BF16)],
        scratch_shapes=[pltpu.VMEM((tm, d), BF16)],
        compiler_params=_cparams(("arbitrary", "arbitrary")),
        name="in_proj",
    )(x, g.reshape(1, d), mod6, mod6, w_stack)


def _proj_res_kernel(a_ref, w_ref, x_ref, gate_ref, o_ref, *, nseg, nk):
    part = jnp.dot(a_ref[...], w_ref[...], preferred_element_type=F32)

    def finish(acc):
        x = x_ref[...]
        o_ref[...] = (_per_seq(x, nseg) + gate_ref[...] * _per_seq(acc, nseg)).reshape(x.shape)

    if nk == 1:
        finish(part)
        return
    k = pl.program_id(2)

    @pl.when(k == 0)
    def _():
        o_ref[...] = part

    @pl.when((k > 0) & (k < nk - 1))
    def _():
        o_ref[...] += part

    @pl.when(k == nk - 1)
    def _():
        finish(o_ref[...] + part)


def _proj_residual(a, w_stack, layer, x, mod6, k_gate, seq_len, name, x_row0=0):
    n, kdim = a.shape
    d = w_stack.shape[2]
    tn = COL_TILE if kdim <= d else DOWN_TILE
    tk = kdim
    nk = kdim // tk
    tm, nseg, tps = _row_tiling(n, seq_len, ROW_TILE)
    return pl.pallas_call(
        functools.partial(_proj_res_kernel, nseg=nseg, nk=nk),
        grid=(n // tm, d // tn, nk),
        in_specs=[
            pl.BlockSpec((tm, tk), lambda i, j, k: (i, k)),
            pl.BlockSpec((None, tk, tn), lambda i, j, k: (layer, k, j)),
            pl.BlockSpec((tm, tn), lambda i, j, k: (i + x_row0 // tm, j)),
            pl.BlockSpec((None, nseg, 1, tn), lambda i, j, k: (k_gate, i // tps, 0, j)),
        ],
        out_specs=pl.BlockSpec((tm, tn), lambda i, j, k: (i, j)),
        out_shape=jax.ShapeDtypeStruct((n, d), F32),
        compiler_params=_cparams(("arbitrary", "arbitrary", "arbitrary")),
        name=name,
    )(a, w_stack, x, mod6)


def _merge_kernel(ya_ref, yb_ref, yc_ref, wa_ref, wb_ref, wc_ref, ga_ref, gb_ref, gc_ref, o_ref):
    acc = None
    for y_ref, w_ref, g_ref in ((ya_ref, wa_ref, ga_ref), (yb_ref, wb_ref, gb_ref), (yc_ref, wc_ref, gc_ref)):
        gate = 1.0 / (1.0 + jnp.exp(-g_ref[...].astype(F32)))
        term = gate * jnp.dot(y_ref[...], w_ref[...], preferred_element_type=F32)
        acc = term if acc is None else acc + term
    o_ref[...] = acc.astype(o_ref.dtype)


def _merge(ya, yb, yc, w_stack, layer, gates, d):
    n, bw = ya.shape
    tn = MERGE_TILE
    tm = min(ROW_TILE, n)
    gsteps = d // tn
    y_spec = pl.BlockSpec((tm, bw), lambda i, j: (i, 0))

    def w_spec(k):
        return pl.BlockSpec((None, None, bw, tn), lambda i, j: (layer, k, 0, j))

    def g_spec(k):
        return pl.BlockSpec((tm, tn), lambda i, j: (i, k * gsteps + j))

    return pl.pallas_call(
        _merge_kernel,
        grid=(n // tm, d // tn),
        in_specs=[y_spec, y_spec, y_spec, w_spec(0), w_spec(1), w_spec(2), g_spec(0), g_spec(1), g_spec(2)],
        out_specs=pl.BlockSpec((tm, tn), lambda i, j: (i, j)),
        out_shape=jax.ShapeDtypeStruct((n, d), BF16),
        compiler_params=_cparams(("arbitrary", "arbitrary")),
        name="merge",
    )(ya, yb, yc, w_stack, w_stack, w_stack, gates, gates, gates)


def _hgrn_levels(c):
    lv, m = [], c // 2
    while m >= DIAG:
        lv.append(m)
        m //= 2
    return tuple(lv)


def _hgrn_select(c):
    tri = np.arange(c)[None, :] <= np.arange(c)[:, None]
    return jnp.asarray(tri.astype(np.float32), dtype=BF16)


def _hgrn_kernel(*refs, c, n_chunks, heads, has_init):
    if has_init:
        q_ref, f_ref, i_ref, g_ref, lbc_ref, gn_ref, sel_ref, s0_ref, ya_ref, s_ref, bpad, kpad = refs
    else:
        q_ref, f_ref, i_ref, g_ref, lbc_ref, gn_ref, sel_ref, ya_ref, s_ref, bpad, kpad = refs

    @pl.when(pl.program_id(1) == 0)
    def _():
        s_ref[...] = s0_ref[...] if has_init else jnp.zeros(s_ref.shape, F32)

    _hgrn_chunks(q_ref, f_ref, i_ref, g_ref, lbc_ref, gn_ref, sel_ref, ya_ref, s_ref, bpad, kpad,
                 c=c, n_chunks=n_chunks, heads=heads, unroll=False)


def _hgrn_chunks(q_ref, f_ref, i_ref, g_ref, lbc_ref, gn_ref, sel_ref, ya_ref, s_ref, bpad, kpad,
                 *, c, n_chunks, heads, unroll, fillers=()):
    levels = _hgrn_levels(c)
    nl = len(levels)
    width = heads * A_DK
    head_lanes = [slice(hh * A_DK, (hh + 1) * A_DK) for hh in range(heads)]

    bpad[0:DIAG, :] = jnp.zeros((DIAG, width), F32)
    kpad[0:DIAG, :] = jnp.zeros((DIAG, width), F32)

    sel = sel_ref[...]
    log_lb = lbc_ref[0:1, :]
    log1m_lb = lbc_ref[1:2, :]
    one_m_lb = lbc_ref[2:3, :]
    row = lax.broadcasted_iota(jnp.int32, (c, width), 0)
    rr = lax.broadcasted_iota(jnp.int32, (c, c), 0)
    cc = lax.broadcasted_iota(jnp.int32, (c, c), 1)
    upper = [(row & m) != 0 for m in levels]
    same = [(rr ^ cc) < 2 * m for m in levels]
    dmat = jnp.where(((rr ^ cc) < DIAG) & (cc <= rr), rr - cc, -1)
    nt_dims = (((1,), (1,)), ((), ()))
    tn_dims = (((0,), (0,)), ((), ()))

    def chunk(ci):
        rows = slice(ci * c, (ci + 1) * c) if unroll else pl.ds(pl.multiple_of(ci * c, c), c)
        z = f_ref[rows, :]
        aq = q_ref[rows, :]

        log_sig = jnp.minimum(z, 0.0) - jnp.log(1.0 + jnp.exp(-jnp.abs(z)))
        bb = log1m_lb + log_sig
        log_f = jnp.maximum(log_lb, bb) + jnp.log(1.0 + jnp.exp(-jnp.abs(log_lb - bb)))
        k = one_m_lb / (1.0 + jnp.exp(z))
        q = _silu(aq)

        b = _dot3(sel, log_f * LOG2E)
        bpad[DIAG:DIAG + c, :] = b
        kpad[DIAG:DIAG + c, :] = k
        b_tot = bpad[DIAG + c - 1:DIAG + c, :]
        vb = i_ref[rows, :].astype(BF16)
        q_in = (q * jnp.exp2(b)).astype(BF16)
        k_out = (k * jnp.exp2(b_tot - b)).astype(BF16)
        yield

        att = [jnp.zeros((c, c), F32) for _ in range(heads)]
        for li in range(nl):
            m = levels[li]
            ref_b = jnp.concatenate(
                [jnp.broadcast_to(bpad[DIAG + g0 + m - 1:DIAG + g0 + m, :], (2 * m, width))
                 for g0 in range(0, c, 2 * m)], axis=0)
            e = jnp.exp2(-jnp.abs(b - ref_b))
            qs = jnp.where(upper[li], q * e, 0.0).astype(BF16)
            ks = jnp.where(upper[li], 0.0, k * e).astype(BF16)
            for hh, lanes in enumerate(head_lanes):
                a_l = lax.dot_general(qs[:, lanes], ks[:, lanes], nt_dims, preferred_element_type=F32)
                att[hh] = att[hh] + jnp.where(same[li], a_l, 0.0)
        yield

        for dlt in range(DIAG):
            if dlt == 0:
                w = q * k
            else:
                ksh = kpad[DIAG - dlt:DIAG - dlt + c, :]
                bsh = bpad[DIAG - dlt:DIAG - dlt + c, :]
                w = q * ksh * jnp.exp2(b - bsh)
            for hh, lanes in enumerate(head_lanes):
                col = jnp.sum(w[:, lanes], axis=-1, keepdims=True)
                att[hh] = jnp.where(dmat == dlt, col, att[hh])
            if dlt in (DIAG // 2 - 1, DIAG - 1):
                yield

        decay_row = jnp.exp2(b_tot)
        outs = []
        for hh, lanes in enumerate(head_lanes):
            s_prev = s_ref[0, hh]
            o = jnp.dot(q_in[:, lanes], s_prev.astype(BF16), preferred_element_type=F32)
            o = o + jnp.dot(att[hh].astype(BF16), vb[:, lanes], preferred_element_type=F32)
            decay_col = jnp.broadcast_to(decay_row[:, lanes], (A_DK, A_DK)).T
            s_ref[0, hh] = decay_col * s_prev + lax.dot_general(
                k_out[:, lanes], vb[:, lanes], tn_dims, preferred_element_type=F32)
            outs.append(o * lax.rsqrt(jnp.mean(o * o, axis=-1, keepdims=True) + NORM_EPS))
        y = jnp.concatenate(outs, axis=1) * gn_ref[...] * _silu(g_ref[rows, :])
        ya_ref[rows, :] = y.astype(ya_ref.dtype)
        yield

    if not unroll:
        def body(ci, carry):
            for _ in chunk(ci):
                pass
            return carry

        lax.fori_loop(0, n_chunks, body, 0)
        return

    n_phases = 5
    _emit_interleaved(fillers, [functools.partial(next, gen, None)
                                for gen in map(chunk, range(n_chunks)) for _ in range(n_phases)])


def _hgrn_specs(p, lbc_stack, layer, gn, heads, t_blk, row_block):
    aw = heads * A_DK
    c = min(t_blk, SCAN_CHUNK)
    sel = _hgrn_select(c)

    def col_spec(seg):
        return pl.BlockSpec((t_blk, aw), lambda *g: (row_block(*g), seg))

    in_specs = [col_spec(0), col_spec(1), col_spec(2), col_spec(3),
                pl.BlockSpec((None, SUBLANES, aw), lambda *g: (layer, 0, 0)),
                pl.BlockSpec((1, aw), lambda *g: (0, 0)),
                pl.BlockSpec(sel.shape, lambda *g: (0, 0))]
    args = [p, p, p, p, lbc_stack, gn.reshape(1, aw), sel]
    scratch = [pltpu.VMEM((c + DIAG, aw), F32), pltpu.VMEM((c + DIAG, aw), F32)]
    return c, in_specs, args, pl.BlockSpec((t_blk, aw), lambda *g: (row_block(*g), 0)), scratch


def _mlp_up_hgrn_kernel(x_ref, g_ref, sc_ref, sh_ref, w_ref, q_ref, f_ref, i_ref, gg_ref, lbc_ref, gn_ref, sel_ref,
                        o_ref, ya_ref, s_ref, h_scr, acc_scr, bpad, kpad,
                        *, nseg, c, n_chunks, heads, steps_per_seq):
    j = pl.program_id(1)
    step = pl.program_id(0) * pl.num_programs(1) + j

    @pl.when(j == 0)
    def _():
        _norm_mod_rows(x_ref, g_ref, sc_ref, sh_ref, h_scr, nseg)

    @pl.when(step % steps_per_seq == 0)
    def _():
        s_ref[...] = jnp.zeros(s_ref.shape, F32)

    kc = h_scr.shape[1] // UP_K_SPLIT

    def sub_dot(c0, ki):
        cols = slice(c0, c0 + EPILOGUE_COLS)
        krows = slice(ki * kc, (ki + 1) * kc)
        part = jnp.dot(h_scr[:, krows], w_ref[krows, cols], preferred_element_type=F32)
        if UP_K_SPLIT == 1:
            acc = part
        elif ki == 0:
            acc_scr[...] = part
            return
        elif ki < UP_K_SPLIT - 1:
            acc_scr[...] += part
            return
        else:
            acc = acc_scr[...] + part
        y = jnp.maximum(acc, 0.0)
        o_ref[:, cols] = (y * y).astype(o_ref.dtype)

    fillers = [functools.partial(sub_dot, c0, ki)
               for c0 in range(0, w_ref.shape[1], EPILOGUE_COLS) for ki in range(UP_K_SPLIT)]
    _hgrn_chunks(q_ref, f_ref, i_ref, gg_ref, lbc_ref, gn_ref, sel_ref, ya_ref, s_ref, bpad, kpad,
                 c=c, n_chunks=n_chunks, heads=heads, unroll=True, fillers=fillers)


def _mlp_up_hgrn(x, g, mod6, w_stack, layer, seq_len, p_other, lbc_stack, gn, heads):
    n, d = x.shape
    m = w_stack.shape[2]
    tn = COL_TILE
    tm, nseg, tps = _row_tiling(n, seq_len, ROW_TILE)
    ni, nj = n // tm, m // tn
    n_other = p_other.shape[0]
    t_step = n_other // (ni * nj)
    assert t_step * ni * nj == n_other and seq_len % t_step == 0 and t_step % SUBLANES == 0
    steps_per_seq = seq_len // t_step
    aw = heads * A_DK
    c, h_in_specs, h_args, ya_spec, h_scratch = _hgrn_specs(
        p_other, lbc_stack, layer, gn, heads, t_step, lambda i, j: i * nj + j)

    def mod_spec(k):
        return pl.BlockSpec((None, nseg, 1, d), lambda i, j: (k, i // tps, 0, 0))

    state_spec = pl.BlockSpec((1, heads, A_DK, A_DK), lambda i, j: ((i * nj + j) // steps_per_seq, 0, 0, 0))
    return pl.pallas_call(
        functools.partial(_mlp_up_hgrn_kernel, nseg=nseg, c=c, n_chunks=t_step // c, heads=heads,
                          steps_per_seq=steps_per_seq),
        grid=(ni, nj),
        in_specs=[
            pl.BlockSpec((tm, d), lambda i, j: (i, 0)),
            pl.BlockSpec((1, d), lambda i, j: (0, 0)),
            mod_spec(4),
            mod_spec(3),
            pl.BlockSpec((None, d, tn), lambda i, j: (layer, 0, j)),
        ] + h_in_specs,
        out_specs=[pl.BlockSpec((tm, tn), lambda i, j: (i, j)), ya_spec, state_spec],
        out_shape=[jax.ShapeDtypeStruct((n, m), BF16),
                   jax.ShapeDtypeStruct((n_other, aw), BF16),
                   jax.ShapeDtypeStruct((n_other // seq_len, heads, A_DK, A_DK), F32)],
        scratch_shapes=[pltpu.VMEM((tm, d), BF16), pltpu.VMEM((tm, EPILOGUE_COLS), F32)] + h_scratch,
        compiler_params=_cparams(("arbitrary", "arbitrary")),
        name="mlp_up_hgrn2",
    )(x, g.reshape(1, d), mod6, mod6, w_stack, *h_args)


def _hgrn(p, lbc_stack, layer, gn, s0, bsz, seq_len, heads):
    c = min(seq_len, SCAN_CHUNK)
    t_blk = min(seq_len, 256)
    nt = seq_len // t_blk
    aw = heads * A_DK

    def col_spec(seg):
        return pl.BlockSpec((t_blk, aw), lambda b, t: (b * nt + t, seg))

    sel = _hgrn_select(c)
    in_specs = [col_spec(0), col_spec(1), col_spec(2), col_spec(3),
                pl.BlockSpec((None, SUBLANES, aw), lambda b, t: (layer, 0, 0)),
                pl.BlockSpec((1, aw), lambda b, t: (0, 0)),
                pl.BlockSpec(sel.shape, lambda b, t: (0, 0))]
    args = [p, p, p, p, lbc_stack, gn.reshape(1, aw), sel]
    state_spec = pl.BlockSpec((1, heads, A_DK, A_DK), lambda b, t: (b, 0, 0, 0))
    if s0 is not None:
        in_specs.append(pl.BlockSpec((None, 1, heads, A_DK, A_DK), lambda b, t: (layer, b, 0, 0, 0)))
        args.append(s0)
    return pl.pallas_call(
        functools.partial(_hgrn_kernel, c=c, n_chunks=t_blk // c, heads=heads, has_init=s0 is not None),
        grid=(bsz, nt),
        in_specs=in_specs,
        out_specs=[pl.BlockSpec((t_blk, aw), lambda b, t: (b * nt + t, 0)), state_spec],
        out_shape=[jax.ShapeDtypeStruct((bsz * seq_len, aw), BF16),
                   jax.ShapeDtypeStruct((bsz, heads, A_DK, A_DK), F32)],
        scratch_shapes=[pltpu.VMEM((c + DIAG, aw), F32), pltpu.VMEM((c + DIAG, aw), F32)],
        compiler_params=_cparams(("arbitrary", "arbitrary")),
        name="hgrn2",
    )(*args)


def _ssd_kernel(*refs, t, bw, has_init):
    n_in = SSD_N_INPUTS
    s0_ref, c0_ref = (refs[n_in], refs[n_in + 1]) if has_init else (None, None)
    rest = refs[n_in + 2:] if has_init else refs[n_in:]
    yb_ref, s_out_ref, conv_out_ref, xpad, st = rest
    ti = pl.program_id(1)
    _ssd_reset(s0_ref, c0_ref, xpad, st, ti == 0)
    for _ in _ssd_phases(*refs[:n_in], yb_ref, xpad, st, t=t, bw=bw):
        pass
    _ssd_flush(conv_out_ref, s_out_ref, xpad, st, ti == pl.num_programs(1) - 1)


SSD_N_INPUTS = 12
SSD_PHASES = 9


def _ssd_phases(z_ref, xs_ref, bc_ref, dt_ref, cw_ref, cb_ref, dtb_ref, alog_ref, dsk_ref, gn_ref, exp_ref, sel_ref,
                yb_ref, xpad, st, *, t, bw):
    gw = bw // B_GROUPS
    n_bc = B_GROUPS * B_DSTATE
    pad = SUBLANES

    cur = jnp.concatenate([xs_ref[...], bc_ref[...]], axis=1)
    prev = xpad[...]
    row8 = lax.broadcasted_iota(jnp.int32, (pad, cur.shape[1]), 0)
    conv = cb_ref[...]
    for j in range(CONV_W):
        s = CONV_W - 1 - j
        if s == 0:
            tap = cur
        else:
            rolled = pltpu.roll(cur, s, 0)
            head = jnp.where(row8 < s, pltpu.roll(prev, s, 0), rolled[0:pad])
            tap = jnp.concatenate([head, rolled[pad:]], axis=0)
        conv = conv + tap * cw_ref[j:j + 1, :]
    xbc = _silu(conv)
    x = xbc[:, 0:bw]
    yield

    dt = _softplus(dt_ref[...] + dtb_ref[...])
    a = dt * (-jnp.exp(alog_ref[...]))
    cs = _dot3(sel_ref[...], a)
    a_cum = cs[0:t]
    a_tot = cs[t:2 * t]
    ex = _dot3_rhs(jnp.concatenate([dt, a_cum, a_tot - a_cum], axis=0), exp_ref[...])
    dt_e = ex[0:t]
    acum_e = ex[t:2 * t]
    dec_e = ex[2 * t:3 * t]
    atot_e = acum_e[t - 1:t, :]

    xdt = x * dt_e
    xw = (xdt * jnp.exp(dec_e)).astype(BF16)
    xdt_b = xdt.astype(BF16)
    a_cum_t = a_cum.T
    rr = lax.broadcasted_iota(jnp.int32, (t, t), 0)
    cc = lax.broadcasted_iota(jnp.int32, (t, t), 1)
    causal = cc <= rr
    lane = lax.broadcasted_iota(jnp.int32, (t, LANES), 1)
    heads_per_group = gw // B_HEADDIM
    pairs_per_group = gw // LANES
    yield

    y_groups = []
    for g in range(B_GROUPS):
        bg = xbc[:, bw + g * B_DSTATE:bw + (g + 1) * B_DSTATE].astype(BF16)
        cg = xbc[:, bw + n_bc + g * B_DSTATE:bw + n_bc + (g + 1) * B_DSTATE].astype(BF16)
        gs = slice(g * gw, (g + 1) * gw)
        st_g = st[:, gs]
        scores = lax.dot_general(cg, bg, (((1,), (1,)), ((), ())), preferred_element_type=F32)
        y_off = jnp.dot(cg, st_g.astype(BF16), preferred_element_type=F32) * jnp.exp(acum_e[:, gs])
        st[:, gs] = jnp.exp(atot_e[:, gs]) * st_g + lax.dot_general(
            bg, xw[:, gs], (((0,), (0,)), ((), ())), preferred_element_type=F32)
        yield
        y_pairs = []
        for pr in range(pairs_per_group):
            cols = slice(g * gw + pr * LANES, g * gw + (pr + 1) * LANES)
            xp = xdt_b[:, cols]
            acc = None
            for half in range(LANES // B_HEADDIM):
                h = g * heads_per_group + pr * (LANES // B_HEADDIM) + half
                diff = a_cum[:, h:h + 1] - a_cum_t[h:h + 1, :]
                lmat = jnp.where(causal, jnp.exp(jnp.where(causal, diff, 0.0)), 0.0)
                m = (scores * lmat).astype(BF16)
                in_half = (lane // B_HEADDIM) == half
                term = jnp.dot(m, jnp.where(in_half, xp, jnp.zeros_like(xp)), preferred_element_type=F32)
                acc = term if acc is None else acc + term
            y_pairs.append(acc)
            if pr % 2 == 1:
                yield
        y_groups.append(jnp.concatenate(y_pairs, axis=1) + y_off)
    y = jnp.concatenate(y_groups, axis=1) + dsk_ref[...] * x
    y = y * _silu(z_ref[...])
    outs = []
    for g in range(B_GROUPS):
        yg = y[:, g * gw:(g + 1) * gw]
        outs.append(yg * lax.rsqrt(jnp.mean(yg * yg, axis=-1, keepdims=True) + NORM_EPS))
    yb_ref[...] = (jnp.concatenate(outs, axis=1) * gn_ref[...]).astype(yb_ref.dtype)

    xpad[...] = cur[t - pad:t, :]
    yield


def _ssd_reset(s0_ref, c0_ref, xpad, st, first):
    @pl.when(first)
    def _():
        if s0_ref is not None:
            xpad[...] = c0_ref[0]
            st[...] = s0_ref[0].reshape(st.shape[1], st.shape[0]).T
        else:
            xpad[...] = jnp.zeros(xpad.shape, F32)
            st[...] = jnp.zeros(st.shape, F32)


def _ssd_flush(conv_out_ref, s_out_ref, xpad, st, last):
    @pl.when(last)
    def _():
        conv_out_ref[0] = xpad[...]
        s_out_ref[0] = st[...].T.reshape(s_out_ref.shape[1:])


def _ssd_operands(p, cols, prm, t, n_rows, n_seq, row_block, seq_block):
    bw = prm["bw"]
    heads = bw // B_HEADDIM
    n_bc = B_GROUPS * B_DSTATE
    cdim = bw + 2 * n_bc
    tri = np.arange(t)[None, :] <= np.arange(t)[:, None]
    sel = jnp.asarray(np.concatenate([tri, np.ones((t, t), bool)], 0).astype(np.float32), dtype=BF16)
    expand = np.zeros((LANES, bw), np.float32)
    expand[np.arange(bw) // B_HEADDIM, np.arange(bw)] = 1.0
    expand = jnp.asarray(expand, dtype=BF16)

    def blk(width, off):
        return pl.BlockSpec((t, width), lambda *g: (row_block(*g), off // width))

    def full(shape):
        return pl.BlockSpec(shape, lambda *g: (0,) * len(shape))

    in_specs = [blk(bw, cols["z"]), blk(bw, cols["xs"]), blk(2 * n_bc, cols["bc"]), blk(LANES, cols["dt"]),
                full((CONV_W, cdim)), full((1, cdim)), full((1, LANES)), full((1, LANES)),
                full((1, bw)), full((1, bw)), full(expand.shape), full(sel.shape)]
    args = [p, p, p, p, prm["conv_w"], prm["conv_b"], prm["dt_bias"], prm["a_log"], prm["d_skip"], prm["gn"],
            expand, sel]
    assert len(args) == SSD_N_INPUTS
    out_specs = [pl.BlockSpec((t, bw), lambda *g: (row_block(*g), 0)),
                 pl.BlockSpec((1, heads, B_HEADDIM, B_DSTATE), lambda *g: (seq_block(*g), 0, 0, 0)),
                 pl.BlockSpec((1, SUBLANES, cdim), lambda *g: (seq_block(*g), 0, 0))]
    out_shape = [jax.ShapeDtypeStruct((n_rows, bw), BF16),
                 jax.ShapeDtypeStruct((n_seq, heads, B_HEADDIM, B_DSTATE), F32),
                 jax.ShapeDtypeStruct((n_seq, SUBLANES, cdim), F32)]
    scratch = [pltpu.VMEM((SUBLANES, cdim), F32), pltpu.VMEM((B_DSTATE, bw), F32)]
    return in_specs, args, out_specs, out_shape, scratch


def _ssd(p, cols, prm, s0, c0, bsz, seq_len):
    bw = prm["bw"]
    heads = bw // B_HEADDIM
    cdim = bw + 2 * B_GROUPS * B_DSTATE
    t = min(seq_len, SSD_ROWS)
    nt = seq_len // t
    in_specs, args, out_specs, out_shape, scratch = _ssd_operands(
        p, cols, prm, t, bsz * seq_len, bsz, lambda b, ti: b * nt + ti, lambda b, ti: b)
    if s0 is not None:
        layer = prm["layer"]
        in_specs += [pl.BlockSpec((None, 1, heads, B_HEADDIM, B_DSTATE), lambda b, ti: (layer, b, 0, 0, 0)),
                     pl.BlockSpec((None, 1, SUBLANES, cdim), lambda b, ti: (layer, b, 0, 0))]
        args += [s0, c0]
    return pl.pallas_call(
        functools.partial(_ssd_kernel, t=t, bw=bw, has_init=s0 is not None),
        grid=(bsz, nt),
        in_specs=in_specs,
        out_specs=out_specs,
        out_shape=out_shape,
        scratch_shapes=scratch,
        compiler_params=_cparams(("arbitrary", "arbitrary")),
        name="ssd",
    )(*args)


def _down_sub_dots(a_ref, w_ref, x_ref, gate_ref, o_ref, acc_scr, nseg):
    kc = a_ref.shape[1] // DOWN_K_SPLIT

    def sub_dot(ki):
        rows = slice(ki * kc, (ki + 1) * kc)
        part = jnp.dot(a_ref[:, rows], w_ref[rows, :], preferred_element_type=F32)
        if ki == 0:
            acc_scr[...] = part
        elif ki < DOWN_K_SPLIT - 1:
            acc_scr[...] += part
        else:
            x = x_ref[...]
            acc = acc_scr[...] + part
            o_ref[...] = (_per_seq(x, nseg) + gate_ref[...] * _per_seq(acc, nseg)).reshape(x.shape)

    return [functools.partial(sub_dot, ki) for ki in range(DOWN_K_SPLIT)]


def _emit_interleaved(fillers, phases):
    per_filler = -(-len(phases) // max(len(fillers), 1))
    pos = 0
    for fill in fillers:
        fill()
        for ph in phases[pos:pos + per_filler]:
            ph()
        pos += per_filler
    for ph in phases[pos:]:
        ph()


def _mlp_down_ssd_kernel(a_ref, w_ref, x_ref, gate_ref, *refs, nseg, t, bw, steps_per_seq):
    ssd_in = refs[:SSD_N_INPUTS]
    o_ref, yb_ref, s_out_ref, conv_out_ref, acc_scr, xpad, st = refs[SSD_N_INPUTS:]
    step = pl.program_id(0) * pl.num_programs(1) + pl.program_id(1)
    pos = step % steps_per_seq
    _ssd_reset(None, None, xpad, st, pos == 0)
    gen = _ssd_phases(*ssd_in, yb_ref, xpad, st, t=t, bw=bw)
    phases = [functools.partial(next, gen, None) for _ in range(SSD_PHASES)]
    _emit_interleaved(_down_sub_dots(a_ref, w_ref, x_ref, gate_ref, o_ref, acc_scr, nseg), phases)
    _ssd_flush(conv_out_ref, s_out_ref, xpad, st, pos == steps_per_seq - 1)


def _mlp_down_ssd(a, w_stack, layer, x, mod6, k_gate, seq_len, p_other, cols, prm):
    n, kdim = a.shape
    d = w_stack.shape[2]
    tn = DOWN_TILE
    tm, nseg, tps = _row_tiling(n, seq_len, ROW_TILE)
    ni, nj = n // tm, d // tn
    n_other = p_other.shape[0]
    t = n_other // (ni * nj)
    assert t * ni * nj == n_other and seq_len % t == 0 and t % SUBLANES == 0 and kdim % DOWN_K_SPLIT == 0
    steps_per_seq = seq_len // t
    s_in_specs, s_args, s_out_specs, s_out_shape, s_scratch = _ssd_operands(
        p_other, cols, prm, t, n_other, n_other // seq_len,
        lambda i, j: i * nj + j, lambda i, j: (i * nj + j) // steps_per_seq)
    return pl.pallas_call(
        functools.partial(_mlp_down_ssd_kernel, nseg=nseg, t=t, bw=prm["bw"], steps_per_seq=steps_per_seq),
        grid=(ni, nj),
        in_specs=[
            pl.BlockSpec((tm, kdim), lambda i, j: (i, 0)),
            pl.BlockSpec((None, kdim, tn), lambda i, j: (layer, 0, j)),
            pl.BlockSpec((tm, tn), lambda i, j: (i, j)),
            pl.BlockSpec((None, nseg, 1, tn), lambda i, j: (k_gate, i // tps, 0, j)),
        ] + s_in_specs,
        out_specs=[pl.BlockSpec((tm, tn), lambda i, j: (i, j))] + s_out_specs,
        out_shape=[jax.ShapeDtypeStruct((n, d), F32)] + s_out_shape,
        scratch_shapes=[pltpu.VMEM((tm, tn), F32)] + s_scratch,
        compiler_params=_cparams(("arbitrary", "arbitrary")),
        name="mlp_down_ssd",
    )(a, w_stack, x, mod6, *s_args)


def _cmlp_kernel(u_ref, v_ref, lng_ref, lnb_ref, ws_ref, bst_ref, *out_refs, t, n_chunks, keep_v):
    yc_ref = out_refs[0]
    cw = u_ref.shape[1] // C_GROUPS
    rr = lax.broadcasted_iota(jnp.int32, (t, t), 0)
    cc = lax.broadcasted_iota(jnp.int32, (t, t), 1)
    wts = [jnp.where(cc <= rr, ws_ref[g, 0:t, 0:t], 0.0).astype(BF16) for g in range(C_GROUPS)]
    for ci in range(n_chunks):
        rows = slice(ci * t, (ci + 1) * t)
        u = _gelu(u_ref[rows, :])
        gv = _gelu(v_ref[rows, :])
        mu = jnp.mean(gv, axis=-1, keepdims=True)
        dv = gv - mu
        var = jnp.mean(dv * dv, axis=-1, keepdims=True)
        v = dv * lax.rsqrt(var + NORM_EPS) * lng_ref[...] + lnb_ref[...]
        if keep_v:
            out_refs[1][rows, :] = v
        vb = v.astype(BF16)
        for g in range(C_GROUPS):
            lanes = slice(g * cw, (g + 1) * cw)
            mixed = jnp.dot(wts[g], vb[:, lanes], preferred_element_type=F32) + bst_ref[0:t, g:g + 1]
            yc_ref[rows, lanes] = (u[:, lanes] * mixed).astype(yc_ref.dtype)


def _cmlp(p, col_u, col_v, prm, bsz, seq_len, keep_v):
    cw = prm["cw"]
    t = min(seq_len, CMLP_CHUNK)
    t_blk = min(seq_len, 4 * CMLP_CHUNK)
    n = bsz * seq_len

    def full(shape):
        return pl.BlockSpec(shape, lambda i: (0,) * len(shape))

    out_specs = [pl.BlockSpec((t_blk, cw), lambda i: (i, 0))]
    out_shape = [jax.ShapeDtypeStruct((n, cw), BF16)]
    if keep_v:
        out_specs.append(pl.BlockSpec((t_blk, cw), lambda i: (i, 0)))
        out_shape.append(jax.ShapeDtypeStruct((n, cw), F32))
    return pl.pallas_call(
        functools.partial(_cmlp_kernel, t=t, n_chunks=t_blk // t, keep_v=keep_v),
        grid=(n // t_blk,),
        in_specs=[pl.BlockSpec((t_blk, cw), lambda i: (i, col_u // cw)),
                  pl.BlockSpec((t_blk, cw), lambda i: (i, col_v // cw)),
                  full((1, cw)), full((1, cw)),
                  full((C_GROUPS, CMLP_CHUNK, CMLP_CHUNK)), full((CMLP_CHUNK, C_GROUPS))],
        out_specs=out_specs,
        out_shape=out_shape,
        compiler_params=_cparams(("arbitrary",)),
        name="cmlp",
    )(p, p, prm["ln_g"], prm["ln_b"], prm["ws"], prm["bs_t"])


def _run_trunk(x3, mod, st_hgrn, st_ssm, st_conv, keep_v, w):
    bsz, seq_len, d = x3.shape
    depth = mod.shape[0]
    aw = bw = cw = d // 2
    heads_a = aw // A_DK
    x = x3.reshape(bsz * seq_len, d)
    col = w["cols"]
    hgrn_out, ssm_out, conv_out, v_out = [], [], [], []
    for l in range(depth):
        mod6 = mod[l].reshape(bsz, 6, 1, d).transpose(1, 0, 2, 3)
        p, gates = _in_proj(x, w["norm1_g"][l], mod6, w["w_in"], l, seq_len, col["gate"])

        y_a, s_h = _hgrn(p, w["lbc"], l, w["hgrn_onorm_g"][l], st_hgrn, bsz, seq_len, heads_a)
        ssd_prm = dict(bw=bw, layer=l, conv_w=w["ssm_conv_w"][l], conv_b=w["ssm_conv_b"][l][None],
                       dt_bias=w["dt_bias_pad"][l][None], a_log=w["a_log_pad"][l][None],
                       d_skip=w["d_skip"][l][None], gn=w["ssm_onorm_g"][l][None])
        y_b, s_s, conv_tail = _ssd(p, col, ssd_prm, st_ssm, st_conv, bsz, seq_len)
        cm_prm = dict(cw=cw, ln_g=w["cmlp_ln_g"][l][None], ln_b=w["cmlp_ln_b"][l][None],
                      ws=w["cmlp_ws"][l], bs_t=w["cmlp_bs"][l].T)
        c_res = _cmlp(p, col["u"], col["v"], cm_prm, bsz, seq_len, keep_v)
        merged = _merge(y_a, y_b, c_res[0], w["w_branch"], l, gates, d)
        x = _proj_residual(merged, w["w_out"], l, x, mod6, 2, seq_len, "out_proj")
        hid = _mlp_up(x, w["norm2_g"][l], mod6, w["w_up"], l, seq_len)
        x = _proj_residual(hid, w["w_down"], l, x, mod6, 5, seq_len, "mlp_down")

        hgrn_out.append(s_h)
        ssm_out.append(s_s)
        conv_out.append(conv_tail[:, SUBLANES - (CONV_W - 1):, :])
        if keep_v:
            v_out.append(c_res[1].reshape(bsz, seq_len, cw))
    y = _final_norm(x, w["final_g"]).reshape(bsz, seq_len, d)
    return (y, jnp.stack(hgrn_out), jnp.stack(ssm_out), jnp.stack(conv_out),
            jnp.stack(v_out) if keep_v else None)


def _mlp_down_hgrn_kernel(a_ref, w_ref, x_ref, gate_ref, q_ref, f_ref, i_ref, gg_ref, lbc_ref, gn_ref, sel_ref,
                          o_ref, ya_ref, s_ref, acc_scr, bpad, kpad, *, nseg, c, n_chunks, heads, steps_per_seq):
    step = pl.program_id(0) * pl.num_programs(1) + pl.program_id(1)

    @pl.when(step % steps_per_seq == 0)
    def _():
        s_ref[...] = jnp.zeros(s_ref.shape, F32)

    fillers = _down_sub_dots(a_ref, w_ref, x_ref, gate_ref, o_ref, acc_scr, nseg)
    _hgrn_chunks(q_ref, f_ref, i_ref, gg_ref, lbc_ref, gn_ref, sel_ref, ya_ref, s_ref, bpad, kpad,
                 c=c, n_chunks=n_chunks, heads=heads, unroll=True, fillers=fillers)


def _mlp_down_hgrn(a, w_stack, layer, x, mod6, k_gate, seq_len, p_other, lbc_stack, lbc_layer, gn, heads):
    n, kdim = a.shape
    d = w_stack.shape[2]
    tn = DOWN_TILE
    tm, nseg, tps = _row_tiling(n, seq_len, ROW_TILE)
    ni, nj = n // tm, d // tn
    n_other = p_other.shape[0]
    t_step = n_other // (ni * nj)
    assert t_step * ni * nj == n_other and seq_len % t_step == 0 and t_step % SUBLANES == 0
    assert kdim % DOWN_K_SPLIT == 0
    steps_per_seq = seq_len // t_step
    aw = heads * A_DK
    c, h_in_specs, h_args, ya_spec, h_scratch = _hgrn_specs(
        p_other, lbc_stack, lbc_layer, gn, heads, t_step, lambda i, j: i * nj + j)
    state_spec = pl.BlockSpec((1, heads, A_DK, A_DK), lambda i, j: ((i * nj + j) // steps_per_seq, 0, 0, 0))
    return pl.pallas_call(
        functools.partial(_mlp_down_hgrn_kernel, nseg=nseg, c=c, n_chunks=t_step // c, heads=heads,
                          steps_per_seq=steps_per_seq),
        grid=(ni, nj),
        in_specs=[
            pl.BlockSpec((tm, kdim), lambda i, j: (i, 0)),
            pl.BlockSpec((None, kdim, tn), lambda i, j: (layer, 0, j)),
            pl.BlockSpec((tm, tn), lambda i, j: (i, j)),
            pl.BlockSpec((None, nseg, 1, tn), lambda i, j: (k_gate, i // tps, 0, j)),
        ] + h_in_specs,
        out_specs=[pl.BlockSpec((tm, tn), lambda i, j: (i, j)), ya_spec, state_spec],
        out_shape=[jax.ShapeDtypeStruct((n, d), F32),
                   jax.ShapeDtypeStruct((n_other, aw), BF16),
                   jax.ShapeDtypeStruct((n_other // seq_len, heads, A_DK, A_DK), F32)],
        scratch_shapes=[pltpu.VMEM((tm, tn), F32)] + h_scratch,
        compiler_params=_cparams(("arbitrary", "arbitrary")),
        name="mlp_down_hgrn2",
    )(a, w_stack, x, mod6, *h_args)


def _run_trunk_halves(x3, mod, w):
    bsz, seq_len, d = x3.shape
    depth = mod.shape[0]
    hb = bsz // 2
    aw = bw = cw = d // 2
    heads_a = aw // A_DK
    col = w["cols"]
    n_half = hb * seq_len
    x_full = x3.reshape(bsz * seq_len, d)
    xs = [x_full, x_full]
    row0 = [0, n_half]
    mods = [mod[:, :hb], mod[:, hb:]]
    hgrn_out, ssm_out, conv_out = [], [], []
    pending = None
    for l in range(depth):
        mod6 = [m[l].reshape(hb, 6, 1, d).transpose(1, 0, 2, 3) for m in mods]
        ssd_prm = dict(bw=bw, layer=l, conv_w=w["ssm_conv_w"][l], conv_b=w["ssm_conv_b"][l][None],
                       dt_bias=w["dt_bias_pad"][l][None], a_log=w["a_log_pad"][l][None],
                       d_skip=w["d_skip"][l][None], gn=w["ssm_onorm_g"][l][None])
        cm_prm = dict(cw=cw, ln_g=w["cmlp_ln_g"][l][None], ln_b=w["cmlp_ln_b"][l][None],
                      ws=w["cmlp_ws"][l], bs_t=w["cmlp_bs"][l].T)
        gn_a = w["hgrn_onorm_g"][l]

        def mix_merge(h, p, gates, y_a, ssd_res=None):
            y_b, s_s, conv_tail = ssd_res or _ssd(p, col, ssd_prm, None, None, hb, seq_len)
            y_c = _cmlp(p, col["u"], col["v"], cm_prm, hb, seq_len, False)[0]
            merged = _merge(y_a, y_b, y_c, w["w_branch"], l, gates, d)
            x_new = _proj_residual(merged, w["w_out"], l, xs[h], mod6[h], 2, seq_len, "out_proj", row0[h])
            row0[h] = 0
            return x_new, s_s, conv_tail

        p0, g0 = _in_proj(xs[0], w["norm1_g"][l], mod6[0], w["w_in"], l, seq_len, col["gate"], (row0[0], n_half))
        if pending is None:
            ya0, sh0 = _hgrn(p0, w["lbc"], l, gn_a, None, hb, seq_len, heads_a)
        else:
            hid1, x1_mid, mod6_prev = pending
            xs[1], ya0, sh0 = _mlp_down_hgrn(hid1, w["w_down"], l - 1, x1_mid, mod6_prev, 5, seq_len,
                                              p0, w["lbc"], l, gn_a, heads_a)
        p1, g1 = _in_proj(xs[1], w["norm1_g"][l], mod6[1], w["w_in"], l, seq_len, col["gate"], (row0[1], n_half))
        x0, ss0, ct0 = mix_merge(0, p0, g0, ya0)
        hid0, ya1, sh1 = _mlp_up_hgrn(x0, w["norm2_g"][l], mod6[0], w["w_up"], l, seq_len,
                                      p1, w["lbc"], gn_a, heads_a)
        xs[0], *ssd1 = _mlp_down_ssd(hid0, w["w_down"], l, x0, mod6[0], 5, seq_len, p1, col, ssd_prm)
        x1_mid, ss1, ct1 = mix_merge(1, p1, g1, ya1, tuple(ssd1))
        hid1 = _mlp_up(x1_mid, w["norm2_g"][l], mod6[1], w["w_up"], l, seq_len)
        pending = (hid1, x1_mid, mod6[1])

        hgrn_out.append(jnp.concatenate([sh0, sh1], axis=0))
        ssm_out.append(jnp.concatenate([ss0, ss1], axis=0))
        conv_out.append(jnp.concatenate([ct0, ct1], axis=0)[:, SUBLANES - (CONV_W - 1):, :])
    hid1, x1_mid, mod6_prev = pending
    xs[1] = _proj_residual(hid1, w["w_down"], depth - 1, x1_mid, mod6_prev, 5, seq_len, "mlp_down")
    y = _final_norm_pair(xs[0], xs[1], w["final_g"]).reshape(bsz, seq_len, d)
    return (y, jnp.stack(hgrn_out), jnp.stack(ssm_out), jnp.stack(conv_out), None)


def kernel(x_prompt, x_sample, state_hgrn, state_ssm, state_conv, c_prompt, c_sample, norm1_g, norm2_g,
           w_mod, b_mod, w_in, hgrn_lb, hgrn_onorm_g, ssm_conv_w, ssm_conv_b, ssm_dt_bias, ssm_a_log, ssm_d,
           ssm_onorm_g, cmlp_ln_g, cmlp_ln_b, cmlp_ws, cmlp_bs, w_branch, w_out, w_up, w_down, final_g):
    d = x_prompt.shape[-1]
    depth = w_in.shape[0]
    aw = bw = cw = d // 2
    n_bc = B_GROUPS * B_DSTATE
    heads_b = bw // B_HEADDIM
    assert heads_b <= LANES and bw % LANES == 0 and DT_PAD % LANES == 0

    o_dt = 4 * aw + bw + bw + 2 * n_bc
    o_u = o_dt + heads_b
    w_in_t = jnp.swapaxes(w_in, 1, 2).astype(BF16)
    w_in_r = jnp.concatenate(
        [w_in_t[:, :o_u], jnp.zeros((depth, DT_PAD - heads_b, d), BF16), w_in_t[:, o_u:]], axis=1)
    cols = dict(z=4 * aw, xs=4 * aw + bw, bc=4 * aw + 2 * bw, dt=o_dt)
    cols["u"] = o_dt + DT_PAD
    cols["v"] = cols["u"] + cw
    cols["gate"] = cols["v"] + cw
    assert cols["gate"] % IN_TILE == 0 and w_in_r.shape[1] % IN_TILE == 0

    pad_h = lambda a: jnp.pad(a.astype(F32), ((0, 0), (0, LANES - heads_b)))
    w = dict(
        cols=cols, w_in=w_in_r,
        norm1_g=norm1_g, norm2_g=norm2_g, final_g=final_g,
        lbc=_lb_consts(hgrn_lb), hgrn_onorm_g=hgrn_onorm_g,
        ssm_conv_w=ssm_conv_w, ssm_conv_b=ssm_conv_b,
        dt_bias_pad=pad_h(ssm_dt_bias), a_log_pad=pad_h(ssm_a_log),
        d_skip=jnp.repeat(ssm_d.astype(F32), B_HEADDIM, axis=1), ssm_onorm_g=ssm_onorm_g,
        cmlp_ln_g=cmlp_ln_g, cmlp_ln_b=cmlp_ln_b, cmlp_ws=cmlp_ws, cmlp_bs=cmlp_bs,
        w_branch=w_branch.astype(BF16).reshape(depth, N_BRANCH, aw, d), w_out=w_out.astype(BF16),
        w_up=w_up.astype(BF16), w_down=w_down.astype(BF16),
    )

    nb = x_prompt.shape[0]
    mod = _modulation(jnp.concatenate([c_prompt, c_sample], axis=0), w_mod, b_mod)
    if nb % 2 == 0:
        y_p, hgrn_p, ssm_p, conv_p, _ = _run_trunk_halves(x_prompt, mod[:, :nb], w)
    else:
        y_p, hgrn_p, ssm_p, conv_p, _ = _run_trunk(x_prompt, mod[:, :nb], None, None, None, False, w)
    conv_pad = jnp.pad(state_conv, ((0, 0), (0, 0), (SUBLANES - (CONV_W - 1), 0), (0, 0)))
    y_s, hgrn_s, ssm_s, conv_s, v_s = _run_trunk(x_sample, mod[:, nb:], state_hgrn, state_ssm, conv_pad, True, w)
    return (y_p, y_s, hgrn_p, ssm_p, conv_p, hgrn_s, ssm_s, conv_s, v_s)
```

```python
import functools
import math

import numpy as np
import jax
import jax.numpy as jnp
from jax import lax
from jax.experimental import pallas as pl
from jax.experimental.pallas import tpu as pltpu

F32 = jnp.float32
BF16 = jnp.bfloat16

A_DK = 128
B_HEADDIM = 64
B_GROUPS = 2
B_DSTATE = 128
CONV_W = 4
C_GROUPS = 4
CMLP_CHUNK = 128
N_BRANCH = 3
SCAN_CHUNK = 64
NORM_EPS = 1e-6
LB_FLOOR = 1e-30
LOG2E = 1.4426950408889634

LANES = 128
SUBLANES = 8
VMEM_LIMIT = 56 * 1024 * 1024
DIAG = SUBLANES
DT_PAD = 512
ROW_TILE = 1024
COL_TILE = 1024
IN_TILE = 1536
MERGE_TILE = 512
MERGE_ROW_TILE = 1024
OUT_TILE = 512
OUT_ROW_TILE = 2048
DOWN_TILE = 256
EPILOGUE_COLS = 256
SSD_ROWS = 128
UP_K_SPLIT = 1
DOWN_K_SPLIT = 16
NORM_ROWS = 128


def _cparams(sem):
    return pltpu.CompilerParams(dimension_semantics=sem, vmem_limit_bytes=VMEM_LIMIT)


def _split3(x):
    hi = x.astype(BF16)
    r1 = x - hi.astype(F32)
    mid = r1.astype(BF16)
    lo = (r1 - mid.astype(F32)).astype(BF16)
    return hi, mid, lo


def _dot3(sel, x):
    hi, mid, lo = _split3(x)
    d = lambda p: jnp.dot(sel, p, preferred_element_type=F32)
    return d(hi) + d(mid) + d(lo)


def _dot3_rhs(x, sel):
    hi, mid, lo = _split3(x)
    d = lambda p: jnp.dot(p, sel, preferred_element_type=F32)
    return d(hi) + d(mid) + d(lo)


def _silu(x):
    return x / (1.0 + jnp.exp(-x))


def _softplus(x):
    return jnp.maximum(x, 0.0) + jnp.log1p(jnp.exp(-jnp.abs(x)))


def _gelu(x):
    return 0.5 * x * (1.0 + lax.erf(x * (1.0 / math.sqrt(2.0))))


def _lb_kernel(lb_ref, out_ref, *, depth):
    x = lb_ref[...]
    m = jnp.max(x, axis=0, keepdims=True)
    e = jnp.exp(x - m)
    p = e / jnp.sum(e, axis=0, keepdims=True)
    acc = jnp.zeros_like(p[0:1])
    zeros5 = jnp.zeros((SUBLANES - 3, x.shape[1]), F32)
    for l in range(depth):
        acc = acc + p[l:l + 1]
        lb = acc - p[0:1]
        out_ref[l] = jnp.concatenate(
            [jnp.log(jnp.maximum(lb, LB_FLOOR)), jnp.log1p(-lb), 1.0 - lb, zeros5], axis=0)


def _lb_consts(hgrn_lb):
    depth, aw = hgrn_lb.shape
    return pl.pallas_call(
        functools.partial(_lb_kernel, depth=depth),
        out_shape=jax.ShapeDtypeStruct((depth, SUBLANES, aw), F32),
        name="hgrn_lb",
    )(hgrn_lb.astype(F32))


def _mod_kernel(c_ref, w_ref, b_ref, o_ref):
    cs = _silu(c_ref[...]).astype(BF16)
    o_ref[...] = jnp.dot(cs, w_ref[...].astype(BF16), preferred_element_type=F32) + b_ref[...]


def _modulation(c_all, w_mod, b_mod):
    depth, d, n6 = w_mod.shape
    s = c_all.shape[0]
    tn = 1024
    return pl.pallas_call(
        _mod_kernel,
        grid=(depth, n6 // tn),
        in_specs=[
            pl.BlockSpec((s, d), lambda l, j: (0, 0)),
            pl.BlockSpec((None, d, tn), lambda l, j: (l, 0, j)),
            pl.BlockSpec((None, 1, tn), lambda l, j: (l, 0, j)),
        ],
        out_specs=pl.BlockSpec((None, s, tn), lambda l, j: (l, 0, j)),
        out_shape=jax.ShapeDtypeStruct((depth, s, n6), F32),
        compiler_params=_cparams(("arbitrary", "arbitrary")),
        name="adaln_mod",
    )(c_all, w_mod, b_mod.reshape(depth, 1, n6))


def _row_tiling(n_tok, seq_len, tm_max):
    tm = min(tm_max, n_tok)
    if seq_len >= tm:
        assert seq_len % tm == 0
        return tm, 1, seq_len // tm
    assert tm % seq_len == 0
    return tm, tm // seq_len, 1


def _per_seq(x, nseg):
    tm, d = x.shape
    return x.reshape(nseg, tm // nseg, d)


def _final_norm_kernel(x_ref, g_ref, o_ref):
    x = x_ref[...]
    o_ref[...] = x * lax.rsqrt(jnp.mean(x * x, axis=-1, keepdims=True) + NORM_EPS) * g_ref[...]


def _final_norm(x, g):
    n, d = x.shape
    tm = min(512, n)
    return pl.pallas_call(
        _final_norm_kernel,
        grid=(n // tm,),
        in_specs=[pl.BlockSpec((tm, d), lambda i: (i, 0)), pl.BlockSpec((1, d), lambda i: (0, 0))],
        out_specs=pl.BlockSpec((tm, d), lambda i: (i, 0)),
        out_shape=jax.ShapeDtypeStruct((n, d), F32),
        compiler_params=_cparams(("arbitrary",)),
        name="final_norm",
    )(x, g.reshape(1, d))


def _final_norm_pair_kernel(xa_ref, xb_ref, g_ref, o_ref, *, nt):
    def norm(x):
        return x * lax.rsqrt(jnp.mean(x * x, axis=-1, keepdims=True) + NORM_EPS) * g_ref[...]

    @pl.when(pl.program_id(0) < nt)
    def _():
        o_ref[...] = norm(xa_ref[...])

    @pl.when(pl.program_id(0) >= nt)
    def _():
        o_ref[...] = norm(xb_ref[...])


def _final_norm_pair(xa, xb, g):
    n, d = xa.shape
    tm = min(512, n)
    nt = n // tm
    return pl.pallas_call(
        functools.partial(_final_norm_pair_kernel, nt=nt),
        grid=(2 * nt,),
        in_specs=[pl.BlockSpec((tm, d), lambda i: (jnp.minimum(i, nt - 1), 0)),
                  pl.BlockSpec((tm, d), lambda i: (jnp.maximum(i - nt, 0), 0)),
                  pl.BlockSpec((1, d), lambda i: (0, 0))],
        out_specs=pl.BlockSpec((tm, d), lambda i: (i, 0)),
        out_shape=jax.ShapeDtypeStruct((2 * n, d), F32),
        compiler_params=_cparams(("arbitrary",)),
        name="final_norm",
    )(xa, xb, g.reshape(1, d))


def _norm_mod_rows(x_ref, g_ref, sc_ref, sh_ref, h_scr, nseg):
    tm = x_ref.shape[0]
    seg = tm // nseg
    rb = min(seg, NORM_ROWS)
    g = g_ref[...]

    def body(r, carry):
        rows = pl.ds(pl.multiple_of(r * rb, rb), rb)
        s = r // (seg // rb)
        x = x_ref[rows, :]
        gain = g * (1.0 + sc_ref[s])
        inv = lax.rsqrt(jnp.mean(x * x, axis=-1, keepdims=True) + NORM_EPS)
        h_scr[rows, :] = (x * inv * gain + sh_ref[s]).astype(BF16)
        return carry

    lax.fori_loop(0, tm // rb, body, 0, unroll=2 if (tm // rb) % 2 == 0 else 1)


def _mlp_up_kernel(x_ref, g_ref, sc_ref, sh_ref, w_ref, o_ref, h_scr, *, nseg):
    @pl.when(pl.program_id(1) == 0)
    def _():
        _norm_mod_rows(x_ref, g_ref, sc_ref, sh_ref, h_scr, nseg)

    y = jnp.maximum(jnp.dot(h_scr[...], w_ref[...], preferred_element_type=F32), 0.0)
    o_ref[...] = (y * y).astype(o_ref.dtype)


def _in_proj_kernel(x_ref, g_ref, sc_ref, sh_ref, w_ref, p_ref, gate_ref, h_scr, *, nseg, n_main):
    j = pl.program_id(1)

    @pl.when(j == 0)
    def _():
        _norm_mod_rows(x_ref, g_ref, sc_ref, sh_ref, h_scr, nseg)

    def project():
        return lax.dot_general(h_scr[...], w_ref[...], (((1,), (1,)), ((), ())), preferred_element_type=F32)

    @pl.when(j < n_main)
    def _():
        p_ref[...] = project()

    @pl.when(j >= n_main)
    def _():
        gate_ref[...] = project().astype(gate_ref.dtype)


def _norm_proj_call(kern, x, g, mod6, k_scale, k_shift, w_stack, layer, seq_len, tn, out_specs, out_shape, name):
    n, d = x.shape
    m = w_stack.shape[2]
    tm, nseg, tps = _row_tiling(n, seq_len, ROW_TILE)

    def mod_spec(k):
        return pl.BlockSpec((None, nseg, 1, d), lambda i, j: (k, i // tps, 0, 0))

    return pl.pallas_call(
        functools.partial(kern, nseg=nseg),
        grid=(n // tm, m // tn),
        in_specs=[
            pl.BlockSpec((tm, d), lambda i, j: (i, 0)),
            pl.BlockSpec((1, d), lambda i, j: (0, 0)),
            mod_spec(k_scale),
            mod_spec(k_shift),
            pl.BlockSpec((None, d, tn), lambda i, j: (layer, 0, j)),
        ],
        out_specs=out_specs(tm),
        out_shape=out_shape,
        scratch_shapes=[pltpu.VMEM((tm, d), BF16)],
        compiler_params=_cparams(("arbitrary", "arbitrary")),
        name=name,
    )(x, g.reshape(1, d), mod6, mod6, w_stack)


def _mlp_up(x, g, mod6, w_stack, layer, seq_len):
    n = x.shape[0]
    m = w_stack.shape[2]
    tn = COL_TILE
    return _norm_proj_call(
        _mlp_up_kernel, x, g, mod6, 4, 3, w_stack, layer, seq_len, tn,
        lambda tm: pl.BlockSpec((tm, tn), lambda i, j: (i, j)),
        jax.ShapeDtypeStruct((n, m), BF16), "mlp_up")


def _in_proj(x, g, mod6, w_stack, layer, seq_len, main_cols, rows=None):
    d = x.shape[1]
    row0, n = rows if rows is not None else (0, x.shape[0])
    m = w_stack.shape[1]
    tn = IN_TILE
    n_main = main_cols // tn
    tm, nseg, tps = _row_tiling(n, seq_len, ROW_TILE)

    def mod_spec(k):
        return pl.BlockSpec((None, nseg, 1, d), lambda i, j: (k, i // tps, 0, 0))

    return pl.pallas_call(
        functools.partial(_in_proj_kernel, nseg=nseg, n_main=n_main),
        grid=(n // tm, m // tn),
        in_specs=[
            pl.BlockSpec((tm, d), lambda i, j: (i + row0 // tm, 0)),
            pl.BlockSpec((1, d), lambda i, j: (0, 0)),
            mod_spec(1),
            mod_spec(0),
            pl.BlockSpec((None, tn, d), lambda i, j: (layer, j, 0)),
        ],
        out_specs=[pl.BlockSpec((tm, tn), lambda i, j: (i, jnp.minimum(j, n_main - 1))),
                   pl.BlockSpec((tm, tn), lambda i, j: (i, jnp.maximum(j - n_main, 0)))],
        out_shape=[jax.ShapeDtypeStruct((n, main_cols), F32), jax.ShapeDtypeStruct((n, m - main_cols), BF16)],
        scratch_shapes=[pltpu.VMEM((tm, d), BF16)],
        compiler_params=_cparams(("arbitrary", "arbitrary")),
        name="in_proj",
    )(x, g.reshape(1, d), mod6, mod6, w_stack)


def _proj_res_kernel(a_ref, w_ref, x_ref, gate_ref, o_ref, *, nseg, nk):
    part = jnp.dot(a_ref[...], w_ref[...], preferred_element_type=F32)

    def finish(acc):
        x = x_ref[...]
        o_ref[...] = (_per_seq(x, nseg) + gate_ref[...] * _per_seq(acc, nseg)).reshape(x.shape)

    if nk == 1:
        finish(part)
        return
    k = pl.program_id(2)

    @pl.when(k == 0)
    def _():
        o_ref[...] = part

    @pl.when((k > 0) & (k < nk - 1))
    def _():
        o_ref[...] += part

    @pl.when(k == nk - 1)
    def _():
        finish(o_ref[...] + part)


def _proj_residual(a, w_stack, layer, x, mod6, k_gate, seq_len, name, x_row0=0):
    n, kdim = a.shape
    d = w_stack.shape[2]
    tn, tm_max = (OUT_TILE, OUT_ROW_TILE) if kdim <= d else (DOWN_TILE, ROW_TILE)
    tk = kdim
    nk = kdim // tk
    tm, nseg, tps = _row_tiling(n, seq_len, tm_max)
    return pl.pallas_call(
        functools.partial(_proj_res_kernel, nseg=nseg, nk=nk),
        grid=(n // tm, d // tn, nk),
        in_specs=[
            pl.BlockSpec((tm, tk), lambda i, j, k: (i, k)),
            pl.BlockSpec((None, tk, tn), lambda i, j, k: (layer, k, j)),
            pl.BlockSpec((tm, tn), lambda i, j, k: (i + x_row0 // tm, j)),
            pl.BlockSpec((None, nseg, 1, tn), lambda i, j, k: (k_gate, i // tps, 0, j)),
        ],
        out_specs=pl.BlockSpec((tm, tn), lambda i, j, k: (i, j)),
        out_shape=jax.ShapeDtypeStruct((n, d), F32),
        compiler_params=_cparams(("arbitrary", "arbitrary", "arbitrary")),
        name=name,
    )(a, w_stack, x, mod6)


def _merge_kernel(ya_ref, yb_ref, yc_ref, wa_ref, wb_ref, wc_ref, ga_ref, gb_ref, gc_ref, o_ref):
    acc = None
    for y_ref, w_ref, g_ref in ((ya_ref, wa_ref, ga_ref), (yb_ref, wb_ref, gb_ref), (yc_ref, wc_ref, gc_ref)):
        gate = 1.0 / (1.0 + jnp.exp(-g_ref[...].astype(F32)))
        term = gate * jnp.dot(y_ref[...], w_ref[...], preferred_element_type=F32)
        acc = term if acc is None else acc + term
    o_ref[...] = acc.astype(o_ref.dtype)


def _merge(ya, yb, yc, w_stack, layer, gates, d):
    n, bw = ya.shape
    tn = MERGE_TILE
    tm = min(MERGE_ROW_TILE, n)
    gsteps = d // tn
    y_spec = pl.BlockSpec((tm, bw), lambda i, j: (i, 0))

    def w_spec(k):
        return pl.BlockSpec((None, None, bw, tn), lambda i, j: (layer, k, 0, j))

    def g_spec(k):
        return pl.BlockSpec((tm, tn), lambda i, j: (i, k * gsteps + j))

    return pl.pallas_call(
        _merge_kernel,
        grid=(n // tm, d // tn),
        in_specs=[y_spec, y_spec, y_spec, w_spec(0), w_spec(1), w_spec(2), g_spec(0), g_spec(1), g_spec(2)],
        out_specs=pl.BlockSpec((tm, tn), lambda i, j: (i, j)),
        out_shape=jax.ShapeDtypeStruct((n, d), BF16),
        compiler_params=_cparams(("arbitrary", "arbitrary")),
        name="merge",
    )(ya, yb, yc, w_stack, w_stack, w_stack, gates, gates, gates)


def _hgrn_levels(c):
    lv, m = [], c // 2
    while m >= DIAG:
        lv.append(m)
        m //= 2
    return tuple(lv)


def _hgrn_select(c):
    tri = np.arange(c)[None, :] <= np.arange(c)[:, None]
    return jnp.asarray(tri.astype(np.float32), dtype=BF16)


def _hgrn_kernel(*refs, c, n_chunks, heads, has_init):
    if has_init:
        q_ref, f_ref, i_ref, g_ref, lbc_ref, gn_ref, sel_ref, s0_ref, ya_ref, s_ref, bpad, kpad = refs
    else:
        q_ref, f_ref, i_ref, g_ref, lbc_ref, gn_ref, sel_ref, ya_ref, s_ref, bpad, kpad = refs

    @pl.when(pl.program_id(1) == 0)
    def _():
        s_ref[...] = s0_ref[...] if has_init else jnp.zeros(s_ref.shape, F32)

    _hgrn_chunks(q_ref, f_ref, i_ref, g_ref, lbc_ref, gn_ref, sel_ref, ya_ref, s_ref, bpad, kpad,
                 c=c, n_chunks=n_chunks, heads=heads, unroll=False)


def _hgrn_chunks(q_ref, f_ref, i_ref, g_ref, lbc_ref, gn_ref, sel_ref, ya_ref, s_ref, bpad, kpad,
                 *, c, n_chunks, heads, unroll, fillers=()):
    levels = _hgrn_levels(c)
    nl = len(levels)
    width = heads * A_DK
    head_lanes = [slice(hh * A_DK, (hh + 1) * A_DK) for hh in range(heads)]

    bpad[0:DIAG, :] = jnp.zeros((DIAG, width), F32)
    kpad[0:DIAG, :] = jnp.zeros((DIAG, width), F32)

    sel = sel_ref[...]
    log_lb = lbc_ref[0:1, :]
    log1m_lb = lbc_ref[1:2, :]
    one_m_lb = lbc_ref[2:3, :]
    row = lax.broadcasted_iota(jnp.int32, (c, width), 0)
    rr = lax.broadcasted_iota(jnp.int32, (c, c), 0)
    cc = lax.broadcasted_iota(jnp.int32, (c, c), 1)
    upper = [(row & m) != 0 for m in levels]
    same = [(rr ^ cc) < 2 * m for m in levels]
    dmat = jnp.where(((rr ^ cc) < DIAG) & (cc <= rr), rr - cc, -1)
    nt_dims = (((1,), (1,)), ((), ()))
    tn_dims = (((0,), (0,)), ((), ()))

    def chunk(ci):
        rows = slice(ci * c, (ci + 1) * c) if unroll else pl.ds(pl.multiple_of(ci * c, c), c)
        z = f_ref[rows, :]
        aq = q_ref[rows, :]

        log_sig = jnp.minimum(z, 0.0) - jnp.log(1.0 + jnp.exp(-jnp.abs(z)))
        bb = log1m_lb + log_sig
        log_f = jnp.maximum(log_lb, bb) + jnp.log(1.0 + jnp.exp(-jnp.abs(log_lb - bb)))
        k = one_m_lb / (1.0 + jnp.exp(z))
        q = _silu(aq)

        b = _dot3(sel, log_f * LOG2E)
        bpad[DIAG:DIAG + c, :] = b
        kpad[DIAG:DIAG + c, :] = k
        b_tot = bpad[DIAG + c - 1:DIAG + c, :]
        vb = i_ref[rows, :].astype(BF16)
        q_in = (q * jnp.exp2(b)).astype(BF16)
        k_out = (k * jnp.exp2(b_tot - b)).astype(BF16)
        yield

        att = [jnp.zeros((c, c), F32) for _ in range(heads)]
        for li in range(nl):
            m = levels[li]
            ref_b = jnp.concatenate(
                [jnp.broadcast_to(bpad[DIAG + g0 + m - 1:DIAG + g0 + m, :], (2 * m, width))
                 for g0 in range(0, c, 2 * m)], axis=0)
            e = jnp.exp2(-jnp.abs(b - ref_b))
            qs = jnp.where(upper[li], q * e, 0.0).astype(BF16)
            ks = jnp.where(upper[li], 0.0, k * e).astype(BF16)
            for hh, lanes in enumerate(head_lanes):
                a_l = lax.dot_general(qs[:, lanes], ks[:, lanes], nt_dims, preferred_element_type=F32)
                att[hh] = att[hh] + jnp.where(same[li], a_l, 0.0)
        yield

        for dlt in range(DIAG):
            if dlt == 0:
                w = q * k
            else:
                ksh = kpad[DIAG - dlt:DIAG - dlt + c, :]
                bsh = bpad[DIAG - dlt:DIAG - dlt + c, :]
                w = q * ksh * jnp.exp2(b - bsh)
            for hh, lanes in enumerate(head_lanes):
                col = jnp.sum(w[:, lanes], axis=-1, keepdims=True)
                att[hh] = jnp.where(dmat == dlt, col, att[hh])
            if dlt in (DIAG // 2 - 1, DIAG - 1):
                yield

        decay_row = jnp.exp2(b_tot)
        outs = []
        for hh, lanes in enumerate(head_lanes):
            s_prev = s_ref[0, hh]
            o = jnp.dot(q_in[:, lanes], s_prev.astype(BF16), preferred_element_type=F32)
            o = o + jnp.dot(att[hh].astype(BF16), vb[:, lanes], preferred_element_type=F32)
            decay_col = jnp.broadcast_to(decay_row[:, lanes], (A_DK, A_DK)).T
            s_ref[0, hh] = decay_col * s_prev + lax.dot_general(
                k_out[:, lanes], vb[:, lanes], tn_dims, preferred_element_type=F32)
            outs.append(o * lax.rsqrt(jnp.mean(o * o, axis=-1, keepdims=True) + NORM_EPS))
        y = jnp.concatenate(outs, axis=1) * gn_ref[...] * _silu(g_ref[rows, :])
        ya_ref[rows, :] = y.astype(ya_ref.dtype)
        yield

    if not unroll:
        def body(ci, carry):
            for _ in chunk(ci):
                pass
            return carry

        lax.fori_loop(0, n_chunks, body, 0)
        return

    n_phases = 5
    _emit_interleaved(fillers, [functools.partial(next, gen, None)
                                for gen in map(chunk, range(n_chunks)) for _ in range(n_phases)])


def _hgrn_specs(p, lbc_stack, layer, gn, heads, t_blk, row_block):
    aw = heads * A_DK
    c = min(t_blk, SCAN_CHUNK)
    sel = _hgrn_select(c)

    def col_spec(seg):
        return pl.BlockSpec((t_blk, aw), lambda *g: (row_block(*g), seg))

    in_specs = [col_spec(0), col_spec(1), col_spec(2), col_spec(3),
                pl.BlockSpec((None, SUBLANES, aw), lambda *g: (layer, 0, 0)),
                pl.BlockSpec((1, aw), lambda *g: (0, 0)),
                pl.BlockSpec(sel.shape, lambda *g: (0, 0))]
    args = [p, p, p, p, lbc_stack, gn.reshape(1, aw), sel]
    scratch = [pltpu.VMEM((c + DIAG, aw), F32), pltpu.VMEM((c + DIAG, aw), F32)]
    return c, in_specs, args, pl.BlockSpec((t_blk, aw), lambda *g: (row_block(*g), 0)), scratch


def _mlp_up_hgrn_kernel(x_ref, g_ref, sc_ref, sh_ref, w_ref, q_ref, f_ref, i_ref, gg_ref, lbc_ref, gn_ref, sel_ref,
                        o_ref, ya_ref, s_ref, h_scr, acc_scr, bpad, kpad,
                        *, nseg, c, n_chunks, heads, steps_per_seq):
    j = pl.program_id(1)
    step = pl.program_id(0) * pl.num_programs(1) + j

    @pl.when(j == 0)
    def _():
        _norm_mod_rows(x_ref, g_ref, sc_ref, sh_ref, h_scr, nseg)

    @pl.when(step % steps_per_seq == 0)
    def _():
        s_ref[...] = jnp.zeros(s_ref.shape, F32)

    kc = h_scr.shape[1] // UP_K_SPLIT

    def sub_dot(c0, ki):
        cols = slice(c0, c0 + EPILOGUE_COLS)
        krows = slice(ki * kc, (ki + 1) * kc)
        part = jnp.dot(h_scr[:, krows], w_ref[krows, cols], preferred_element_type=F32)
        if UP_K_SPLIT == 1:
            acc = part
        elif ki == 0:
            acc_scr[...] = part
            return
        elif ki < UP_K_SPLIT - 1:
            acc_scr[...] += part
            return
        else:
            acc = acc_scr[...] + part
        y = jnp.maximum(acc, 0.0)
        o_ref[:, cols] = (y * y).astype(o_ref.dtype)

    fillers = [functools.partial(sub_dot, c0, ki)
               for c0 in range(0, w_ref.shape[1], EPILOGUE_COLS) for ki in range(UP_K_SPLIT)]
    _hgrn_chunks(q_ref, f_ref, i_ref, gg_ref, lbc_ref, gn_ref, sel_ref, ya_ref, s_ref, bpad, kpad,
                 c=c, n_chunks=n_chunks, heads=heads, unroll=True, fillers=fillers)


def _mlp_up_hgrn(x, g, mod6, w_stack, layer, seq_len, p_other, lbc_stack, gn, heads):
    n, d = x.shape
    m = w_stack.shape[2]
    tn = COL_TILE
    tm, nseg, tps = _row_tiling(n, seq_len, ROW_TILE)
    ni, nj = n // tm, m // tn
    n_other = p_other.shape[0]
    t_step = n_other // (ni * nj)
    assert t_step * ni * nj == n_other and seq_len % t_step == 0 and t_step % SUBLANES == 0
    steps_per_seq = seq_len // t_step
    aw = heads * A_DK
    c, h_in_specs, h_args, ya_spec, h_scratch = _hgrn_specs(
        p_other, lbc_stack, layer, gn, heads, t_step, lambda i, j: i * nj + j)

    def mod_spec(k):
        return pl.BlockSpec((None, nseg, 1, d), lambda i, j: (k, i // tps, 0, 0))

    state_spec = pl.BlockSpec((1, heads, A_DK, A_DK), lambda i, j: ((i * nj + j) // steps_per_seq, 0, 0, 0))
    return pl.pallas_call(
        functools.partial(_mlp_up_hgrn_kernel, nseg=nseg, c=c, n_chunks=t_step // c, heads=heads,
                          steps_per_seq=steps_per_seq),
        grid=(ni, nj),
        in_specs=[
            pl.BlockSpec((tm, d), lambda i, j: (i, 0)),
            pl.BlockSpec((1, d), lambda i, j: (0, 0)),
            mod_spec(4),
            mod_spec(3),
            pl.BlockSpec((None, d, tn), lambda i, j: (layer, 0, j)),
        ] + h_in_specs,
        out_specs=[pl.BlockSpec((tm, tn), lambda i, j: (i, j)), ya_spec, state_spec],
        out_shape=[jax.ShapeDtypeStruct((n, m), BF16),
                   jax.ShapeDtypeStruct((n_other, aw), BF16),
                   jax.ShapeDtypeStruct((n_other // seq_len, heads, A_DK, A_DK), F32)],
        scratch_shapes=[pltpu.VMEM((tm, d), BF16), pltpu.VMEM((tm, EPILOGUE_COLS), F32)] + h_scratch,
        compiler_params=_cparams(("arbitrary", "arbitrary")),
        name="mlp_up_hgrn2",
    )(x, g.reshape(1, d), mod6, mod6, w_stack, *h_args)


def _hgrn(p, lbc_stack, layer, gn, s0, bsz, seq_len, heads):
    c = min(seq_len, SCAN_CHUNK)
    t_blk = min(seq_len, 256)
    nt = seq_len // t_blk
    aw = heads * A_DK

    def col_spec(seg):
        return pl.BlockSpec((t_blk, aw), lambda b, t: (b * nt + t, seg))

    sel = _hgrn_select(c)
    in_specs = [col_spec(0), col_spec(1), col_spec(2), col_spec(3),
                pl.BlockSpec((None, SUBLANES, aw), lambda b, t: (layer, 0, 0)),
                pl.BlockSpec((1, aw), lambda b, t: (0, 0)),
                pl.BlockSpec(sel.shape, lambda b, t: (0, 0))]
    args = [p, p, p, p, lbc_stack, gn.reshape(1, aw), sel]
    state_spec = pl.BlockSpec((1, heads, A_DK, A_DK), lambda b, t: (b, 0, 0, 0))
    if s0 is not None:
        in_specs.append(pl.BlockSpec((None, 1, heads, A_DK, A_DK), lambda b, t: (layer, b, 0, 0, 0)))
        args.append(s0)
    return pl.pallas_call(
        functools.partial(_hgrn_kernel, c=c, n_chunks=t_blk // c, heads=heads, has_init=s0 is not None),
        grid=(bsz, nt),
        in_specs=in_specs,
        out_specs=[pl.BlockSpec((t_blk, aw), lambda b, t: (b * nt + t, 0)), state_spec],
        out_shape=[jax.ShapeDtypeStruct((bsz * seq_len, aw), BF16),
                   jax.ShapeDtypeStruct((bsz, heads, A_DK, A_DK), F32)],
        scratch_shapes=[pltpu.VMEM((c + DIAG, aw), F32), pltpu.VMEM((c + DIAG, aw), F32)],
        compiler_params=_cparams(("arbitrary", "arbitrary")),
        name="hgrn2",
    )(*args)


def _ssd_kernel(*refs, t, bw, has_init):
    n_in = SSD_N_INPUTS
    s0_ref, c0_ref = (refs[n_in], refs[n_in + 1]) if has_init else (None, None)
    rest = refs[n_in + 2:] if has_init else refs[n_in:]
    yb_ref, s_out_ref, conv_out_ref, xpad, st = rest
    ti = pl.program_id(1)
    _ssd_reset(s0_ref, c0_ref, xpad, st, ti == 0)
    for _ in _ssd_phases(*refs[:n_in], yb_ref, xpad, st, t=t, bw=bw):
        pass
    _ssd_flush(conv_out_ref, s_out_ref, xpad, st, ti == pl.num_programs(1) - 1)


SSD_N_INPUTS = 12
SSD_PHASES = 9


def _ssd_phases(z_ref, xs_ref, bc_ref, dt_ref, cw_ref, cb_ref, dtb_ref, alog_ref, dsk_ref, gn_ref, exp_ref, sel_ref,
                yb_ref, xpad, st, *, t, bw):
    gw = bw // B_GROUPS
    n_bc = B_GROUPS * B_DSTATE
    pad = SUBLANES

    cur = jnp.concatenate([xs_ref[...], bc_ref[...]], axis=1)
    prev = xpad[...]
    row8 = lax.broadcasted_iota(jnp.int32, (pad, cur.shape[1]), 0)
    conv = cb_ref[...]
    for j in range(CONV_W):
        s = CONV_W - 1 - j
        if s == 0:
            tap = cur
        else:
            rolled = pltpu.roll(cur, s, 0)
            head = jnp.where(row8 < s, pltpu.roll(prev, s, 0), rolled[0:pad])
            tap = jnp.concatenate([head, rolled[pad:]], axis=0)
        conv = conv + tap * cw_ref[j:j + 1, :]
    xbc = _silu(conv)
    x = xbc[:, 0:bw]
    yield

    dt = _softplus(dt_ref[...] + dtb_ref[...])
    a = dt * (-jnp.exp(alog_ref[...]))
    cs = _dot3(sel_ref[...], a)
    a_cum = cs[0:t]
    a_tot = cs[t:2 * t]
    ex = _dot3_rhs(jnp.concatenate([dt, a_cum, a_tot - a_cum], axis=0), exp_ref[...])
    dt_e = ex[0:t]
    acum_e = ex[t:2 * t]
    dec_e = ex[2 * t:3 * t]
    atot_e = acum_e[t - 1:t, :]

    xdt = x * dt_e
    xw = (xdt * jnp.exp(dec_e)).astype(BF16)
    xdt_b = xdt.astype(BF16)
    a_cum_t = a_cum.T
    rr = lax.broadcasted_iota(jnp.int32, (t, t), 0)
    cc = lax.broadcasted_iota(jnp.int32, (t, t), 1)
    causal = cc <= rr
    lane = lax.broadcasted_iota(jnp.int32, (t, LANES), 1)
    heads_per_group = gw // B_HEADDIM
    pairs_per_group = gw // LANES
    yield

    y_groups = []
    for g in range(B_GROUPS):
        bg = xbc[:, bw + g * B_DSTATE:bw + (g + 1) * B_DSTATE].astype(BF16)
        cg = xbc[:, bw + n_bc + g * B_DSTATE:bw + n_bc + (g + 1) * B_DSTATE].astype(BF16)
        gs = slice(g * gw, (g + 1) * gw)
        st_g = st[:, gs]
        scores = lax.dot_general(cg, bg, (((1,), (1,)), ((), ())), preferred_element_type=F32)
        y_off = jnp.dot(cg, st_g.astype(BF16), preferred_element_type=F32) * jnp.exp(acum_e[:, gs])
        st[:, gs] = jnp.exp(atot_e[:, gs]) * st_g + lax.dot_general(
            bg, xw[:, gs], (((0,), (0,)), ((), ())), preferred_element_type=F32)
        yield
        y_pairs = []
        for pr in range(pairs_per_group):
            cols = slice(g * gw + pr * LANES, g * gw + (pr + 1) * LANES)
            xp = xdt_b[:, cols]
            ms, xs_blocks = [], []
            for half in range(LANES // B_HEADDIM):
                h = g * heads_per_group + pr * (LANES // B_HEADDIM) + half
                diff = a_cum[:, h:h + 1] - a_cum_t[h:h + 1, :]
                lmat = jnp.where(causal, jnp.exp(jnp.where(causal, diff, 0.0)), 0.0)
                ms.append((scores * lmat).astype(BF16))
                in_half = (lane // B_HEADDIM) == half
                xs_blocks.append(jnp.where(in_half, xp, jnp.zeros_like(xp)))
            y_pairs.append(jnp.dot(jnp.concatenate(ms, axis=1), jnp.concatenate(xs_blocks, axis=0),
                                   preferred_element_type=F32))
            if pr % 2 == 1:
                yield
        y_groups.append(jnp.concatenate(y_pairs, axis=1) + y_off)
    y = jnp.concatenate(y_groups, axis=1) + dsk_ref[...] * x
    y = y * _silu(z_ref[...])
    outs = []
    for g in range(B_GROUPS):
        yg = y[:, g * gw:(g + 1) * gw]
        outs.append(yg * lax.rsqrt(jnp.mean(yg * yg, axis=-1, keepdims=True) + NORM_EPS))
    yb_ref[...] = (jnp.concatenate(outs, axis=1) * gn_ref[...]).astype(yb_ref.dtype)

    xpad[...] = cur[t - pad:t, :]
    yield


def _ssd_reset(s0_ref, c0_ref, xpad, st, first):
    @pl.when(first)
    def _():
        if s0_ref is not None:
            xpad[...] = c0_ref[0]
            st[...] = s0_ref[0].reshape(st.shape[1], st.shape[0]).T
        else:
            xpad[...] = jnp.zeros(xpad.shape, F32)
            st[...] = jnp.zeros(st.shape, F32)


def _ssd_flush(conv_out_ref, s_out_ref, xpad, st, last):
    @pl.when(last)
    def _():
        conv_out_ref[0] = xpad[...]
        s_out_ref[0] = st[...].T.reshape(s_out_ref.shape[1:])


def _ssd_operands(p, cols, prm, t, n_rows, n_seq, row_block, seq_block):
    bw = prm["bw"]
    heads = bw // B_HEADDIM
    n_bc = B_GROUPS * B_DSTATE
    cdim = bw + 2 * n_bc
    tri = np.arange(t)[None, :] <= np.arange(t)[:, None]
    sel = jnp.asarray(np.concatenate([tri, np.ones((t, t), bool)], 0).astype(np.float32), dtype=BF16)
    expand = np.zeros((LANES, bw), np.float32)
    expand[np.arange(bw) // B_HEADDIM, np.arange(bw)] = 1.0
    expand = jnp.asarray(expand, dtype=BF16)

    def blk(width, off):
        return pl.BlockSpec((t, width), lambda *g: (row_block(*g), off // width))

    def full(shape):
        return pl.BlockSpec(shape, lambda *g: (0,) * len(shape))

    in_specs = [blk(bw, cols["z"]), blk(bw, cols["xs"]), blk(2 * n_bc, cols["bc"]), blk(LANES, cols["dt"]),
                full((CONV_W, cdim)), full((1, cdim)), full((1, LANES)), full((1, LANES)),
                full((1, bw)), full((1, bw)), full(expand.shape), full(sel.shape)]
    args = [p, p, p, p, prm["conv_w"], prm["conv_b"], prm["dt_bias"], prm["a_log"], prm["d_skip"], prm["gn"],
            expand, sel]
    assert len(args) == SSD_N_INPUTS
    out_specs = [pl.BlockSpec((t, bw), lambda *g: (row_block(*g), 0)),
                 pl.BlockSpec((1, heads, B_HEADDIM, B_DSTATE), lambda *g: (seq_block(*g), 0, 0, 0)),
                 pl.BlockSpec((1, SUBLANES, cdim), lambda *g: (seq_block(*g), 0, 0))]
    out_shape = [jax.ShapeDtypeStruct((n_rows, bw), BF16),
                 jax.ShapeDtypeStruct((n_seq, heads, B_HEADDIM, B_DSTATE), F32),
                 jax.ShapeDtypeStruct((n_seq, SUBLANES, cdim), F32)]
    scratch = [pltpu.VMEM((SUBLANES, cdim), F32), pltpu.VMEM((B_DSTATE, bw), F32)]
    return in_specs, args, out_specs, out_shape, scratch


def _ssd(p, cols, prm, s0, c0, bsz, seq_len):
    bw = prm["bw"]
    heads = bw // B_HEADDIM
    cdim = bw + 2 * B_GROUPS * B_DSTATE
    t = min(seq_len, SSD_ROWS)
    nt = seq_len // t
    in_specs, args, out_specs, out_shape, scratch = _ssd_operands(
        p, cols, prm, t, bsz * seq_len, bsz, lambda b, ti: b * nt + ti, lambda b, ti: b)
    if s0 is not None:
        layer = prm["layer"]
        in_specs += [pl.BlockSpec((None, 1, heads, B_HEADDIM, B_DSTATE), lambda b, ti: (layer, b, 0, 0, 0)),
                     pl.BlockSpec((None, 1, SUBLANES, cdim), lambda b, ti: (layer, b, 0, 0))]
        args += [s0, c0]
    return pl.pallas_call(
        functools.partial(_ssd_kernel, t=t, bw=bw, has_init=s0 is not None),
        grid=(bsz, nt),
        in_specs=in_specs,
        out_specs=out_specs,
        out_shape=out_shape,
        scratch_shapes=scratch,
        compiler_params=_cparams(("arbitrary", "arbitrary")),
        name="ssd",
    )(*args)


def _down_sub_dots(a_ref, w_ref, x_ref, gate_ref, o_ref, acc_scr, nseg):
    kc = a_ref.shape[1] // DOWN_K_SPLIT

    def sub_dot(ki):
        rows = slice(ki * kc, (ki + 1) * kc)
        part = jnp.dot(a_ref[:, rows], w_ref[rows, :], preferred_element_type=F32)
        if ki == 0:
            acc_scr[...] = part
        elif ki < DOWN_K_SPLIT - 1:
            acc_scr[...] += part
        else:
            x = x_ref[...]
            acc = acc_scr[...] + part
            o_ref[...] = (_per_seq(x, nseg) + gate_ref[...] * _per_seq(acc, nseg)).reshape(x.shape)

    return [functools.partial(sub_dot, ki) for ki in range(DOWN_K_SPLIT)]


def _emit_interleaved(fillers, phases):
    per_filler = -(-len(phases) // max(len(fillers), 1))
    pos = 0
    for fill in fillers:
        fill()
        for ph in phases[pos:pos + per_filler]:
            ph()
        pos += per_filler
    for ph in phases[pos:]:
        ph()


def _mlp_down_ssd_kernel(a_ref, w_ref, x_ref, gate_ref, *refs, nseg, t, bw, steps_per_seq):
    ssd_in = refs[:SSD_N_INPUTS]
    o_ref, yb_ref, s_out_ref, conv_out_ref, acc_scr, xpad, st = refs[SSD_N_INPUTS:]
    step = pl.program_id(0) * pl.num_programs(1) + pl.program_id(1)
    pos = step % steps_per_seq
    _ssd_reset(None, None, xpad, st, pos == 0)
    gen = _ssd_phases(*ssd_in, yb_ref, xpad, st, t=t, bw=bw)
    phases = [functools.partial(next, gen, None) for _ in range(SSD_PHASES)]
    _emit_interleaved(_down_sub_dots(a_ref, w_ref, x_ref, gate_ref, o_ref, acc_scr, nseg), phases)
    _ssd_flush(conv_out_ref, s_out_ref, xpad, st, pos == steps_per_seq - 1)


def _mlp_down_ssd(a, w_stack, layer, x, mod6, k_gate, seq_len, p_other, cols, prm):
    n, kdim = a.shape
    d = w_stack.shape[2]
    tn = DOWN_TILE
    tm, nseg, tps = _row_tiling(n, seq_len, ROW_TILE)
    ni, nj = n // tm, d // tn
    n_other = p_other.shape[0]
    t = n_other // (ni * nj)
    assert t * ni * nj == n_other and seq_len % t == 0 and t % SUBLANES == 0 and kdim % DOWN_K_SPLIT == 0
    steps_per_seq = seq_len // t
    s_in_specs, s_args, s_out_specs, s_out_shape, s_scratch = _ssd_operands(
        p_other, cols, prm, t, n_other, n_other // seq_len,
        lambda i, j: i * nj + j, lambda i, j: (i * nj + j) // steps_per_seq)
    return pl.pallas_call(
        functools.partial(_mlp_down_ssd_kernel, nseg=nseg, t=t, bw=prm["bw"], steps_per_seq=steps_per_seq),
        grid=(ni, nj),
        in_specs=[
            pl.BlockSpec((tm, kdim), lambda i, j: (i, 0)),
            pl.BlockSpec((None, kdim, tn), lambda i, j: (layer, 0, j)),
            pl.BlockSpec((tm, tn), lambda i, j: (i, j)),
            pl.BlockSpec((None, nseg, 1, tn), lambda i, j: (k_gate, i // tps, 0, j)),
        ] + s_in_specs,
        out_specs=[pl.BlockSpec((tm, tn), lambda i, j: (i, j))] + s_out_specs,
        out_shape=[jax.ShapeDtypeStruct((n, d), F32)] + s_out_shape,
        scratch_shapes=[pltpu.VMEM((tm, tn), F32)] + s_scratch,
        compiler_params=_cparams(("arbitrary", "arbitrary")),
        name="mlp_down_ssd",
    )(a, w_stack, x, mod6, *s_args)


def _cmlp_kernel(u_ref, v_ref, lng_ref, lnb_ref, ws_ref, bst_ref, *out_refs, t, n_chunks, keep_v):
    yc_ref = out_refs[0]
    cw = u_ref.shape[1] // C_GROUPS
    rr = lax.broadcasted_iota(jnp.int32, (t, t), 0)
    cc = lax.broadcasted_iota(jnp.int32, (t, t), 1)
    wts = [jnp.where(cc <= rr, ws_ref[g, 0:t, 0:t], 0.0).astype(BF16) for g in range(C_GROUPS)]
    for ci in range(n_chunks):
        rows = slice(ci * t, (ci + 1) * t)
        u = _gelu(u_ref[rows, :])
        gv = _gelu(v_ref[rows, :])
        mu = jnp.mean(gv, axis=-1, keepdims=True)
        dv = gv - mu
        var = jnp.mean(dv * dv, axis=-1, keepdims=True)
        v = dv * lax.rsqrt(var + NORM_EPS) * lng_ref[...] + lnb_ref[...]
        if keep_v:
            out_refs[1][rows, :] = v
        vb = v.astype(BF16)
        for g in range(C_GROUPS):
            lanes = slice(g * cw, (g + 1) * cw)
            mixed = jnp.dot(wts[g], vb[:, lanes], preferred_element_type=F32) + bst_ref[0:t, g:g + 1]
            yc_ref[rows, lanes] = (u[:, lanes] * mixed).astype(yc_ref.dtype)


def _cmlp(p, col_u, col_v, prm, bsz, seq_len, keep_v):
    cw = prm["cw"]
    t = min(seq_len, CMLP_CHUNK)
    t_blk = min(seq_len, 4 * CMLP_CHUNK)
    n = bsz * seq_len

    def full(shape):
        return pl.BlockSpec(shape, lambda i: (0,) * len(shape))

    out_specs = [pl.BlockSpec((t_blk, cw), lambda i: (i, 0))]
    out_shape = [jax.ShapeDtypeStruct((n, cw), BF16)]
    if keep_v:
        out_specs.append(pl.BlockSpec((t_blk, cw), lambda i: (i, 0)))
        out_shape.append(jax.ShapeDtypeStruct((n, cw), F32))
    return pl.pallas_call(
        functools.partial(_cmlp_kernel, t=t, n_chunks=t_blk // t, keep_v=keep_v),
        grid=(n // t_blk,),
        in_specs=[pl.BlockSpec((t_blk, cw), lambda i: (i, col_u // cw)),
                  pl.BlockSpec((t_blk, cw), lambda i: (i, col_v // cw)),
                  full((1, cw)), full((1, cw)),
                  full((C_GROUPS, CMLP_CHUNK, CMLP_CHUNK)), full((CMLP_CHUNK, C_GROUPS))],
        out_specs=out_specs,
        out_shape=out_shape,
        compiler_params=_cparams(("arbitrary",)),
        name="cmlp",
    )(p, p, prm["ln_g"], prm["ln_b"], prm["ws"], prm["bs_t"])


def _run_trunk(x3, mod, st_hgrn, st_ssm, st_conv, keep_v, w):
    bsz, seq_len, d = x3.shape
    depth = mod.shape[0]
    aw = bw = cw = d // 2
    heads_a = aw // A_DK
    x = x3.reshape(bsz * seq_len, d)
    col = w["cols"]
    hgrn_out, ssm_out, conv_out, v_out = [], [], [], []
    for l in range(depth):
        mod6 = mod[l].reshape(bsz, 6, 1, d).transpose(1, 0, 2, 3)
        p, gates = _in_proj(x, w["norm1_g"][l], mod6, w["w_in"], l, seq_len, col["gate"])

        y_a, s_h = _hgrn(p, w["lbc"], l, w["hgrn_onorm_g"][l], st_hgrn, bsz, seq_len, heads_a)
        ssd_prm = dict(bw=bw, layer=l, conv_w=w["ssm_conv_w"][l], conv_b=w["ssm_conv_b"][l][None],
                       dt_bias=w["dt_bias_pad"][l][None], a_log=w["a_log_pad"][l][None],
                       d_skip=w["d_skip"][l][None], gn=w["ssm_onorm_g"][l][None])
        y_b, s_s, conv_tail = _ssd(p, col, ssd_prm, st_ssm, st_conv, bsz, seq_len)
        cm_prm = dict(cw=cw, ln_g=w["cmlp_ln_g"][l][None], ln_b=w["cmlp_ln_b"][l][None],
                      ws=w["cmlp_ws"][l], bs_t=w["cmlp_bs"][l].T)
        c_res = _cmlp(p, col["u"], col["v"], cm_prm, bsz, seq_len, keep_v)
        merged = _merge(y_a, y_b, c_res[0], w["w_branch"], l, gates, d)
        x = _proj_residual(merged, w["w_out"], l, x, mod6, 2, seq_len, "out_proj")
        hid = _mlp_up(x, w["norm2_g"][l], mod6, w["w_up"], l, seq_len)
        x = _proj_residual(hid, w["w_down"], l, x, mod6, 5, seq_len, "mlp_down")

        hgrn_out.append(s_h)
        ssm_out.append(s_s)
        conv_out.append(conv_tail[:, SUBLANES - (CONV_W - 1):, :])
        if keep_v:
            v_out.append(c_res[1].reshape(bsz, seq_len, cw))
    y = _final_norm(x, w["final_g"]).reshape(bsz, seq_len, d)
    return (y, jnp.stack(hgrn_out), jnp.stack(ssm_out), jnp.stack(conv_out),
            jnp.stack(v_out) if keep_v else None)


def _mlp_down_hgrn_kernel(a_ref, w_ref, x_ref, gate_ref, q_ref, f_ref, i_ref, gg_ref, lbc_ref, gn_ref, sel_ref,
                          o_ref, ya_ref, s_ref, acc_scr, bpad, kpad, *, nseg, c, n_chunks, heads, steps_per_seq):
    step = pl.program_id(0) * pl.num_programs(1) + pl.program_id(1)

    @pl.when(step % steps_per_seq == 0)
    def _():
        s_ref[...] = jnp.zeros(s_ref.shape, F32)

    fillers = _down_sub_dots(a_ref, w_ref, x_ref, gate_ref, o_ref, acc_scr, nseg)
    _hgrn_chunks(q_ref, f_ref, i_ref, gg_ref, lbc_ref, gn_ref, sel_ref, ya_ref, s_ref, bpad, kpad,
                 c=c, n_chunks=n_chunks, heads=heads, unroll=True, fillers=fillers)


def _mlp_down_hgrn(a, w_stack, layer, x, mod6, k_gate, seq_len, p_other, lbc_stack, lbc_layer, gn, heads):
    n, kdim = a.shape
    d = w_stack.shape[2]
    tn = DOWN_TILE
    tm, nseg, tps = _row_tiling(n, seq_len, ROW_TILE)
    ni, nj = n // tm, d // tn
    n_other = p_other.shape[0]
    t_step = n_other // (ni * nj)
    assert t_step * ni * nj == n_other and seq_len % t_step == 0 and t_step % SUBLANES == 0
    assert kdim % DOWN_K_SPLIT == 0
    steps_per_seq = seq_len // t_step
    aw = heads * A_DK
    c, h_in_specs, h_args, ya_spec, h_scratch = _hgrn_specs(
        p_other, lbc_stack, lbc_layer, gn, heads, t_step, lambda i, j: i * nj + j)
    state_spec = pl.BlockSpec((1, heads, A_DK, A_DK), lambda i, j: ((i * nj + j) // steps_per_seq, 0, 0, 0))
    return pl.pallas_call(
        functools.partial(_mlp_down_hgrn_kernel, nseg=nseg, c=c, n_chunks=t_step // c, heads=heads,
                          steps_per_seq=steps_per_seq),
        grid=(ni, nj),
        in_specs=[
            pl.BlockSpec((tm, kdim), lambda i, j: (i, 0)),
            pl.BlockSpec((None, kdim, tn), lambda i, j: (layer, 0, j)),
            pl.BlockSpec((tm, tn), lambda i, j: (i, j)),
            pl.BlockSpec((None, nseg, 1, tn), lambda i, j: (k_gate, i // tps, 0, j)),
        ] + h_in_specs,
        out_specs=[pl.BlockSpec((tm, tn), lambda i, j: (i, j)), ya_spec, state_spec],
        out_shape=[jax.ShapeDtypeStruct((n, d), F32),
                   jax.ShapeDtypeStruct((n_other, aw), BF16),
                   jax.ShapeDtypeStruct((n_other // seq_len, heads, A_DK, A_DK), F32)],
        scratch_shapes=[pltpu.VMEM((tm, tn), F32)] + h_scratch,
        compiler_params=_cparams(("arbitrary", "arbitrary")),
        name="mlp_down_hgrn2",
    )(a, w_stack, x, mod6, *h_args)


def _run_trunk_halves(x3, mod, w):
    bsz, seq_len, d = x3.shape
    depth = mod.shape[0]
    hb = bsz // 2
    aw = bw = cw = d // 2
    heads_a = aw // A_DK
    col = w["cols"]
    n_half = hb * seq_len
    x_full = x3.reshape(bsz * seq_len, d)
    xs = [x_full, x_full]
    row0 = [0, n_half]
    mods = [mod[:, :hb], mod[:, hb:]]
    hgrn_out, ssm_out, conv_out = [], [], []
    pending = None
    for l in range(depth):
        mod6 = [m[l].reshape(hb, 6, 1, d).transpose(1, 0, 2, 3) for m in mods]
        ssd_prm = dict(bw=bw, layer=l, conv_w=w["ssm_conv_w"][l], conv_b=w["ssm_conv_b"][l][None],
                       dt_bias=w["dt_bias_pad"][l][None], a_log=w["a_log_pad"][l][None],
                       d_skip=w["d_skip"][l][None], gn=w["ssm_onorm_g"][l][None])
        cm_prm = dict(cw=cw, ln_g=w["cmlp_ln_g"][l][None], ln_b=w["cmlp_ln_b"][l][None],
                      ws=w["cmlp_ws"][l], bs_t=w["cmlp_bs"][l].T)
        gn_a = w["hgrn_onorm_g"][l]

        def mix_merge(h, p, gates, y_a, ssd_res=None):
            y_b, s_s, conv_tail = ssd_res or _ssd(p, col, ssd_prm, None, None, hb, seq_len)
            y_c = _cmlp(p, col["u"], col["v"], cm_prm, hb, seq_len, False)[0]
            merged = _merge(y_a, y_b, y_c, w["w_branch"], l, gates, d)
            x_new = _proj_residual(merged, w["w_out"], l, xs[h], mod6[h], 2, seq_len, "out_proj", row0[h])
            row0[h] = 0
            return x_new, s_s, conv_tail

        p0, g0 = _in_proj(xs[0], w["norm1_g"][l], mod6[0], w["w_in"], l, seq_len, col["gate"], (row0[0], n_half))
        if pending is None:
            ya0, sh0 = _hgrn(p0, w["lbc"], l, gn_a, None, hb, seq_len, heads_a)
        else:
            hid1, x1_mid, mod6_prev = pending
            xs[1], ya0, sh0 = _mlp_down_hgrn(hid1, w["w_down"], l - 1, x1_mid, mod6_prev, 5, seq_len,
                                              p0, w["lbc"], l, gn_a, heads_a)
        p1, g1 = _in_proj(xs[1], w["norm1_g"][l], mod6[1], w["w_in"], l, seq_len, col["gate"], (row0[1], n_half))
        x0, ss0, ct0 = mix_merge(0, p0, g0, ya0)
        hid0, ya1, sh1 = _mlp_up_hgrn(x0, w["norm2_g"][l], mod6[0], w["w_up"], l, seq_len,
                                      p1, w["lbc"], gn_a, heads_a)
        xs[0], *ssd1 = _mlp_down_ssd(hid0, w["w_down"], l, x0, mod6[0], 5, seq_len, p1, col, ssd_prm)
        x1_mid, ss1, ct1 = mix_merge(1, p1, g1, ya1, tuple(ssd1))
        hid1 = _mlp_up(x1_mid, w["norm2_g"][l], mod6[1], w["w_up"], l, seq_len)
        pending = (hid1, x1_mid, mod6[1])

        hgrn_out.append(jnp.concatenate([sh0, sh1], axis=0))
        ssm_out.append(jnp.concatenate([ss0, ss1], axis=0))
        conv_out.append(jnp.concatenate([ct0, ct1], axis=0)[:, SUBLANES - (CONV_W - 1):, :])
    hid1, x1_mid, mod6_prev = pending
    xs[1] = _proj_residual(hid1, w["w_down"], depth - 1, x1_mid, mod6_prev, 5, seq_len, "mlp_down")
    y = _final_norm_pair(xs[0], xs[1], w["final_g"]).reshape(bsz, seq_len, d)
    return (y, jnp.stack(hgrn_out), jnp.stack(ssm_out), jnp.stack(conv_out), None)


def kernel(x_prompt, x_sample, state_hgrn, state_ssm, state_conv, c_prompt, c_sample, norm1_g, norm2_g,
           w_mod, b_mod, w_in, hgrn_lb, hgrn_onorm_g, ssm_conv_w, ssm_conv_b, ssm_dt_bias, ssm_a_log, ssm_d,
           ssm_onorm_g, cmlp_ln_g, cmlp_ln_b, cmlp_ws, cmlp_bs, w_branch, w_out, w_up, w_down, final_g):
    d = x_prompt.shape[-1]
    depth = w_in.shape[0]
    aw = bw = cw = d // 2
    n_bc = B_GROUPS * B_DSTATE
    heads_b = bw // B_HEADDIM
    assert heads_b <= LANES and bw % LANES == 0 and DT_PAD % LANES == 0

    o_dt = 4 * aw + bw + bw + 2 * n_bc
    o_u = o_dt + heads_b
    w_in_t = jnp.swapaxes(w_in, 1, 2).astype(BF16)
    w_in_r = jnp.concatenate(
        [w_in_t[:, :o_u], jnp.zeros((depth, DT_PAD - heads_b, d), BF16), w_in_t[:, o_u:]], axis=1)
    cols = dict(z=4 * aw, xs=4 * aw + bw, bc=4 * aw + 2 * bw, dt=o_dt)
    cols["u"] = o_dt + DT_PAD
    cols["v"] = cols["u"] + cw
    cols["gate"] = cols["v"] + cw
    assert cols["gate"] % IN_TILE == 0 and w_in_r.shape[1] % IN_TILE == 0

    pad_h = lambda a: jnp.pad(a.astype(F32), ((0, 0), (0, LANES - heads_b)))
    w = dict(
        cols=cols, w_in=w_in_r,
        norm1_g=norm1_g, norm2_g=norm2_g, final_g=final_g,
        lbc=_lb_consts(hgrn_lb), hgrn_onorm_g=hgrn_onorm_g,
        ssm_conv_w=ssm_conv_w, ssm_conv_b=ssm_conv_b,
        dt_bias_pad=pad_h(ssm_dt_bias), a_log_pad=pad_h(ssm_a_log),
        d_skip=jnp.repeat(ssm_d.astype(F32), B_HEADDIM, axis=1), ssm_onorm_g=ssm_onorm_g,
        cmlp_ln_g=cmlp_ln_g, cmlp_ln_b=cmlp_ln_b, cmlp_ws=cmlp_ws, cmlp_bs=cmlp_bs,
        w_branch=w_branch.astype(BF16).reshape(depth, N_BRANCH, aw, d), w_out=w_out.astype(BF16),
        w_up=w_up.astype(BF16), w_down=w_down.astype(BF16),
    )

    nb = x_prompt.shape[0]
    mod = _modulation(jnp.concatenate([c_prompt, c_sample], axis=0), w_mod, b_mod)
    if nb % 2 == 0:
        y_p, hgrn_p, ssm_p, conv_p, _ = _run_trunk_halves(x_prompt, mod[:, :nb], w)
    else:
        y_p, hgrn_p, ssm_p, conv_p, _ = _run_trunk(x_prompt, mod[:, :nb], None, None, None, False, w)
    conv_pad = jnp.pad(state_conv, ((0, 0), (0, 0), (SUBLANES - (CONV_W - 1), 0), (0, 0)))
    y_s, hgrn_s, ssm_s, conv_s, v_s = _run_trunk(x_sample, mod[:, nb:], state_hgrn, state_ssm, conv_pad, True, w)
    return (y_p, y_s, hgrn_p, ssm_p, conv_p, hgrn_s, ssm_s, conv_s, v_s)
```

```python
import functools
import math

import numpy as np
import jax
import jax.numpy as jnp
from jax import lax
from jax.experimental import pallas as pl
from jax.experimental.pallas import tpu as pltpu

F32 = jnp.float32
BF16 = jnp.bfloat16

A_DK = 128
B_HEADDIM = 64
B_GROUPS = 2
B_DSTATE = 128
CONV_W = 4
C_GROUPS = 4
CMLP_CHUNK = 128
N_BRANCH = 3
SCAN_CHUNK = 64
NORM_EPS = 1e-6
LB_FLOOR = 1e-30
LOG2E = 1.4426950408889634

LANES = 128
SUBLANES = 8
VMEM_LIMIT = 56 * 1024 * 1024
DIAG = SUBLANES
DT_PAD = 512
ROW_TILE = 1024
COL_TILE = 1024
IN_TILE = 1536
MERGE_TILE = 512
MERGE_ROW_TILE = 1024
OUT_TILE = 512
OUT_ROW_TILE = 2048
DOWN_TILE = 256
EPILOGUE_COLS = 256
SSD_ROWS = 128
UP_K_SPLIT = 1
DOWN_K_SPLIT = 16
NORM_ROWS = 128


def _cparams(sem):
    return pltpu.CompilerParams(dimension_semantics=sem, vmem_limit_bytes=VMEM_LIMIT)


def _split3(x):
    hi = x.astype(BF16)
    r1 = x - hi.astype(F32)
    mid = r1.astype(BF16)
    lo = (r1 - mid.astype(F32)).astype(BF16)
    return hi, mid, lo


def _dot3(sel, x):
    hi, mid, lo = _split3(x)
    d = lambda p: jnp.dot(sel, p, preferred_element_type=F32)
    return d(hi) + d(mid) + d(lo)


def _dot3_rhs(x, sel):
    hi, mid, lo = _split3(x)
    d = lambda p: jnp.dot(p, sel, preferred_element_type=F32)
    return d(hi) + d(mid) + d(lo)


def _silu(x):
    return x / (1.0 + jnp.exp(-x))


def _softplus(x):
    return jnp.maximum(x, 0.0) + jnp.log1p(jnp.exp(-jnp.abs(x)))


def _gelu(x):
    return 0.5 * x * (1.0 + lax.erf(x * (1.0 / math.sqrt(2.0))))


def _lb_kernel(lb_ref, out_ref, *, depth):
    x = lb_ref[...]
    m = jnp.max(x, axis=0, keepdims=True)
    e = jnp.exp(x - m)
    p = e / jnp.sum(e, axis=0, keepdims=True)
    acc = jnp.zeros_like(p[0:1])
    zeros5 = jnp.zeros((SUBLANES - 3, x.shape[1]), F32)
    for l in range(depth):
        acc = acc + p[l:l + 1]
        lb = acc - p[0:1]
        out_ref[l] = jnp.concatenate(
            [jnp.log(jnp.maximum(lb, LB_FLOOR)), jnp.log1p(-lb), 1.0 - lb, zeros5], axis=0)


def _lb_consts(hgrn_lb):
    depth, aw = hgrn_lb.shape
    return pl.pallas_call(
        functools.partial(_lb_kernel, depth=depth),
        out_shape=jax.ShapeDtypeStruct((depth, SUBLANES, aw), F32),
        name="hgrn_lb",
    )(hgrn_lb.astype(F32))


def _mod_kernel(c_ref, w_ref, b_ref, o_ref):
    cs = _silu(c_ref[...]).astype(BF16)
    o_ref[...] = jnp.dot(cs, w_ref[...].astype(BF16), preferred_element_type=F32) + b_ref[...]


def _modulation(c_all, w_mod, b_mod):
    depth, d, n6 = w_mod.shape
    s = c_all.shape[0]
    tn = 1024
    return pl.pallas_call(
        _mod_kernel,
        grid=(depth, n6 // tn),
        in_specs=[
            pl.BlockSpec((s, d), lambda l, j: (0, 0)),
            pl.BlockSpec((None, d, tn), lambda l, j: (l, 0, j)),
            pl.BlockSpec((None, 1, tn), lambda l, j: (l, 0, j)),
        ],
        out_specs=pl.BlockSpec((None, s, tn), lambda l, j: (l, 0, j)),
        out_shape=jax.ShapeDtypeStruct((depth, s, n6), F32),
        compiler_params=_cparams(("arbitrary", "arbitrary")),
        name="adaln_mod",
    )(c_all, w_mod, b_mod.reshape(depth, 1, n6))


def _row_tiling(n_tok, seq_len, tm_max):
    tm = min(tm_max, n_tok)
    if seq_len >= tm:
        assert seq_len % tm == 0
        return tm, 1, seq_len // tm
    assert tm % seq_len == 0
    return tm, tm // seq_len, 1


def _per_seq(x, nseg):
    tm, d = x.shape
    return x.reshape(nseg, tm // nseg, d)


def _final_norm_kernel(x_ref, g_ref, o_ref):
    x = x_ref[...]
    o_ref[...] = x * lax.rsqrt(jnp.mean(x * x, axis=-1, keepdims=True) + NORM_EPS) * g_ref[...]


def _final_norm(x, g):
    n, d = x.shape
    tm = min(512, n)
    return pl.pallas_call(
        _final_norm_kernel,
        grid=(n // tm,),
        in_specs=[pl.BlockSpec((tm, d), lambda i: (i, 0)), pl.BlockSpec((1, d), lambda i: (0, 0))],
        out_specs=pl.BlockSpec((tm, d), lambda i: (i, 0)),
        out_shape=jax.ShapeDtypeStruct((n, d), F32),
        compiler_params=_cparams(("arbitrary",)),
        name="final_norm",
    )(x, g.reshape(1, d))


def _final_norm_pair_kernel(xa_ref, xb_ref, g_ref, o_ref, *, nt):
    def norm(x):
        return x * lax.rsqrt(jnp.mean(x * x, axis=-1, keepdims=True) + NORM_EPS) * g_ref[...]

    @pl.when(pl.program_id(0) < nt)
    def _():
        o_ref[...] = norm(xa_ref[...])

    @pl.when(pl.program_id(0) >= nt)
    def _():
        o_ref[...] = norm(xb_ref[...])


def _final_norm_pair(xa, xb, g):
    n, d = xa.shape
    tm = min(512, n)
    nt = n // tm
    return pl.pallas_call(
        functools.partial(_final_norm_pair_kernel, nt=nt),
        grid=(2 * nt,),
        in_specs=[pl.BlockSpec((tm, d), lambda i: (jnp.minimum(i, nt - 1), 0)),
                  pl.BlockSpec((tm, d), lambda i: (jnp.maximum(i - nt, 0), 0)),
                  pl.BlockSpec((1, d), lambda i: (0, 0))],
        out_specs=pl.BlockSpec((tm, d), lambda i: (i, 0)),
        out_shape=jax.ShapeDtypeStruct((2 * n, d), F32),
        compiler_params=_cparams(("arbitrary",)),
        name="final_norm",
    )(xa, xb, g.reshape(1, d))


def _norm_mod_rows(x_ref, g_ref, sc_ref, sh_ref, h_scr, nseg):
    tm = x_ref.shape[0]
    seg = tm // nseg
    rb = min(seg, NORM_ROWS)
    g = g_ref[...]

    def body(r, carry):
        rows = pl.ds(pl.multiple_of(r * rb, rb), rb)
        s = r // (seg // rb)
        x = x_ref[rows, :]
        gain = g * (1.0 + sc_ref[s])
        inv = lax.rsqrt(jnp.mean(x * x, axis=-1, keepdims=True) + NORM_EPS)
        h_scr[rows, :] = (x * inv * gain + sh_ref[s]).astype(BF16)
        return carry

    lax.fori_loop(0, tm // rb, body, 0, unroll=2 if (tm // rb) % 2 == 0 else 1)


def _mlp_up_kernel(x_ref, g_ref, sc_ref, sh_ref, w_ref, o_ref, h_scr, *, nseg):
    @pl.when(pl.program_id(1) == 0)
    def _():
        _norm_mod_rows(x_ref, g_ref, sc_ref, sh_ref, h_scr, nseg)

    y = jnp.maximum(jnp.dot(h_scr[...], w_ref[...], preferred_element_type=F32), 0.0)
    o_ref[...] = (y * y).astype(o_ref.dtype)


def _in_proj_kernel(x_ref, g_ref, sc_ref, sh_ref, w_ref, p_ref, gate_ref, h_scr, *, nseg, n_main):
    j = pl.program_id(1)

    @pl.when(j == 0)
    def _():
        _norm_mod_rows(x_ref, g_ref, sc_ref, sh_ref, h_scr, nseg)

    def project():
        return lax.dot_general(h_scr[...], w_ref[...], (((1,), (1,)), ((), ())), preferred_element_type=F32)

    @pl.when(j < n_main)
    def _():
        p_ref[...] = project()

    @pl.when(j >= n_main)
    def _():
        gate_ref[...] = project().astype(gate_ref.dtype)


def _norm_proj_call(kern, x, g, mod6, k_scale, k_shift, w_stack, layer, seq_len, tn, out_specs, out_shape, name):
    n, d = x.shape
    m = w_stack.shape[2]
    tm, nseg, tps = _row_tiling(n, seq_len, ROW_TILE)

    def mod_spec(k):
        return pl.BlockSpec((None, nseg, 1, d), lambda i, j: (k, i // tps, 0, 0))

    return pl.pallas_call(
        functools.partial(kern, nseg=nseg),
        grid=(n // tm, m // tn),
        in_specs=[
            pl.BlockSpec((tm, d), lambda i, j: (i, 0)),
            pl.BlockSpec((1, d), lambda i, j: (0, 0)),
            mod_spec(k_scale),
            mod_spec(k_shift),
            pl.BlockSpec((None, d, tn), lambda i, j: (layer, 0, j)),
        ],
        out_specs=out_specs(tm),
        out_shape=out_shape,
        scratch_shapes=[pltpu.VMEM((tm, d), BF16)],
        compiler_params=_cparams(("arbitrary", "arbitrary")),
        name=name,
    )(x, g.reshape(1, d), mod6, mod6, w_stack)


def _mlp_up(x, g, mod6, w_stack, layer, seq_len):
    n = x.shape[0]
    m = w_stack.shape[2]
    tn = COL_TILE
    return _norm_proj_call(
        _mlp_up_kernel, x, g, mod6, 4, 3, w_stack, layer, seq_len, tn,
        lambda tm: pl.BlockSpec((tm, tn), lambda i, j: (i, j)),
        jax.ShapeDtypeStruct((n, m), BF16), "mlp_up")


def _in_proj(x, g, mod6, w_stack, layer, seq_len, main_cols, rows=None):
    d = x.shape[1]
    row0, n = rows if rows is not None else (0, x.shape[0])
    m = w_stack.shape[1]
    tn = IN_TILE
    n_main = main_cols // tn
    tm, nseg, tps = _row_tiling(n, seq_len, ROW_TILE)

    def mod_spec(k):
        return pl.BlockSpec((None, nseg, 1, d), lambda i, j: (k, i // tps, 0, 0))

    return pl.pallas_call(
        functools.partial(_in_proj_kernel, nseg=nseg, n_main=n_main),
        grid=(n // tm, m // tn),
        in_specs=[
            pl.BlockSpec((tm, d), lambda i, j: (i + row0 // tm, 0)),
            pl.BlockSpec((1, d), lambda i, j: (0, 0)),
            mod_spec(1),
            mod_spec(0),
            pl.BlockSpec((None, tn, d), lambda i, j: (layer, j, 0)),
        ],
        out_specs=[pl.BlockSpec((tm, tn), lambda i, j: (i, jnp.minimum(j, n_main - 1))),
                   pl.BlockSpec((tm, tn), lambda i, j: (i, jnp.maximum(j - n_main, 0)))],
        out_shape=[jax.ShapeDtypeStruct((n, main_cols), F32), jax.ShapeDtypeStruct((n, m - main_cols), BF16)],
        scratch_shapes=[pltpu.VMEM((tm, d), BF16)],
        compiler_params=_cparams(("arbitrary", "arbitrary")),
        name="in_proj",
    )(x, g.reshape(1, d), mod6, mod6, w_stack)


def _proj_res_kernel(a_ref, w_ref, x_ref, gate_ref, o_ref, *, nseg, nk):
    part = jnp.dot(a_ref[...], w_ref[...], preferred_element_type=F32)

    def finish(acc):
        x = x_ref[...]
        o_ref[...] = (_per_seq(x, nseg) + gate_ref[...] * _per_seq(acc, nseg)).reshape(x.shape)

    if nk == 1:
        finish(part)
        return
    k = pl.program_id(2)

    @pl.when(k == 0)
    def _():
        o_ref[...] = part

    @pl.when((k > 0) & (k < nk - 1))
    def _():
        o_ref[...] += part

    @pl.when(k == nk - 1)
    def _():
        finish(o_ref[...] + part)


def _proj_residual(a, w_stack, layer, x, mod6, k_gate, seq_len, name, x_row0=0):
    n, kdim = a.shape
    d = w_stack.shape[2]
    tn, tm_max = (OUT_TILE, OUT_ROW_TILE) if kdim <= d else (DOWN_TILE, ROW_TILE)
    tk = kdim
    nk = kdim // tk
    tm, nseg, tps = _row_tiling(n, seq_len, tm_max)
    return pl.pallas_call(
        functools.partial(_proj_res_kernel, nseg=nseg, nk=nk),
        grid=(n // tm, d // tn, nk),
        in_specs=[
            pl.BlockSpec((tm, tk), lambda i, j, k: (i, k)),
            pl.BlockSpec((None, tk, tn), lambda i, j, k: (layer, k, j)),
            pl.BlockSpec((tm, tn), lambda i, j, k: (i + x_row0 // tm, j)),
            pl.BlockSpec((None, nseg, 1, tn), lambda i, j, k: (k_gate, i // tps, 0, j)),
        ],
        out_specs=pl.BlockSpec((tm, tn), lambda i, j, k: (i, j)),
        out_shape=jax.ShapeDtypeStruct((n, d), F32),
        compiler_params=_cparams(("arbitrary", "arbitrary", "arbitrary")),
        name=name,
    )(a, w_stack, x, mod6)


def _merge_kernel(ya_ref, yb_ref, yc_ref, wa_ref, wb_ref, wc_ref, ga_ref, gb_ref, gc_ref, o_ref):
    acc = None
    for y_ref, w_ref, g_ref in ((ya_ref, wa_ref, ga_ref), (yb_ref, wb_ref, gb_ref), (yc_ref, wc_ref, gc_ref)):
        gate = 1.0 / (1.0 + jnp.exp(-g_ref[...].astype(F32)))
        term = gate * jnp.dot(y_ref[...], w_ref[...], preferred_element_type=F32)
        acc = term if acc is None else acc + term
    o_ref[...] = acc.astype(o_ref.dtype)


def _merge(ya, yb, yc, w_stack, layer, gates, d):
    n, bw = ya.shape
    tn = MERGE_TILE
    tm = min(MERGE_ROW_TILE, n)
    gsteps = d // tn
    y_spec = pl.BlockSpec((tm, bw), lambda i, j: (i, 0))

    def w_spec(k):
        return pl.BlockSpec((None, None, bw, tn), lambda i, j: (layer, k, 0, j))

    def g_spec(k):
        return pl.BlockSpec((tm, tn), lambda i, j: (i, k * gsteps + j))

    return pl.pallas_call(
        _merge_kernel,
        grid=(n // tm, d // tn),
        in_specs=[y_spec, y_spec, y_spec, w_spec(0), w_spec(1), w_spec(2), g_spec(0), g_spec(1), g_spec(2)],
        out_specs=pl.BlockSpec((tm, tn), lambda i, j: (i, j)),
        out_shape=jax.ShapeDtypeStruct((n, d), BF16),
        compiler_params=_cparams(("arbitrary", "arbitrary")),
        name="merge",
    )(ya, yb, yc, w_stack, w_stack, w_stack, gates, gates, gates)


def _merge_out_kernel(ya_ref, yb_ref, yc_ref, wa_ref, wb_ref, wc_ref, ga_ref, gb_ref, gc_ref,
                      wo_ref, x_ref, gate_ref, o_ref, merged_scr, *, nseg, n_merge):
    j = pl.program_id(1)

    @pl.when(j < n_merge)
    def _():
        acc = None
        for y_ref, w_ref, g_ref in ((ya_ref, wa_ref, ga_ref), (yb_ref, wb_ref, gb_ref), (yc_ref, wc_ref, gc_ref)):
            gate = 1.0 / (1.0 + jnp.exp(-g_ref[...].astype(F32)))
            term = gate * jnp.dot(y_ref[...], w_ref[...], preferred_element_type=F32)
            acc = term if acc is None else acc + term
        merged_scr[j] = acc.astype(merged_scr.dtype)

    @pl.when(j >= n_merge)
    def _():
        tk = merged_scr.shape[2]
        acc = None
        for q in range(n_merge):
            part = jnp.dot(merged_scr[q], wo_ref[q * tk:(q + 1) * tk, :], preferred_element_type=F32)
            acc = part if acc is None else acc + part
        x = x_ref[...]
        o_ref[...] = (_per_seq(x, nseg) + gate_ref[...] * _per_seq(acc, nseg)).reshape(x.shape)


def _merge_out(ya, yb, yc, wb_stack, wo_stack, layer, gates, x, mod6, k_gate, seq_len, x_row0=0):
    n, bw = ya.shape
    d = wo_stack.shape[2]
    tn = MERGE_TILE
    n_merge = d // tn
    tm, nseg, tps = _row_tiling(n, seq_len, ROW_TILE)
    mj = lambda j: jnp.minimum(j, n_merge - 1)
    oj = lambda j: jnp.maximum(j - n_merge, 0)
    y_spec = pl.BlockSpec((tm, bw), lambda i, j: (i, 0))

    def w_spec(k):
        return pl.BlockSpec((None, None, bw, tn), lambda i, j: (layer, k, 0, mj(j)))

    def g_spec(k):
        return pl.BlockSpec((tm, tn), lambda i, j: (i, k * n_merge + mj(j)))

    return pl.pallas_call(
        functools.partial(_merge_out_kernel, nseg=nseg, n_merge=n_merge),
        grid=(n // tm, 2 * n_merge),
        in_specs=[y_spec, y_spec, y_spec, w_spec(0), w_spec(1), w_spec(2), g_spec(0), g_spec(1), g_spec(2),
                  pl.BlockSpec((None, d, tn), lambda i, j: (layer, 0, oj(j))),
                  pl.BlockSpec((tm, tn), lambda i, j: (i + x_row0 // tm, oj(j))),
                  pl.BlockSpec((None, nseg, 1, tn), lambda i, j: (k_gate, i // tps, 0, oj(j)))],
        out_specs=pl.BlockSpec((tm, tn), lambda i, j: (i, oj(j))),
        out_shape=jax.ShapeDtypeStruct((n, d), F32),
        scratch_shapes=[pltpu.VMEM((n_merge, tm, tn), BF16)],
        compiler_params=_cparams(("arbitrary", "arbitrary")),
        name="merge_out",
    )(ya, yb, yc, wb_stack, wb_stack, wb_stack, gates, gates, gates, wo_stack, x, mod6)


def _hgrn_levels(c):
    lv, m = [], c // 2
    while m >= DIAG:
        lv.append(m)
        m //= 2
    return tuple(lv)


def _hgrn_select(c):
    tri = np.arange(c)[None, :] <= np.arange(c)[:, None]
    return jnp.asarray(tri.astype(np.float32), dtype=BF16)


def _hgrn_kernel(*refs, c, n_chunks, heads, has_init):
    if has_init:
        q_ref, f_ref, i_ref, g_ref, lbc_ref, gn_ref, sel_ref, s0_ref, ya_ref, s_ref, bpad, kpad = refs
    else:
        q_ref, f_ref, i_ref, g_ref, lbc_ref, gn_ref, sel_ref, ya_ref, s_ref, bpad, kpad = refs

    @pl.when(pl.program_id(1) == 0)
    def _():
        s_ref[...] = s0_ref[...] if has_init else jnp.zeros(s_ref.shape, F32)

    _hgrn_chunks(q_ref, f_ref, i_ref, g_ref, lbc_ref, gn_ref, sel_ref, ya_ref, s_ref, bpad, kpad,
                 c=c, n_chunks=n_chunks, heads=heads, unroll=False)


def _hgrn_chunks(q_ref, f_ref, i_ref, g_ref, lbc_ref, gn_ref, sel_ref, ya_ref, s_ref, bpad, kpad,
                 *, c, n_chunks, heads, unroll, fillers=()):
    levels = _hgrn_levels(c)
    nl = len(levels)
    width = heads * A_DK
    head_lanes = [slice(hh * A_DK, (hh + 1) * A_DK) for hh in range(heads)]

    bpad[0:DIAG, :] = jnp.zeros((DIAG, width), F32)
    kpad[0:DIAG, :] = jnp.zeros((DIAG, width), F32)

    sel = sel_ref[...]
    log_lb = lbc_ref[0:1, :]
    log1m_lb = lbc_ref[1:2, :]
    one_m_lb = lbc_ref[2:3, :]
    row = lax.broadcasted_iota(jnp.int32, (c, width), 0)
    rr = lax.broadcasted_iota(jnp.int32, (c, c), 0)
    cc = lax.broadcasted_iota(jnp.int32, (c, c), 1)
    upper = [(row & m) != 0 for m in levels]
    same = [(rr ^ cc) < 2 * m for m in levels]
    dmat = jnp.where(((rr ^ cc) < DIAG) & (cc <= rr), rr - cc, -1)
    nt_dims = (((1,), (1,)), ((), ()))
    tn_dims = (((0,), (0,)), ((), ()))

    def chunk(ci):
        rows = slice(ci * c, (ci + 1) * c) if unroll else pl.ds(pl.multiple_of(ci * c, c), c)
        z = f_ref[rows, :]
        aq = q_ref[rows, :]

        log_sig = jnp.minimum(z, 0.0) - jnp.log(1.0 + jnp.exp(-jnp.abs(z)))
        bb = log1m_lb + log_sig
        log_f = jnp.maximum(log_lb, bb) + jnp.log(1.0 + jnp.exp(-jnp.abs(log_lb - bb)))
        k = one_m_lb / (1.0 + jnp.exp(z))
        q = _silu(aq)

        b = _dot3(sel, log_f * LOG2E)
        bpad[DIAG:DIAG + c, :] = b
        kpad[DIAG:DIAG + c, :] = k
        b_tot = bpad[DIAG + c - 1:DIAG + c, :]
        vb = i_ref[rows, :].astype(BF16)
        q_in = (q * jnp.exp2(b)).astype(BF16)
        k_out = (k * jnp.exp2(b_tot - b)).astype(BF16)
        yield

        att = [jnp.zeros((c, c), F32) for _ in range(heads)]
        for li in range(nl):
            m = levels[li]
            ref_b = jnp.concatenate(
                [jnp.broadcast_to(bpad[DIAG + g0 + m - 1:DIAG + g0 + m, :], (2 * m, width))
                 for g0 in range(0, c, 2 * m)], axis=0)
            e = jnp.exp2(-jnp.abs(b - ref_b))
            qs = jnp.where(upper[li], q * e, 0.0).astype(BF16)
            ks = jnp.where(upper[li], 0.0, k * e).astype(BF16)
            for hh, lanes in enumerate(head_lanes):
                a_l = lax.dot_general(qs[:, lanes], ks[:, lanes], nt_dims, preferred_element_type=F32)
                att[hh] = att[hh] + jnp.where(same[li], a_l, 0.0)
        yield

        for dlt in range(DIAG):
            if dlt == 0:
                w = q * k
            else:
                ksh = kpad[DIAG - dlt:DIAG - dlt + c, :]
                bsh = bpad[DIAG - dlt:DIAG - dlt + c, :]
                w = q * ksh * jnp.exp2(b - bsh)
            for hh, lanes in enumerate(head_lanes):
                col = jnp.sum(w[:, lanes], axis=-1, keepdims=True)
                att[hh] = jnp.where(dmat == dlt, col, att[hh])
            if dlt in (DIAG // 2 - 1, DIAG - 1):
                yield

        decay_row = jnp.exp2(b_tot)
        outs = []
        for hh, lanes in enumerate(head_lanes):
            s_prev = s_ref[0, hh]
            o = jnp.dot(q_in[:, lanes], s_prev.astype(BF16), preferred_element_type=F32)
            o = o + jnp.dot(att[hh].astype(BF16), vb[:, lanes], preferred_element_type=F32)
            decay_col = jnp.broadcast_to(decay_row[:, lanes], (A_DK, A_DK)).T
            s_ref[0, hh] = decay_col * s_prev + lax.dot_general(
                k_out[:, lanes], vb[:, lanes], tn_dims, preferred_element_type=F32)
            outs.append(o * lax.rsqrt(jnp.mean(o * o, axis=-1, keepdims=True) + NORM_EPS))
        y = jnp.concatenate(outs, axis=1) * gn_ref[...] * _silu(g_ref[rows, :])
        ya_ref[rows, :] = y.astype(ya_ref.dtype)
        yield

    if not unroll:
        def body(ci, carry):
            for _ in chunk(ci):
                pass
            return carry

        lax.fori_loop(0, n_chunks, body, 0)
        return

    n_phases = 5
    _emit_interleaved(fillers, [functools.partial(next, gen, None)
                                for gen in map(chunk, range(n_chunks)) for _ in range(n_phases)])


def _hgrn_specs(p, lbc_stack, layer, gn, heads, t_blk, row_block):
    aw = heads * A_DK
    c = min(t_blk, SCAN_CHUNK)
    sel = _hgrn_select(c)

    def col_spec(seg):
        return pl.BlockSpec((t_blk, aw), lambda *g: (row_block(*g), seg))

    in_specs = [col_spec(0), col_spec(1), col_spec(2), col_spec(3),
                pl.BlockSpec((None, SUBLANES, aw), lambda *g: (layer, 0, 0)),
                pl.BlockSpec((1, aw), lambda *g: (0, 0)),
                pl.BlockSpec(sel.shape, lambda *g: (0, 0))]
    args = [p, p, p, p, lbc_stack, gn.reshape(1, aw), sel]
    scratch = [pltpu.VMEM((c + DIAG, aw), F32), pltpu.VMEM((c + DIAG, aw), F32)]
    return c, in_specs, args, pl.BlockSpec((t_blk, aw), lambda *g: (row_block(*g), 0)), scratch


def _mlp_up_hgrn_kernel(x_ref, g_ref, sc_ref, sh_ref, w_ref, q_ref, f_ref, i_ref, gg_ref, lbc_ref, gn_ref, sel_ref,
                        o_ref, ya_ref, s_ref, h_scr, acc_scr, bpad, kpad,
                        *, nseg, c, n_chunks, heads, steps_per_seq):
    j = pl.program_id(1)
    step = pl.program_id(0) * pl.num_programs(1) + j

    @pl.when(j == 0)
    def _():
        _norm_mod_rows(x_ref, g_ref, sc_ref, sh_ref, h_scr, nseg)

    @pl.when(step % steps_per_seq == 0)
    def _():
        s_ref[...] = jnp.zeros(s_ref.shape, F32)

    kc = h_scr.shape[1] // UP_K_SPLIT

    def sub_dot(c0, ki):
        cols = slice(c0, c0 + EPILOGUE_COLS)
        krows = slice(ki * kc, (ki + 1) * kc)
        part = jnp.dot(h_scr[:, krows], w_ref[krows, cols], preferred_element_type=F32)
        if UP_K_SPLIT == 1:
            acc = part
        elif ki == 0:
            acc_scr[...] = part
            return
        elif ki < UP_K_SPLIT - 1:
            acc_scr[...] += part
            return
        else:
            acc = acc_scr[...] + part
        y = jnp.maximum(acc, 0.0)
        o_ref[:, cols] = (y * y).astype(o_ref.dtype)

    fillers = [functools.partial(sub_dot, c0, ki)
               for c0 in range(0, w_ref.shape[1], EPILOGUE_COLS) for ki in range(UP_K_SPLIT)]
    _hgrn_chunks(q_ref, f_ref, i_ref, gg_ref, lbc_ref, gn_ref, sel_ref, ya_ref, s_ref, bpad, kpad,
                 c=c, n_chunks=n_chunks, heads=heads, unroll=True, fillers=fillers)


def _mlp_up_hgrn(x, g, mod6, w_stack, layer, seq_len, p_other, lbc_stack, gn, heads):
    n, d = x.shape
    m = w_stack.shape[2]
    tn = COL_TILE
    tm, nseg, tps = _row_tiling(n, seq_len, ROW_TILE)
    ni, nj = n // tm, m // tn
    n_other = p_other.shape[0]
    t_step = n_other // (ni * nj)
    assert t_step * ni * nj == n_other and seq_len % t_step == 0 and t_step % SUBLANES == 0
    steps_per_seq = seq_len // t_step
    aw = heads * A_DK
    c, h_in_specs, h_args, ya_spec, h_scratch = _hgrn_specs(
        p_other, lbc_stack, layer, gn, heads, t_step, lambda i, j: i * nj + j)

    def mod_spec(k):
        return pl.BlockSpec((None, nseg, 1, d), lambda i, j: (k, i // tps, 0, 0))

    state_spec = pl.BlockSpec((1, heads, A_DK, A_DK), lambda i, j: ((i * nj + j) // steps_per_seq, 0, 0, 0))
    return pl.pallas_call(
        functools.partial(_mlp_up_hgrn_kernel, nseg=nseg, c=c, n_chunks=t_step // c, heads=heads,
                          steps_per_seq=steps_per_seq),
        grid=(ni, nj),
        in_specs=[
            pl.BlockSpec((tm, d), lambda i, j: (i, 0)),
            pl.BlockSpec((1, d), lambda i, j: (0, 0)),
            mod_spec(4),
            mod_spec(3),
            pl.BlockSpec((None, d, tn), lambda i, j: (layer, 0, j)),
        ] + h_in_specs,
        out_specs=[pl.BlockSpec((tm, tn), lambda i, j: (i, j)), ya_spec, state_spec],
        out_shape=[jax.ShapeDtypeStruct((n, m), BF16),
                   jax.ShapeDtypeStruct((n_other, aw), BF16),
                   jax.ShapeDtypeStruct((n_other // seq_len, heads, A_DK, A_DK), F32)],
        scratch_shapes=[pltpu.VMEM((tm, d), BF16), pltpu.VMEM((tm, EPILOGUE_COLS), F32)] + h_scratch,
        compiler_params=_cparams(("arbitrary", "arbitrary")),
        name="mlp_up_hgrn2",
    )(x, g.reshape(1, d), mod6, mod6, w_stack, *h_args)


def _hgrn(p, lbc_stack, layer, gn, s0, bsz, seq_len, heads):
    c = min(seq_len, SCAN_CHUNK)
    t_blk = min(seq_len, 256)
    nt = seq_len // t_blk
    aw = heads * A_DK

    def col_spec(seg):
        return pl.BlockSpec((t_blk, aw), lambda b, t: (b * nt + t, seg))

    sel = _hgrn_select(c)
    in_specs = [col_spec(0), col_spec(1), col_spec(2), col_spec(3),
                pl.BlockSpec((None, SUBLANES, aw), lambda b, t: (layer, 0, 0)),
                pl.BlockSpec((1, aw), lambda b, t: (0, 0)),
                pl.BlockSpec(sel.shape, lambda b, t: (0, 0))]
    args = [p, p, p, p, lbc_stack, gn.reshape(1, aw), sel]
    state_spec = pl.BlockSpec((1, heads, A_DK, A_DK), lambda b, t: (b, 0, 0, 0))
    if s0 is not None:
        in_specs.append(pl.BlockSpec((None, 1, heads, A_DK, A_DK), lambda b, t: (layer, b, 0, 0, 0)))
        args.append(s0)
    return pl.pallas_call(
        functools.partial(_hgrn_kernel, c=c, n_chunks=t_blk // c, heads=heads, has_init=s0 is not None),
        grid=(bsz, nt),
        in_specs=in_specs,
        out_specs=[pl.BlockSpec((t_blk, aw), lambda b, t: (b * nt + t, 0)), state_spec],
        out_shape=[jax.ShapeDtypeStruct((bsz * seq_len, aw), BF16),
                   jax.ShapeDtypeStruct((bsz, heads, A_DK, A_DK), F32)],
        scratch_shapes=[pltpu.VMEM((c + DIAG, aw), F32), pltpu.VMEM((c + DIAG, aw), F32)],
        compiler_params=_cparams(("arbitrary", "arbitrary")),
        name="hgrn2",
    )(*args)


def _ssd_kernel(*refs, t, bw, has_init):
    n_in = SSD_N_INPUTS
    s0_ref, c0_ref = (refs[n_in], refs[n_in + 1]) if has_init else (None, None)
    rest = refs[n_in + 2:] if has_init else refs[n_in:]
    yb_ref, s_out_ref, conv_out_ref, xpad, st = rest
    ti = pl.program_id(1)
    _ssd_reset(s0_ref, c0_ref, xpad, st, ti == 0)
    for _ in _ssd_phases(*refs[:n_in], yb_ref, xpad, st, t=t, bw=bw):
        pass
    _ssd_flush(conv_out_ref, s_out_ref, xpad, st, ti == pl.num_programs(1) - 1)


SSD_N_INPUTS = 12
SSD_PHASES = 9


def _ssd_phases(z_ref, xs_ref, bc_ref, dt_ref, cw_ref, cb_ref, dtb_ref, alog_ref, dsk_ref, gn_ref, exp_ref, sel_ref,
                yb_ref, xpad, st, *, t, bw):
    gw = bw // B_GROUPS
    n_bc = B_GROUPS * B_DSTATE
    pad = SUBLANES

    cur = jnp.concatenate([xs_ref[...], bc_ref[...]], axis=1)
    prev = xpad[...]
    row8 = lax.broadcasted_iota(jnp.int32, (pad, cur.shape[1]), 0)
    conv = cb_ref[...]
    for j in range(CONV_W):
        s = CONV_W - 1 - j
        if s == 0:
            tap = cur
        else:
            rolled = pltpu.roll(cur, s, 0)
            head = jnp.where(row8 < s, pltpu.roll(prev, s, 0), rolled[0:pad])
            tap = jnp.concatenate([head, rolled[pad:]], axis=0)
        conv = conv + tap * cw_ref[j:j + 1, :]
    xbc = _silu(conv)
    x = xbc[:, 0:bw]
    yield

    dt = _softplus(dt_ref[...] + dtb_ref[...])
    a = dt * (-jnp.exp(alog_ref[...]))
    cs = _dot3(sel_ref[...], a)
    a_cum = cs[0:t]
    a_tot = cs[t:2 * t]
    ex = _dot3_rhs(jnp.concatenate([dt, a_cum, a_tot - a_cum], axis=0), exp_ref[...])
    dt_e = ex[0:t]
    acum_e = ex[t:2 * t]
    dec_e = ex[2 * t:3 * t]
    atot_e = acum_e[t - 1:t, :]

    xdt = x * dt_e
    xw = (xdt * jnp.exp(dec_e)).astype(BF16)
    xdt_b = xdt.astype(BF16)
    a_cum_t = a_cum.T
    rr = lax.broadcasted_iota(jnp.int32, (t, t), 0)
    cc = lax.broadcasted_iota(jnp.int32, (t, t), 1)
    causal = cc <= rr
    lane = lax.broadcasted_iota(jnp.int32, (t, LANES), 1)
    heads_per_group = gw // B_HEADDIM
    pairs_per_group = gw // LANES
    yield

    y_groups = []
    for g in range(B_GROUPS):
        bg = xbc[:, bw + g * B_DSTATE:bw + (g + 1) * B_DSTATE].astype(BF16)
        cg = xbc[:, bw + n_bc + g * B_DSTATE:bw + n_bc + (g + 1) * B_DSTATE].astype(BF16)
        gs = slice(g * gw, (g + 1) * gw)
        st_g = st[:, gs]
        scores = lax.dot_general(cg, bg, (((1,), (1,)), ((), ())), preferred_element_type=F32)
        y_off = jnp.dot(cg, st_g.astype(BF16), preferred_element_type=F32) * jnp.exp(acum_e[:, gs])
        st[:, gs] = jnp.exp(atot_e[:, gs]) * st_g + lax.dot_general(
            bg, xw[:, gs], (((0,), (0,)), ((), ())), preferred_element_type=F32)
        yield
        y_pairs = []
        for pr in range(pairs_per_group):
            cols = slice(g * gw + pr * LANES, g * gw + (pr + 1) * LANES)
            xp = xdt_b[:, cols]
            ms, xs_blocks = [], []
            for half in range(LANES // B_HEADDIM):
                h = g * heads_per_group + pr * (LANES // B_HEADDIM) + half
                diff = a_cum[:, h:h + 1] - a_cum_t[h:h + 1, :]
                lmat = jnp.where(causal, jnp.exp(jnp.where(causal, diff, 0.0)), 0.0)
                ms.append((scores * lmat).astype(BF16))
                in_half = (lane // B_HEADDIM) == half
                xs_blocks.append(jnp.where(in_half, xp, jnp.zeros_like(xp)))
            y_pairs.append(jnp.dot(jnp.concatenate(ms, axis=1), jnp.concatenate(xs_blocks, axis=0),
                                   preferred_element_type=F32))
            if pr % 2 == 1:
                yield
        y_groups.append(jnp.concatenate(y_pairs, axis=1) + y_off)
    y = jnp.concatenate(y_groups, axis=1) + dsk_ref[...] * x
    y = y * _silu(z_ref[...])
    outs = []
    for g in range(B_GROUPS):
        yg = y[:, g * gw:(g + 1) * gw]
        outs.append(yg * lax.rsqrt(jnp.mean(yg * yg, axis=-1, keepdims=True) + NORM_EPS))
    yb_ref[...] = (jnp.concatenate(outs, axis=1) * gn_ref[...]).astype(yb_ref.dtype)

    xpad[...] = cur[t - pad:t, :]
    yield


def _ssd_reset(s0_ref, c0_ref, xpad, st, first):
    @pl.when(first)
    def _():
        if s0_ref is not None:
            xpad[...] = c0_ref[0]
            st[...] = s0_ref[0].reshape(st.shape[1], st.shape[0]).T
        else:
            xpad[...] = jnp.zeros(xpad.shape, F32)
            st[...] = jnp.zeros(st.shape, F32)


def _ssd_flush(conv_out_ref, s_out_ref, xpad, st, last):
    @pl.when(last)
    def _():
        conv_out_ref[0] = xpad[...]
        s_out_ref[0] = st[...].T.reshape(s_out_ref.shape[1:])


def _ssd_operands(p, cols, prm, t, n_rows, n_seq, row_block, seq_block):
    bw = prm["bw"]
    heads = bw // B_HEADDIM
    n_bc = B_GROUPS * B_DSTATE
    cdim = bw + 2 * n_bc
    tri = np.arange(t)[None, :] <= np.arange(t)[:, None]
    sel = jnp.asarray(np.concatenate([tri, np.ones((t, t), bool)], 0).astype(np.float32), dtype=BF16)
    expand = np.zeros((LANES, bw), np.float32)
    expand[np.arange(bw) // B_HEADDIM, np.arange(bw)] = 1.0
    expand = jnp.asarray(expand, dtype=BF16)

    def blk(width, off):
        return pl.BlockSpec((t, width), lambda *g: (row_block(*g), off // width))

    def full(shape):
        return pl.BlockSpec(shape, lambda *g: (0,) * len(shape))

    in_specs = [blk(bw, cols["z"]), blk(bw, cols["xs"]), blk(2 * n_bc, cols["bc"]), blk(LANES, cols["dt"]),
                full((CONV_W, cdim)), full((1, cdim)), full((1, LANES)), full((1, LANES)),
                full((1, bw)), full((1, bw)), full(expand.shape), full(sel.shape)]
    args = [p, p, p, p, prm["conv_w"], prm["conv_b"], prm["dt_bias"], prm["a_log"], prm["d_skip"], prm["gn"],
            expand, sel]
    assert len(args) == SSD_N_INPUTS
    out_specs = [pl.BlockSpec((t, bw), lambda *g: (row_block(*g), 0)),
                 pl.BlockSpec((1, heads, B_HEADDIM, B_DSTATE), lambda *g: (seq_block(*g), 0, 0, 0)),
                 pl.BlockSpec((1, SUBLANES, cdim), lambda *g: (seq_block(*g), 0, 0))]
    out_shape = [jax.ShapeDtypeStruct((n_rows, bw), BF16),
                 jax.ShapeDtypeStruct((n_seq, heads, B_HEADDIM, B_DSTATE), F32),
                 jax.ShapeDtypeStruct((n_seq, SUBLANES, cdim), F32)]
    scratch = [pltpu.VMEM((SUBLANES, cdim), F32), pltpu.VMEM((B_DSTATE, bw), F32)]
    return in_specs, args, out_specs, out_shape, scratch


def _ssd(p, cols, prm, s0, c0, bsz, seq_len):
    bw = prm["bw"]
    heads = bw // B_HEADDIM
    cdim = bw + 2 * B_GROUPS * B_DSTATE
    t = min(seq_len, SSD_ROWS)
    nt = seq_len // t
    in_specs, args, out_specs, out_shape, scratch = _ssd_operands(
        p, cols, prm, t, bsz * seq_len, bsz, lambda b, ti: b * nt + ti, lambda b, ti: b)
    if s0 is not None:
        layer = prm["layer"]
        in_specs += [pl.BlockSpec((None, 1, heads, B_HEADDIM, B_DSTATE), lambda b, ti: (layer, b, 0, 0, 0)),
                     pl.BlockSpec((None, 1, SUBLANES, cdim), lambda b, ti: (layer, b, 0, 0))]
        args += [s0, c0]
    return pl.pallas_call(
        functools.partial(_ssd_kernel, t=t, bw=bw, has_init=s0 is not None),
        grid=(bsz, nt),
        in_specs=in_specs,
        out_specs=out_specs,
        out_shape=out_shape,
        scratch_shapes=scratch,
        compiler_params=_cparams(("arbitrary", "arbitrary")),
        name="ssd",
    )(*args)


def _down_sub_dots(a_ref, w_ref, x_ref, gate_ref, o_ref, acc_scr, nseg):
    kc = a_ref.shape[1] // DOWN_K_SPLIT

    def sub_dot(ki):
        rows = slice(ki * kc, (ki + 1) * kc)
        part = jnp.dot(a_ref[:, rows], w_ref[rows, :], preferred_element_type=F32)
        if ki == 0:
            acc_scr[...] = part
        elif ki < DOWN_K_SPLIT - 1:
            acc_scr[...] += part
        else:
            x = x_ref[...]
            acc = acc_scr[...] + part
            o_ref[...] = (_per_seq(x, nseg) + gate_ref[...] * _per_seq(acc, nseg)).reshape(x.shape)

    return [functools.partial(sub_dot, ki) for ki in range(DOWN_K_SPLIT)]


def _emit_interleaved(fillers, phases):
    per_filler = -(-len(phases) // max(len(fillers), 1))
    pos = 0
    for fill in fillers:
        fill()
        for ph in phases[pos:pos + per_filler]:
            ph()
        pos += per_filler
    for ph in phases[pos:]:
        ph()


def _mlp_down_ssd_kernel(a_ref, w_ref, x_ref, gate_ref, *refs, nseg, t, bw, steps_per_seq):
    ssd_in = refs[:SSD_N_INPUTS]
    o_ref, yb_ref, s_out_ref, conv_out_ref, acc_scr, xpad, st = refs[SSD_N_INPUTS:]
    step = pl.program_id(0) * pl.num_programs(1) + pl.program_id(1)
    pos = step % steps_per_seq
    _ssd_reset(None, None, xpad, st, pos == 0)
    gen = _ssd_phases(*ssd_in, yb_ref, xpad, st, t=t, bw=bw)
    phases = [functools.partial(next, gen, None) for _ in range(SSD_PHASES)]
    _emit_interleaved(_down_sub_dots(a_ref, w_ref, x_ref, gate_ref, o_ref, acc_scr, nseg), phases)
    _ssd_flush(conv_out_ref, s_out_ref, xpad, st, pos == steps_per_seq - 1)


def _mlp_down_ssd(a, w_stack, layer, x, mod6, k_gate, seq_len, p_other, cols, prm):
    n, kdim = a.shape
    d = w_stack.shape[2]
    tn = DOWN_TILE
    tm, nseg, tps = _row_tiling(n, seq_len, ROW_TILE)
    ni, nj = n // tm, d // tn
    n_other = p_other.shape[0]
    t = n_other // (ni * nj)
    assert t * ni * nj == n_other and seq_len % t == 0 and t % SUBLANES == 0 and kdim % DOWN_K_SPLIT == 0
    steps_per_seq = seq_len // t
    s_in_specs, s_args, s_out_specs, s_out_shape, s_scratch = _ssd_operands(
        p_other, cols, prm, t, n_other, n_other // seq_len,
        lambda i, j: i * nj + j, lambda i, j: (i * nj + j) // steps_per_seq)
    return pl.pallas_call(
        functools.partial(_mlp_down_ssd_kernel, nseg=nseg, t=t, bw=prm["bw"], steps_per_seq=steps_per_seq),
        grid=(ni, nj),
        in_specs=[
            pl.BlockSpec((tm, kdim), lambda i, j: (i, 0)),
            pl.BlockSpec((None, kdim, tn), lambda i, j: (layer, 0, j)),
            pl.BlockSpec((tm, tn), lambda i, j: (i, j)),
            pl.BlockSpec((None, nseg, 1, tn), lambda i, j: (k_gate, i // tps, 0, j)),
        ] + s_in_specs,
        out_specs=[pl.BlockSpec((tm, tn), lambda i, j: (i, j))] + s_out_specs,
        out_shape=[jax.ShapeDtypeStruct((n, d), F32)] + s_out_shape,
        scratch_shapes=[pltpu.VMEM((tm, tn), F32)] + s_scratch,
        compiler_params=_cparams(("arbitrary", "arbitrary")),
        name="mlp_down_ssd",
    )(a, w_stack, x, mod6, *s_args)


def _cmlp_kernel(u_ref, v_ref, lng_ref, lnb_ref, ws_ref, bst_ref, *out_refs, t, n_chunks, keep_v):
    yc_ref = out_refs[0]
    cw = u_ref.shape[1] // C_GROUPS
    rr = lax.broadcasted_iota(jnp.int32, (t, t), 0)
    cc = lax.broadcasted_iota(jnp.int32, (t, t), 1)
    wts = [jnp.where(cc <= rr, ws_ref[g, 0:t, 0:t], 0.0).astype(BF16) for g in range(C_GROUPS)]
    for ci in range(n_chunks):
        rows = slice(ci * t, (ci + 1) * t)
        u = _gelu(u_ref[rows, :])
        gv = _gelu(v_ref[rows, :])
        mu = jnp.mean(gv, axis=-1, keepdims=True)
        dv = gv - mu
        var = jnp.mean(dv * dv, axis=-1, keepdims=True)
        v = dv * lax.rsqrt(var + NORM_EPS) * lng_ref[...] + lnb_ref[...]
        if keep_v:
            out_refs[1][rows, :] = v
        vb = v.astype(BF16)
        for g in range(C_GROUPS):
            lanes = slice(g * cw, (g + 1) * cw)
            mixed = jnp.dot(wts[g], vb[:, lanes], preferred_element_type=F32) + bst_ref[0:t, g:g + 1]
            yc_ref[rows, lanes] = (u[:, lanes] * mixed).astype(yc_ref.dtype)


def _cmlp(p, col_u, col_v, prm, bsz, seq_len, keep_v):
    cw = prm["cw"]
    t = min(seq_len, CMLP_CHUNK)
    t_blk = min(seq_len, 4 * CMLP_CHUNK)
    n = bsz * seq_len

    def full(shape):
        return pl.BlockSpec(shape, lambda i: (0,) * len(shape))

    out_specs = [pl.BlockSpec((t_blk, cw), lambda i: (i, 0))]
    out_shape = [jax.ShapeDtypeStruct((n, cw), BF16)]
    if keep_v:
        out_specs.append(pl.BlockSpec((t_blk, cw), lambda i: (i, 0)))
        out_shape.append(jax.ShapeDtypeStruct((n, cw), F32))
    return pl.pallas_call(
        functools.partial(_cmlp_kernel, t=t, n_chunks=t_blk // t, keep_v=keep_v),
        grid=(n // t_blk,),
        in_specs=[pl.BlockSpec((t_blk, cw), lambda i: (i, col_u // cw)),
                  pl.BlockSpec((t_blk, cw), lambda i: (i, col_v // cw)),
                  full((1, cw)), full((1, cw)),
                  full((C_GROUPS, CMLP_CHUNK, CMLP_CHUNK)), full((CMLP_CHUNK, C_GROUPS))],
        out_specs=out_specs,
        out_shape=out_shape,
        compiler_params=_cparams(("arbitrary",)),
        name="cmlp",
    )(p, p, prm["ln_g"], prm["ln_b"], prm["ws"], prm["bs_t"])


def _run_trunk(x3, mod, st_hgrn, st_ssm, st_conv, keep_v, w):
    bsz, seq_len, d = x3.shape
    depth = mod.shape[0]
    aw = bw = cw = d // 2
    heads_a = aw // A_DK
    x = x3.reshape(bsz * seq_len, d)
    col = w["cols"]
    hgrn_out, ssm_out, conv_out, v_out = [], [], [], []
    for l in range(depth):
        mod6 = mod[l].reshape(bsz, 6, 1, d).transpose(1, 0, 2, 3)
        p, gates = _in_proj(x, w["norm1_g"][l], mod6, w["w_in"], l, seq_len, col["gate"])

        y_a, s_h = _hgrn(p, w["lbc"], l, w["hgrn_onorm_g"][l], st_hgrn, bsz, seq_len, heads_a)
        ssd_prm = dict(bw=bw, layer=l, conv_w=w["ssm_conv_w"][l], conv_b=w["ssm_conv_b"][l][None],
                       dt_bias=w["dt_bias_pad"][l][None], a_log=w["a_log_pad"][l][None],
                       d_skip=w["d_skip"][l][None], gn=w["ssm_onorm_g"][l][None])
        y_b, s_s, conv_tail = _ssd(p, col, ssd_prm, st_ssm, st_conv, bsz, seq_len)
        cm_prm = dict(cw=cw, ln_g=w["cmlp_ln_g"][l][None], ln_b=w["cmlp_ln_b"][l][None],
                      ws=w["cmlp_ws"][l], bs_t=w["cmlp_bs"][l].T)
        c_res = _cmlp(p, col["u"], col["v"], cm_prm, bsz, seq_len, keep_v)
        x = _merge_out(y_a, y_b, c_res[0], w["w_branch"], w["w_out"], l, gates, x, mod6, 2, seq_len)
        hid = _mlp_up(x, w["norm2_g"][l], mod6, w["w_up"], l, seq_len)
        x = _proj_residual(hid, w["w_down"], l, x, mod6, 5, seq_len, "mlp_down")

        hgrn_out.append(s_h)
        ssm_out.append(s_s)
        conv_out.append(conv_tail[:, SUBLANES - (CONV_W - 1):, :])
        if keep_v:
            v_out.append(c_res[1].reshape(bsz, seq_len, cw))
    y = _final_norm(x, w["final_g"]).reshape(bsz, seq_len, d)
    return (y, jnp.stack(hgrn_out), jnp.stack(ssm_out), jnp.stack(conv_out),
            jnp.stack(v_out) if keep_v else None)


def _mlp_down_hgrn_kernel(a_ref, w_ref, x_ref, gate_ref, q_ref, f_ref, i_ref, gg_ref, lbc_ref, gn_ref, sel_ref,
                          o_ref, ya_ref, s_ref, acc_scr, bpad, kpad, *, nseg, c, n_chunks, heads, steps_per_seq):
    step = pl.program_id(0) * pl.num_programs(1) + pl.program_id(1)

    @pl.when(step % steps_per_seq == 0)
    def _():
        s_ref[...] = jnp.zeros(s_ref.shape, F32)

    fillers = _down_sub_dots(a_ref, w_ref, x_ref, gate_ref, o_ref, acc_scr, nseg)
    _hgrn_chunks(q_ref, f_ref, i_ref, gg_ref, lbc_ref, gn_ref, sel_ref, ya_ref, s_ref, bpad, kpad,
                 c=c, n_chunks=n_chunks, heads=heads, unroll=True, fillers=fillers)


def _mlp_down_hgrn(a, w_stack, layer, x, mod6, k_gate, seq_len, p_other, lbc_stack, lbc_layer, gn, heads):
    n, kdim = a.shape
    d = w_stack.shape[2]
    tn = DOWN_TILE
    tm, nseg, tps = _row_tiling(n, seq_len, ROW_TILE)
    ni, nj = n // tm, d // tn
    n_other = p_other.shape[0]
    t_step = n_other // (ni * nj)
    assert t_step * ni * nj == n_other and seq_len % t_step == 0 and t_step % SUBLANES == 0
    assert kdim % DOWN_K_SPLIT == 0
    steps_per_seq = seq_len // t_step
    aw = heads * A_DK
    c, h_in_specs, h_args, ya_spec, h_scratch = _hgrn_specs(
        p_other, lbc_stack, lbc_layer, gn, heads, t_step, lambda i, j: i * nj + j)
    state_spec = pl.BlockSpec((1, heads, A_DK, A_DK), lambda i, j: ((i * nj + j) // steps_per_seq, 0, 0, 0))
    return pl.pallas_call(
        functools.partial(_mlp_down_hgrn_kernel, nseg=nseg, c=c, n_chunks=t_step // c, heads=heads,
                          steps_per_seq=steps_per_seq),
        grid=(ni, nj),
        in_specs=[
            pl.BlockSpec((tm, kdim), lambda i, j: (i, 0)),
            pl.BlockSpec((None, kdim, tn), lambda i, j: (layer, 0, j)),
            pl.BlockSpec((tm, tn), lambda i, j: (i, j)),
            pl.BlockSpec((None, nseg, 1, tn), lambda i, j: (k_gate, i // tps, 0, j)),
        ] + h_in_specs,
        out_specs=[pl.BlockSpec((tm, tn), lambda i, j: (i, j)), ya_spec, state_spec],
        out_shape=[jax.ShapeDtypeStruct((n, d), F32),
                   jax.ShapeDtypeStruct((n_other, aw), BF16),
                   jax.ShapeDtypeStruct((n_other // seq_len, heads, A_DK, A_DK), F32)],
        scratch_shapes=[pltpu.VMEM((tm, tn), F32)] + h_scratch,
        compiler_params=_cparams(("arbitrary", "arbitrary")),
        name="mlp_down_hgrn2",
    )(a, w_stack, x, mod6, *h_args)


def _run_trunk_halves(x3, mod, w):
    bsz, seq_len, d = x3.shape
    depth = mod.shape[0]
    hb = bsz // 2
    aw = bw = cw = d // 2
    heads_a = aw // A_DK
    col = w["cols"]
    n_half = hb * seq_len
    x_full = x3.reshape(bsz * seq_len, d)
    xs = [x_full, x_full]
    row0 = [0, n_half]
    mods = [mod[:, :hb], mod[:, hb:]]
    hgrn_out, ssm_out, conv_out = [], [], []
    pending = None
    for l in range(depth):
        mod6 = [m[l].reshape(hb, 6, 1, d).transpose(1, 0, 2, 3) for m in mods]
        ssd_prm = dict(bw=bw, layer=l, conv_w=w["ssm_conv_w"][l], conv_b=w["ssm_conv_b"][l][None],
                       dt_bias=w["dt_bias_pad"][l][None], a_log=w["a_log_pad"][l][None],
                       d_skip=w["d_skip"][l][None], gn=w["ssm_onorm_g"][l][None])
        cm_prm = dict(cw=cw, ln_g=w["cmlp_ln_g"][l][None], ln_b=w["cmlp_ln_b"][l][None],
                      ws=w["cmlp_ws"][l], bs_t=w["cmlp_bs"][l].T)
        gn_a = w["hgrn_onorm_g"][l]

        def mix_merge(h, p, gates, y_a, ssd_res=None):
            y_b, s_s, conv_tail = ssd_res or _ssd(p, col, ssd_prm, None, None, hb, seq_len)
            y_c = _cmlp(p, col["u"], col["v"], cm_prm, hb, seq_len, False)[0]
            x_new = _merge_out(y_a, y_b, y_c, w["w_branch"], w["w_out"], l, gates, xs[h], mod6[h], 2, seq_len,
                               row0[h])
            row0[h] = 0
            return x_new, s_s, conv_tail

        p0, g0 = _in_proj(xs[0], w["norm1_g"][l], mod6[0], w["w_in"], l, seq_len, col["gate"], (row0[0], n_half))
        if pending is None:
            ya0, sh0 = _hgrn(p0, w["lbc"], l, gn_a, None, hb, seq_len, heads_a)
        else:
            hid1, x1_mid, mod6_prev = pending
            xs[1], ya0, sh0 = _mlp_down_hgrn(hid1, w["w_down"], l - 1, x1_mid, mod6_prev, 5, seq_len,
                                              p0, w["lbc"], l, gn_a, heads_a)
        p1, g1 = _in_proj(xs[1], w["norm1_g"][l], mod6[1], w["w_in"], l, seq_len, col["gate"], (row0[1], n_half))
        x0, ss0, ct0 = mix_merge(0, p0, g0, ya0)
        hid0, ya1, sh1 = _mlp_up_hgrn(x0, w["norm2_g"][l], mod6[0], w["w_up"], l, seq_len,
                                      p1, w["lbc"], gn_a, heads_a)
        xs[0], *ssd1 = _mlp_down_ssd(hid0, w["w_down"], l, x0, mod6[0], 5, seq_len, p1, col, ssd_prm)
        x1_mid, ss1, ct1 = mix_merge(1, p1, g1, ya1, tuple(ssd1))
        hid1 = _mlp_up(x1_mid, w["norm2_g"][l], mod6[1], w["w_up"], l, seq_len)
        pending = (hid1, x1_mid, mod6[1])

        hgrn_out.append(jnp.concatenate([sh0, sh1], axis=0))
        ssm_out.append(jnp.concatenate([ss0, ss1], axis=0))
        conv_out.append(jnp.concatenate([ct0, ct1], axis=0)[:, SUBLANES - (CONV_W - 1):, :])
    hid1, x1_mid, mod6_prev = pending
    xs[1] = _proj_residual(hid1, w["w_down"], depth - 1, x1_mid, mod6_prev, 5, seq_len, "mlp_down")
    y = _final_norm_pair(xs[0], xs[1], w["final_g"]).reshape(bsz, seq_len, d)
    return (y, jnp.stack(hgrn_out), jnp.stack(ssm_out), jnp.stack(conv_out), None)


def kernel(x_prompt, x_sample, state_hgrn, state_ssm, state_conv, c_prompt, c_sample, norm1_g, norm2_g,
           w_mod, b_mod, w_in, hgrn_lb, hgrn_onorm_g, ssm_conv_w, ssm_conv_b, ssm_dt_bias, ssm_a_log, ssm_d,
           ssm_onorm_g, cmlp_ln_g, cmlp_ln_b, cmlp_ws, cmlp_bs, w_branch, w_out, w_up, w_down, final_g):
    d = x_prompt.shape[-1]
    depth = w_in.shape[0]
    aw = bw = cw = d // 2
    n_bc = B_GROUPS * B_DSTATE
    heads_b = bw // B_HEADDIM
    assert heads_b <= LANES and bw % LANES == 0 and DT_PAD % LANES == 0

    o_dt = 4 * aw + bw + bw + 2 * n_bc
    o_u = o_dt + heads_b
    w_in_t = jnp.swapaxes(w_in, 1, 2).astype(BF16)
    w_in_r = jnp.concatenate(
        [w_in_t[:, :o_u], jnp.zeros((depth, DT_PAD - heads_b, d), BF16), w_in_t[:, o_u:]], axis=1)
    cols = dict(z=4 * aw, xs=4 * aw + bw, bc=4 * aw + 2 * bw, dt=o_dt)
    cols["u"] = o_dt + DT_PAD
    cols["v"] = cols["u"] + cw
    cols["gate"] = cols["v"] + cw
    assert cols["gate"] % IN_TILE == 0 and w_in_r.shape[1] % IN_TILE == 0

    pad_h = lambda a: jnp.pad(a.astype(F32), ((0, 0), (0, LANES - heads_b)))
    w = dict(
        cols=cols, w_in=w_in_r,
        norm1_g=norm1_g, norm2_g=norm2_g, final_g=final_g,
        lbc=_lb_consts(hgrn_lb), hgrn_onorm_g=hgrn_onorm_g,
        ssm_conv_w=ssm_conv_w, ssm_conv_b=ssm_conv_b,
        dt_bias_pad=pad_h(ssm_dt_bias), a_log_pad=pad_h(ssm_a_log),
        d_skip=jnp.repeat(ssm_d.astype(F32), B_HEADDIM, axis=1), ssm_onorm_g=ssm_onorm_g,
        cmlp_ln_g=cmlp_ln_g, cmlp_ln_b=cmlp_ln_b, cmlp_ws=cmlp_ws, cmlp_bs=cmlp_bs,
        w_branch=w_branch.astype(BF16).reshape(depth, N_BRANCH, aw, d), w_out=w_out.astype(BF16),
        w_up=w_up.astype(BF16), w_down=w_down.astype(BF16),
    )

    nb = x_prompt.shape[0]
    mod = _modulation(jnp.concatenate([c_prompt, c_sample], axis=0), w_mod, b_mod)
    if nb % 2 == 0:
        y_p, hgrn_p, ssm_p, conv_p, _ = _run_trunk_halves(x_prompt, mod[:, :nb], w)
    else:
        y_p, hgrn_p, ssm_p, conv_p, _ = _run_trunk(x_prompt, mod[:, :nb], None, None, None, False, w)
    conv_pad = jnp.pad(state_conv, ((0, 0), (0, 0), (SUBLANES - (CONV_W - 1), 0), (0, 0)))
    y_s, hgrn_s, ssm_s, conv_s, v_s = _run_trunk(x_sample, mod[:, nb:], state_hgrn, state_ssm, conv_pad, True, w)
    return (y_p, y_s, hgrn_p, ssm_p, conv_p, hgrn_s, ssm_s, conv_s, v_s)
```

```python
import functools
import math

import numpy as np
import jax
import jax.numpy as jnp
from jax import lax
from jax.experimental import pallas as pl
from jax.experimental.pallas import tpu as pltpu

F32 = jnp.float32
BF16 = jnp.bfloat16

A_DK = 128
B_HEADDIM = 64
B_GROUPS = 2
B_DSTATE = 128
CONV_W = 4
C_GROUPS = 4
CMLP_CHUNK = 128
N_BRANCH = 3
SCAN_CHUNK = 64
NORM_EPS = 1e-6
LB_FLOOR = 1e-30
LOG2E = 1.4426950408889634

LANES = 128
SUBLANES = 8
VMEM_LIMIT = 56 * 1024 * 1024
DIAG = SUBLANES
DT_PAD = 512
ROW_TILE = 1024
COL_TILE = 1024
IN_TILE = 1536
MERGE_TILE = 512
MERGE_ROW_TILE = 1024
UP_TILE = 2048
OUT_TILE = 512
OUT_ROW_TILE = 2048
DOWN_TILE = 256
EPILOGUE_COLS = 256
SSD_ROWS = 128
DOWN_K_SPLIT = 16
NORM_ROWS = 128


def _cparams(sem):
    return pltpu.CompilerParams(dimension_semantics=sem, vmem_limit_bytes=VMEM_LIMIT)


def _split3(x):
    hi = x.astype(BF16)
    r1 = x - hi.astype(F32)
    mid = r1.astype(BF16)
    lo = (r1 - mid.astype(F32)).astype(BF16)
    return hi, mid, lo


def _dot3(sel, x):
    hi, mid, lo = _split3(x)
    d = lambda p: jnp.dot(sel, p, preferred_element_type=F32)
    return d(hi) + d(mid) + d(lo)


def _dot3_rhs(x, sel):
    hi, mid, lo = _split3(x)
    d = lambda p: jnp.dot(p, sel, preferred_element_type=F32)
    return d(hi) + d(mid) + d(lo)


def _silu(x):
    return x / (1.0 + jnp.exp(-x))


def _softplus(x):
    return jnp.maximum(x, 0.0) + jnp.log1p(jnp.exp(-jnp.abs(x)))


def _gelu(x):
    return 0.5 * x * (1.0 + lax.erf(x * (1.0 / math.sqrt(2.0))))


def _lb_kernel(lb_ref, out_ref, *, depth):
    x = lb_ref[...]
    m = jnp.max(x, axis=0, keepdims=True)
    e = jnp.exp(x - m)
    p = e / jnp.sum(e, axis=0, keepdims=True)
    acc = jnp.zeros_like(p[0:1])
    zeros5 = jnp.zeros((SUBLANES - 3, x.shape[1]), F32)
    for l in range(depth):
        acc = acc + p[l:l + 1]
        lb = acc - p[0:1]
        out_ref[l] = jnp.concatenate(
            [jnp.log(jnp.maximum(lb, LB_FLOOR)), jnp.log1p(-lb), 1.0 - lb, zeros5], axis=0)


def _lb_consts(hgrn_lb):
    depth, aw = hgrn_lb.shape
    return pl.pallas_call(
        functools.partial(_lb_kernel, depth=depth),
        out_shape=jax.ShapeDtypeStruct((depth, SUBLANES, aw), F32),
        name="hgrn_lb",
    )(hgrn_lb.astype(F32))


def _mod_kernel(c_ref, w_ref, b_ref, o_ref):
    cs = _silu(c_ref[...]).astype(BF16)
    o_ref[...] = jnp.dot(cs, w_ref[...].astype(BF16), preferred_element_type=F32) + b_ref[...]


def _modulation(c_all, w_mod, b_mod):
    depth, d, n6 = w_mod.shape
    s = c_all.shape[0]
    tn = 1024
    return pl.pallas_call(
        _mod_kernel,
        grid=(depth, n6 // tn),
        in_specs=[
            pl.BlockSpec((s, d), lambda l, j: (0, 0)),
            pl.BlockSpec((None, d, tn), lambda l, j: (l, 0, j)),
            pl.BlockSpec((None, 1, tn), lambda l, j: (l, 0, j)),
        ],
        out_specs=pl.BlockSpec((None, s, tn), lambda l, j: (l, 0, j)),
        out_shape=jax.ShapeDtypeStruct((depth, s, n6), F32),
        compiler_params=_cparams(("arbitrary", "arbitrary")),
        name="adaln_mod",
    )(c_all, w_mod, b_mod.reshape(depth, 1, n6))


def _row_tiling(n_tok, seq_len, tm_max):
    tm = min(tm_max, n_tok)
    if seq_len >= tm:
        assert seq_len % tm == 0
        return tm, 1, seq_len // tm
    assert tm % seq_len == 0
    return tm, tm // seq_len, 1


def _per_seq(x, nseg):
    tm, d = x.shape
    return x.reshape(nseg, tm // nseg, d)


def _final_norm_kernel(x_ref, g_ref, o_ref):
    x = x_ref[...]
    o_ref[...] = x * lax.rsqrt(jnp.mean(x * x, axis=-1, keepdims=True) + NORM_EPS) * g_ref[...]


def _final_norm(x, g):
    n, d = x.shape
    tm = min(512, n)
    return pl.pallas_call(
        _final_norm_kernel,
        grid=(n // tm,),
        in_specs=[pl.BlockSpec((tm, d), lambda i: (i, 0)), pl.BlockSpec((1, d), lambda i: (0, 0))],
        out_specs=pl.BlockSpec((tm, d), lambda i: (i, 0)),
        out_shape=jax.ShapeDtypeStruct((n, d), F32),
        compiler_params=_cparams(("arbitrary",)),
        name="final_norm",
    )(x, g.reshape(1, d))


def _final_norm_pair_kernel(xa_ref, xb_ref, g_ref, o_ref, *, nt):
    def norm(x):
        return x * lax.rsqrt(jnp.mean(x * x, axis=-1, keepdims=True) + NORM_EPS) * g_ref[...]

    @pl.when(pl.program_id(0) < nt)
    def _():
        o_ref[...] = norm(xa_ref[...])

    @pl.when(pl.program_id(0) >= nt)
    def _():
        o_ref[...] = norm(xb_ref[...])


def _final_norm_pair(xa, xb, g):
    n, d = xa.shape
    tm = min(512, n)
    nt = n // tm
    return pl.pallas_call(
        functools.partial(_final_norm_pair_kernel, nt=nt),
        grid=(2 * nt,),
        in_specs=[pl.BlockSpec((tm, d), lambda i: (jnp.minimum(i, nt - 1), 0)),
                  pl.BlockSpec((tm, d), lambda i: (jnp.maximum(i - nt, 0), 0)),
                  pl.BlockSpec((1, d), lambda i: (0, 0))],
        out_specs=pl.BlockSpec((tm, d), lambda i: (i, 0)),
        out_shape=jax.ShapeDtypeStruct((2 * n, d), F32),
        compiler_params=_cparams(("arbitrary",)),
        name="final_norm",
    )(xa, xb, g.reshape(1, d))


def _norm_mod_rows(x_ref, g_ref, sc_ref, sh_ref, h_scr, nseg):
    tm = x_ref.shape[0]
    seg = tm // nseg
    rb = min(seg, NORM_ROWS)
    g = g_ref[...]

    def body(r, carry):
        rows = pl.ds(pl.multiple_of(r * rb, rb), rb)
        s = r // (seg // rb)
        x = x_ref[rows, :]
        gain = g * (1.0 + sc_ref[s])
        inv = lax.rsqrt(jnp.mean(x * x, axis=-1, keepdims=True) + NORM_EPS)
        h_scr[rows, :] = (x * inv * gain + sh_ref[s]).astype(BF16)
        return carry

    lax.fori_loop(0, tm // rb, body, 0, unroll=2 if (tm // rb) % 2 == 0 else 1)


def _mlp_up_kernel(x_ref, g_ref, sc_ref, sh_ref, w_ref, o_ref, h_scr, *, nseg):
    @pl.when(pl.program_id(1) == 0)
    def _():
        _norm_mod_rows(x_ref, g_ref, sc_ref, sh_ref, h_scr, nseg)

    y = jnp.maximum(jnp.dot(h_scr[...], w_ref[...], preferred_element_type=F32), 0.0)
    o_ref[...] = (y * y).astype(o_ref.dtype)


def _in_proj_kernel(x_ref, g_ref, sc_ref, sh_ref, w_ref, p_ref, gate_ref, h_scr, *, nseg, n_main):
    j = pl.program_id(1)

    @pl.when(j == 0)
    def _():
        _norm_mod_rows(x_ref, g_ref, sc_ref, sh_ref, h_scr, nseg)

    def project():
        return lax.dot_general(h_scr[...], w_ref[...], (((1,), (1,)), ((), ())), preferred_element_type=F32)

    @pl.when(j < n_main)
    def _():
        p_ref[...] = project()

    @pl.when(j >= n_main)
    def _():
        gate_ref[...] = project().astype(gate_ref.dtype)


def _norm_proj_call(kern, x, g, mod6, k_scale, k_shift, w_stack, layer, seq_len, tn, out_specs, out_shape, name):
    n, d = x.shape
    m = w_stack.shape[2]
    tm, nseg, tps = _row_tiling(n, seq_len, ROW_TILE)

    def mod_spec(k):
        return pl.BlockSpec((None, nseg, 1, d), lambda i, j: (k, i // tps, 0, 0))

    return pl.pallas_call(
        functools.partial(kern, nseg=nseg),
        grid=(n // tm, m // tn),
        in_specs=[
            pl.BlockSpec((tm, d), lambda i, j: (i, 0)),
            pl.BlockSpec((1, d), lambda i, j: (0, 0)),
            mod_spec(k_scale),
            mod_spec(k_shift),
            pl.BlockSpec((None, d, tn), lambda i, j: (layer, 0, j)),
        ],
        out_specs=out_specs(tm),
        out_shape=out_shape,
        scratch_shapes=[pltpu.VMEM((tm, d), BF16)],
        compiler_params=_cparams(("arbitrary", "arbitrary")),
        name=name,
    )(x, g.reshape(1, d), mod6, mod6, w_stack)


def _mlp_up(x, g, mod6, w_stack, layer, seq_len):
    n = x.shape[0]
    m = w_stack.shape[2]
    tn = UP_TILE
    return _norm_proj_call(
        _mlp_up_kernel, x, g, mod6, 4, 3, w_stack, layer, seq_len, tn,
        lambda tm: pl.BlockSpec((tm, tn), lambda i, j: (i, j)),
        jax.ShapeDtypeStruct((n, m), BF16), "mlp_up")


def _in_proj(x, g, mod6, w_stack, layer, seq_len, main_cols, rows=None):
    d = x.shape[1]
    row0, n = rows if rows is not None else (0, x.shape[0])
    m = w_stack.shape[1]
    tn = IN_TILE
    n_main = main_cols // tn
    tm, nseg, tps = _row_tiling(n, seq_len, ROW_TILE)

    def mod_spec(k):
        return pl.BlockSpec((None, nseg, 1, d), lambda i, j: (k, i // tps, 0, 0))

    return pl.pallas_call(
        functools.partial(_in_proj_kernel, nseg=nseg, n_main=n_main),
        grid=(n // tm, m // tn),
        in_specs=[
            pl.BlockSpec((tm, d), lambda i, j: (i + row0 // tm, 0)),
            pl.BlockSpec((1, d), lambda i, j: (0, 0)),
            mod_spec(1),
            mod_spec(0),
            pl.BlockSpec((None, tn, d), lambda i, j: (layer, j, 0)),
        ],
        out_specs=[pl.BlockSpec((tm, tn), lambda i, j: (i, jnp.minimum(j, n_main - 1))),
                   pl.BlockSpec((tm, tn), lambda i, j: (i, jnp.maximum(j - n_main, 0)))],
        out_shape=[jax.ShapeDtypeStruct((n, main_cols), F32), jax.ShapeDtypeStruct((n, m - main_cols), BF16)],
        scratch_shapes=[pltpu.VMEM((tm, d), BF16)],
        compiler_params=_cparams(("arbitrary", "arbitrary")),
        name="in_proj",
    )(x, g.reshape(1, d), mod6, mod6, w_stack)


def _proj_res_kernel(a_ref, w_ref, x_ref, gate_ref, o_ref, *, nseg):
    acc = jnp.dot(a_ref[...], w_ref[...], preferred_element_type=F32)
    x = x_ref[...]
    o_ref[...] = (_per_seq(x, nseg) + gate_ref[...] * _per_seq(acc, nseg)).reshape(x.shape)


def _proj_residual(a, w_stack, layer, x, mod6, k_gate, seq_len, name, x_row0=0):
    n, kdim = a.shape
    d = w_stack.shape[2]
    tn, tm_max = (OUT_TILE, OUT_ROW_TILE) if kdim <= d else (DOWN_TILE, ROW_TILE)
    tm, nseg, tps = _row_tiling(n, seq_len, tm_max)
    return pl.pallas_call(
        functools.partial(_proj_res_kernel, nseg=nseg),
        grid=(n // tm, d // tn),
        in_specs=[
            pl.BlockSpec((tm, kdim), lambda i, j: (i, 0)),
            pl.BlockSpec((None, kdim, tn), lambda i, j: (layer, 0, j)),
            pl.BlockSpec((tm, tn), lambda i, j: (i + x_row0 // tm, j)),
            pl.BlockSpec((None, nseg, 1, tn), lambda i, j: (k_gate, i // tps, 0, j)),
        ],
        out_specs=pl.BlockSpec((tm, tn), lambda i, j: (i, j)),
        out_shape=jax.ShapeDtypeStruct((n, d), F32),
        compiler_params=_cparams(("arbitrary", "arbitrary")),
        name=name,
    )(a, w_stack, x, mod6)


def _merge_kernel(ya_ref, yb_ref, yc_ref, wa_ref, wb_ref, wc_ref, ga_ref, gb_ref, gc_ref, o_ref):
    acc = None
    for y_ref, w_ref, g_ref in ((ya_ref, wa_ref, ga_ref), (yb_ref, wb_ref, gb_ref), (yc_ref, wc_ref, gc_ref)):
        gate = 1.0 / (1.0 + jnp.exp(-g_ref[...].astype(F32)))
        term = gate * jnp.dot(y_ref[...], w_ref[...], preferred_element_type=F32)
        acc = term if acc is None else acc + term
    o_ref[...] = acc.astype(o_ref.dtype)


def _merge(ya, yb, yc, w_stack, layer, gates, d):
    n, bw = ya.shape
    tn = MERGE_TILE
    tm = min(MERGE_ROW_TILE, n)
    gsteps = d // tn
    y_spec = pl.BlockSpec((tm, bw), lambda i, j: (i, 0))

    def w_spec(k):
        return pl.BlockSpec((None, None, bw, tn), lambda i, j: (layer, k, 0, j))

    def g_spec(k):
        return pl.BlockSpec((tm, tn), lambda i, j: (i, k * gsteps + j))

    return pl.pallas_call(
        _merge_kernel,
        grid=(n // tm, d // tn),
        in_specs=[y_spec, y_spec, y_spec, w_spec(0), w_spec(1), w_spec(2), g_spec(0), g_spec(1), g_spec(2)],
        out_specs=pl.BlockSpec((tm, tn), lambda i, j: (i, j)),
        out_shape=jax.ShapeDtypeStruct((n, d), BF16),
        compiler_params=_cparams(("arbitrary", "arbitrary")),
        name="merge",
    )(ya, yb, yc, w_stack, w_stack, w_stack, gates, gates, gates)


def _hgrn_levels(c):
    lv, m = [], c // 2
    while m >= DIAG:
        lv.append(m)
        m //= 2
    return tuple(lv)


def _hgrn_select(c):
    tri = np.arange(c)[None, :] <= np.arange(c)[:, None]
    return jnp.asarray(tri.astype(np.float32), dtype=BF16)


def _hgrn_kernel(*refs, c, n_chunks, heads, has_init):
    if has_init:
        q_ref, f_ref, i_ref, g_ref, lbc_ref, gn_ref, sel_ref, s0_ref, ya_ref, s_ref, bpad, kpad = refs
    else:
        q_ref, f_ref, i_ref, g_ref, lbc_ref, gn_ref, sel_ref, ya_ref, s_ref, bpad, kpad = refs

    @pl.when(pl.program_id(1) == 0)
    def _():
        s_ref[...] = s0_ref[...] if has_init else jnp.zeros(s_ref.shape, F32)

    _hgrn_chunks(q_ref, f_ref, i_ref, g_ref, lbc_ref, gn_ref, sel_ref, ya_ref, s_ref, bpad, kpad,
                 c=c, n_chunks=n_chunks, heads=heads, unroll=False)


def _hgrn_chunks(q_ref, f_ref, i_ref, g_ref, lbc_ref, gn_ref, sel_ref, ya_ref, s_ref, bpad, kpad,
                 *, c, n_chunks, heads, unroll, fillers=()):
    levels = _hgrn_levels(c)
    nl = len(levels)
    width = heads * A_DK
    head_lanes = [slice(hh * A_DK, (hh + 1) * A_DK) for hh in range(heads)]

    bpad[0:DIAG, :] = jnp.zeros((DIAG, width), F32)
    kpad[0:DIAG, :] = jnp.zeros((DIAG, width), F32)

    sel = sel_ref[...]
    log_lb = lbc_ref[0:1, :]
    log1m_lb = lbc_ref[1:2, :]
    one_m_lb = lbc_ref[2:3, :]
    row = lax.broadcasted_iota(jnp.int32, (c, width), 0)
    rr = lax.broadcasted_iota(jnp.int32, (c, c), 0)
    cc = lax.broadcasted_iota(jnp.int32, (c, c), 1)
    upper = [(row & m) != 0 for m in levels]
    same = [(rr ^ cc) < 2 * m for m in levels]
    dmat = jnp.where(((rr ^ cc) < DIAG) & (cc <= rr), rr - cc, -1)
    nt_dims = (((1,), (1,)), ((), ()))
    tn_dims = (((0,), (0,)), ((), ()))

    def chunk(ci):
        rows = slice(ci * c, (ci + 1) * c) if unroll else pl.ds(pl.multiple_of(ci * c, c), c)
        z = f_ref[rows, :]
        aq = q_ref[rows, :]

        log_sig = jnp.minimum(z, 0.0) - jnp.log(1.0 + jnp.exp(-jnp.abs(z)))
        bb = log1m_lb + log_sig
        log_f = jnp.maximum(log_lb, bb) + jnp.log(1.0 + jnp.exp(-jnp.abs(log_lb - bb)))
        k = one_m_lb / (1.0 + jnp.exp(z))
        q = _silu(aq)

        b = _dot3(sel, log_f * LOG2E)
        bpad[DIAG:DIAG + c, :] = b
        kpad[DIAG:DIAG + c, :] = k
        b_tot = bpad[DIAG + c - 1:DIAG + c, :]
        vb = i_ref[rows, :].astype(BF16)
        q_in = (q * jnp.exp2(b)).astype(BF16)
        k_out = (k * jnp.exp2(b_tot - b)).astype(BF16)
        yield

        att = [jnp.zeros((c, c), F32) for _ in range(heads)]
        for li in range(nl):
            m = levels[li]
            ref_b = jnp.concatenate(
                [jnp.broadcast_to(bpad[DIAG + g0 + m - 1:DIAG + g0 + m, :], (2 * m, width))
                 for g0 in range(0, c, 2 * m)], axis=0)
            e = jnp.exp2(-jnp.abs(b - ref_b))
            qs = jnp.where(upper[li], q * e, 0.0).astype(BF16)
            ks = jnp.where(upper[li], 0.0, k * e).astype(BF16)
            for hh, lanes in enumerate(head_lanes):
                a_l = lax.dot_general(qs[:, lanes], ks[:, lanes], nt_dims, preferred_element_type=F32)
                att[hh] = att[hh] + jnp.where(same[li], a_l, 0.0)
        yield

        for dlt in range(DIAG):
            if dlt == 0:
                w = q * k
            else:
                ksh = kpad[DIAG - dlt:DIAG - dlt + c, :]
                bsh = bpad[DIAG - dlt:DIAG - dlt + c, :]
                w = q * ksh * jnp.exp2(b - bsh)
            for hh, lanes in enumerate(head_lanes):
                col = jnp.sum(w[:, lanes], axis=-1, keepdims=True)
                att[hh] = jnp.where(dmat == dlt, col, att[hh])
            if dlt in (DIAG // 2 - 1, DIAG - 1):
                yield

        decay_row = jnp.exp2(b_tot)
        outs = []
        for hh, lanes in enumerate(head_lanes):
            s_prev = s_ref[0, hh]
            o = jnp.dot(q_in[:, lanes], s_prev.astype(BF16), preferred_element_type=F32)
            o = o + jnp.dot(att[hh].astype(BF16), vb[:, lanes], preferred_element_type=F32)
            decay_col = jnp.broadcast_to(decay_row[:, lanes], (A_DK, A_DK)).T
            s_ref[0, hh] = decay_col * s_prev + lax.dot_general(
                k_out[:, lanes], vb[:, lanes], tn_dims, preferred_element_type=F32)
            outs.append(o * lax.rsqrt(jnp.mean(o * o, axis=-1, keepdims=True) + NORM_EPS))
        y = jnp.concatenate(outs, axis=1) * gn_ref[...] * _silu(g_ref[rows, :])
        ya_ref[rows, :] = y.astype(ya_ref.dtype)
        yield

    if not unroll:
        def body(ci, carry):
            for _ in chunk(ci):
                pass
            return carry

        lax.fori_loop(0, n_chunks, body, 0)
        return

    n_phases = 5
    _emit_interleaved(fillers, [functools.partial(next, gen, None)
                                for gen in map(chunk, range(n_chunks)) for _ in range(n_phases)])


def _hgrn_specs(p, lbc_stack, layer, gn, heads, t_blk, row_block):
    aw = heads * A_DK
    c = min(t_blk, SCAN_CHUNK)
    sel = _hgrn_select(c)

    def col_spec(seg):
        return pl.BlockSpec((t_blk, aw), lambda *g: (row_block(*g), seg))

    in_specs = [col_spec(0), col_spec(1), col_spec(2), col_spec(3),
                pl.BlockSpec((None, SUBLANES, aw), lambda *g: (layer, 0, 0)),
                pl.BlockSpec((1, aw), lambda *g: (0, 0)),
                pl.BlockSpec(sel.shape, lambda *g: (0, 0))]
    args = [p, p, p, p, lbc_stack, gn.reshape(1, aw), sel]
    scratch = [pltpu.VMEM((c + DIAG, aw), F32), pltpu.VMEM((c + DIAG, aw), F32)]
    return c, in_specs, args, pl.BlockSpec((t_blk, aw), lambda *g: (row_block(*g), 0)), scratch


def _mlp_up_hgrn_kernel(x_ref, g_ref, sc_ref, sh_ref, w_ref, q_ref, f_ref, i_ref, gg_ref, lbc_ref, gn_ref, sel_ref,
                        o_ref, ya_ref, s_ref, h_scr, bpad, kpad, *, nseg, c, n_chunks, heads, steps_per_seq):
    j = pl.program_id(1)
    step = pl.program_id(0) * pl.num_programs(1) + j

    @pl.when(j == 0)
    def _():
        _norm_mod_rows(x_ref, g_ref, sc_ref, sh_ref, h_scr, nseg)

    @pl.when(step % steps_per_seq == 0)
    def _():
        s_ref[...] = jnp.zeros(s_ref.shape, F32)

    def sub_dot(c0):
        cols = slice(c0, c0 + EPILOGUE_COLS)
        y = jnp.maximum(jnp.dot(h_scr[...], w_ref[:, cols], preferred_element_type=F32), 0.0)
        o_ref[:, cols] = (y * y).astype(o_ref.dtype)

    fillers = [functools.partial(sub_dot, c0) for c0 in range(0, w_ref.shape[1], EPILOGUE_COLS)]
    _hgrn_chunks(q_ref, f_ref, i_ref, gg_ref, lbc_ref, gn_ref, sel_ref, ya_ref, s_ref, bpad, kpad,
                 c=c, n_chunks=n_chunks, heads=heads, unroll=True, fillers=fillers)


def _mlp_up_hgrn(x, g, mod6, w_stack, layer, seq_len, p_other, lbc_stack, gn, heads):
    n, d = x.shape
    m = w_stack.shape[2]
    tn = COL_TILE
    tm, nseg, tps = _row_tiling(n, seq_len, ROW_TILE)
    ni, nj = n // tm, m // tn
    n_other = p_other.shape[0]
    t_step = n_other // (ni * nj)
    assert t_step * ni * nj == n_other and seq_len % t_step == 0 and t_step % SUBLANES == 0
    steps_per_seq = seq_len // t_step
    aw = heads * A_DK
    c, h_in_specs, h_args, ya_spec, h_scratch = _hgrn_specs(
        p_other, lbc_stack, layer, gn, heads, t_step, lambda i, j: i * nj + j)

    def mod_spec(k):
        return pl.BlockSpec((None, nseg, 1, d), lambda i, j: (k, i // tps, 0, 0))

    state_spec = pl.BlockSpec((1, heads, A_DK, A_DK), lambda i, j: ((i * nj + j) // steps_per_seq, 0, 0, 0))
    return pl.pallas_call(
        functools.partial(_mlp_up_hgrn_kernel, nseg=nseg, c=c, n_chunks=t_step // c, heads=heads,
                          steps_per_seq=steps_per_seq),
        grid=(ni, nj),
        in_specs=[
            pl.BlockSpec((tm, d), lambda i, j: (i, 0)),
            pl.BlockSpec((1, d), lambda i, j: (0, 0)),
            mod_spec(4),
            mod_spec(3),
            pl.BlockSpec((None, d, tn), lambda i, j: (layer, 0, j)),
        ] + h_in_specs,
        out_specs=[pl.BlockSpec((tm, tn), lambda i, j: (i, j)), ya_spec, state_spec],
        out_shape=[jax.ShapeDtypeStruct((n, m), BF16),
                   jax.ShapeDtypeStruct((n_other, aw), BF16),
                   jax.ShapeDtypeStruct((n_other // seq_len, heads, A_DK, A_DK), F32)],
        scratch_shapes=[pltpu.VMEM((tm, d), BF16)] + h_scratch,
        compiler_params=_cparams(("arbitrary", "arbitrary")),
        name="mlp_up_hgrn2",
    )(x, g.reshape(1, d), mod6, mod6, w_stack, *h_args)


def _hgrn(p, lbc_stack, layer, gn, s0, bsz, seq_len, heads):
    c = min(seq_len, SCAN_CHUNK)
    t_blk = min(seq_len, 256)
    nt = seq_len // t_blk
    aw = heads * A_DK

    def col_spec(seg):
        return pl.BlockSpec((t_blk, aw), lambda b, t: (b * nt + t, seg))

    sel = _hgrn_select(c)
    in_specs = [col_spec(0), col_spec(1), col_spec(2), col_spec(3),
                pl.BlockSpec((None, SUBLANES, aw), lambda b, t: (layer, 0, 0)),
                pl.BlockSpec((1, aw), lambda b, t: (0, 0)),
                pl.BlockSpec(sel.shape, lambda b, t: (0, 0))]
    args = [p, p, p, p, lbc_stack, gn.reshape(1, aw), sel]
    state_spec = pl.BlockSpec((1, heads, A_DK, A_DK), lambda b, t: (b, 0, 0, 0))
    if s0 is not None:
        in_specs.append(pl.BlockSpec((None, 1, heads, A_DK, A_DK), lambda b, t: (layer, b, 0, 0, 0)))
        args.append(s0)
    return pl.pallas_call(
        functools.partial(_hgrn_kernel, c=c, n_chunks=t_blk // c, heads=heads, has_init=s0 is not None),
        grid=(bsz, nt),
        in_specs=in_specs,
        out_specs=[pl.BlockSpec((t_blk, aw), lambda b, t: (b * nt + t, 0)), state_spec],
        out_shape=[jax.ShapeDtypeStruct((bsz * seq_len, aw), BF16),
                   jax.ShapeDtypeStruct((bsz, heads, A_DK, A_DK), F32)],
        scratch_shapes=[pltpu.VMEM((c + DIAG, aw), F32), pltpu.VMEM((c + DIAG, aw), F32)],
        compiler_params=_cparams(("arbitrary", "arbitrary")),
        name="hgrn2",
    )(*args)


def _ssd_kernel(*refs, t, bw, has_init):
    n_in = SSD_N_INPUTS
    s0_ref, c0_ref = (refs[n_in], refs[n_in + 1]) if has_init else (None, None)
    rest = refs[n_in + 2:] if has_init else refs[n_in:]
    yb_ref, s_out_ref, conv_out_ref, xpad, st = rest
    ti = pl.program_id(1)
    _ssd_reset(s0_ref, c0_ref, xpad, st, ti == 0)
    for _ in _ssd_phases(*refs[:n_in], yb_ref, xpad, st, t=t, bw=bw):
        pass
    _ssd_flush(conv_out_ref, s_out_ref, xpad, st, ti == pl.num_programs(1) - 1)


SSD_N_INPUTS = 12
SSD_PHASES = 9


def _ssd_phases(z_ref, xs_ref, bc_ref, dt_ref, cw_ref, cb_ref, dtb_ref, alog_ref, dsk_ref, gn_ref, exp_ref, sel_ref,
                yb_ref, xpad, st, *, t, bw):
    gw = bw // B_GROUPS
    n_bc = B_GROUPS * B_DSTATE
    pad = SUBLANES

    cur = jnp.concatenate([xs_ref[...], bc_ref[...]], axis=1)
    prev = xpad[...]
    row8 = lax.broadcasted_iota(jnp.int32, (pad, cur.shape[1]), 0)
    conv = cb_ref[...]
    for j in range(CONV_W):
        s = CONV_W - 1 - j
        if s == 0:
            tap = cur
        else:
            rolled = pltpu.roll(cur, s, 0)
            head = jnp.where(row8 < s, pltpu.roll(prev, s, 0), rolled[0:pad])
            tap = jnp.concatenate([head, rolled[pad:]], axis=0)
        conv = conv + tap * cw_ref[j:j + 1, :]
    xbc = _silu(conv)
    x = xbc[:, 0:bw]
    yield

    dt = _softplus(dt_ref[...] + dtb_ref[...])
    a = dt * (-jnp.exp(alog_ref[...]))
    cs = _dot3(sel_ref[...], a)
    a_cum = cs[0:t]
    a_tot = cs[t:2 * t]
    ex = _dot3_rhs(jnp.concatenate([dt, a_cum, a_tot - a_cum], axis=0), exp_ref[...])
    dt_e = ex[0:t]
    acum_e = ex[t:2 * t]
    dec_e = ex[2 * t:3 * t]
    atot_e = acum_e[t - 1:t, :]

    xdt = x * dt_e
    xw = (xdt * jnp.exp(dec_e)).astype(BF16)
    xdt_b = xdt.astype(BF16)
    a_cum_t = a_cum.T
    rr = lax.broadcasted_iota(jnp.int32, (t, t), 0)
    cc = lax.broadcasted_iota(jnp.int32, (t, t), 1)
    causal = cc <= rr
    lane = lax.broadcasted_iota(jnp.int32, (t, LANES), 1)
    heads_per_group = gw // B_HEADDIM
    pairs_per_group = gw // LANES
    yield

    y_groups = []
    for g in range(B_GROUPS):
        bg = xbc[:, bw + g * B_DSTATE:bw + (g + 1) * B_DSTATE].astype(BF16)
        cg = xbc[:, bw + n_bc + g * B_DSTATE:bw + n_bc + (g + 1) * B_DSTATE].astype(BF16)
        gs = slice(g * gw, (g + 1) * gw)
        st_g = st[:, gs]
        scores = lax.dot_general(cg, bg, (((1,), (1,)), ((), ())), preferred_element_type=F32)
        y_off = jnp.dot(cg, st_g.astype(BF16), preferred_element_type=F32) * jnp.exp(acum_e[:, gs])
        st[:, gs] = jnp.exp(atot_e[:, gs]) * st_g + lax.dot_general(
            bg, xw[:, gs], (((0,), (0,)), ((), ())), preferred_element_type=F32)
        yield
        y_pairs = []
        for pr in range(pairs_per_group):
            cols = slice(g * gw + pr * LANES, g * gw + (pr + 1) * LANES)
            xp = xdt_b[:, cols]
            ms, xs_blocks = [], []
            for half in range(LANES // B_HEADDIM):
                h = g * heads_per_group + pr * (LANES // B_HEADDIM) + half
                diff = a_cum[:, h:h + 1] - a_cum_t[h:h + 1, :]
                lmat = jnp.where(causal, jnp.exp(jnp.where(causal, diff, 0.0)), 0.0)
                ms.append((scores * lmat).astype(BF16))
                in_half = (lane // B_HEADDIM) == half
                xs_blocks.append(jnp.where(in_half, xp, jnp.zeros_like(xp)))
            y_pairs.append(jnp.dot(jnp.concatenate(ms, axis=1), jnp.concatenate(xs_blocks, axis=0),
                                   preferred_element_type=F32))
            if pr % 2 == 1:
                yield
        y_groups.append(jnp.concatenate(y_pairs, axis=1) + y_off)
    y = jnp.concatenate(y_groups, axis=1) + dsk_ref[...] * x
    y = y * _silu(z_ref[...])
    outs = []
    for g in range(B_GROUPS):
        yg = y[:, g * gw:(g + 1) * gw]
        outs.append(yg * lax.rsqrt(jnp.mean(yg * yg, axis=-1, keepdims=True) + NORM_EPS))
    yb_ref[...] = (jnp.concatenate(outs, axis=1) * gn_ref[...]).astype(yb_ref.dtype)

    xpad[...] = cur[t - pad:t, :]
    yield


def _ssd_reset(s0_ref, c0_ref, xpad, st, first):
    @pl.when(first)
    def _():
        if s0_ref is not None:
            xpad[...] = c0_ref[0]
            st[...] = s0_ref[0].reshape(st.shape[1], st.shape[0]).T
        else:
            xpad[...] = jnp.zeros(xpad.shape, F32)
            st[...] = jnp.zeros(st.shape, F32)


def _ssd_flush(conv_out_ref, s_out_ref, xpad, st, last):
    @pl.when(last)
    def _():
        conv_out_ref[0] = xpad[...]
        s_out_ref[0] = st[...].T.reshape(s_out_ref.shape[1:])


def _ssd_operands(p, cols, prm, t, n_rows, n_seq, row_block, seq_block):
    bw = prm["bw"]
    heads = bw // B_HEADDIM
    n_bc = B_GROUPS * B_DSTATE
    cdim = bw + 2 * n_bc
    tri = np.arange(t)[None, :] <= np.arange(t)[:, None]
    sel = jnp.asarray(np.concatenate([tri, np.ones((t, t), bool)], 0).astype(np.float32), dtype=BF16)
    expand = np.zeros((LANES, bw), np.float32)
    expand[np.arange(bw) // B_HEADDIM, np.arange(bw)] = 1.0
    expand = jnp.asarray(expand, dtype=BF16)

    def blk(width, off):
        return pl.BlockSpec((t, width), lambda *g: (row_block(*g), off // width))

    def full(shape):
        return pl.BlockSpec(shape, lambda *g: (0,) * len(shape))

    in_specs = [blk(bw, cols["z"]), blk(bw, cols["xs"]), blk(2 * n_bc, cols["bc"]), blk(LANES, cols["dt"]),
                full((CONV_W, cdim)), full((1, cdim)), full((1, LANES)), full((1, LANES)),
                full((1, bw)), full((1, bw)), full(expand.shape), full(sel.shape)]
    args = [p, p, p, p, prm["conv_w"], prm["conv_b"], prm["dt_bias"], prm["a_log"], prm["d_skip"], prm["gn"],
            expand, sel]
    assert len(args) == SSD_N_INPUTS
    out_specs = [pl.BlockSpec((t, bw), lambda *g: (row_block(*g), 0)),
                 pl.BlockSpec((1, heads, B_HEADDIM, B_DSTATE), lambda *g: (seq_block(*g), 0, 0, 0)),
                 pl.BlockSpec((1, SUBLANES, cdim), lambda *g: (seq_block(*g), 0, 0))]
    out_shape = [jax.ShapeDtypeStruct((n_rows, bw), BF16),
                 jax.ShapeDtypeStruct((n_seq, heads, B_HEADDIM, B_DSTATE), F32),
                 jax.ShapeDtypeStruct((n_seq, SUBLANES, cdim), F32)]
    scratch = [pltpu.VMEM((SUBLANES, cdim), F32), pltpu.VMEM((B_DSTATE, bw), F32)]
    return in_specs, args, out_specs, out_shape, scratch


def _ssd(p, cols, prm, s0, c0, bsz, seq_len):
    bw = prm["bw"]
    heads = bw // B_HEADDIM
    cdim = bw + 2 * B_GROUPS * B_DSTATE
    t = min(seq_len, SSD_ROWS)
    nt = seq_len // t
    in_specs, args, out_specs, out_shape, scratch = _ssd_operands(
        p, cols, prm, t, bsz * seq_len, bsz, lambda b, ti: b * nt + ti, lambda b, ti: b)
    if s0 is not None:
        layer = prm["layer"]
        in_specs += [pl.BlockSpec((None, 1, heads, B_HEADDIM, B_DSTATE), lambda b, ti: (layer, b, 0, 0, 0)),
                     pl.BlockSpec((None, 1, SUBLANES, cdim), lambda b, ti: (layer, b, 0, 0))]
        args += [s0, c0]
    return pl.pallas_call(
        functools.partial(_ssd_kernel, t=t, bw=bw, has_init=s0 is not None),
        grid=(bsz, nt),
        in_specs=in_specs,
        out_specs=out_specs,
        out_shape=out_shape,
        scratch_shapes=scratch,
        compiler_params=_cparams(("arbitrary", "arbitrary")),
        name="ssd",
    )(*args)


def _down_sub_dots(a_ref, w_ref, x_ref, gate_ref, o_ref, acc_scr, nseg):
    kc = a_ref.shape[1] // DOWN_K_SPLIT

    def sub_dot(ki):
        rows = slice(ki * kc, (ki + 1) * kc)
        part = jnp.dot(a_ref[:, rows], w_ref[rows, :], preferred_element_type=F32)
        if ki == 0:
            acc_scr[...] = part
        elif ki < DOWN_K_SPLIT - 1:
            acc_scr[...] += part
        else:
            x = x_ref[...]
            acc = acc_scr[...] + part
            o_ref[...] = (_per_seq(x, nseg) + gate_ref[...] * _per_seq(acc, nseg)).reshape(x.shape)

    return [functools.partial(sub_dot, ki) for ki in range(DOWN_K_SPLIT)]


def _emit_interleaved(fillers, phases):
    per_filler = -(-len(phases) // max(len(fillers), 1))
    pos = 0
    for fill in fillers:
        fill()
        for ph in phases[pos:pos + per_filler]:
            ph()
        pos += per_filler
    for ph in phases[pos:]:
        ph()


def _mlp_down_ssd_kernel(a_ref, w_ref, x_ref, gate_ref, *refs, nseg, t, bw, steps_per_seq):
    ssd_in = refs[:SSD_N_INPUTS]
    o_ref, yb_ref, s_out_ref, conv_out_ref, acc_scr, xpad, st = refs[SSD_N_INPUTS:]
    step = pl.program_id(0) * pl.num_programs(1) + pl.program_id(1)
    pos = step % steps_per_seq
    _ssd_reset(None, None, xpad, st, pos == 0)
    gen = _ssd_phases(*ssd_in, yb_ref, xpad, st, t=t, bw=bw)
    phases = [functools.partial(next, gen, None) for _ in range(SSD_PHASES)]
    _emit_interleaved(_down_sub_dots(a_ref, w_ref, x_ref, gate_ref, o_ref, acc_scr, nseg), phases)
    _ssd_flush(conv_out_ref, s_out_ref, xpad, st, pos == steps_per_seq - 1)


def _mlp_down_ssd(a, w_stack, layer, x, mod6, k_gate, seq_len, p_other, cols, prm):
    n, kdim = a.shape
    d = w_stack.shape[2]
    tn = DOWN_TILE
    tm, nseg, tps = _row_tiling(n, seq_len, ROW_TILE)
    ni, nj = n // tm, d // tn
    n_other = p_other.shape[0]
    t = n_other // (ni * nj)
    assert t * ni * nj == n_other and seq_len % t == 0 and t % SUBLANES == 0 and kdim % DOWN_K_SPLIT == 0
    steps_per_seq = seq_len // t
    s_in_specs, s_args, s_out_specs, s_out_shape, s_scratch = _ssd_operands(
        p_other, cols, prm, t, n_other, n_other // seq_len,
        lambda i, j: i * nj + j, lambda i, j: (i * nj + j) // steps_per_seq)
    return pl.pallas_call(
        functools.partial(_mlp_down_ssd_kernel, nseg=nseg, t=t, bw=prm["bw"], steps_per_seq=steps_per_seq),
        grid=(ni, nj),
        in_specs=[
            pl.BlockSpec((tm, kdim), lambda i, j: (i, 0)),
            pl.BlockSpec((None, kdim, tn), lambda i, j: (layer, 0, j)),
            pl.BlockSpec((tm, tn), lambda i, j: (i, j)),
            pl.BlockSpec((None, nseg, 1, tn), lambda i, j: (k_gate, i // tps, 0, j)),
        ] + s_in_specs,
        out_specs=[pl.BlockSpec((tm, tn), lambda i, j: (i, j))] + s_out_specs,
        out_shape=[jax.ShapeDtypeStruct((n, d), F32)] + s_out_shape,
        scratch_shapes=[pltpu.VMEM((tm, tn), F32)] + s_scratch,
        compiler_params=_cparams(("arbitrary", "arbitrary")),
        name="mlp_down_ssd",
    )(a, w_stack, x, mod6, *s_args)


def _cmlp_kernel(u_ref, v_ref, lng_ref, lnb_ref, ws_ref, bst_ref, *out_refs, t, n_chunks, keep_v):
    yc_ref = out_refs[0]
    cw = u_ref.shape[1] // C_GROUPS
    rr = lax.broadcasted_iota(jnp.int32, (t, t), 0)
    cc = lax.broadcasted_iota(jnp.int32, (t, t), 1)
    wts = [jnp.where(cc <= rr, ws_ref[g, 0:t, 0:t], 0.0).astype(BF16) for g in range(C_GROUPS)]
    for ci in range(n_chunks):
        rows = slice(ci * t, (ci + 1) * t)
        u = _gelu(u_ref[rows, :])
        gv = _gelu(v_ref[rows, :])
        mu = jnp.mean(gv, axis=-1, keepdims=True)
        dv = gv - mu
        var = jnp.mean(dv * dv, axis=-1, keepdims=True)
        v = dv * lax.rsqrt(var + NORM_EPS) * lng_ref[...] + lnb_ref[...]
        if keep_v:
            out_refs[1][rows, :] = v
        vb = v.astype(BF16)
        for g in range(C_GROUPS):
            lanes = slice(g * cw, (g + 1) * cw)
            mixed = jnp.dot(wts[g], vb[:, lanes], preferred_element_type=F32) + bst_ref[0:t, g:g + 1]
            yc_ref[rows, lanes] = (u[:, lanes] * mixed).astype(yc_ref.dtype)


def _cmlp(p, col_u, col_v, prm, bsz, seq_len, keep_v):
    cw = prm["cw"]
    t = min(seq_len, CMLP_CHUNK)
    t_blk = min(seq_len, 4 * CMLP_CHUNK)
    n = bsz * seq_len

    def full(shape):
        return pl.BlockSpec(shape, lambda i: (0,) * len(shape))

    out_specs = [pl.BlockSpec((t_blk, cw), lambda i: (i, 0))]
    out_shape = [jax.ShapeDtypeStruct((n, cw), BF16)]
    if keep_v:
        out_specs.append(pl.BlockSpec((t_blk, cw), lambda i: (i, 0)))
        out_shape.append(jax.ShapeDtypeStruct((n, cw), F32))
    return pl.pallas_call(
        functools.partial(_cmlp_kernel, t=t, n_chunks=t_blk // t, keep_v=keep_v),
        grid=(n // t_blk,),
        in_specs=[pl.BlockSpec((t_blk, cw), lambda i: (i, col_u // cw)),
                  pl.BlockSpec((t_blk, cw), lambda i: (i, col_v // cw)),
                  full((1, cw)), full((1, cw)),
                  full((C_GROUPS, CMLP_CHUNK, CMLP_CHUNK)), full((CMLP_CHUNK, C_GROUPS))],
        out_specs=out_specs,
        out_shape=out_shape,
        compiler_params=_cparams(("arbitrary",)),
        name="cmlp",
    )(p, p, prm["ln_g"], prm["ln_b"], prm["ws"], prm["bs_t"])


def _run_trunk(x3, mod, st_hgrn, st_ssm, st_conv, keep_v, w):
    bsz, seq_len, d = x3.shape
    depth = mod.shape[0]
    aw = bw = cw = d // 2
    heads_a = aw // A_DK
    x = x3.reshape(bsz * seq_len, d)
    col = w["cols"]
    hgrn_out, ssm_out, conv_out, v_out = [], [], [], []
    for l in range(depth):
        mod6 = mod[l].reshape(bsz, 6, 1, d).transpose(1, 0, 2, 3)
        p, gates = _in_proj(x, w["norm1_g"][l], mod6, w["w_in"], l, seq_len, col["gate"])

        y_a, s_h = _hgrn(p, w["lbc"], l, w["hgrn_onorm_g"][l], st_hgrn, bsz, seq_len, heads_a)
        ssd_prm = dict(bw=bw, layer=l, conv_w=w["ssm_conv_w"][l], conv_b=w["ssm_conv_b"][l][None],
                       dt_bias=w["dt_bias_pad"][l][None], a_log=w["a_log_pad"][l][None],
                       d_skip=w["d_skip"][l][None], gn=w["ssm_onorm_g"][l][None])
        y_b, s_s, conv_tail = _ssd(p, col, ssd_prm, st_ssm, st_conv, bsz, seq_len)
        cm_prm = dict(cw=cw, ln_g=w["cmlp_ln_g"][l][None], ln_b=w["cmlp_ln_b"][l][None],
                      ws=w["cmlp_ws"][l], bs_t=w["cmlp_bs"][l].T)
        c_res = _cmlp(p, col["u"], col["v"], cm_prm, bsz, seq_len, keep_v)
        merged = _merge(y_a, y_b, c_res[0], w["w_branch"], l, gates, d)
        x = _proj_residual(merged, w["w_out"], l, x, mod6, 2, seq_len, "out_proj")
        hid = _mlp_up(x, w["norm2_g"][l], mod6, w["w_up"], l, seq_len)
        x = _proj_residual(hid, w["w_down"], l, x, mod6, 5, seq_len, "mlp_down")

        hgrn_out.append(s_h)
        ssm_out.append(s_s)
        conv_out.append(conv_tail[:, SUBLANES - (CONV_W - 1):, :])
        if keep_v:
            v_out.append(c_res[1].reshape(bsz, seq_len, cw))
    y = _final_norm(x, w["final_g"]).reshape(bsz, seq_len, d)
    return (y, jnp.stack(hgrn_out), jnp.stack(ssm_out), jnp.stack(conv_out),
            jnp.stack(v_out) if keep_v else None)


def _mlp_down_hgrn_kernel(a_ref, w_ref, x_ref, gate_ref, q_ref, f_ref, i_ref, gg_ref, lbc_ref, gn_ref, sel_ref,
                          o_ref, ya_ref, s_ref, acc_scr, bpad, kpad, *, nseg, c, n_chunks, heads, steps_per_seq):
    step = pl.program_id(0) * pl.num_programs(1) + pl.program_id(1)

    @pl.when(step % steps_per_seq == 0)
    def _():
        s_ref[...] = jnp.zeros(s_ref.shape, F32)

    fillers = _down_sub_dots(a_ref, w_ref, x_ref, gate_ref, o_ref, acc_scr, nseg)
    _hgrn_chunks(q_ref, f_ref, i_ref, gg_ref, lbc_ref, gn_ref, sel_ref, ya_ref, s_ref, bpad, kpad,
                 c=c, n_chunks=n_chunks, heads=heads, unroll=True, fillers=fillers)


def _mlp_down_hgrn(a, w_stack, layer, x, mod6, k_gate, seq_len, p_other, lbc_stack, lbc_layer, gn, heads):
    n, kdim = a.shape
    d = w_stack.shape[2]
    tn = DOWN_TILE
    tm, nseg, tps = _row_tiling(n, seq_len, ROW_TILE)
    ni, nj = n // tm, d // tn
    n_other = p_other.shape[0]
    t_step = n_other // (ni * nj)
    assert t_step * ni * nj == n_other and seq_len % t_step == 0 and t_step % SUBLANES == 0
    assert kdim % DOWN_K_SPLIT == 0
    steps_per_seq = seq_len // t_step
    aw = heads * A_DK
    c, h_in_specs, h_args, ya_spec, h_scratch = _hgrn_specs(
        p_other, lbc_stack, lbc_layer, gn, heads, t_step, lambda i, j: i * nj + j)
    state_spec = pl.BlockSpec((1, heads, A_DK, A_DK), lambda i, j: ((i * nj + j) // steps_per_seq, 0, 0, 0))
    return pl.pallas_call(
        functools.partial(_mlp_down_hgrn_kernel, nseg=nseg, c=c, n_chunks=t_step // c, heads=heads,
                          steps_per_seq=steps_per_seq),
        grid=(ni, nj),
        in_specs=[
            pl.BlockSpec((tm, kdim), lambda i, j: (i, 0)),
            pl.BlockSpec((None, kdim, tn), lambda i, j: (layer, 0, j)),
            pl.BlockSpec((tm, tn), lambda i, j: (i, j)),
            pl.BlockSpec((None, nseg, 1, tn), lambda i, j: (k_gate, i // tps, 0, j)),
        ] + h_in_specs,
        out_specs=[pl.BlockSpec((tm, tn), lambda i, j: (i, j)), ya_spec, state_spec],
        out_shape=[jax.ShapeDtypeStruct((n, d), F32),
                   jax.ShapeDtypeStruct((n_other, aw), BF16),
                   jax.ShapeDtypeStruct((n_other // seq_len, heads, A_DK, A_DK), F32)],
        scratch_shapes=[pltpu.VMEM((tm, tn), F32)] + h_scratch,
        compiler_params=_cparams(("arbitrary", "arbitrary")),
        name="mlp_down_hgrn2",
    )(a, w_stack, x, mod6, *h_args)


def _run_trunk_halves(x3, mod, w):
    bsz, seq_len, d = x3.shape
    depth = mod.shape[0]
    hb = bsz // 2
    aw = bw = cw = d // 2
    heads_a = aw // A_DK
    col = w["cols"]
    n_half = hb * seq_len
    x_full = x3.reshape(bsz * seq_len, d)
    xs = [x_full, x_full]
    row0 = [0, n_half]
    mods = [mod[:, :hb], mod[:, hb:]]
    hgrn_out, ssm_out, conv_out = [], [], []
    pending = None
    for l in range(depth):
        mod6 = [m[l].reshape(hb, 6, 1, d).transpose(1, 0, 2, 3) for m in mods]
        ssd_prm = dict(bw=bw, layer=l, conv_w=w["ssm_conv_w"][l], conv_b=w["ssm_conv_b"][l][None],
                       dt_bias=w["dt_bias_pad"][l][None], a_log=w["a_log_pad"][l][None],
                       d_skip=w["d_skip"][l][None], gn=w["ssm_onorm_g"][l][None])
        cm_prm = dict(cw=cw, ln_g=w["cmlp_ln_g"][l][None], ln_b=w["cmlp_ln_b"][l][None],
                      ws=w["cmlp_ws"][l], bs_t=w["cmlp_bs"][l].T)
        gn_a = w["hgrn_onorm_g"][l]

        def mix_merge(h, p, gates, y_a, ssd_res=None):
            y_b, s_s, conv_tail = ssd_res or _ssd(p, col, ssd_prm, None, None, hb, seq_len)
            y_c = _cmlp(p, col["u"], col["v"], cm_prm, hb, seq_len, False)[0]
            merged = _merge(y_a, y_b, y_c, w["w_branch"], l, gates, d)
            x_new = _proj_residual(merged, w["w_out"], l, xs[h], mod6[h], 2, seq_len, "out_proj", row0[h])
            row0[h] = 0
            return x_new, s_s, conv_tail

        p0, g0 = _in_proj(xs[0], w["norm1_g"][l], mod6[0], w["w_in"], l, seq_len, col["gate"], (row0[0], n_half))
        if pending is None:
            ya0, sh0 = _hgrn(p0, w["lbc"], l, gn_a, None, hb, seq_len, heads_a)
        else:
            hid1, x1_mid, mod6_prev = pending
            xs[1], ya0, sh0 = _mlp_down_hgrn(hid1, w["w_down"], l - 1, x1_mid, mod6_prev, 5, seq_len,
                                              p0, w["lbc"], l, gn_a, heads_a)
        p1, g1 = _in_proj(xs[1], w["norm1_g"][l], mod6[1], w["w_in"], l, seq_len, col["gate"], (row0[1], n_half))
        x0, ss0, ct0 = mix_merge(0, p0, g0, ya0)
        hid0, ya1, sh1 = _mlp_up_hgrn(x0, w["norm2_g"][l], mod6[0], w["w_up"], l, seq_len,
                                      p1, w["lbc"], gn_a, heads_a)
        xs[0], *ssd1 = _mlp_down_ssd(hid0, w["w_down"], l, x0, mod6[0], 5, seq_len, p1, col, ssd_prm)
        x1_mid, ss1, ct1 = mix_merge(1, p1, g1, ya1, tuple(ssd1))
        hid1 = _mlp_up(x1_mid, w["norm2_g"][l], mod6[1], w["w_up"], l, seq_len)
        pending = (hid1, x1_mid, mod6[1])

        hgrn_out.append(jnp.concatenate([sh0, sh1], axis=0))
        ssm_out.append(jnp.concatenate([ss0, ss1], axis=0))
        conv_out.append(jnp.concatenate([ct0, ct1], axis=0)[:, SUBLANES - (CONV_W - 1):, :])
    hid1, x1_mid, mod6_prev = pending
    xs[1] = _proj_residual(hid1, w["w_down"], depth - 1, x1_mid, mod6_prev, 5, seq_len, "mlp_down")
    y = _final_norm_pair(xs[0], xs[1], w["final_g"]).reshape(bsz, seq_len, d)
    return (y, jnp.stack(hgrn_out), jnp.stack(ssm_out), jnp.stack(conv_out), None)


def kernel(x_prompt, x_sample, state_hgrn, state_ssm, state_conv, c_prompt, c_sample, norm1_g, norm2_g,
           w_mod, b_mod, w_in, hgrn_lb, hgrn_onorm_g, ssm_conv_w, ssm_conv_b, ssm_dt_bias, ssm_a_log, ssm_d,
           ssm_onorm_g, cmlp_ln_g, cmlp_ln_b, cmlp_ws, cmlp_bs, w_branch, w_out, w_up, w_down, final_g):
    d = x_prompt.shape[-1]
    depth = w_in.shape[0]
    aw = bw = cw = d // 2
    n_bc = B_GROUPS * B_DSTATE
    heads_b = bw // B_HEADDIM
    assert heads_b <= LANES and bw % LANES == 0 and DT_PAD % LANES == 0

    o_dt = 4 * aw + bw + bw + 2 * n_bc
    o_u = o_dt + heads_b
    w_in_t = jnp.swapaxes(w_in, 1, 2).astype(BF16)
    w_in_r = jnp.concatenate(
        [w_in_t[:, :o_u], jnp.zeros((depth, DT_PAD - heads_b, d), BF16), w_in_t[:, o_u:]], axis=1)
    cols = dict(z=4 * aw, xs=4 * aw + bw, bc=4 * aw + 2 * bw, dt=o_dt)
    cols["u"] = o_dt + DT_PAD
    cols["v"] = cols["u"] + cw
    cols["gate"] = cols["v"] + cw
    assert cols["gate"] % IN_TILE == 0 and w_in_r.shape[1] % IN_TILE == 0

    pad_h = lambda a: jnp.pad(a.astype(F32), ((0, 0), (0, LANES - heads_b)))
    w = dict(
        cols=cols, w_in=w_in_r,
        norm1_g=norm1_g, norm2_g=norm2_g, final_g=final_g,
        lbc=_lb_consts(hgrn_lb), hgrn_onorm_g=hgrn_onorm_g,
        ssm_conv_w=ssm_conv_w, ssm_conv_b=ssm_conv_b,
        dt_bias_pad=pad_h(ssm_dt_bias), a_log_pad=pad_h(ssm_a_log),
        d_skip=jnp.repeat(ssm_d.astype(F32), B_HEADDIM, axis=1), ssm_onorm_g=ssm_onorm_g,
        cmlp_ln_g=cmlp_ln_g, cmlp_ln_b=cmlp_ln_b, cmlp_ws=cmlp_ws, cmlp_bs=cmlp_bs,
        w_branch=w_branch.astype(BF16).reshape(depth, N_BRANCH, aw, d), w_out=w_out.astype(BF16),
        w_up=w_up.astype(BF16), w_down=w_down.astype(BF16),
    )

    nb = x_prompt.shape[0]
    mod = _modulation(jnp.concatenate([c_prompt, c_sample], axis=0), w_mod, b_mod)
    if nb % 2 == 0:
        y_p, hgrn_p, ssm_p, conv_p, _ = _run_trunk_halves(x_prompt, mod[:, :nb], w)
    else:
        y_p, hgrn_p, ssm_p, conv_p, _ = _run_trunk(x_prompt, mod[:, :nb], None, None, None, False, w)
    conv_pad = jnp.pad(state_conv, ((0, 0), (0, 0), (SUBLANES - (CONV_W - 1), 0), (0, 0)))
    y_s, hgrn_s, ssm_s, conv_s, v_s = _run_trunk(x_sample, mod[:, nb:], state_hgrn, state_ssm, conv_pad, True, w)
    return (y_p, y_s, hgrn_p, ssm_p, conv_p, hgrn_s, ssm_s, conv_s, v_s)
```

```python
import functools
import math

import numpy as np
import jax
import jax.numpy as jnp
from jax import lax
from jax.experimental import pallas as pl
from jax.experimental.pallas import tpu as pltpu

F32 = jnp.float32
BF16 = jnp.bfloat16

A_DK = 128
B_HEADDIM = 64
B_GROUPS = 2
B_DSTATE = 128
CONV_W = 4
C_GROUPS = 4
CMLP_CHUNK = 128
N_BRANCH = 3
SCAN_CHUNK = 64
NORM_EPS = 1e-6
LB_FLOOR = 1e-30
LOG2E = 1.4426950408889634

LANES = 128
SUBLANES = 8
VMEM_LIMIT = 56 * 1024 * 1024
DIAG = SUBLANES
DT_PAD = 512
ROW_TILE = 1024
COL_TILE = 1024
IN_TILE = 1536
MERGE_TILE = 512
MERGE_ROW_TILE = 1024
UP_TILE = 2048
OUT_TILE = 512
OUT_ROW_TILE = 2048
DOWN_TILE = 256
EPILOGUE_COLS = 256
SSD_ROWS = 128
DOWN_K_SPLIT = 16
NORM_ROWS = 128


def _cparams(sem):
    return pltpu.CompilerParams(dimension_semantics=sem, vmem_limit_bytes=VMEM_LIMIT)


def _split3(x):
    hi = x.astype(BF16)
    r1 = x - hi.astype(F32)
    mid = r1.astype(BF16)
    lo = (r1 - mid.astype(F32)).astype(BF16)
    return hi, mid, lo


def _dot3(sel, x):
    hi, mid, lo = _split3(x)
    d = lambda p: jnp.dot(sel, p, preferred_element_type=F32)
    return d(hi) + d(mid) + d(lo)


def _dot3_rhs(x, sel):
    hi, mid, lo = _split3(x)
    d = lambda p: jnp.dot(p, sel, preferred_element_type=F32)
    return d(hi) + d(mid) + d(lo)


def _silu(x):
    return x / (1.0 + jnp.exp(-x))


def _softplus(x):
    return jnp.maximum(x, 0.0) + jnp.log1p(jnp.exp(-jnp.abs(x)))


def _gelu(x):
    return 0.5 * x * (1.0 + lax.erf(x * (1.0 / math.sqrt(2.0))))


def _lb_kernel(lb_ref, out_ref, *, depth):
    x = lb_ref[...]
    m = jnp.max(x, axis=0, keepdims=True)
    e = jnp.exp(x - m)
    p = e / jnp.sum(e, axis=0, keepdims=True)
    acc = jnp.zeros_like(p[0:1])
    zeros5 = jnp.zeros((SUBLANES - 3, x.shape[1]), F32)
    for l in range(depth):
        acc = acc + p[l:l + 1]
        lb = acc - p[0:1]
        out_ref[l] = jnp.concatenate(
            [jnp.log(jnp.maximum(lb, LB_FLOOR)), jnp.log1p(-lb), 1.0 - lb, zeros5], axis=0)


def _lb_consts(hgrn_lb):
    depth, aw = hgrn_lb.shape
    return pl.pallas_call(
        functools.partial(_lb_kernel, depth=depth),
        out_shape=jax.ShapeDtypeStruct((depth, SUBLANES, aw), F32),
        name="hgrn_lb",
    )(hgrn_lb.astype(F32))


def _mod_kernel(c_ref, w_ref, b_ref, o_ref):
    cs = _silu(c_ref[...]).astype(BF16)
    o_ref[...] = jnp.dot(cs, w_ref[...].astype(BF16), preferred_element_type=F32) + b_ref[...]


def _modulation(c_all, w_mod, b_mod):
    depth, d, n6 = w_mod.shape
    s = c_all.shape[0]
    tn = 1024
    return pl.pallas_call(
        _mod_kernel,
        grid=(depth, n6 // tn),
        in_specs=[
            pl.BlockSpec((s, d), lambda l, j: (0, 0)),
            pl.BlockSpec((None, d, tn), lambda l, j: (l, 0, j)),
            pl.BlockSpec((None, 1, tn), lambda l, j: (l, 0, j)),
        ],
        out_specs=pl.BlockSpec((None, s, tn), lambda l, j: (l, 0, j)),
        out_shape=jax.ShapeDtypeStruct((depth, s, n6), F32),
        compiler_params=_cparams(("arbitrary", "arbitrary")),
        name="adaln_mod",
    )(c_all, w_mod, b_mod.reshape(depth, 1, n6))


def _row_tiling(n_tok, seq_len, tm_max):
    tm = min(tm_max, n_tok)
    if seq_len >= tm:
        assert seq_len % tm == 0
        return tm, 1, seq_len // tm
    assert tm % seq_len == 0
    return tm, tm // seq_len, 1


def _per_seq(x, nseg):
    tm, d = x.shape
    return x.reshape(nseg, tm // nseg, d)


def _final_norm_kernel(x_ref, g_ref, o_ref):
    x = x_ref[...]
    o_ref[...] = x * lax.rsqrt(jnp.mean(x * x, axis=-1, keepdims=True) + NORM_EPS) * g_ref[...]


def _final_norm(x, g):
    n, d = x.shape
    tm = min(512, n)
    return pl.pallas_call(
        _final_norm_kernel,
        grid=(n // tm,),
        in_specs=[pl.BlockSpec((tm, d), lambda i: (i, 0)), pl.BlockSpec((1, d), lambda i: (0, 0))],
        out_specs=pl.BlockSpec((tm, d), lambda i: (i, 0)),
        out_shape=jax.ShapeDtypeStruct((n, d), F32),
        compiler_params=_cparams(("arbitrary",)),
        name="final_norm",
    )(x, g.reshape(1, d))


def _final_norm_pair_kernel(xa_ref, xb_ref, g_ref, o_ref, *, nt):
    def norm(x):
        return x * lax.rsqrt(jnp.mean(x * x, axis=-1, keepdims=True) + NORM_EPS) * g_ref[...]

    @pl.when(pl.program_id(0) < nt)
    def _():
        o_ref[...] = norm(xa_ref[...])

    @pl.when(pl.program_id(0) >= nt)
    def _():
        o_ref[...] = norm(xb_ref[...])


def _final_norm_pair(xa, xb, g):
    n, d = xa.shape
    tm = min(512, n)
    nt = n // tm
    return pl.pallas_call(
        functools.partial(_final_norm_pair_kernel, nt=nt),
        grid=(2 * nt,),
        in_specs=[pl.BlockSpec((tm, d), lambda i: (jnp.minimum(i, nt - 1), 0)),
                  pl.BlockSpec((tm, d), lambda i: (jnp.maximum(i - nt, 0), 0)),
                  pl.BlockSpec((1, d), lambda i: (0, 0))],
        out_specs=pl.BlockSpec((tm, d), lambda i: (i, 0)),
        out_shape=jax.ShapeDtypeStruct((2 * n, d), F32),
        compiler_params=_cparams(("arbitrary",)),
        name="final_norm",
    )(xa, xb, g.reshape(1, d))


def _norm_mod_rows(x_ref, g_ref, sc_ref, sh_ref, h_scr, nseg):
    tm = x_ref.shape[0]
    seg = tm // nseg
    rb = min(seg, NORM_ROWS)
    g = g_ref[...]

    def body(r, carry):
        rows = pl.ds(pl.multiple_of(r * rb, rb), rb)
        s = r // (seg // rb)
        x = x_ref[rows, :]
        gain = g * (1.0 + sc_ref[s])
        inv = lax.rsqrt(jnp.mean(x * x, axis=-1, keepdims=True) + NORM_EPS)
        h_scr[rows, :] = (x * inv * gain + sh_ref[s]).astype(BF16)
        return carry

    lax.fori_loop(0, tm // rb, body, 0, unroll=2 if (tm // rb) % 2 == 0 else 1)


def _mlp_up_kernel(x_ref, g_ref, sc_ref, sh_ref, w_ref, o_ref, h_scr, *, nseg):
    @pl.when(pl.program_id(1) == 0)
    def _():
        _norm_mod_rows(x_ref, g_ref, sc_ref, sh_ref, h_scr, nseg)

    y = jnp.maximum(jnp.dot(h_scr[...], w_ref[...], preferred_element_type=F32), 0.0)
    o_ref[...] = (y * y).astype(o_ref.dtype)


def _in_proj_kernel(x_ref, g_ref, sc_ref, sh_ref, w_ref, p_ref, gate_ref, h_scr, *, nseg, n_main):
    j = pl.program_id(1)

    @pl.when(j == 0)
    def _():
        _norm_mod_rows(x_ref, g_ref, sc_ref, sh_ref, h_scr, nseg)

    def project():
        return lax.dot_general(h_scr[...], w_ref[...], (((1,), (1,)), ((), ())), preferred_element_type=F32)

    @pl.when(j < n_main)
    def _():
        p_ref[...] = project()

    @pl.when(j >= n_main)
    def _():
        gate_ref[...] = project().astype(gate_ref.dtype)


def _norm_proj_call(kern, x, g, mod6, k_scale, k_shift, w_stack, layer, seq_len, tn, out_specs, out_shape, name):
    n, d = x.shape
    m = w_stack.shape[2]
    tm, nseg, tps = _row_tiling(n, seq_len, ROW_TILE)

    def mod_spec(k):
        return pl.BlockSpec((None, nseg, 1, d), lambda i, j: (k, i // tps, 0, 0))

    return pl.pallas_call(
        functools.partial(kern, nseg=nseg),
        grid=(n // tm, m // tn),
        in_specs=[
            pl.BlockSpec((tm, d), lambda i, j: (i, 0)),
            pl.BlockSpec((1, d), lambda i, j: (0, 0)),
            mod_spec(k_scale),
            mod_spec(k_shift),
            pl.BlockSpec((None, d, tn), lambda i, j: (layer, 0, j)),
        ],
        out_specs=out_specs(tm),
        out_shape=out_shape,
        scratch_shapes=[pltpu.VMEM((tm, d), BF16)],
        compiler_params=_cparams(("arbitrary", "arbitrary")),
        name=name,
    )(x, g.reshape(1, d), mod6, mod6, w_stack)


def _mlp_up(x, g, mod6, w_stack, layer, seq_len):
    n = x.shape[0]
    m = w_stack.shape[2]
    tn = UP_TILE
    return _norm_proj_call(
        _mlp_up_kernel, x, g, mod6, 4, 3, w_stack, layer, seq_len, tn,
        lambda tm: pl.BlockSpec((tm, tn), lambda i, j: (i, j)),
        jax.ShapeDtypeStruct((n, m), BF16), "mlp_up")


def _in_proj(x, g, mod6, w_stack, layer, seq_len, main_cols, rows=None):
    d = x.shape[1]
    row0, n = rows if rows is not None else (0, x.shape[0])
    m = w_stack.shape[1]
    tn = IN_TILE
    n_main = main_cols // tn
    tm, nseg, tps = _row_tiling(n, seq_len, ROW_TILE)

    def mod_spec(k):
        return pl.BlockSpec((None, nseg, 1, d), lambda i, j: (k, i // tps, 0, 0))

    return pl.pallas_call(
        functools.partial(_in_proj_kernel, nseg=nseg, n_main=n_main),
        grid=(n // tm, m // tn),
        in_specs=[
            pl.BlockSpec((tm, d), lambda i, j: (i + row0 // tm, 0)),
            pl.BlockSpec((1, d), lambda i, j: (0, 0)),
            mod_spec(1),
            mod_spec(0),
            pl.BlockSpec((None, tn, d), lambda i, j: (layer, j, 0)),
        ],
        out_specs=[pl.BlockSpec((tm, tn), lambda i, j: (i, jnp.minimum(j, n_main - 1))),
                   pl.BlockSpec((tm, tn), lambda i, j: (i, jnp.maximum(j - n_main, 0)))],
        out_shape=[jax.ShapeDtypeStruct((n, main_cols), F32), jax.ShapeDtypeStruct((n, m - main_cols), BF16)],
        scratch_shapes=[pltpu.VMEM((tm, d), BF16)],
        compiler_params=_cparams(("arbitrary", "arbitrary")),
        name="in_proj",
    )(x, g.reshape(1, d), mod6, mod6, w_stack)


def _proj_res_kernel(a_ref, w_ref, x_ref, gate_ref, o_ref, *, nseg):
    acc = jnp.dot(a_ref[...], w_ref[...], preferred_element_type=F32)
    x = x_ref[...]
    o_ref[...] = (_per_seq(x, nseg) + gate_ref[...] * _per_seq(acc, nseg)).reshape(x.shape)


def _proj_residual(a, w_stack, layer, x, mod6, k_gate, seq_len, name, x_row0=0):
    n, kdim = a.shape
    d = w_stack.shape[2]
    tn, tm_max = (OUT_TILE, OUT_ROW_TILE) if kdim <= d else (DOWN_TILE, ROW_TILE)
    tm, nseg, tps = _row_tiling(n, seq_len, tm_max)
    return pl.pallas_call(
        functools.partial(_proj_res_kernel, nseg=nseg),
        grid=(n // tm, d // tn),
        in_specs=[
            pl.BlockSpec((tm, kdim), lambda i, j: (i, 0)),
            pl.BlockSpec((None, kdim, tn), lambda i, j: (layer, 0, j)),
            pl.BlockSpec((tm, tn), lambda i, j: (i + x_row0 // tm, j)),
            pl.BlockSpec((None, nseg, 1, tn), lambda i, j: (k_gate, i // tps, 0, j)),
        ],
        out_specs=pl.BlockSpec((tm, tn), lambda i, j: (i, j)),
        out_shape=jax.ShapeDtypeStruct((n, d), F32),
        compiler_params=_cparams(("arbitrary", "arbitrary")),
        name=name,
    )(a, w_stack, x, mod6)


def _merge_kernel(ya_ref, yb_ref, yc_ref, wa_ref, wb_ref, wc_ref, ga_ref, gb_ref, gc_ref, o_ref):
    acc = None
    for y_ref, w_ref, g_ref in ((ya_ref, wa_ref, ga_ref), (yb_ref, wb_ref, gb_ref), (yc_ref, wc_ref, gc_ref)):
        gate = 1.0 / (1.0 + jnp.exp(-g_ref[...].astype(F32)))
        term = gate * jnp.dot(y_ref[...], w_ref[...], preferred_element_type=F32)
        acc = term if acc is None else acc + term
    o_ref[...] = acc.astype(o_ref.dtype)


def _merge(ya, yb, yc, w_stack, layer, gates, d):
    n, bw = ya.shape
    tn = MERGE_TILE
    tm = min(MERGE_ROW_TILE, n)
    gsteps = d // tn
    y_spec = pl.BlockSpec((tm, bw), lambda i, j: (i, 0))

    def w_spec(k):
        return pl.BlockSpec((None, None, bw, tn), lambda i, j: (layer, k, 0, j))

    def g_spec(k):
        return pl.BlockSpec((tm, tn), lambda i, j: (i, k * gsteps + j))

    return pl.pallas_call(
        _merge_kernel,
        grid=(n // tm, d // tn),
        in_specs=[y_spec, y_spec, y_spec, w_spec(0), w_spec(1), w_spec(2), g_spec(0), g_spec(1), g_spec(2)],
        out_specs=pl.BlockSpec((tm, tn), lambda i, j: (i, j)),
        out_shape=jax.ShapeDtypeStruct((n, d), BF16),
        compiler_params=_cparams(("arbitrary", "arbitrary")),
        name="merge",
    )(ya, yb, yc, w_stack, w_stack, w_stack, gates, gates, gates)


def _hgrn_levels(c):
    lv, m = [], c // 2
    while m >= DIAG:
        lv.append(m)
        m //= 2
    return tuple(lv)


def _hgrn_select(c):
    tri = np.arange(c)[None, :] <= np.arange(c)[:, None]
    return jnp.asarray(tri.astype(np.float32), dtype=BF16)


def _hgrn_kernel(*refs, c, n_chunks, heads, has_init):
    if has_init:
        q_ref, f_ref, i_ref, g_ref, lbc_ref, gn_ref, sel_ref, s0_ref, ya_ref, s_ref, bpad, kpad = refs
    else:
        q_ref, f_ref, i_ref, g_ref, lbc_ref, gn_ref, sel_ref, ya_ref, s_ref, bpad, kpad = refs

    @pl.when(pl.program_id(1) == 0)
    def _():
        s_ref[...] = s0_ref[...] if has_init else jnp.zeros(s_ref.shape, F32)

    _hgrn_chunks(q_ref, f_ref, i_ref, g_ref, lbc_ref, gn_ref, sel_ref, ya_ref, s_ref, bpad, kpad,
                 c=c, n_chunks=n_chunks, heads=heads, unroll=False)


def _hgrn_chunks(q_ref, f_ref, i_ref, g_ref, lbc_ref, gn_ref, sel_ref, ya_ref, s_ref, bpad, kpad,
                 *, c, n_chunks, heads, unroll, fillers=()):
    levels = _hgrn_levels(c)
    nl = len(levels)
    width = heads * A_DK
    head_lanes = [slice(hh * A_DK, (hh + 1) * A_DK) for hh in range(heads)]

    bpad[0:DIAG, :] = jnp.zeros((DIAG, width), F32)
    kpad[0:DIAG, :] = jnp.zeros((DIAG, width), F32)

    sel = sel_ref[...]
    log_lb = lbc_ref[0:1, :]
    log1m_lb = lbc_ref[1:2, :]
    one_m_lb = lbc_ref[2:3, :]
    row = lax.broadcasted_iota(jnp.int32, (c, width), 0)
    rr = lax.broadcasted_iota(jnp.int32, (c, c), 0)
    cc = lax.broadcasted_iota(jnp.int32, (c, c), 1)
    upper = [(row & m) != 0 for m in levels]
    same = [(rr ^ cc) < 2 * m for m in levels]
    dmat = jnp.where(((rr ^ cc) < DIAG) & (cc <= rr), rr - cc, -1)
    nt_dims = (((1,), (1,)), ((), ()))
    tn_dims = (((0,), (0,)), ((), ()))

    def chunk(ci):
        rows = slice(ci * c, (ci + 1) * c) if unroll else pl.ds(pl.multiple_of(ci * c, c), c)
        z = f_ref[rows, :]
        aq = q_ref[rows, :]

        log_sig = jnp.minimum(z, 0.0) - jnp.log(1.0 + jnp.exp(-jnp.abs(z)))
        bb = log1m_lb + log_sig
        log_f = jnp.maximum(log_lb, bb) + jnp.log(1.0 + jnp.exp(-jnp.abs(log_lb - bb)))
        k = one_m_lb / (1.0 + jnp.exp(z))
        q = _silu(aq)

        b = _dot3(sel, log_f * LOG2E)
        bpad[DIAG:DIAG + c, :] = b
        kpad[DIAG:DIAG + c, :] = k
        b_tot = bpad[DIAG + c - 1:DIAG + c, :]
        vb = i_ref[rows, :].astype(BF16)
        q_in = (q * jnp.exp2(b)).astype(BF16)
        k_out = (k * jnp.exp2(b_tot - b)).astype(BF16)
        yield

        att = [jnp.zeros((c, c), F32) for _ in range(heads)]
        for li in range(nl):
            m = levels[li]
            ref_b = jnp.concatenate(
                [jnp.broadcast_to(bpad[DIAG + g0 + m - 1:DIAG + g0 + m, :], (2 * m, width))
                 for g0 in range(0, c, 2 * m)], axis=0)
            e = jnp.exp2(-jnp.abs(b - ref_b))
            qs = jnp.where(upper[li], q * e, 0.0).astype(BF16)
            ks = jnp.where(upper[li], 0.0, k * e).astype(BF16)
            for hh, lanes in enumerate(head_lanes):
                a_l = lax.dot_general(qs[:, lanes], ks[:, lanes], nt_dims, preferred_element_type=F32)
                att[hh] = att[hh] + jnp.where(same[li], a_l, 0.0)
        yield

        for dlt in range(DIAG):
            if dlt == 0:
                w = q * k
            else:
                ksh = kpad[DIAG - dlt:DIAG - dlt + c, :]
                bsh = bpad[DIAG - dlt:DIAG - dlt + c, :]
                w = q * ksh * jnp.exp2(b - bsh)
            for hh, lanes in enumerate(head_lanes):
                col = jnp.sum(w[:, lanes], axis=-1, keepdims=True)
                att[hh] = jnp.where(dmat == dlt, col, att[hh])
            if dlt in (DIAG // 2 - 1, DIAG - 1):
                yield

        decay_row = jnp.exp2(b_tot)
        outs = []
        for hh, lanes in enumerate(head_lanes):
            s_prev = s_ref[0, hh]
            o = jnp.dot(q_in[:, lanes], s_prev.astype(BF16), preferred_element_type=F32)
            o = o + jnp.dot(att[hh].astype(BF16), vb[:, lanes], preferred_element_type=F32)
            decay_col = jnp.broadcast_to(decay_row[:, lanes], (A_DK, A_DK)).T
            s_ref[0, hh] = decay_col * s_prev + lax.dot_general(
                k_out[:, lanes], vb[:, lanes], tn_dims, preferred_element_type=F32)
            outs.append(o * lax.rsqrt(jnp.mean(o * o, axis=-1, keepdims=True) + NORM_EPS))
        y = jnp.concatenate(outs, axis=1) * gn_ref[...] * _silu(g_ref[rows, :])
        ya_ref[rows, :] = y.astype(ya_ref.dtype)
        yield

    if not unroll:
        def body(ci, carry):
            for _ in chunk(ci):
                pass
            return carry

        lax.fori_loop(0, n_chunks, body, 0)
        return

    n_phases = 5
    _emit_interleaved(fillers, [functools.partial(next, gen, None)
                                for gen in map(chunk, range(n_chunks)) for _ in range(n_phases)])


def _hgrn_specs(p, lbc_stack, layer, gn, heads, t_blk, row_block):
    aw = heads * A_DK
    c = min(t_blk, SCAN_CHUNK)
    sel = _hgrn_select(c)

    def col_spec(seg):
        return pl.BlockSpec((t_blk, aw), lambda *g: (row_block(*g), seg))

    in_specs = [col_spec(0), col_spec(1), col_spec(2), col_spec(3),
                pl.BlockSpec((None, SUBLANES, aw), lambda *g: (layer, 0, 0)),
                pl.BlockSpec((1, aw), lambda *g: (0, 0)),
                pl.BlockSpec(sel.shape, lambda *g: (0, 0))]
    args = [p, p, p, p, lbc_stack, gn.reshape(1, aw), sel]
    scratch = [pltpu.VMEM((c + DIAG, aw), F32), pltpu.VMEM((c + DIAG, aw), F32)]
    return c, in_specs, args, pl.BlockSpec((t_blk, aw), lambda *g: (row_block(*g), 0)), scratch


def _mlp_up_hgrn_kernel(x_ref, g_ref, sc_ref, sh_ref, w_ref, q_ref, f_ref, i_ref, gg_ref, lbc_ref, gn_ref, sel_ref,
                        o_ref, ya_ref, s_ref, h_scr, bpad, kpad, *, nseg, c, n_chunks, heads, steps_per_seq):
    j = pl.program_id(1)
    step = pl.program_id(0) * pl.num_programs(1) + j

    @pl.when(j == 0)
    def _():
        _norm_mod_rows(x_ref, g_ref, sc_ref, sh_ref, h_scr, nseg)

    @pl.when(step % steps_per_seq == 0)
    def _():
        s_ref[...] = jnp.zeros(s_ref.shape, F32)

    def sub_dot(c0):
        cols = slice(c0, c0 + EPILOGUE_COLS)
        y = jnp.maximum(jnp.dot(h_scr[...], w_ref[:, cols], preferred_element_type=F32), 0.0)
        o_ref[:, cols] = (y * y).astype(o_ref.dtype)

    fillers = [functools.partial(sub_dot, c0) for c0 in range(0, w_ref.shape[1], EPILOGUE_COLS)]
    _hgrn_chunks(q_ref, f_ref, i_ref, gg_ref, lbc_ref, gn_ref, sel_ref, ya_ref, s_ref, bpad, kpad,
                 c=c, n_chunks=n_chunks, heads=heads, unroll=True, fillers=fillers)


def _mlp_up_hgrn(x, g, mod6, w_stack, layer, seq_len, p_other, lbc_stack, gn, heads):
    n, d = x.shape
    m = w_stack.shape[2]
    tn = COL_TILE
    tm, nseg, tps = _row_tiling(n, seq_len, ROW_TILE)
    ni, nj = n // tm, m // tn
    n_other = p_other.shape[0]
    t_step = n_other // (ni * nj)
    assert t_step * ni * nj == n_other and seq_len % t_step == 0 and t_step % SUBLANES == 0
    steps_per_seq = seq_len // t_step
    aw = heads * A_DK
    c, h_in_specs, h_args, ya_spec, h_scratch = _hgrn_specs(
        p_other, lbc_stack, layer, gn, heads, t_step, lambda i, j: i * nj + j)

    def mod_spec(k):
        return pl.BlockSpec((None, nseg, 1, d), lambda i, j: (k, i // tps, 0, 0))

    state_spec = pl.BlockSpec((1, heads, A_DK, A_DK), lambda i, j: ((i * nj + j) // steps_per_seq, 0, 0, 0))
    return pl.pallas_call(
        functools.partial(_mlp_up_hgrn_kernel, nseg=nseg, c=c, n_chunks=t_step // c, heads=heads,
                          steps_per_seq=steps_per_seq),
        grid=(ni, nj),
        in_specs=[
            pl.BlockSpec((tm, d), lambda i, j: (i, 0)),
            pl.BlockSpec((1, d), lambda i, j: (0, 0)),
            mod_spec(4),
            mod_spec(3),
            pl.BlockSpec((None, d, tn), lambda i, j: (layer, 0, j)),
        ] + h_in_specs,
        out_specs=[pl.BlockSpec((tm, tn), lambda i, j: (i, j)), ya_spec, state_spec],
        out_shape=[jax.ShapeDtypeStruct((n, m), BF16),
                   jax.ShapeDtypeStruct((n_other, aw), BF16),
                   jax.ShapeDtypeStruct((n_other // seq_len, heads, A_DK, A_DK), F32)],
        scratch_shapes=[pltpu.VMEM((tm, d), BF16)] + h_scratch,
        compiler_params=_cparams(("arbitrary", "arbitrary")),
        name="mlp_up_hgrn2",
    )(x, g.reshape(1, d), mod6, mod6, w_stack, *h_args)


def _hgrn(p, lbc_stack, layer, gn, s0, bsz, seq_len, heads):
    c = min(seq_len, SCAN_CHUNK)
    t_blk = min(seq_len, 256)
    nt = seq_len // t_blk
    aw = heads * A_DK

    def col_spec(seg):
        return pl.BlockSpec((t_blk, aw), lambda b, t: (b * nt + t, seg))

    sel = _hgrn_select(c)
    in_specs = [col_spec(0), col_spec(1), col_spec(2), col_spec(3),
                pl.BlockSpec((None, SUBLANES, aw), lambda b, t: (layer, 0, 0)),
                pl.BlockSpec((1, aw), lambda b, t: (0, 0)),
                pl.BlockSpec(sel.shape, lambda b, t: (0, 0))]
    args = [p, p, p, p, lbc_stack, gn.reshape(1, aw), sel]
    state_spec = pl.BlockSpec((1, heads, A_DK, A_DK), lambda b, t: (b, 0, 0, 0))
    if s0 is not None:
        in_specs.append(pl.BlockSpec((None, 1, heads, A_DK, A_DK), lambda b, t: (layer, b, 0, 0, 0)))
        args.append(s0)
    return pl.pallas_call(
        functools.partial(_hgrn_kernel, c=c, n_chunks=t_blk // c, heads=heads, has_init=s0 is not None),
        grid=(bsz, nt),
        in_specs=in_specs,
        out_specs=[pl.BlockSpec((t_blk, aw), lambda b, t: (b * nt + t, 0)), state_spec],
        out_shape=[jax.ShapeDtypeStruct((bsz * seq_len, aw), BF16),
                   jax.ShapeDtypeStruct((bsz, heads, A_DK, A_DK), F32)],
        scratch_shapes=[pltpu.VMEM((c + DIAG, aw), F32), pltpu.VMEM((c + DIAG, aw), F32)],
        compiler_params=_cparams(("arbitrary", "arbitrary")),
        name="hgrn2",
    )(*args)


def _ssd_kernel(*refs, t, bw, has_init):
    n_in = SSD_N_INPUTS
    s0_ref, c0_ref = (refs[n_in], refs[n_in + 1]) if has_init else (None, None)
    rest = refs[n_in + 2:] if has_init else refs[n_in:]
    yb_ref, s_out_ref, conv_out_ref, xpad, st = rest
    ti = pl.program_id(1)
    _ssd_reset(s0_ref, c0_ref, xpad, st, ti == 0)
    for _ in _ssd_phases(*refs[:n_in], yb_ref, xpad, st, t=t, bw=bw):
        pass
    _ssd_flush(conv_out_ref, s_out_ref, xpad, st, ti == pl.num_programs(1) - 1)


SSD_N_INPUTS = 12
SSD_PHASES = 9


def _ssd_phases(z_ref, xs_ref, bc_ref, dt_ref, cw_ref, cb_ref, dtb_ref, alog_ref, dsk_ref, gn_ref, exp_ref, sel_ref,
                yb_ref, xpad, st, *, t, bw):
    gw = bw // B_GROUPS
    n_bc = B_GROUPS * B_DSTATE
    pad = SUBLANES

    cur = jnp.concatenate([xs_ref[...], bc_ref[...]], axis=1)
    prev = xpad[...]
    row8 = lax.broadcasted_iota(jnp.int32, (pad, cur.shape[1]), 0)
    conv = cb_ref[...]
    for j in range(CONV_W):
        s = CONV_W - 1 - j
        if s == 0:
            tap = cur
        else:
            rolled = pltpu.roll(cur, s, 0)
            head = jnp.where(row8 < s, pltpu.roll(prev, s, 0), rolled[0:pad])
            tap = jnp.concatenate([head, rolled[pad:]], axis=0)
        conv = conv + tap * cw_ref[j:j + 1, :]
    xbc = _silu(conv)
    x = xbc[:, 0:bw]
    yield

    dt = _softplus(dt_ref[...] + dtb_ref[...])
    a = dt * (-jnp.exp(alog_ref[...]))
    cs = _dot3(sel_ref[...], a)
    a_cum = cs[0:t]
    a_tot = cs[t:2 * t]
    ex = _dot3_rhs(jnp.concatenate([dt, a_cum, a_tot - a_cum], axis=0), exp_ref[...])
    dt_e = ex[0:t]
    acum_e = ex[t:2 * t]
    dec_e = ex[2 * t:3 * t]
    atot_e = acum_e[t - 1:t, :]

    xdt = x * dt_e
    xw = (xdt * jnp.exp(dec_e)).astype(BF16)
    xdt_b = xdt.astype(BF16)
    a_cum_t = a_cum.T
    rr = lax.broadcasted_iota(jnp.int32, (t, t), 0)
    cc = lax.broadcasted_iota(jnp.int32, (t, t), 1)
    causal = cc <= rr
    lane = lax.broadcasted_iota(jnp.int32, (t, LANES), 1)
    heads_per_group = gw // B_HEADDIM
    pairs_per_group = gw // LANES
    yield

    y_groups = []
    for g in range(B_GROUPS):
        bg = xbc[:, bw + g * B_DSTATE:bw + (g + 1) * B_DSTATE].astype(BF16)
        cg = xbc[:, bw + n_bc + g * B_DSTATE:bw + n_bc + (g + 1) * B_DSTATE].astype(BF16)
        gs = slice(g * gw, (g + 1) * gw)
        st_g = st[:, gs]
        scores = lax.dot_general(cg, bg, (((1,), (1,)), ((), ())), preferred_element_type=F32)
        y_off = jnp.dot(cg, st_g.astype(BF16), preferred_element_type=F32) * jnp.exp(acum_e[:, gs])
        st[:, gs] = jnp.exp(atot_e[:, gs]) * st_g + lax.dot_general(
            bg, xw[:, gs], (((0,), (0,)), ((), ())), preferred_element_type=F32)
        yield
        y_pairs = []
        for pr in range(pairs_per_group):
            cols = slice(g * gw + pr * LANES, g * gw + (pr + 1) * LANES)
            xp = xdt_b[:, cols]
            ms, xs_blocks = [], []
            for half in range(LANES // B_HEADDIM):
                h = g * heads_per_group + pr * (LANES // B_HEADDIM) + half
                diff = a_cum[:, h:h + 1] - a_cum_t[h:h + 1, :]
                lmat = jnp.where(causal, jnp.exp(jnp.where(causal, diff, 0.0)), 0.0)
                ms.append((scores * lmat).astype(BF16))
                in_half = (lane // B_HEADDIM) == half
                xs_blocks.append(jnp.where(in_half, xp, jnp.zeros_like(xp)))
            y_pairs.append(jnp.dot(jnp.concatenate(ms, axis=1), jnp.concatenate(xs_blocks, axis=0),
                                   preferred_element_type=F32))
            if pr % 2 == 1:
                yield
        y_groups.append(jnp.concatenate(y_pairs, axis=1) + y_off)
    y = jnp.concatenate(y_groups, axis=1) + dsk_ref[...] * x
    y = y * _silu(z_ref[...])
    outs = []
    for g in range(B_GROUPS):
        yg = y[:, g * gw:(g + 1) * gw]
        outs.append(yg * lax.rsqrt(jnp.mean(yg * yg, axis=-1, keepdims=True) + NORM_EPS))
    yb_ref[...] = (jnp.concatenate(outs, axis=1) * gn_ref[...]).astype(yb_ref.dtype)

    xpad[...] = cur[t - pad:t, :]
    yield


def _ssd_reset(s0_ref, c0_ref, xpad, st, first):
    @pl.when(first)
    def _():
        if s0_ref is not None:
            xpad[...] = c0_ref[0]
            st[...] = s0_ref[0].reshape(st.shape[1], st.shape[0]).T
        else:
            xpad[...] = jnp.zeros(xpad.shape, F32)
            st[...] = jnp.zeros(st.shape, F32)


def _ssd_flush(conv_out_ref, s_out_ref, xpad, st, last):
    @pl.when(last)
    def _():
        conv_out_ref[0] = xpad[...]
        s_out_ref[0] = st[...].T.reshape(s_out_ref.shape[1:])


def _ssd_operands(p, cols, prm, t, n_rows, n_seq, row_block, seq_block):
    bw = prm["bw"]
    heads = bw // B_HEADDIM
    n_bc = B_GROUPS * B_DSTATE
    cdim = bw + 2 * n_bc
    tri = np.arange(t)[None, :] <= np.arange(t)[:, None]
    sel = jnp.asarray(np.concatenate([tri, np.ones((t, t), bool)], 0).astype(np.float32), dtype=BF16)
    expand = np.zeros((LANES, bw), np.float32)
    expand[np.arange(bw) // B_HEADDIM, np.arange(bw)] = 1.0
    expand = jnp.asarray(expand, dtype=BF16)

    def blk(width, off):
        return pl.BlockSpec((t, width), lambda *g: (row_block(*g), off // width))

    def full(shape):
        return pl.BlockSpec(shape, lambda *g: (0,) * len(shape))

    in_specs = [blk(bw, cols["z"]), blk(bw, cols["xs"]), blk(2 * n_bc, cols["bc"]), blk(LANES, cols["dt"]),
                full((CONV_W, cdim)), full((1, cdim)), full((1, LANES)), full((1, LANES)),
                full((1, bw)), full((1, bw)), full(expand.shape), full(sel.shape)]
    args = [p, p, p, p, prm["conv_w"], prm["conv_b"], prm["dt_bias"], prm["a_log"], prm["d_skip"], prm["gn"],
            expand, sel]
    assert len(args) == SSD_N_INPUTS
    out_specs = [pl.BlockSpec((t, bw), lambda *g: (row_block(*g), 0)),
                 pl.BlockSpec((1, heads, B_HEADDIM, B_DSTATE), lambda *g: (seq_block(*g), 0, 0, 0)),
                 pl.BlockSpec((1, SUBLANES, cdim), lambda *g: (seq_block(*g), 0, 0))]
    out_shape = [jax.ShapeDtypeStruct((n_rows, bw), BF16),
                 jax.ShapeDtypeStruct((n_seq, heads, B_HEADDIM, B_DSTATE), F32),
                 jax.ShapeDtypeStruct((n_seq, SUBLANES, cdim), F32)]
    scratch = [pltpu.VMEM((SUBLANES, cdim), F32), pltpu.VMEM((B_DSTATE, bw), F32)]
    return in_specs, args, out_specs, out_shape, scratch


def _ssd(p, cols, prm, s0, c0, bsz, seq_len):
    bw = prm["bw"]
    heads = bw // B_HEADDIM
    cdim = bw + 2 * B_GROUPS * B_DSTATE
    t = min(seq_len, SSD_ROWS)
    nt = seq_len // t
    in_specs, args, out_specs, out_shape, scratch = _ssd_operands(
        p, cols, prm, t, bsz * seq_len, bsz, lambda b, ti: b * nt + ti, lambda b, ti: b)
    if s0 is not None:
        layer = prm["layer"]
        in_specs += [pl.BlockSpec((None, 1, heads, B_HEADDIM, B_DSTATE), lambda b, ti: (layer, b, 0, 0, 0)),
                     pl.BlockSpec((None, 1, SUBLANES, cdim), lambda b, ti: (layer, b, 0, 0))]
        args += [s0, c0]
    return pl.pallas_call(
        functools.partial(_ssd_kernel, t=t, bw=bw, has_init=s0 is not None),
        grid=(bsz, nt),
        in_specs=in_specs,
        out_specs=out_specs,
        out_shape=out_shape,
        scratch_shapes=scratch,
        compiler_params=_cparams(("arbitrary", "arbitrary")),
        name="ssd",
    )(*args)


def _down_sub_dots(a_ref, w_ref, x_ref, gate_ref, o_ref, acc_scr, nseg):
    kc = a_ref.shape[1] // DOWN_K_SPLIT

    def sub_dot(ki):
        rows = slice(ki * kc, (ki + 1) * kc)
        part = jnp.dot(a_ref[:, rows], w_ref[rows, :], preferred_element_type=F32)
        if ki == 0:
            acc_scr[...] = part
        elif ki < DOWN_K_SPLIT - 1:
            acc_scr[...] += part
        else:
            x = x_ref[...]
            acc = acc_scr[...] + part
            o_ref[...] = (_per_seq(x, nseg) + gate_ref[...] * _per_seq(acc, nseg)).reshape(x.shape)

    return [functools.partial(sub_dot, ki) for ki in range(DOWN_K_SPLIT)]


def _emit_interleaved(fillers, phases):
    per_filler = -(-len(phases) // max(len(fillers), 1))
    pos = 0
    for fill in fillers:
        fill()
        for ph in phases[pos:pos + per_filler]:
            ph()
        pos += per_filler
    for ph in phases[pos:]:
        ph()


def _mlp_down_ssd_kernel(a_ref, w_ref, x_ref, gate_ref, *refs, nseg, t, bw, steps_per_seq):
    ssd_in = refs[:SSD_N_INPUTS]
    o_ref, yb_ref, s_out_ref, conv_out_ref, acc_scr, xpad, st = refs[SSD_N_INPUTS:]
    step = pl.program_id(0) * pl.num_programs(1) + pl.program_id(1)
    pos = step % steps_per_seq
    _ssd_reset(None, None, xpad, st, pos == 0)
    gen = _ssd_phases(*ssd_in, yb_ref, xpad, st, t=t, bw=bw)
    phases = [functools.partial(next, gen, None) for _ in range(SSD_PHASES)]
    _emit_interleaved(_down_sub_dots(a_ref, w_ref, x_ref, gate_ref, o_ref, acc_scr, nseg), phases)
    _ssd_flush(conv_out_ref, s_out_ref, xpad, st, pos == steps_per_seq - 1)


def _mlp_down_ssd(a, w_stack, layer, x, mod6, k_gate, seq_len, p_other, cols, prm):
    n, kdim = a.shape
    d = w_stack.shape[2]
    tn = DOWN_TILE
    tm, nseg, tps = _row_tiling(n, seq_len, ROW_TILE)
    ni, nj = n // tm, d // tn
    n_other = p_other.shape[0]
    t = n_other // (ni * nj)
    assert t * ni * nj == n_other and seq_len % t == 0 and t % SUBLANES == 0 and kdim % DOWN_K_SPLIT == 0
    steps_per_seq = seq_len // t
    s_in_specs, s_args, s_out_specs, s_out_shape, s_scratch = _ssd_operands(
        p_other, cols, prm, t, n_other, n_other // seq_len,
        lambda i, j: i * nj + j, lambda i, j: (i * nj + j) // steps_per_seq)
    return pl.pallas_call(
        functools.partial(_mlp_down_ssd_kernel, nseg=nseg, t=t, bw=prm["bw"], steps_per_seq=steps_per_seq),
        grid=(ni, nj),
        in_specs=[
            pl.BlockSpec((tm, kdim), lambda i, j: (i, 0)),
            pl.BlockSpec((None, kdim, tn), lambda i, j: (layer, 0, j)),
            pl.BlockSpec((tm, tn), lambda i, j: (i, j)),
            pl.BlockSpec((None, nseg, 1, tn), lambda i, j: (k_gate, i // tps, 0, j)),
        ] + s_in_specs,
        out_specs=[pl.BlockSpec((tm, tn), lambda i, j: (i, j))] + s_out_specs,
        out_shape=[jax.ShapeDtypeStruct((n, d), F32)] + s_out_shape,
        scratch_shapes=[pltpu.VMEM((tm, tn), F32)] + s_scratch,
        compiler_params=_cparams(("arbitrary", "arbitrary")),
        name="mlp_down_ssd",
    )(a, w_stack, x, mod6, *s_args)


def _mlp_up_ssd_kernel(x_ref, g_ref, sc_ref, sh_ref, w_ref, *refs, nseg, t, bw, steps_per_seq):
    ssd_in = refs[:SSD_N_INPUTS]
    o_ref, yb_ref, s_out_ref, conv_out_ref, h_scr, xpad, st = refs[SSD_N_INPUTS:]
    j = pl.program_id(1)
    pos = (pl.program_id(0) * pl.num_programs(1) + j) % steps_per_seq

    @pl.when(j == 0)
    def _():
        _norm_mod_rows(x_ref, g_ref, sc_ref, sh_ref, h_scr, nseg)

    _ssd_reset(None, None, xpad, st, pos == 0)

    def sub_dot(c0):
        cols = slice(c0, c0 + EPILOGUE_COLS)
        y = jnp.maximum(jnp.dot(h_scr[...], w_ref[:, cols], preferred_element_type=F32), 0.0)
        o_ref[:, cols] = (y * y).astype(o_ref.dtype)

    gen = _ssd_phases(*ssd_in, yb_ref, xpad, st, t=t, bw=bw)
    _emit_interleaved([functools.partial(sub_dot, c0) for c0 in range(0, w_ref.shape[1], EPILOGUE_COLS)],
                      [functools.partial(next, gen, None) for _ in range(SSD_PHASES)])
    _ssd_flush(conv_out_ref, s_out_ref, xpad, st, pos == steps_per_seq - 1)


def _mlp_up_ssd(x, g, mod6, w_stack, layer, seq_len, p_other, cols, prm):
    n, d = x.shape
    m = w_stack.shape[2]
    tn = COL_TILE
    tm, nseg, tps = _row_tiling(n, seq_len, ROW_TILE)
    ni, nj = n // tm, m // tn
    n_other = p_other.shape[0]
    t = n_other // (ni * nj)
    assert t * ni * nj == n_other and seq_len % t == 0 and t % SUBLANES == 0
    steps_per_seq = seq_len // t
    s_in_specs, s_args, s_out_specs, s_out_shape, s_scratch = _ssd_operands(
        p_other, cols, prm, t, n_other, n_other // seq_len,
        lambda i, j: i * nj + j, lambda i, j: (i * nj + j) // steps_per_seq)

    def mod_spec(k):
        return pl.BlockSpec((None, nseg, 1, d), lambda i, j: (k, i // tps, 0, 0))

    return pl.pallas_call(
        functools.partial(_mlp_up_ssd_kernel, nseg=nseg, t=t, bw=prm["bw"], steps_per_seq=steps_per_seq),
        grid=(ni, nj),
        in_specs=[
            pl.BlockSpec((tm, d), lambda i, j: (i, 0)),
            pl.BlockSpec((1, d), lambda i, j: (0, 0)),
            mod_spec(4),
            mod_spec(3),
            pl.BlockSpec((None, d, tn), lambda i, j: (layer, 0, j)),
        ] + s_in_specs,
        out_specs=[pl.BlockSpec((tm, tn), lambda i, j: (i, j))] + s_out_specs,
        out_shape=[jax.ShapeDtypeStruct((n, m), BF16)] + s_out_shape,
        scratch_shapes=[pltpu.VMEM((tm, d), BF16)] + s_scratch,
        compiler_params=_cparams(("arbitrary", "arbitrary")),
        name="mlp_up_ssd",
    )(x, g.reshape(1, d), mod6, mod6, w_stack, *s_args)


def _cmlp_kernel(u_ref, v_ref, lng_ref, lnb_ref, ws_ref, bst_ref, *out_refs, t, n_chunks, keep_v):
    yc_ref = out_refs[0]
    cw = u_ref.shape[1] // C_GROUPS
    rr = lax.broadcasted_iota(jnp.int32, (t, t), 0)
    cc = lax.broadcasted_iota(jnp.int32, (t, t), 1)
    wts = [jnp.where(cc <= rr, ws_ref[g, 0:t, 0:t], 0.0).astype(BF16) for g in range(C_GROUPS)]
    for ci in range(n_chunks):
        rows = slice(ci * t, (ci + 1) * t)
        u = _gelu(u_ref[rows, :])
        gv = _gelu(v_ref[rows, :])
        mu = jnp.mean(gv, axis=-1, keepdims=True)
        dv = gv - mu
        var = jnp.mean(dv * dv, axis=-1, keepdims=True)
        v = dv * lax.rsqrt(var + NORM_EPS) * lng_ref[...] + lnb_ref[...]
        if keep_v:
            out_refs[1][rows, :] = v
        vb = v.astype(BF16)
        for g in range(C_GROUPS):
            lanes = slice(g * cw, (g + 1) * cw)
            mixed = jnp.dot(wts[g], vb[:, lanes], preferred_element_type=F32) + bst_ref[0:t, g:g + 1]
            yc_ref[rows, lanes] = (u[:, lanes] * mixed).astype(yc_ref.dtype)


def _cmlp(p, col_u, col_v, prm, bsz, seq_len, keep_v):
    cw = prm["cw"]
    t = min(seq_len, CMLP_CHUNK)
    t_blk = min(seq_len, 4 * CMLP_CHUNK)
    n = bsz * seq_len

    def full(shape):
        return pl.BlockSpec(shape, lambda i: (0,) * len(shape))

    out_specs = [pl.BlockSpec((t_blk, cw), lambda i: (i, 0))]
    out_shape = [jax.ShapeDtypeStruct((n, cw), BF16)]
    if keep_v:
        out_specs.append(pl.BlockSpec((t_blk, cw), lambda i: (i, 0)))
        out_shape.append(jax.ShapeDtypeStruct((n, cw), F32))
    return pl.pallas_call(
        functools.partial(_cmlp_kernel, t=t, n_chunks=t_blk // t, keep_v=keep_v),
        grid=(n // t_blk,),
        in_specs=[pl.BlockSpec((t_blk, cw), lambda i: (i, col_u // cw)),
                  pl.BlockSpec((t_blk, cw), lambda i: (i, col_v // cw)),
                  full((1, cw)), full((1, cw)),
                  full((C_GROUPS, CMLP_CHUNK, CMLP_CHUNK)), full((CMLP_CHUNK, C_GROUPS))],
        out_specs=out_specs,
        out_shape=out_shape,
        compiler_params=_cparams(("arbitrary",)),
        name="cmlp",
    )(p, p, prm["ln_g"], prm["ln_b"], prm["ws"], prm["bs_t"])


def _run_trunk(x3, mod, st_hgrn, st_ssm, st_conv, keep_v, w):
    bsz, seq_len, d = x3.shape
    depth = mod.shape[0]
    aw = bw = cw = d // 2
    heads_a = aw // A_DK
    x = x3.reshape(bsz * seq_len, d)
    col = w["cols"]
    hgrn_out, ssm_out, conv_out, v_out = [], [], [], []
    for l in range(depth):
        mod6 = mod[l].reshape(bsz, 6, 1, d).transpose(1, 0, 2, 3)
        p, gates = _in_proj(x, w["norm1_g"][l], mod6, w["w_in"], l, seq_len, col["gate"])

        y_a, s_h = _hgrn(p, w["lbc"], l, w["hgrn_onorm_g"][l], st_hgrn, bsz, seq_len, heads_a)
        ssd_prm = dict(bw=bw, layer=l, conv_w=w["ssm_conv_w"][l], conv_b=w["ssm_conv_b"][l][None],
                       dt_bias=w["dt_bias_pad"][l][None], a_log=w["a_log_pad"][l][None],
                       d_skip=w["d_skip"][l][None], gn=w["ssm_onorm_g"][l][None])
        y_b, s_s, conv_tail = _ssd(p, col, ssd_prm, st_ssm, st_conv, bsz, seq_len)
        cm_prm = dict(cw=cw, ln_g=w["cmlp_ln_g"][l][None], ln_b=w["cmlp_ln_b"][l][None],
                      ws=w["cmlp_ws"][l], bs_t=w["cmlp_bs"][l].T)
        c_res = _cmlp(p, col["u"], col["v"], cm_prm, bsz, seq_len, keep_v)
        merged = _merge(y_a, y_b, c_res[0], w["w_branch"], l, gates, d)
        x = _proj_residual(merged, w["w_out"], l, x, mod6, 2, seq_len, "out_proj")
        hid = _mlp_up(x, w["norm2_g"][l], mod6, w["w_up"], l, seq_len)
        x = _proj_residual(hid, w["w_down"], l, x, mod6, 5, seq_len, "mlp_down")

        hgrn_out.append(s_h)
        ssm_out.append(s_s)
        conv_out.append(conv_tail[:, SUBLANES - (CONV_W - 1):, :])
        if keep_v:
            v_out.append(c_res[1].reshape(bsz, seq_len, cw))
    y = _final_norm(x, w["final_g"]).reshape(bsz, seq_len, d)
    return (y, jnp.stack(hgrn_out), jnp.stack(ssm_out), jnp.stack(conv_out),
            jnp.stack(v_out) if keep_v else None)


def _mlp_down_hgrn_kernel(a_ref, w_ref, x_ref, gate_ref, q_ref, f_ref, i_ref, gg_ref, lbc_ref, gn_ref, sel_ref,
                          o_ref, ya_ref, s_ref, acc_scr, bpad, kpad, *, nseg, c, n_chunks, heads, steps_per_seq):
    step = pl.program_id(0) * pl.num_programs(1) + pl.program_id(1)

    @pl.when(step % steps_per_seq == 0)
    def _():
        s_ref[...] = jnp.zeros(s_ref.shape, F32)

    fillers = _down_sub_dots(a_ref, w_ref, x_ref, gate_ref, o_ref, acc_scr, nseg)
    _hgrn_chunks(q_ref, f_ref, i_ref, gg_ref, lbc_ref, gn_ref, sel_ref, ya_ref, s_ref, bpad, kpad,
                 c=c, n_chunks=n_chunks, heads=heads, unroll=True, fillers=fillers)


def _mlp_down_hgrn(a, w_stack, layer, x, mod6, k_gate, seq_len, p_other, lbc_stack, lbc_layer, gn, heads):
    n, kdim = a.shape
    d = w_stack.shape[2]
    tn = DOWN_TILE
    tm, nseg, tps = _row_tiling(n, seq_len, ROW_TILE)
    ni, nj = n // tm, d // tn
    n_other = p_other.shape[0]
    t_step = n_other // (ni * nj)
    assert t_step * ni * nj == n_other and seq_len % t_step == 0 and t_step % SUBLANES == 0
    assert kdim % DOWN_K_SPLIT == 0
    steps_per_seq = seq_len // t_step
    aw = heads * A_DK
    c, h_in_specs, h_args, ya_spec, h_scratch = _hgrn_specs(
        p_other, lbc_stack, lbc_layer, gn, heads, t_step, lambda i, j: i * nj + j)
    state_spec = pl.BlockSpec((1, heads, A_DK, A_DK), lambda i, j: ((i * nj + j) // steps_per_seq, 0, 0, 0))
    return pl.pallas_call(
        functools.partial(_mlp_down_hgrn_kernel, nseg=nseg, c=c, n_chunks=t_step // c, heads=heads,
                          steps_per_seq=steps_per_seq),
        grid=(ni, nj),
        in_specs=[
            pl.BlockSpec((tm, kdim), lambda i, j: (i, 0)),
            pl.BlockSpec((None, kdim, tn), lambda i, j: (layer, 0, j)),
            pl.BlockSpec((tm, tn), lambda i, j: (i, j)),
            pl.BlockSpec((None, nseg, 1, tn), lambda i, j: (k_gate, i // tps, 0, j)),
        ] + h_in_specs,
        out_specs=[pl.BlockSpec((tm, tn), lambda i, j: (i, j)), ya_spec, state_spec],
        out_shape=[jax.ShapeDtypeStruct((n, d), F32),
                   jax.ShapeDtypeStruct((n_other, aw), BF16),
                   jax.ShapeDtypeStruct((n_other // seq_len, heads, A_DK, A_DK), F32)],
        scratch_shapes=[pltpu.VMEM((tm, tn), F32)] + h_scratch,
        compiler_params=_cparams(("arbitrary", "arbitrary")),
        name="mlp_down_hgrn2",
    )(a, w_stack, x, mod6, *h_args)


def _run_trunk_halves(x3, mod, w):
    bsz, seq_len, d = x3.shape
    depth = mod.shape[0]
    hb = bsz // 2
    aw = bw = cw = d // 2
    heads_a = aw // A_DK
    col = w["cols"]
    n_half = hb * seq_len
    x_full = x3.reshape(bsz * seq_len, d)
    xs = [x_full, x_full]
    row0 = [0, n_half]
    mods = [mod[:, :hb], mod[:, hb:]]
    hgrn_out, ssm_out, conv_out = [], [], []
    pending = None
    ahead = None

    def mod6_of(layer, h):
        return mods[h][layer].reshape(hb, 6, 1, d).transpose(1, 0, 2, 3)

    def ssd_prm_of(layer):
        return dict(bw=bw, layer=layer, conv_w=w["ssm_conv_w"][layer], conv_b=w["ssm_conv_b"][layer][None],
                    dt_bias=w["dt_bias_pad"][layer][None], a_log=w["a_log_pad"][layer][None],
                    d_skip=w["d_skip"][layer][None], gn=w["ssm_onorm_g"][layer][None])

    for l in range(depth):
        mod6 = [mod6_of(l, 0), mod6_of(l, 1)]
        ssd_prm = ssd_prm_of(l)
        cm_prm = dict(cw=cw, ln_g=w["cmlp_ln_g"][l][None], ln_b=w["cmlp_ln_b"][l][None],
                      ws=w["cmlp_ws"][l], bs_t=w["cmlp_bs"][l].T)
        gn_a = w["hgrn_onorm_g"][l]

        def mix_merge(h, p, gates, y_a, ssd_res=None):
            y_b, s_s, conv_tail = ssd_res or _ssd(p, col, ssd_prm, None, None, hb, seq_len)
            y_c = _cmlp(p, col["u"], col["v"], cm_prm, hb, seq_len, False)[0]
            merged = _merge(y_a, y_b, y_c, w["w_branch"], l, gates, d)
            x_new = _proj_residual(merged, w["w_out"], l, xs[h], mod6[h], 2, seq_len, "out_proj", row0[h])
            row0[h] = 0
            return x_new, s_s, conv_tail

        if ahead is None:
            p0, g0 = _in_proj(xs[0], w["norm1_g"][l], mod6[0], w["w_in"], l, seq_len, col["gate"],
                              (row0[0], n_half))
            ssd0 = None
        else:
            p0, g0, ssd0 = ahead
        if pending is None:
            ya0, sh0 = _hgrn(p0, w["lbc"], l, gn_a, None, hb, seq_len, heads_a)
        else:
            hid1, x1_mid, mod6_prev = pending
            xs[1], ya0, sh0 = _mlp_down_hgrn(hid1, w["w_down"], l - 1, x1_mid, mod6_prev, 5, seq_len,
                                              p0, w["lbc"], l, gn_a, heads_a)
        p1, g1 = _in_proj(xs[1], w["norm1_g"][l], mod6[1], w["w_in"], l, seq_len, col["gate"], (row0[1], n_half))
        x0, ss0, ct0 = mix_merge(0, p0, g0, ya0, ssd0)
        hid0, ya1, sh1 = _mlp_up_hgrn(x0, w["norm2_g"][l], mod6[0], w["w_up"], l, seq_len,
                                      p1, w["lbc"], gn_a, heads_a)
        xs[0], *ssd1 = _mlp_down_ssd(hid0, w["w_down"], l, x0, mod6[0], 5, seq_len, p1, col, ssd_prm)
        x1_mid, ss1, ct1 = mix_merge(1, p1, g1, ya1, tuple(ssd1))
        if l + 1 < depth:
            p0n, g0n = _in_proj(xs[0], w["norm1_g"][l + 1], mod6_of(l + 1, 0), w["w_in"], l + 1, seq_len,
                                col["gate"], (0, n_half))
            hid1, *ssd0n = _mlp_up_ssd(x1_mid, w["norm2_g"][l], mod6[1], w["w_up"], l, seq_len,
                                       p0n, col, ssd_prm_of(l + 1))
            ahead = (p0n, g0n, tuple(ssd0n))
        else:
            hid1 = _mlp_up(x1_mid, w["norm2_g"][l], mod6[1], w["w_up"], l, seq_len)
        pending = (hid1, x1_mid, mod6[1])

        hgrn_out.append(jnp.concatenate([sh0, sh1], axis=0))
        ssm_out.append(jnp.concatenate([ss0, ss1], axis=0))
        conv_out.append(jnp.concatenate([ct0, ct1], axis=0)[:, SUBLANES - (CONV_W - 1):, :])
    hid1, x1_mid, mod6_prev = pending
    xs[1] = _proj_residual(hid1, w["w_down"], depth - 1, x1_mid, mod6_prev, 5, seq_len, "mlp_down")
    y = _final_norm_pair(xs[0], xs[1], w["final_g"]).reshape(bsz, seq_len, d)
    return (y, jnp.stack(hgrn_out), jnp.stack(ssm_out), jnp.stack(conv_out), None)


def kernel(x_prompt, x_sample, state_hgrn, state_ssm, state_conv, c_prompt, c_sample, norm1_g, norm2_g,
           w_mod, b_mod, w_in, hgrn_lb, hgrn_onorm_g, ssm_conv_w, ssm_conv_b, ssm_dt_bias, ssm_a_log, ssm_d,
           ssm_onorm_g, cmlp_ln_g, cmlp_ln_b, cmlp_ws, cmlp_bs, w_branch, w_out, w_up, w_down, final_g):
    d = x_prompt.shape[-1]
    depth = w_in.shape[0]
    aw = bw = cw = d // 2
    n_bc = B_GROUPS * B_DSTATE
    heads_b = bw // B_HEADDIM
    assert heads_b <= LANES and bw % LANES == 0 and DT_PAD % LANES == 0

    o_dt = 4 * aw + bw + bw + 2 * n_bc
    o_u = o_dt + heads_b
    w_in_t = jnp.swapaxes(w_in, 1, 2).astype(BF16)
    w_in_r = jnp.concatenate(
        [w_in_t[:, :o_u], jnp.zeros((depth, DT_PAD - heads_b, d), BF16), w_in_t[:, o_u:]], axis=1)
    cols = dict(z=4 * aw, xs=4 * aw + bw, bc=4 * aw + 2 * bw, dt=o_dt)
    cols["u"] = o_dt + DT_PAD
    cols["v"] = cols["u"] + cw
    cols["gate"] = cols["v"] + cw
    assert cols["gate"] % IN_TILE == 0 and w_in_r.shape[1] % IN_TILE == 0

    pad_h = lambda a: jnp.pad(a.astype(F32), ((0, 0), (0, LANES - heads_b)))
    w = dict(
        cols=cols, w_in=w_in_r,
        norm1_g=norm1_g, norm2_g=norm2_g, final_g=final_g,
        lbc=_lb_consts(hgrn_lb), hgrn_onorm_g=hgrn_onorm_g,
        ssm_conv_w=ssm_conv_w, ssm_conv_b=ssm_conv_b,
        dt_bias_pad=pad_h(ssm_dt_bias), a_log_pad=pad_h(ssm_a_log),
        d_skip=jnp.repeat(ssm_d.astype(F32), B_HEADDIM, axis=1), ssm_onorm_g=ssm_onorm_g,
        cmlp_ln_g=cmlp_ln_g, cmlp_ln_b=cmlp_ln_b, cmlp_ws=cmlp_ws, cmlp_bs=cmlp_bs,
        w_branch=w_branch.astype(BF16).reshape(depth, N_BRANCH, aw, d), w_out=w_out.astype(BF16),
        w_up=w_up.astype(BF16), w_down=w_down.astype(BF16),
    )

    nb = x_prompt.shape[0]
    mod = _modulation(jnp.concatenate([c_prompt, c_sample], axis=0), w_mod, b_mod)
    if nb % 2 == 0:
        y_p, hgrn_p, ssm_p, conv_p, _ = _run_trunk_halves(x_prompt, mod[:, :nb], w)
    else:
        y_p, hgrn_p, ssm_p, conv_p, _ = _run_trunk(x_prompt, mod[:, :nb], None, None, None, False, w)
    conv_pad = jnp.pad(state_conv, ((0, 0), (0, 0), (SUBLANES - (CONV_W - 1), 0), (0, 0)))
    y_s, hgrn_s, ssm_s, conv_s, v_s = _run_trunk(x_sample, mod[:, nb:], state_hgrn, state_ssm, conv_pad, True, w)
    return (y_p, y_s, hgrn_p, ssm_p, conv_p, hgrn_s, ssm_s, conv_s, v_s)
```

```python
import functools
import math

import numpy as np
import jax
import jax.numpy as jnp
from jax import lax
from jax.experimental import pallas as pl
from jax.experimental.pallas import tpu as pltpu

F32 = jnp.float32
BF16 = jnp.bfloat16

A_DK = 128
B_HEADDIM = 64
B_GROUPS = 2
B_DSTATE = 128
CONV_W = 4
C_GROUPS = 4
CMLP_CHUNK = 128
N_BRANCH = 3
SCAN_CHUNK = 64
NORM_EPS = 1e-6
LB_FLOOR = 1e-30
LOG2E = 1.4426950408889634

LANES = 128
SUBLANES = 8
VMEM_LIMIT = 56 * 1024 * 1024
DIAG = SUBLANES
DT_PAD = 512
ROW_TILE = 1024
COL_TILE = 1024
IN_TILE = 1536
MERGE_TILE = 512
MERGE_ROW_TILE = 1024
UP_TILE = 2048
OUT_TILE = 512
OUT_ROW_TILE = 2048
DOWN_TILE = 256
EPILOGUE_COLS = 256
SSD_ROWS = 128
DOWN_PIECES_SSD = 16
DOWN_PIECES_HGRN = 32
NORM_ROWS = 128


def _cparams(sem):
    return pltpu.CompilerParams(dimension_semantics=sem, vmem_limit_bytes=VMEM_LIMIT)


def _split3(x):
    hi = x.astype(BF16)
    r1 = x - hi.astype(F32)
    mid = r1.astype(BF16)
    lo = (r1 - mid.astype(F32)).astype(BF16)
    return hi, mid, lo


def _dot3(sel, x):
    hi, mid, lo = _split3(x)
    d = lambda p: jnp.dot(sel, p, preferred_element_type=F32)
    return d(hi) + d(mid) + d(lo)


def _dot3_rhs(x, sel):
    hi, mid, lo = _split3(x)
    d = lambda p: jnp.dot(p, sel, preferred_element_type=F32)
    return d(hi) + d(mid) + d(lo)


def _silu(x):
    return x / (1.0 + jnp.exp(-x))


def _softplus(x):
    return jnp.maximum(x, 0.0) + jnp.log1p(jnp.exp(-jnp.abs(x)))


def _gelu(x):
    return 0.5 * x * (1.0 + lax.erf(x * (1.0 / math.sqrt(2.0))))


def _lb_kernel(lb_ref, out_ref, *, depth):
    x = lb_ref[...]
    m = jnp.max(x, axis=0, keepdims=True)
    e = jnp.exp(x - m)
    p = e / jnp.sum(e, axis=0, keepdims=True)
    acc = jnp.zeros_like(p[0:1])
    zeros5 = jnp.zeros((SUBLANES - 3, x.shape[1]), F32)
    for l in range(depth):
        acc = acc + p[l:l + 1]
        lb = acc - p[0:1]
        out_ref[l] = jnp.concatenate(
            [jnp.log(jnp.maximum(lb, LB_FLOOR)), jnp.log1p(-lb), 1.0 - lb, zeros5], axis=0)


def _lb_consts(hgrn_lb):
    depth, aw = hgrn_lb.shape
    return pl.pallas_call(
        functools.partial(_lb_kernel, depth=depth),
        out_shape=jax.ShapeDtypeStruct((depth, SUBLANES, aw), F32),
        name="hgrn_lb",
    )(hgrn_lb.astype(F32))


def _mod_kernel(c_ref, w_ref, b_ref, o_ref):
    cs = _silu(c_ref[...]).astype(BF16)
    o_ref[...] = jnp.dot(cs, w_ref[...].astype(BF16), preferred_element_type=F32) + b_ref[...]


def _modulation(c_all, w_mod, b_mod):
    depth, d, n6 = w_mod.shape
    s = c_all.shape[0]
    tn = 1024
    return pl.pallas_call(
        _mod_kernel,
        grid=(depth, n6 // tn),
        in_specs=[
            pl.BlockSpec((s, d), lambda l, j: (0, 0)),
            pl.BlockSpec((None, d, tn), lambda l, j: (l, 0, j)),
            pl.BlockSpec((None, 1, tn), lambda l, j: (l, 0, j)),
        ],
        out_specs=pl.BlockSpec((None, s, tn), lambda l, j: (l, 0, j)),
        out_shape=jax.ShapeDtypeStruct((depth, s, n6), F32),
        compiler_params=_cparams(("arbitrary", "arbitrary")),
        name="adaln_mod",
    )(c_all, w_mod, b_mod.reshape(depth, 1, n6))


def _row_tiling(n_tok, seq_len, tm_max):
    tm = min(tm_max, n_tok)
    if seq_len >= tm:
        assert seq_len % tm == 0
        return tm, 1, seq_len // tm
    assert tm % seq_len == 0
    return tm, tm // seq_len, 1


def _per_seq(x, nseg):
    tm, d = x.shape
    return x.reshape(nseg, tm // nseg, d)


def _final_norm_kernel(x_ref, g_ref, o_ref):
    x = x_ref[...]
    o_ref[...] = x * lax.rsqrt(jnp.mean(x * x, axis=-1, keepdims=True) + NORM_EPS) * g_ref[...]


def _final_norm(x, g):
    n, d = x.shape
    tm = min(512, n)
    return pl.pallas_call(
        _final_norm_kernel,
        grid=(n // tm,),
        in_specs=[pl.BlockSpec((tm, d), lambda i: (i, 0)), pl.BlockSpec((1, d), lambda i: (0, 0))],
        out_specs=pl.BlockSpec((tm, d), lambda i: (i, 0)),
        out_shape=jax.ShapeDtypeStruct((n, d), F32),
        compiler_params=_cparams(("arbitrary",)),
        name="final_norm",
    )(x, g.reshape(1, d))


def _final_norm_pair_kernel(xa_ref, xb_ref, g_ref, o_ref, *, nt):
    def norm(x):
        return x * lax.rsqrt(jnp.mean(x * x, axis=-1, keepdims=True) + NORM_EPS) * g_ref[...]

    @pl.when(pl.program_id(0) < nt)
    def _():
        o_ref[...] = norm(xa_ref[...])

    @pl.when(pl.program_id(0) >= nt)
    def _():
        o_ref[...] = norm(xb_ref[...])


def _final_norm_pair(xa, xb, g):
    n, d = xa.shape
    tm = min(512, n)
    nt = n // tm
    return pl.pallas_call(
        functools.partial(_final_norm_pair_kernel, nt=nt),
        grid=(2 * nt,),
        in_specs=[pl.BlockSpec((tm, d), lambda i: (jnp.minimum(i, nt - 1), 0)),
                  pl.BlockSpec((tm, d), lambda i: (jnp.maximum(i - nt, 0), 0)),
                  pl.BlockSpec((1, d), lambda i: (0, 0))],
        out_specs=pl.BlockSpec((tm, d), lambda i: (i, 0)),
        out_shape=jax.ShapeDtypeStruct((2 * n, d), F32),
        compiler_params=_cparams(("arbitrary",)),
        name="final_norm",
    )(xa, xb, g.reshape(1, d))


def _norm_mod_rows(x_ref, g_ref, sc_ref, sh_ref, h_scr, nseg):
    tm = x_ref.shape[0]
    seg = tm // nseg
    rb = min(seg, NORM_ROWS)
    g = g_ref[...]

    def body(r, carry):
        rows = pl.ds(pl.multiple_of(r * rb, rb), rb)
        s = r // (seg // rb)
        x = x_ref[rows, :]
        gain = g * (1.0 + sc_ref[s])
        inv = lax.rsqrt(jnp.mean(x * x, axis=-1, keepdims=True) + NORM_EPS)
        h_scr[rows, :] = (x * inv * gain + sh_ref[s]).astype(BF16)
        return carry

    lax.fori_loop(0, tm // rb, body, 0, unroll=2 if (tm // rb) % 2 == 0 else 1)


def _mlp_up_kernel(x_ref, g_ref, sc_ref, sh_ref, w_ref, o_ref, h_scr, *, nseg):
    @pl.when(pl.program_id(1) == 0)
    def _():
        _norm_mod_rows(x_ref, g_ref, sc_ref, sh_ref, h_scr, nseg)

    y = jnp.maximum(jnp.dot(h_scr[...], w_ref[...], preferred_element_type=F32), 0.0)
    o_ref[...] = (y * y).astype(o_ref.dtype)


def _in_proj_kernel(x_ref, g_ref, sc_ref, sh_ref, w_ref, p_ref, gate_ref, h_scr, *, nseg, n_main):
    j = pl.program_id(1)

    @pl.when(j == 0)
    def _():
        _norm_mod_rows(x_ref, g_ref, sc_ref, sh_ref, h_scr, nseg)

    def project():
        return lax.dot_general(h_scr[...], w_ref[...], (((1,), (1,)), ((), ())), preferred_element_type=F32)

    @pl.when(j < n_main)
    def _():
        p_ref[...] = project()

    @pl.when(j >= n_main)
    def _():
        gate_ref[...] = project().astype(gate_ref.dtype)


def _norm_proj_call(kern, x, g, mod6, k_scale, k_shift, w_stack, layer, seq_len, tn, out_specs, out_shape, name):
    n, d = x.shape
    m = w_stack.shape[2]
    tm, nseg, tps = _row_tiling(n, seq_len, ROW_TILE)

    def mod_spec(k):
        return pl.BlockSpec((None, nseg, 1, d), lambda i, j: (k, i // tps, 0, 0))

    return pl.pallas_call(
        functools.partial(kern, nseg=nseg),
        grid=(n // tm, m // tn),
        in_specs=[
            pl.BlockSpec((tm, d), lambda i, j: (i, 0)),
            pl.BlockSpec((1, d), lambda i, j: (0, 0)),
            mod_spec(k_scale),
            mod_spec(k_shift),
            pl.BlockSpec((None, d, tn), lambda i, j: (layer, 0, j)),
        ],
        out_specs=out_specs(tm),
        out_shape=out_shape,
        scratch_shapes=[pltpu.VMEM((tm, d), BF16)],
        compiler_params=_cparams(("arbitrary", "arbitrary")),
        name=name,
    )(x, g.reshape(1, d), mod6, mod6, w_stack)


def _mlp_up(x, g, mod6, w_stack, layer, seq_len):
    n = x.shape[0]
    m = w_stack.shape[2]
    tn = UP_TILE
    return _norm_proj_call(
        _mlp_up_kernel, x, g, mod6, 4, 3, w_stack, layer, seq_len, tn,
        lambda tm: pl.BlockSpec((tm, tn), lambda i, j: (i, j)),
        jax.ShapeDtypeStruct((n, m), BF16), "mlp_up")


def _in_proj(x, g, mod6, w_stack, layer, seq_len, main_cols, rows=None):
    d = x.shape[1]
    row0, n = rows if rows is not None else (0, x.shape[0])
    m = w_stack.shape[1]
    tn = IN_TILE
    n_main = main_cols // tn
    tm, nseg, tps = _row_tiling(n, seq_len, ROW_TILE)

    def mod_spec(k):
        return pl.BlockSpec((None, nseg, 1, d), lambda i, j: (k, i // tps, 0, 0))

    return pl.pallas_call(
        functools.partial(_in_proj_kernel, nseg=nseg, n_main=n_main),
        grid=(n // tm, m // tn),
        in_specs=[
            pl.BlockSpec((tm, d), lambda i, j: (i + row0 // tm, 0)),
            pl.BlockSpec((1, d), lambda i, j: (0, 0)),
            mod_spec(1),
            mod_spec(0),
            pl.BlockSpec((None, tn, d), lambda i, j: (layer, j, 0)),
        ],
        out_specs=[pl.BlockSpec((tm, tn), lambda i, j: (i, jnp.minimum(j, n_main - 1))),
                   pl.BlockSpec((tm, tn), lambda i, j: (i, jnp.maximum(j - n_main, 0)))],
        out_shape=[jax.ShapeDtypeStruct((n, main_cols), F32), jax.ShapeDtypeStruct((n, m - main_cols), BF16)],
        scratch_shapes=[pltpu.VMEM((tm, d), BF16)],
        compiler_params=_cparams(("arbitrary", "arbitrary")),
        name="in_proj",
    )(x, g.reshape(1, d), mod6, mod6, w_stack)


def _proj_res_kernel(a_ref, w_ref, x_ref, gate_ref, o_ref, *, nseg):
    acc = jnp.dot(a_ref[...], w_ref[...], preferred_element_type=F32)
    x = x_ref[...]
    o_ref[...] = (_per_seq(x, nseg) + gate_ref[...] * _per_seq(acc, nseg)).reshape(x.shape)


def _proj_residual(a, w_stack, layer, x, mod6, k_gate, seq_len, name, x_row0=0):
    n, kdim = a.shape
    d = w_stack.shape[2]
    tn, tm_max = (OUT_TILE, OUT_ROW_TILE) if kdim <= d else (DOWN_TILE, ROW_TILE)
    tm, nseg, tps = _row_tiling(n, seq_len, tm_max)
    return pl.pallas_call(
        functools.partial(_proj_res_kernel, nseg=nseg),
        grid=(n // tm, d // tn),
        in_specs=[
            pl.BlockSpec((tm, kdim), lambda i, j: (i, 0)),
            pl.BlockSpec((None, kdim, tn), lambda i, j: (layer, 0, j)),
            pl.BlockSpec((tm, tn), lambda i, j: (i + x_row0 // tm, j)),
            pl.BlockSpec((None, nseg, 1, tn), lambda i, j: (k_gate, i // tps, 0, j)),
        ],
        out_specs=pl.BlockSpec((tm, tn), lambda i, j: (i, j)),
        out_shape=jax.ShapeDtypeStruct((n, d), F32),
        compiler_params=_cparams(("arbitrary", "arbitrary")),
        name=name,
    )(a, w_stack, x, mod6)


def _merge_kernel(ya_ref, yb_ref, yc_ref, wa_ref, wb_ref, wc_ref, ga_ref, gb_ref, gc_ref, o_ref):
    acc = None
    for y_ref, w_ref, g_ref in ((ya_ref, wa_ref, ga_ref), (yb_ref, wb_ref, gb_ref), (yc_ref, wc_ref, gc_ref)):
        gate = 1.0 / (1.0 + jnp.exp(-g_ref[...].astype(F32)))
        term = gate * jnp.dot(y_ref[...], w_ref[...], preferred_element_type=F32)
        acc = term if acc is None else acc + term
    o_ref[...] = acc.astype(o_ref.dtype)


def _merge(ya, yb, yc, w_stack, layer, gates, d):
    n, bw = ya.shape
    tn = MERGE_TILE
    tm = min(MERGE_ROW_TILE, n)
    gsteps = d // tn
    y_spec = pl.BlockSpec((tm, bw), lambda i, j: (i, 0))

    def w_spec(k):
        return pl.BlockSpec((None, None, bw, tn), lambda i, j: (layer, k, 0, j))

    def g_spec(k):
        return pl.BlockSpec((tm, tn), lambda i, j: (i, k * gsteps + j))

    return pl.pallas_call(
        _merge_kernel,
        grid=(n // tm, d // tn),
        in_specs=[y_spec, y_spec, y_spec, w_spec(0), w_spec(1), w_spec(2), g_spec(0), g_spec(1), g_spec(2)],
        out_specs=pl.BlockSpec((tm, tn), lambda i, j: (i, j)),
        out_shape=jax.ShapeDtypeStruct((n, d), BF16),
        compiler_params=_cparams(("arbitrary", "arbitrary")),
        name="merge",
    )(ya, yb, yc, w_stack, w_stack, w_stack, gates, gates, gates)


def _hgrn_levels(c):
    lv, m = [], c // 2
    while m >= DIAG:
        lv.append(m)
        m //= 2
    return tuple(lv)


def _hgrn_select(c):
    tri = np.arange(c)[None, :] <= np.arange(c)[:, None]
    return jnp.asarray(tri.astype(np.float32), dtype=BF16)


def _hgrn_kernel(*refs, c, n_chunks, heads, has_init):
    if has_init:
        q_ref, f_ref, i_ref, g_ref, lbc_ref, gn_ref, sel_ref, s0_ref, ya_ref, s_ref, bpad, kpad = refs
    else:
        q_ref, f_ref, i_ref, g_ref, lbc_ref, gn_ref, sel_ref, ya_ref, s_ref, bpad, kpad = refs

    @pl.when(pl.program_id(1) == 0)
    def _():
        s_ref[...] = s0_ref[...] if has_init else jnp.zeros(s_ref.shape, F32)

    _hgrn_chunks(q_ref, f_ref, i_ref, g_ref, lbc_ref, gn_ref, sel_ref, ya_ref, s_ref, bpad, kpad,
                 c=c, n_chunks=n_chunks, heads=heads, unroll=False)


def _hgrn_chunks(q_ref, f_ref, i_ref, g_ref, lbc_ref, gn_ref, sel_ref, ya_ref, s_ref, bpad, kpad,
                 *, c, n_chunks, heads, unroll, fillers=()):
    levels = _hgrn_levels(c)
    nl = len(levels)
    width = heads * A_DK
    head_lanes = [slice(hh * A_DK, (hh + 1) * A_DK) for hh in range(heads)]

    bpad[0:DIAG, :] = jnp.zeros((DIAG, width), F32)
    kpad[0:DIAG, :] = jnp.zeros((DIAG, width), F32)

    sel = sel_ref[...]
    log_lb = lbc_ref[0:1, :]
    log1m_lb = lbc_ref[1:2, :]
    one_m_lb = lbc_ref[2:3, :]
    row = lax.broadcasted_iota(jnp.int32, (c, width), 0)
    rr = lax.broadcasted_iota(jnp.int32, (c, c), 0)
    cc = lax.broadcasted_iota(jnp.int32, (c, c), 1)
    upper = [(row & m) != 0 for m in levels]
    same = [(rr ^ cc) < 2 * m for m in levels]
    dmat = jnp.where(((rr ^ cc) < DIAG) & (cc <= rr), rr - cc, -1)
    nt_dims = (((1,), (1,)), ((), ()))
    tn_dims = (((0,), (0,)), ((), ()))

    def chunk(ci):
        rows = slice(ci * c, (ci + 1) * c) if unroll else pl.ds(pl.multiple_of(ci * c, c), c)
        z = f_ref[rows, :]
        aq = q_ref[rows, :]

        log_sig = jnp.minimum(z, 0.0) - jnp.log(1.0 + jnp.exp(-jnp.abs(z)))
        bb = log1m_lb + log_sig
        log_f = jnp.maximum(log_lb, bb) + jnp.log(1.0 + jnp.exp(-jnp.abs(log_lb - bb)))
        k = one_m_lb / (1.0 + jnp.exp(z))
        q = _silu(aq)

        b = _dot3(sel, log_f * LOG2E)
        bpad[DIAG:DIAG + c, :] = b
        kpad[DIAG:DIAG + c, :] = k
        b_tot = bpad[DIAG + c - 1:DIAG + c, :]
        vb = i_ref[rows, :].astype(BF16)
        q_in = (q * jnp.exp2(b)).astype(BF16)
        k_out = (k * jnp.exp2(b_tot - b)).astype(BF16)
        yield

        att = [jnp.zeros((c, c), F32) for _ in range(heads)]
        for li in range(nl):
            m = levels[li]
            ref_b = jnp.concatenate(
                [jnp.broadcast_to(bpad[DIAG + g0 + m - 1:DIAG + g0 + m, :], (2 * m, width))
                 for g0 in range(0, c, 2 * m)], axis=0)
            e = jnp.exp2(-jnp.abs(b - ref_b))
            qs = jnp.where(upper[li], q * e, 0.0).astype(BF16)
            ks = jnp.where(upper[li], 0.0, k * e).astype(BF16)
            for hh, lanes in enumerate(head_lanes):
                a_l = lax.dot_general(qs[:, lanes], ks[:, lanes], nt_dims, preferred_element_type=F32)
                att[hh] = att[hh] + jnp.where(same[li], a_l, 0.0)
        yield

        for dlt in range(DIAG):
            if dlt == 0:
                w = q * k
            else:
                ksh = kpad[DIAG - dlt:DIAG - dlt + c, :]
                bsh = bpad[DIAG - dlt:DIAG - dlt + c, :]
                w = q * ksh * jnp.exp2(b - bsh)
            for hh, lanes in enumerate(head_lanes):
                col = jnp.sum(w[:, lanes], axis=-1, keepdims=True)
                att[hh] = jnp.where(dmat == dlt, col, att[hh])
            if dlt in (DIAG // 2 - 1, DIAG - 1):
                yield

        decay_row = jnp.exp2(b_tot)
        outs = []
        for hh, lanes in enumerate(head_lanes):
            s_prev = s_ref[0, hh]
            o = jnp.dot(q_in[:, lanes], s_prev.astype(BF16), preferred_element_type=F32)
            o = o + jnp.dot(att[hh].astype(BF16), vb[:, lanes], preferred_element_type=F32)
            decay_col = jnp.broadcast_to(decay_row[:, lanes], (A_DK, A_DK)).T
            s_ref[0, hh] = decay_col * s_prev + lax.dot_general(
                k_out[:, lanes], vb[:, lanes], tn_dims, preferred_element_type=F32)
            outs.append(o * lax.rsqrt(jnp.mean(o * o, axis=-1, keepdims=True) + NORM_EPS))
        y = jnp.concatenate(outs, axis=1) * gn_ref[...] * _silu(g_ref[rows, :])
        ya_ref[rows, :] = y.astype(ya_ref.dtype)
        yield

    if not unroll:
        def body(ci, carry):
            for _ in chunk(ci):
                pass
            return carry

        lax.fori_loop(0, n_chunks, body, 0)
        return

    n_phases = 5
    _emit_interleaved(fillers, [functools.partial(next, gen, None)
                                for gen in map(chunk, range(n_chunks)) for _ in range(n_phases)])


def _hgrn_specs(p, lbc_stack, layer, gn, heads, t_blk, row_block):
    aw = heads * A_DK
    c = min(t_blk, SCAN_CHUNK)
    sel = _hgrn_select(c)

    def col_spec(seg):
        return pl.BlockSpec((t_blk, aw), lambda *g: (row_block(*g), seg))

    in_specs = [col_spec(0), col_spec(1), col_spec(2), col_spec(3),
                pl.BlockSpec((None, SUBLANES, aw), lambda *g: (layer, 0, 0)),
                pl.BlockSpec((1, aw), lambda *g: (0, 0)),
                pl.BlockSpec(sel.shape, lambda *g: (0, 0))]
    args = [p, p, p, p, lbc_stack, gn.reshape(1, aw), sel]
    scratch = [pltpu.VMEM((c + DIAG, aw), F32), pltpu.VMEM((c + DIAG, aw), F32)]
    return c, in_specs, args, pl.BlockSpec((t_blk, aw), lambda *g: (row_block(*g), 0)), scratch


def _mlp_up_hgrn_kernel(x_ref, g_ref, sc_ref, sh_ref, w_ref, q_ref, f_ref, i_ref, gg_ref, lbc_ref, gn_ref, sel_ref,
                        o_ref, ya_ref, s_ref, h_scr, bpad, kpad, *, nseg, c, n_chunks, heads, steps_per_seq):
    j = pl.program_id(1)
    step = pl.program_id(0) * pl.num_programs(1) + j

    @pl.when(j == 0)
    def _():
        _norm_mod_rows(x_ref, g_ref, sc_ref, sh_ref, h_scr, nseg)

    @pl.when(step % steps_per_seq == 0)
    def _():
        s_ref[...] = jnp.zeros(s_ref.shape, F32)

    def sub_dot(c0):
        cols = slice(c0, c0 + EPILOGUE_COLS)
        y = jnp.maximum(jnp.dot(h_scr[...], w_ref[:, cols], preferred_element_type=F32), 0.0)
        o_ref[:, cols] = (y * y).astype(o_ref.dtype)

    fillers = [functools.partial(sub_dot, c0) for c0 in range(0, w_ref.shape[1], EPILOGUE_COLS)]
    _hgrn_chunks(q_ref, f_ref, i_ref, gg_ref, lbc_ref, gn_ref, sel_ref, ya_ref, s_ref, bpad, kpad,
                 c=c, n_chunks=n_chunks, heads=heads, unroll=True, fillers=fillers)


def _mlp_up_hgrn(x, g, mod6, w_stack, layer, seq_len, p_other, lbc_stack, gn, heads):
    n, d = x.shape
    m = w_stack.shape[2]
    tn = COL_TILE
    tm, nseg, tps = _row_tiling(n, seq_len, ROW_TILE)
    ni, nj = n // tm, m // tn
    n_other = p_other.shape[0]
    t_step = n_other // (ni * nj)
    assert t_step * ni * nj == n_other and seq_len % t_step == 0 and t_step % SUBLANES == 0
    steps_per_seq = seq_len // t_step
    aw = heads * A_DK
    c, h_in_specs, h_args, ya_spec, h_scratch = _hgrn_specs(
        p_other, lbc_stack, layer, gn, heads, t_step, lambda i, j: i * nj + j)

    def mod_spec(k):
        return pl.BlockSpec((None, nseg, 1, d), lambda i, j: (k, i // tps, 0, 0))

    state_spec = pl.BlockSpec((1, heads, A_DK, A_DK), lambda i, j: ((i * nj + j) // steps_per_seq, 0, 0, 0))
    return pl.pallas_call(
        functools.partial(_mlp_up_hgrn_kernel, nseg=nseg, c=c, n_chunks=t_step // c, heads=heads,
                          steps_per_seq=steps_per_seq),
        grid=(ni, nj),
        in_specs=[
            pl.BlockSpec((tm, d), lambda i, j: (i, 0)),
            pl.BlockSpec((1, d), lambda i, j: (0, 0)),
            mod_spec(4),
            mod_spec(3),
            pl.BlockSpec((None, d, tn), lambda i, j: (layer, 0, j)),
        ] + h_in_specs,
        out_specs=[pl.BlockSpec((tm, tn), lambda i, j: (i, j)), ya_spec, state_spec],
        out_shape=[jax.ShapeDtypeStruct((n, m), BF16),
                   jax.ShapeDtypeStruct((n_other, aw), BF16),
                   jax.ShapeDtypeStruct((n_other // seq_len, heads, A_DK, A_DK), F32)],
        scratch_shapes=[pltpu.VMEM((tm, d), BF16)] + h_scratch,
        compiler_params=_cparams(("arbitrary", "arbitrary")),
        name="mlp_up_hgrn2",
    )(x, g.reshape(1, d), mod6, mod6, w_stack, *h_args)


def _hgrn(p, lbc_stack, layer, gn, s0, bsz, seq_len, heads):
    c = min(seq_len, SCAN_CHUNK)
    t_blk = min(seq_len, 256)
    nt = seq_len // t_blk
    aw = heads * A_DK

    def col_spec(seg):
        return pl.BlockSpec((t_blk, aw), lambda b, t: (b * nt + t, seg))

    sel = _hgrn_select(c)
    in_specs = [col_spec(0), col_spec(1), col_spec(2), col_spec(3),
                pl.BlockSpec((None, SUBLANES, aw), lambda b, t: (layer, 0, 0)),
                pl.BlockSpec((1, aw), lambda b, t: (0, 0)),
                pl.BlockSpec(sel.shape, lambda b, t: (0, 0))]
    args = [p, p, p, p, lbc_stack, gn.reshape(1, aw), sel]
    state_spec = pl.BlockSpec((1, heads, A_DK, A_DK), lambda b, t: (b, 0, 0, 0))
    if s0 is not None:
        in_specs.append(pl.BlockSpec((None, 1, heads, A_DK, A_DK), lambda b, t: (layer, b, 0, 0, 0)))
        args.append(s0)
    return pl.pallas_call(
        functools.partial(_hgrn_kernel, c=c, n_chunks=t_blk // c, heads=heads, has_init=s0 is not None),
        grid=(bsz, nt),
        in_specs=in_specs,
        out_specs=[pl.BlockSpec((t_blk, aw), lambda b, t: (b * nt + t, 0)), state_spec],
        out_shape=[jax.ShapeDtypeStruct((bsz * seq_len, aw), BF16),
                   jax.ShapeDtypeStruct((bsz, heads, A_DK, A_DK), F32)],
        scratch_shapes=[pltpu.VMEM((c + DIAG, aw), F32), pltpu.VMEM((c + DIAG, aw), F32)],
        compiler_params=_cparams(("arbitrary", "arbitrary")),
        name="hgrn2",
    )(*args)


def _ssd_kernel(*refs, t, bw, has_init):
    n_in = SSD_N_INPUTS
    s0_ref, c0_ref = (refs[n_in], refs[n_in + 1]) if has_init else (None, None)
    rest = refs[n_in + 2:] if has_init else refs[n_in:]
    yb_ref, s_out_ref, conv_out_ref, xpad, st = rest
    ti = pl.program_id(1)
    _ssd_reset(s0_ref, c0_ref, xpad, st, ti == 0)
    for _ in _ssd_phases(*refs[:n_in], yb_ref, xpad, st, t=t, bw=bw):
        pass
    _ssd_flush(conv_out_ref, s_out_ref, xpad, st, ti == pl.num_programs(1) - 1)


SSD_N_INPUTS = 12
SSD_PHASES = 9


def _ssd_phases(z_ref, xs_ref, bc_ref, dt_ref, cw_ref, cb_ref, dtb_ref, alog_ref, dsk_ref, gn_ref, exp_ref, sel_ref,
                yb_ref, xpad, st, *, t, bw):
    gw = bw // B_GROUPS
    n_bc = B_GROUPS * B_DSTATE
    pad = SUBLANES

    cur = jnp.concatenate([xs_ref[...], bc_ref[...]], axis=1)
    prev = xpad[...]
    row8 = lax.broadcasted_iota(jnp.int32, (pad, cur.shape[1]), 0)
    conv = cb_ref[...]
    for j in range(CONV_W):
        s = CONV_W - 1 - j
        if s == 0:
            tap = cur
        else:
            rolled = pltpu.roll(cur, s, 0)
            head = jnp.where(row8 < s, pltpu.roll(prev, s, 0), rolled[0:pad])
            tap = jnp.concatenate([head, rolled[pad:]], axis=0)
        conv = conv + tap * cw_ref[j:j + 1, :]
    xbc = _silu(conv)
    x = xbc[:, 0:bw]
    yield

    dt = _softplus(dt_ref[...] + dtb_ref[...])
    a = dt * (-jnp.exp(alog_ref[...]))
    cs = _dot3(sel_ref[...], a)
    a_cum = cs[0:t]
    a_tot = cs[t:2 * t]
    ex = _dot3_rhs(jnp.concatenate([dt, a_cum, a_tot - a_cum], axis=0), exp_ref[...])
    dt_e = ex[0:t]
    acum_e = ex[t:2 * t]
    dec_e = ex[2 * t:3 * t]
    atot_e = acum_e[t - 1:t, :]

    xdt = x * dt_e
    xw = (xdt * jnp.exp(dec_e)).astype(BF16)
    xdt_b = xdt.astype(BF16)
    a_cum_t = a_cum.T
    rr = lax.broadcasted_iota(jnp.int32, (t, t), 0)
    cc = lax.broadcasted_iota(jnp.int32, (t, t), 1)
    causal = cc <= rr
    lane = lax.broadcasted_iota(jnp.int32, (t, LANES), 1)
    heads_per_group = gw // B_HEADDIM
    pairs_per_group = gw // LANES
    yield

    y_groups = []
    for g in range(B_GROUPS):
        bg = xbc[:, bw + g * B_DSTATE:bw + (g + 1) * B_DSTATE].astype(BF16)
        cg = xbc[:, bw + n_bc + g * B_DSTATE:bw + n_bc + (g + 1) * B_DSTATE].astype(BF16)
        gs = slice(g * gw, (g + 1) * gw)
        st_g = st[:, gs]
        scores = lax.dot_general(cg, bg, (((1,), (1,)), ((), ())), preferred_element_type=F32)
        y_off = jnp.dot(cg, st_g.astype(BF16), preferred_element_type=F32) * jnp.exp(acum_e[:, gs])
        st[:, gs] = jnp.exp(atot_e[:, gs]) * st_g + lax.dot_general(
            bg, xw[:, gs], (((0,), (0,)), ((), ())), preferred_element_type=F32)
        yield
        y_pairs = []
        for pr in range(pairs_per_group):
            cols = slice(g * gw + pr * LANES, g * gw + (pr + 1) * LANES)
            xp = xdt_b[:, cols]
            ms, xs_blocks = [], []
            for half in range(LANES // B_HEADDIM):
                h = g * heads_per_group + pr * (LANES // B_HEADDIM) + half
                diff = a_cum[:, h:h + 1] - a_cum_t[h:h + 1, :]
                lmat = jnp.where(causal, jnp.exp(jnp.where(causal, diff, 0.0)), 0.0)
                ms.append((scores * lmat).astype(BF16))
                in_half = (lane // B_HEADDIM) == half
                xs_blocks.append(jnp.where(in_half, xp, jnp.zeros_like(xp)))
            y_pairs.append(jnp.dot(jnp.concatenate(ms, axis=1), jnp.concatenate(xs_blocks, axis=0),
                                   preferred_element_type=F32))
            if pr % 2 == 1:
                yield
        y_groups.append(jnp.concatenate(y_pairs, axis=1) + y_off)
    y = jnp.concatenate(y_groups, axis=1) + dsk_ref[...] * x
    y = y * _silu(z_ref[...])
    outs = []
    for g in range(B_GROUPS):
        yg = y[:, g * gw:(g + 1) * gw]
        outs.append(yg * lax.rsqrt(jnp.mean(yg * yg, axis=-1, keepdims=True) + NORM_EPS))
    yb_ref[...] = (jnp.concatenate(outs, axis=1) * gn_ref[...]).astype(yb_ref.dtype)

    xpad[...] = cur[t - pad:t, :]
    yield


def _ssd_reset(s0_ref, c0_ref, xpad, st, first):
    @pl.when(first)
    def _():
        if s0_ref is not None:
            xpad[...] = c0_ref[0]
            st[...] = s0_ref[0].reshape(st.shape[1], st.shape[0]).T
        else:
            xpad[...] = jnp.zeros(xpad.shape, F32)
            st[...] = jnp.zeros(st.shape, F32)


def _ssd_flush(conv_out_ref, s_out_ref, xpad, st, last):
    @pl.when(last)
    def _():
        conv_out_ref[0] = xpad[...]
        s_out_ref[0] = st[...].T.reshape(s_out_ref.shape[1:])


def _ssd_operands(p, cols, prm, t, n_rows, n_seq, row_block, seq_block):
    bw = prm["bw"]
    heads = bw // B_HEADDIM
    n_bc = B_GROUPS * B_DSTATE
    cdim = bw + 2 * n_bc
    tri = np.arange(t)[None, :] <= np.arange(t)[:, None]
    sel = jnp.asarray(np.concatenate([tri, np.ones((t, t), bool)], 0).astype(np.float32), dtype=BF16)
    expand = np.zeros((LANES, bw), np.float32)
    expand[np.arange(bw) // B_HEADDIM, np.arange(bw)] = 1.0
    expand = jnp.asarray(expand, dtype=BF16)

    def blk(width, off):
        return pl.BlockSpec((t, width), lambda *g: (row_block(*g), off // width))

    def full(shape):
        return pl.BlockSpec(shape, lambda *g: (0,) * len(shape))

    in_specs = [blk(bw, cols["z"]), blk(bw, cols["xs"]), blk(2 * n_bc, cols["bc"]), blk(LANES, cols["dt"]),
                full((CONV_W, cdim)), full((1, cdim)), full((1, LANES)), full((1, LANES)),
                full((1, bw)), full((1, bw)), full(expand.shape), full(sel.shape)]
    args = [p, p, p, p, prm["conv_w"], prm["conv_b"], prm["dt_bias"], prm["a_log"], prm["d_skip"], prm["gn"],
            expand, sel]
    assert len(args) == SSD_N_INPUTS
    out_specs = [pl.BlockSpec((t, bw), lambda *g: (row_block(*g), 0)),
                 pl.BlockSpec((1, heads, B_HEADDIM, B_DSTATE), lambda *g: (seq_block(*g), 0, 0, 0)),
                 pl.BlockSpec((1, SUBLANES, cdim), lambda *g: (seq_block(*g), 0, 0))]
    out_shape = [jax.ShapeDtypeStruct((n_rows, bw), BF16),
                 jax.ShapeDtypeStruct((n_seq, heads, B_HEADDIM, B_DSTATE), F32),
                 jax.ShapeDtypeStruct((n_seq, SUBLANES, cdim), F32)]
    scratch = [pltpu.VMEM((SUBLANES, cdim), F32), pltpu.VMEM((B_DSTATE, bw), F32)]
    return in_specs, args, out_specs, out_shape, scratch


def _ssd(p, cols, prm, s0, c0, bsz, seq_len):
    bw = prm["bw"]
    heads = bw // B_HEADDIM
    cdim = bw + 2 * B_GROUPS * B_DSTATE
    t = min(seq_len, SSD_ROWS)
    nt = seq_len // t
    in_specs, args, out_specs, out_shape, scratch = _ssd_operands(
        p, cols, prm, t, bsz * seq_len, bsz, lambda b, ti: b * nt + ti, lambda b, ti: b)
    if s0 is not None:
        layer = prm["layer"]
        in_specs += [pl.BlockSpec((None, 1, heads, B_HEADDIM, B_DSTATE), lambda b, ti: (layer, b, 0, 0, 0)),
                     pl.BlockSpec((None, 1, SUBLANES, cdim), lambda b, ti: (layer, b, 0, 0))]
        args += [s0, c0]
    return pl.pallas_call(
        functools.partial(_ssd_kernel, t=t, bw=bw, has_init=s0 is not None),
        grid=(bsz, nt),
        in_specs=in_specs,
        out_specs=out_specs,
        out_shape=out_shape,
        scratch_shapes=scratch,
        compiler_params=_cparams(("arbitrary", "arbitrary")),
        name="ssd",
    )(*args)


def _down_sub_dots(a_ref, w_ref, x_ref, gate_ref, o_ref, acc_scr, nseg, pieces):
    assert a_ref.shape[1] % pieces == 0
    kc = a_ref.shape[1] // pieces

    def sub_dot(ki):
        rows = slice(ki * kc, (ki + 1) * kc)
        part = jnp.dot(a_ref[:, rows], w_ref[rows, :], preferred_element_type=F32)
        if ki == 0:
            acc_scr[...] = part
        elif ki < pieces - 1:
            acc_scr[...] += part
        else:
            x = x_ref[...]
            acc = acc_scr[...] + part
            o_ref[...] = (_per_seq(x, nseg) + gate_ref[...] * _per_seq(acc, nseg)).reshape(x.shape)

    return [functools.partial(sub_dot, ki) for ki in range(pieces)]


def _emit_interleaved(fillers, phases):
    per_filler = -(-len(phases) // max(len(fillers), 1))
    pos = 0
    for fill in fillers:
        fill()
        for ph in phases[pos:pos + per_filler]:
            ph()
        pos += per_filler
    for ph in phases[pos:]:
        ph()


def _mlp_down_ssd_kernel(a_ref, w_ref, x_ref, gate_ref, *refs, nseg, t, bw, steps_per_seq):
    ssd_in = refs[:SSD_N_INPUTS]
    o_ref, yb_ref, s_out_ref, conv_out_ref, acc_scr, xpad, st = refs[SSD_N_INPUTS:]
    step = pl.program_id(0) * pl.num_programs(1) + pl.program_id(1)
    pos = step % steps_per_seq
    _ssd_reset(None, None, xpad, st, pos == 0)
    gen = _ssd_phases(*ssd_in, yb_ref, xpad, st, t=t, bw=bw)
    phases = [functools.partial(next, gen, None) for _ in range(SSD_PHASES)]
    _emit_interleaved(_down_sub_dots(a_ref, w_ref, x_ref, gate_ref, o_ref, acc_scr, nseg, DOWN_PIECES_SSD), phases)
    _ssd_flush(conv_out_ref, s_out_ref, xpad, st, pos == steps_per_seq - 1)


def _mlp_down_ssd(a, w_stack, layer, x, mod6, k_gate, seq_len, p_other, cols, prm):
    n, kdim = a.shape
    d = w_stack.shape[2]
    tn = DOWN_TILE
    tm, nseg, tps = _row_tiling(n, seq_len, ROW_TILE)
    ni, nj = n // tm, d // tn
    n_other = p_other.shape[0]
    t = n_other // (ni * nj)
    assert t * ni * nj == n_other and seq_len % t == 0 and t % SUBLANES == 0
    steps_per_seq = seq_len // t
    s_in_specs, s_args, s_out_specs, s_out_shape, s_scratch = _ssd_operands(
        p_other, cols, prm, t, n_other, n_other // seq_len,
        lambda i, j: i * nj + j, lambda i, j: (i * nj + j) // steps_per_seq)
    return pl.pallas_call(
        functools.partial(_mlp_down_ssd_kernel, nseg=nseg, t=t, bw=prm["bw"], steps_per_seq=steps_per_seq),
        grid=(ni, nj),
        in_specs=[
            pl.BlockSpec((tm, kdim), lambda i, j: (i, 0)),
            pl.BlockSpec((None, kdim, tn), lambda i, j: (layer, 0, j)),
            pl.BlockSpec((tm, tn), lambda i, j: (i, j)),
            pl.BlockSpec((None, nseg, 1, tn), lambda i, j: (k_gate, i // tps, 0, j)),
        ] + s_in_specs,
        out_specs=[pl.BlockSpec((tm, tn), lambda i, j: (i, j))] + s_out_specs,
        out_shape=[jax.ShapeDtypeStruct((n, d), F32)] + s_out_shape,
        scratch_shapes=[pltpu.VMEM((tm, tn), F32)] + s_scratch,
        compiler_params=_cparams(("arbitrary", "arbitrary")),
        name="mlp_down_ssd",
    )(a, w_stack, x, mod6, *s_args)


def _mlp_up_ssd_kernel(x_ref, g_ref, sc_ref, sh_ref, w_ref, *refs, nseg, t, bw, steps_per_seq):
    ssd_in = refs[:SSD_N_INPUTS]
    o_ref, yb_ref, s_out_ref, conv_out_ref, h_scr, xpad, st = refs[SSD_N_INPUTS:]
    j = pl.program_id(1)
    pos = (pl.program_id(0) * pl.num_programs(1) + j) % steps_per_seq

    @pl.when(j == 0)
    def _():
        _norm_mod_rows(x_ref, g_ref, sc_ref, sh_ref, h_scr, nseg)

    _ssd_reset(None, None, xpad, st, pos == 0)

    def sub_dot(c0):
        cols = slice(c0, c0 + EPILOGUE_COLS)
        y = jnp.maximum(jnp.dot(h_scr[...], w_ref[:, cols], preferred_element_type=F32), 0.0)
        o_ref[:, cols] = (y * y).astype(o_ref.dtype)

    gen = _ssd_phases(*ssd_in, yb_ref, xpad, st, t=t, bw=bw)
    _emit_interleaved([functools.partial(sub_dot, c0) for c0 in range(0, w_ref.shape[1], EPILOGUE_COLS)],
                      [functools.partial(next, gen, None) for _ in range(SSD_PHASES)])
    _ssd_flush(conv_out_ref, s_out_ref, xpad, st, pos == steps_per_seq - 1)


def _mlp_up_ssd(x, g, mod6, w_stack, layer, seq_len, p_other, cols, prm):
    n, d = x.shape
    m = w_stack.shape[2]
    tn = COL_TILE
    tm, nseg, tps = _row_tiling(n, seq_len, ROW_TILE)
    ni, nj = n // tm, m // tn
    n_other = p_other.shape[0]
    t = n_other // (ni * nj)
    assert t * ni * nj == n_other and seq_len % t == 0 and t % SUBLANES == 0
    steps_per_seq = seq_len // t
    s_in_specs, s_args, s_out_specs, s_out_shape, s_scratch = _ssd_operands(
        p_other, cols, prm, t, n_other, n_other // seq_len,
        lambda i, j: i * nj + j, lambda i, j: (i * nj + j) // steps_per_seq)

    def mod_spec(k):
        return pl.BlockSpec((None, nseg, 1, d), lambda i, j: (k, i // tps, 0, 0))

    return pl.pallas_call(
        functools.partial(_mlp_up_ssd_kernel, nseg=nseg, t=t, bw=prm["bw"], steps_per_seq=steps_per_seq),
        grid=(ni, nj),
        in_specs=[
            pl.BlockSpec((tm, d), lambda i, j: (i, 0)),
            pl.BlockSpec((1, d), lambda i, j: (0, 0)),
            mod_spec(4),
            mod_spec(3),
            pl.BlockSpec((None, d, tn), lambda i, j: (layer, 0, j)),
        ] + s_in_specs,
        out_specs=[pl.BlockSpec((tm, tn), lambda i, j: (i, j))] + s_out_specs,
        out_shape=[jax.ShapeDtypeStruct((n, m), BF16)] + s_out_shape,
        scratch_shapes=[pltpu.VMEM((tm, d), BF16)] + s_scratch,
        compiler_params=_cparams(("arbitrary", "arbitrary")),
        name="mlp_up_ssd",
    )(x, g.reshape(1, d), mod6, mod6, w_stack, *s_args)


def _cmlp_kernel(u_ref, v_ref, lng_ref, lnb_ref, ws_ref, bst_ref, *out_refs, t, n_chunks, keep_v):
    yc_ref = out_refs[0]
    cw = u_ref.shape[1] // C_GROUPS
    rr = lax.broadcasted_iota(jnp.int32, (t, t), 0)
    cc = lax.broadcasted_iota(jnp.int32, (t, t), 1)
    wts = [jnp.where(cc <= rr, ws_ref[g, 0:t, 0:t], 0.0).astype(BF16) for g in range(C_GROUPS)]
    for ci in range(n_chunks):
        rows = slice(ci * t, (ci + 1) * t)
        u = _gelu(u_ref[rows, :])
        gv = _gelu(v_ref[rows, :])
        mu = jnp.mean(gv, axis=-1, keepdims=True)
        dv = gv - mu
        var = jnp.mean(dv * dv, axis=-1, keepdims=True)
        v = dv * lax.rsqrt(var + NORM_EPS) * lng_ref[...] + lnb_ref[...]
        if keep_v:
            out_refs[1][rows, :] = v
        vb = v.astype(BF16)
        for g in range(C_GROUPS):
            lanes = slice(g * cw, (g + 1) * cw)
            mixed = jnp.dot(wts[g], vb[:, lanes], preferred_element_type=F32) + bst_ref[0:t, g:g + 1]
            yc_ref[rows, lanes] = (u[:, lanes] * mixed).astype(yc_ref.dtype)


def _cmlp(p, col_u, col_v, prm, bsz, seq_len, keep_v):
    cw = prm["cw"]
    t = min(seq_len, CMLP_CHUNK)
    t_blk = min(seq_len, 4 * CMLP_CHUNK)
    n = bsz * seq_len

    def full(shape):
        return pl.BlockSpec(shape, lambda i: (0,) * len(shape))

    out_specs = [pl.BlockSpec((t_blk, cw), lambda i: (i, 0))]
    out_shape = [jax.ShapeDtypeStruct((n, cw), BF16)]
    if keep_v:
        out_specs.append(pl.BlockSpec((t_blk, cw), lambda i: (i, 0)))
        out_shape.append(jax.ShapeDtypeStruct((n, cw), F32))
    return pl.pallas_call(
        functools.partial(_cmlp_kernel, t=t, n_chunks=t_blk // t, keep_v=keep_v),
        grid=(n // t_blk,),
        in_specs=[pl.BlockSpec((t_blk, cw), lambda i: (i, col_u // cw)),
                  pl.BlockSpec((t_blk, cw), lambda i: (i, col_v // cw)),
                  full((1, cw)), full((1, cw)),
                  full((C_GROUPS, CMLP_CHUNK, CMLP_CHUNK)), full((CMLP_CHUNK, C_GROUPS))],
        out_specs=out_specs,
        out_shape=out_shape,
        compiler_params=_cparams(("arbitrary",)),
        name="cmlp",
    )(p, p, prm["ln_g"], prm["ln_b"], prm["ws"], prm["bs_t"])


def _run_trunk(x3, mod, st_hgrn, st_ssm, st_conv, keep_v, w):
    bsz, seq_len, d = x3.shape
    depth = mod.shape[0]
    aw = bw = cw = d // 2
    heads_a = aw // A_DK
    x = x3.reshape(bsz * seq_len, d)
    col = w["cols"]
    hgrn_out, ssm_out, conv_out, v_out = [], [], [], []
    for l in range(depth):
        mod6 = mod[l].reshape(bsz, 6, 1, d).transpose(1, 0, 2, 3)
        p, gates = _in_proj(x, w["norm1_g"][l], mod6, w["w_in"], l, seq_len, col["gate"])

        y_a, s_h = _hgrn(p, w["lbc"], l, w["hgrn_onorm_g"][l], st_hgrn, bsz, seq_len, heads_a)
        ssd_prm = dict(bw=bw, layer=l, conv_w=w["ssm_conv_w"][l], conv_b=w["ssm_conv_b"][l][None],
                       dt_bias=w["dt_bias_pad"][l][None], a_log=w["a_log_pad"][l][None],
                       d_skip=w["d_skip"][l][None], gn=w["ssm_onorm_g"][l][None])
        y_b, s_s, conv_tail = _ssd(p, col, ssd_prm, st_ssm, st_conv, bsz, seq_len)
        cm_prm = dict(cw=cw, ln_g=w["cmlp_ln_g"][l][None], ln_b=w["cmlp_ln_b"][l][None],
                      ws=w["cmlp_ws"][l], bs_t=w["cmlp_bs"][l].T)
        c_res = _cmlp(p, col["u"], col["v"], cm_prm, bsz, seq_len, keep_v)
        merged = _merge(y_a, y_b, c_res[0], w["w_branch"], l, gates, d)
        x = _proj_residual(merged, w["w_out"], l, x, mod6, 2, seq_len, "out_proj")
        hid = _mlp_up(x, w["norm2_g"][l], mod6, w["w_up"], l, seq_len)
        x = _proj_residual(hid, w["w_down"], l, x, mod6, 5, seq_len, "mlp_down")

        hgrn_out.append(s_h)
        ssm_out.append(s_s)
        conv_out.append(conv_tail[:, SUBLANES - (CONV_W - 1):, :])
        if keep_v:
            v_out.append(c_res[1].reshape(bsz, seq_len, cw))
    y = _final_norm(x, w["final_g"]).reshape(bsz, seq_len, d)
    return (y, jnp.stack(hgrn_out), jnp.stack(ssm_out), jnp.stack(conv_out),
            jnp.stack(v_out) if keep_v else None)


def _mlp_down_hgrn_kernel(a_ref, w_ref, x_ref, gate_ref, q_ref, f_ref, i_ref, gg_ref, lbc_ref, gn_ref, sel_ref,
                          o_ref, ya_ref, s_ref, acc_scr, bpad, kpad, *, nseg, c, n_chunks, heads, steps_per_seq):
    step = pl.program_id(0) * pl.num_programs(1) + pl.program_id(1)

    @pl.when(step % steps_per_seq == 0)
    def _():
        s_ref[...] = jnp.zeros(s_ref.shape, F32)

    fillers = _down_sub_dots(a_ref, w_ref, x_ref, gate_ref, o_ref, acc_scr, nseg, DOWN_PIECES_HGRN)
    _hgrn_chunks(q_ref, f_ref, i_ref, gg_ref, lbc_ref, gn_ref, sel_ref, ya_ref, s_ref, bpad, kpad,
                 c=c, n_chunks=n_chunks, heads=heads, unroll=True, fillers=fillers)


def _mlp_down_hgrn(a, w_stack, layer, x, mod6, k_gate, seq_len, p_other, lbc_stack, lbc_layer, gn, heads):
    n, kdim = a.shape
    d = w_stack.shape[2]
    tn = DOWN_TILE
    tm, nseg, tps = _row_tiling(n, seq_len, ROW_TILE)
    ni, nj = n // tm, d // tn
    n_other = p_other.shape[0]
    t_step = n_other // (ni * nj)
    assert t_step * ni * nj == n_other and seq_len % t_step == 0 and t_step % SUBLANES == 0
    steps_per_seq = seq_len // t_step
    aw = heads * A_DK
    c, h_in_specs, h_args, ya_spec, h_scratch = _hgrn_specs(
        p_other, lbc_stack, lbc_layer, gn, heads, t_step, lambda i, j: i * nj + j)
    state_spec = pl.BlockSpec((1, heads, A_DK, A_DK), lambda i, j: ((i * nj + j) // steps_per_seq, 0, 0, 0))
    return pl.pallas_call(
        functools.partial(_mlp_down_hgrn_kernel, nseg=nseg, c=c, n_chunks=t_step // c, heads=heads,
                          steps_per_seq=steps_per_seq),
        grid=(ni, nj),
        in_specs=[
            pl.BlockSpec((tm, kdim), lambda i, j: (i, 0)),
            pl.BlockSpec((None, kdim, tn), lambda i, j: (layer, 0, j)),
            pl.BlockSpec((tm, tn), lambda i, j: (i, j)),
            pl.BlockSpec((None, nseg, 1, tn), lambda i, j: (k_gate, i // tps, 0, j)),
        ] + h_in_specs,
        out_specs=[pl.BlockSpec((tm, tn), lambda i, j: (i, j)), ya_spec, state_spec],
        out_shape=[jax.ShapeDtypeStruct((n, d), F32),
                   jax.ShapeDtypeStruct((n_other, aw), BF16),
                   jax.ShapeDtypeStruct((n_other // seq_len, heads, A_DK, A_DK), F32)],
        scratch_shapes=[pltpu.VMEM((tm, tn), F32)] + h_scratch,
        compiler_params=_cparams(("arbitrary", "arbitrary")),
        name="mlp_down_hgrn2",
    )(a, w_stack, x, mod6, *h_args)


def _run_trunk_halves(x3, mod, w):
    bsz, seq_len, d = x3.shape
    depth = mod.shape[0]
    hb = bsz // 2
    aw = bw = cw = d // 2
    heads_a = aw // A_DK
    col = w["cols"]
    n_half = hb * seq_len
    x_full = x3.reshape(bsz * seq_len, d)
    xs = [x_full, x_full]
    row0 = [0, n_half]
    mods = [mod[:, :hb], mod[:, hb:]]
    hgrn_out, ssm_out, conv_out = [], [], []
    pending = None
    ahead = None

    def mod6_of(layer, h):
        return mods[h][layer].reshape(hb, 6, 1, d).transpose(1, 0, 2, 3)

    def ssd_prm_of(layer):
        return dict(bw=bw, layer=layer, conv_w=w["ssm_conv_w"][layer], conv_b=w["ssm_conv_b"][layer][None],
                    dt_bias=w["dt_bias_pad"][layer][None], a_log=w["a_log_pad"][layer][None],
                    d_skip=w["d_skip"][layer][None], gn=w["ssm_onorm_g"][layer][None])

    for l in range(depth):
        mod6 = [mod6_of(l, 0), mod6_of(l, 1)]
        ssd_prm = ssd_prm_of(l)
        cm_prm = dict(cw=cw, ln_g=w["cmlp_ln_g"][l][None], ln_b=w["cmlp_ln_b"][l][None],
                      ws=w["cmlp_ws"][l], bs_t=w["cmlp_bs"][l].T)
        gn_a = w["hgrn_onorm_g"][l]

        def mix_merge(h, p, gates, y_a, ssd_res=None):
            y_b, s_s, conv_tail = ssd_res or _ssd(p, col, ssd_prm, None, None, hb, seq_len)
            y_c = _cmlp(p, col["u"], col["v"], cm_prm, hb, seq_len, False)[0]
            merged = _merge(y_a, y_b, y_c, w["w_branch"], l, gates, d)
            x_new = _proj_residual(merged, w["w_out"], l, xs[h], mod6[h], 2, seq_len, "out_proj", row0[h])
            row0[h] = 0
            return x_new, s_s, conv_tail

        if ahead is None:
            p0, g0 = _in_proj(xs[0], w["norm1_g"][l], mod6[0], w["w_in"], l, seq_len, col["gate"],
                              (row0[0], n_half))
            ssd0 = None
        else:
            p0, g0, ssd0 = ahead
        if pending is None:
            ya0, sh0 = _hgrn(p0, w["lbc"], l, gn_a, None, hb, seq_len, heads_a)
        else:
            hid1, x1_mid, mod6_prev = pending
            xs[1], ya0, sh0 = _mlp_down_hgrn(hid1, w["w_down"], l - 1, x1_mid, mod6_prev, 5, seq_len,
                                              p0, w["lbc"], l, gn_a, heads_a)
        p1, g1 = _in_proj(xs[1], w["norm1_g"][l], mod6[1], w["w_in"], l, seq_len, col["gate"], (row0[1], n_half))
        x0, ss0, ct0 = mix_merge(0, p0, g0, ya0, ssd0)
        hid0, ya1, sh1 = _mlp_up_hgrn(x0, w["norm2_g"][l], mod6[0], w["w_up"], l, seq_len,
                                      p1, w["lbc"], gn_a, heads_a)
        xs[0], *ssd1 = _mlp_down_ssd(hid0, w["w_down"], l, x0, mod6[0], 5, seq_len, p1, col, ssd_prm)
        x1_mid, ss1, ct1 = mix_merge(1, p1, g1, ya1, tuple(ssd1))
        if l + 1 < depth:
            p0n, g0n = _in_proj(xs[0], w["norm1_g"][l + 1], mod6_of(l + 1, 0), w["w_in"], l + 1, seq_len,
                                col["gate"], (0, n_half))
            hid1, *ssd0n = _mlp_up_ssd(x1_mid, w["norm2_g"][l], mod6[1], w["w_up"], l, seq_len,
                                       p0n, col, ssd_prm_of(l + 1))
            ahead = (p0n, g0n, tuple(ssd0n))
        else:
            hid1 = _mlp_up(x1_mid, w["norm2_g"][l], mod6[1], w["w_up"], l, seq_len)
        pending = (hid1, x1_mid, mod6[1])

        hgrn_out.append(jnp.concatenate([sh0, sh1], axis=0))
        ssm_out.append(jnp.concatenate([ss0, ss1], axis=0))
        conv_out.append(jnp.concatenate([ct0, ct1], axis=0)[:, SUBLANES - (CONV_W - 1):, :])
    hid1, x1_mid, mod6_prev = pending
    xs[1] = _proj_residual(hid1, w["w_down"], depth - 1, x1_mid, mod6_prev, 5, seq_len, "mlp_down")
    y = _final_norm_pair(xs[0], xs[1], w["final_g"]).reshape(bsz, seq_len, d)
    return (y, jnp.stack(hgrn_out), jnp.stack(ssm_out), jnp.stack(conv_out), None)


def kernel(x_prompt, x_sample, state_hgrn, state_ssm, state_conv, c_prompt, c_sample, norm1_g, norm2_g,
           w_mod, b_mod, w_in, hgrn_lb, hgrn_onorm_g, ssm_conv_w, ssm_conv_b, ssm_dt_bias, ssm_a_log, ssm_d,
           ssm_onorm_g, cmlp_ln_g, cmlp_ln_b, cmlp_ws, cmlp_bs, w_branch, w_out, w_up, w_down, final_g):
    d = x_prompt.shape[-1]
    depth = w_in.shape[0]
    aw = bw = cw = d // 2
    n_bc = B_GROUPS * B_DSTATE
    heads_b = bw // B_HEADDIM
    assert heads_b <= LANES and bw % LANES == 0 and DT_PAD % LANES == 0

    o_dt = 4 * aw + bw + bw + 2 * n_bc
    o_u = o_dt + heads_b
    w_in_t = jnp.swapaxes(w_in, 1, 2).astype(BF16)
    w_in_r = jnp.concatenate(
        [w_in_t[:, :o_u], jnp.zeros((depth, DT_PAD - heads_b, d), BF16), w_in_t[:, o_u:]], axis=1)
    cols = dict(z=4 * aw, xs=4 * aw + bw, bc=4 * aw + 2 * bw, dt=o_dt)
    cols["u"] = o_dt + DT_PAD
    cols["v"] = cols["u"] + cw
    cols["gate"] = cols["v"] + cw
    assert cols["gate"] % IN_TILE == 0 and w_in_r.shape[1] % IN_TILE == 0

    pad_h = lambda a: jnp.pad(a.astype(F32), ((0, 0), (0, LANES - heads_b)))
    w = dict(
        cols=cols, w_in=w_in_r,
        norm1_g=norm1_g, norm2_g=norm2_g, final_g=final_g,
        lbc=_lb_consts(hgrn_lb), hgrn_onorm_g=hgrn_onorm_g,
        ssm_conv_w=ssm_conv_w, ssm_conv_b=ssm_conv_b,
        dt_bias_pad=pad_h(ssm_dt_bias), a_log_pad=pad_h(ssm_a_log),
        d_skip=jnp.repeat(ssm_d.astype(F32), B_HEADDIM, axis=1), ssm_onorm_g=ssm_onorm_g,
        cmlp_ln_g=cmlp_ln_g, cmlp_ln_b=cmlp_ln_b, cmlp_ws=cmlp_ws, cmlp_bs=cmlp_bs,
        w_branch=w_branch.astype(BF16).reshape(depth, N_BRANCH, aw, d), w_out=w_out.astype(BF16),
        w_up=w_up.astype(BF16), w_down=w_down.astype(BF16),
    )

    nb = x_prompt.shape[0]
    mod = _modulation(jnp.concatenate([c_prompt, c_sample], axis=0), w_mod, b_mod)
    if nb % 2 == 0:
        y_p, hgrn_p, ssm_p, conv_p, _ = _run_trunk_halves(x_prompt, mod[:, :nb], w)
    else:
        y_p, hgrn_p, ssm_p, conv_p, _ = _run_trunk(x_prompt, mod[:, :nb], None, None, None, False, w)
    conv_pad = jnp.pad(state_conv, ((0, 0), (0, 0), (SUBLANES - (CONV_W - 1), 0), (0, 0)))
    y_s, hgrn_s, ssm_s, conv_s, v_s = _run_trunk(x_sample, mod[:, nb:], state_hgrn, state_ssm, conv_pad, True, w)
    return (y_p, y_s, hgrn_p, ssm_p, conv_p, hgrn_s, ssm_s, conv_s, v_s)
```

```python
import functools
import math

import numpy as np
import jax
import jax.numpy as jnp
from jax import lax
from jax.experimental import pallas as pl
from jax.experimental.pallas import tpu as pltpu

F32 = jnp.float32
BF16 = jnp.bfloat16

A_DK = 128
B_HEADDIM = 64
B_GROUPS = 2
B_DSTATE = 128
CONV_W = 4
C_GROUPS = 4
CMLP_CHUNK = 128
N_BRANCH = 3
SCAN_CHUNK = 64
NORM_EPS = 1e-6
LB_FLOOR = 1e-30
LOG2E = 1.4426950408889634

LANES = 128
SUBLANES = 8
VMEM_LIMIT = 56 * 1024 * 1024
DIAG = SUBLANES // 2
DT_PAD = 512
ROW_TILE = 1024
COL_TILE = 1024
IN_TILE = 1536
MERGE_TILE = 512
MERGE_ROW_TILE = 1024
UP_TILE = 2048
OUT_TILE = 512
OUT_ROW_TILE = 2048
DOWN_TILE = 256
EPILOGUE_COLS = 256
SSD_ROWS = 128
DOWN_K_SPLIT = 16
NORM_ROWS = 128


def _cparams(sem):
    return pltpu.CompilerParams(dimension_semantics=sem, vmem_limit_bytes=VMEM_LIMIT)


def _split3(x):
    hi = x.astype(BF16)
    r1 = x - hi.astype(F32)
    mid = r1.astype(BF16)
    lo = (r1 - mid.astype(F32)).astype(BF16)
    return hi, mid, lo


def _dot3(sel, x):
    hi, mid, lo = _split3(x)
    d = lambda p: jnp.dot(sel, p, preferred_element_type=F32)
    return d(hi) + d(mid) + d(lo)


def _dot3_rhs(x, sel):
    hi, mid, lo = _split3(x)
    d = lambda p: jnp.dot(p, sel, preferred_element_type=F32)
    return d(hi) + d(mid) + d(lo)


def _silu(x):
    return x / (1.0 + jnp.exp(-x))


def _softplus(x):
    return jnp.maximum(x, 0.0) + jnp.log1p(jnp.exp(-jnp.abs(x)))


def _gelu(x):
    return 0.5 * x * (1.0 + lax.erf(x * (1.0 / math.sqrt(2.0))))


def _lb_kernel(lb_ref, out_ref, *, depth):
    x = lb_ref[...]
    m = jnp.max(x, axis=0, keepdims=True)
    e = jnp.exp(x - m)
    p = e / jnp.sum(e, axis=0, keepdims=True)
    acc = jnp.zeros_like(p[0:1])
    zeros5 = jnp.zeros((SUBLANES - 3, x.shape[1]), F32)
    for l in range(depth):
        acc = acc + p[l:l + 1]
        lb = acc - p[0:1]
        out_ref[l] = jnp.concatenate(
            [jnp.log(jnp.maximum(lb, LB_FLOOR)), jnp.log1p(-lb), 1.0 - lb, zeros5], axis=0)


def _lb_consts(hgrn_lb):
    depth, aw = hgrn_lb.shape
    return pl.pallas_call(
        functools.partial(_lb_kernel, depth=depth),
        out_shape=jax.ShapeDtypeStruct((depth, SUBLANES, aw), F32),
        name="hgrn_lb",
    )(hgrn_lb.astype(F32))


def _mod_kernel(c_ref, w_ref, b_ref, o_ref):
    cs = _silu(c_ref[...]).astype(BF16)
    o_ref[...] = jnp.dot(cs, w_ref[...].astype(BF16), preferred_element_type=F32) + b_ref[...]


def _modulation(c_all, w_mod, b_mod):
    depth, d, n6 = w_mod.shape
    s = c_all.shape[0]
    tn = 1024
    return pl.pallas_call(
        _mod_kernel,
        grid=(depth, n6 // tn),
        in_specs=[
            pl.BlockSpec((s, d), lambda l, j: (0, 0)),
            pl.BlockSpec((None, d, tn), lambda l, j: (l, 0, j)),
            pl.BlockSpec((None, 1, tn), lambda l, j: (l, 0, j)),
        ],
        out_specs=pl.BlockSpec((None, s, tn), lambda l, j: (l, 0, j)),
        out_shape=jax.ShapeDtypeStruct((depth, s, n6), F32),
        compiler_params=_cparams(("arbitrary", "arbitrary")),
        name="adaln_mod",
    )(c_all, w_mod, b_mod.reshape(depth, 1, n6))


def _row_tiling(n_tok, seq_len, tm_max):
    tm = min(tm_max, n_tok)
    if seq_len >= tm:
        assert seq_len % tm == 0
        return tm, 1, seq_len // tm
    assert tm % seq_len == 0
    return tm, tm // seq_len, 1


def _per_seq(x, nseg):
    tm, d = x.shape
    return x.reshape(nseg, tm // nseg, d)


def _final_norm_kernel(x_ref, g_ref, o_ref):
    x = x_ref[...]
    o_ref[...] = x * lax.rsqrt(jnp.mean(x * x, axis=-1, keepdims=True) + NORM_EPS) * g_ref[...]


def _final_norm(x, g):
    n, d = x.shape
    tm = min(512, n)
    return pl.pallas_call(
        _final_norm_kernel,
        grid=(n // tm,),
        in_specs=[pl.BlockSpec((tm, d), lambda i: (i, 0)), pl.BlockSpec((1, d), lambda i: (0, 0))],
        out_specs=pl.BlockSpec((tm, d), lambda i: (i, 0)),
        out_shape=jax.ShapeDtypeStruct((n, d), F32),
        compiler_params=_cparams(("arbitrary",)),
        name="final_norm",
    )(x, g.reshape(1, d))


def _final_norm_pair_kernel(xa_ref, xb_ref, g_ref, o_ref, *, nt):
    def norm(x):
        return x * lax.rsqrt(jnp.mean(x * x, axis=-1, keepdims=True) + NORM_EPS) * g_ref[...]

    @pl.when(pl.program_id(0) < nt)
    def _():
        o_ref[...] = norm(xa_ref[...])

    @pl.when(pl.program_id(0) >= nt)
    def _():
        o_ref[...] = norm(xb_ref[...])


def _final_norm_pair(xa, xb, g):
    n, d = xa.shape
    tm = min(512, n)
    nt = n // tm
    return pl.pallas_call(
        functools.partial(_final_norm_pair_kernel, nt=nt),
        grid=(2 * nt,),
        in_specs=[pl.BlockSpec((tm, d), lambda i: (jnp.minimum(i, nt - 1), 0)),
                  pl.BlockSpec((tm, d), lambda i: (jnp.maximum(i - nt, 0), 0)),
                  pl.BlockSpec((1, d), lambda i: (0, 0))],
        out_specs=pl.BlockSpec((tm, d), lambda i: (i, 0)),
        out_shape=jax.ShapeDtypeStruct((2 * n, d), F32),
        compiler_params=_cparams(("arbitrary",)),
        name="final_norm",
    )(xa, xb, g.reshape(1, d))


def _norm_mod_rows(x_ref, g_ref, sc_ref, sh_ref, h_scr, nseg):
    tm = x_ref.shape[0]
    seg = tm // nseg
    rb = min(seg, NORM_ROWS)
    g = g_ref[...]

    def body(r, carry):
        rows = pl.ds(pl.multiple_of(r * rb, rb), rb)
        s = r // (seg // rb)
        x = x_ref[rows, :]
        gain = g * (1.0 + sc_ref[s])
        inv = lax.rsqrt(jnp.mean(x * x, axis=-1, keepdims=True) + NORM_EPS)
        h_scr[rows, :] = (x * inv * gain + sh_ref[s]).astype(BF16)
        return carry

    lax.fori_loop(0, tm // rb, body, 0, unroll=2 if (tm // rb) % 2 == 0 else 1)


def _mlp_up_kernel(x_ref, g_ref, sc_ref, sh_ref, w_ref, o_ref, h_scr, *, nseg):
    @pl.when(pl.program_id(1) == 0)
    def _():
        _norm_mod_rows(x_ref, g_ref, sc_ref, sh_ref, h_scr, nseg)

    y = jnp.maximum(jnp.dot(h_scr[...], w_ref[...], preferred_element_type=F32), 0.0)
    o_ref[...] = (y * y).astype(o_ref.dtype)


def _in_proj_kernel(x_ref, g_ref, sc_ref, sh_ref, w_ref, p_ref, gate_ref, h_scr, *, nseg, n_main):
    j = pl.program_id(1)

    @pl.when(j == 0)
    def _():
        _norm_mod_rows(x_ref, g_ref, sc_ref, sh_ref, h_scr, nseg)

    def project():
        return lax.dot_general(h_scr[...], w_ref[...], (((1,), (1,)), ((), ())), preferred_element_type=F32)

    @pl.when(j < n_main)
    def _():
        p_ref[...] = project()

    @pl.when(j >= n_main)
    def _():
        gate_ref[...] = project().astype(gate_ref.dtype)


def _norm_proj_call(kern, x, g, mod6, k_scale, k_shift, w_stack, layer, seq_len, tn, out_specs, out_shape, name):
    n, d = x.shape
    m = w_stack.shape[2]
    tm, nseg, tps = _row_tiling(n, seq_len, ROW_TILE)

    def mod_spec(k):
        return pl.BlockSpec((None, nseg, 1, d), lambda i, j: (k, i // tps, 0, 0))

    return pl.pallas_call(
        functools.partial(kern, nseg=nseg),
        grid=(n // tm, m // tn),
        in_specs=[
            pl.BlockSpec((tm, d), lambda i, j: (i, 0)),
            pl.BlockSpec((1, d), lambda i, j: (0, 0)),
            mod_spec(k_scale),
            mod_spec(k_shift),
            pl.BlockSpec((None, d, tn), lambda i, j: (layer, 0, j)),
        ],
        out_specs=out_specs(tm),
        out_shape=out_shape,
        scratch_shapes=[pltpu.VMEM((tm, d), BF16)],
        compiler_params=_cparams(("arbitrary", "arbitrary")),
        name=name,
    )(x, g.reshape(1, d), mod6, mod6, w_stack)


def _mlp_up(x, g, mod6, w_stack, layer, seq_len):
    n = x.shape[0]
    m = w_stack.shape[2]
    tn = UP_TILE
    return _norm_proj_call(
        _mlp_up_kernel, x, g, mod6, 4, 3, w_stack, layer, seq_len, tn,
        lambda tm: pl.BlockSpec((tm, tn), lambda i, j: (i, j)),
        jax.ShapeDtypeStruct((n, m), BF16), "mlp_up")


def _in_proj(x, g, mod6, w_stack, layer, seq_len, main_cols, rows=None):
    d = x.shape[1]
    row0, n = rows if rows is not None else (0, x.shape[0])
    m = w_stack.shape[1]
    tn = IN_TILE
    n_main = main_cols // tn
    tm, nseg, tps = _row_tiling(n, seq_len, ROW_TILE)

    def mod_spec(k):
        return pl.BlockSpec((None, nseg, 1, d), lambda i, j: (k, i // tps, 0, 0))

    return pl.pallas_call(
        functools.partial(_in_proj_kernel, nseg=nseg, n_main=n_main),
        grid=(n // tm, m // tn),
        in_specs=[
            pl.BlockSpec((tm, d), lambda i, j: (i + row0 // tm, 0)),
            pl.BlockSpec((1, d), lambda i, j: (0, 0)),
            mod_spec(1),
            mod_spec(0),
            pl.BlockSpec((None, tn, d), lambda i, j: (layer, j, 0)),
        ],
        out_specs=[pl.BlockSpec((tm, tn), lambda i, j: (i, jnp.minimum(j, n_main - 1))),
                   pl.BlockSpec((tm, tn), lambda i, j: (i, jnp.maximum(j - n_main, 0)))],
        out_shape=[jax.ShapeDtypeStruct((n, main_cols), F32), jax.ShapeDtypeStruct((n, m - main_cols), BF16)],
        scratch_shapes=[pltpu.VMEM((tm, d), BF16)],
        compiler_params=_cparams(("arbitrary", "arbitrary")),
        name="in_proj",
    )(x, g.reshape(1, d), mod6, mod6, w_stack)


def _proj_res_kernel(a_ref, w_ref, x_ref, gate_ref, o_ref, *, nseg):
    acc = jnp.dot(a_ref[...], w_ref[...], preferred_element_type=F32)
    x = x_ref[...]
    o_ref[...] = (_per_seq(x, nseg) + gate_ref[...] * _per_seq(acc, nseg)).reshape(x.shape)


def _proj_residual(a, w_stack, layer, x, mod6, k_gate, seq_len, name, x_row0=0):
    n, kdim = a.shape
    d = w_stack.shape[2]
    tn, tm_max = (OUT_TILE, OUT_ROW_TILE) if kdim <= d else (DOWN_TILE, ROW_TILE)
    tm, nseg, tps = _row_tiling(n, seq_len, tm_max)
    return pl.pallas_call(
        functools.partial(_proj_res_kernel, nseg=nseg),
        grid=(n // tm, d // tn),
        in_specs=[
            pl.BlockSpec((tm, kdim), lambda i, j: (i, 0)),
            pl.BlockSpec((None, kdim, tn), lambda i, j: (layer, 0, j)),
            pl.BlockSpec((tm, tn), lambda i, j: (i + x_row0 // tm, j)),
            pl.BlockSpec((None, nseg, 1, tn), lambda i, j: (k_gate, i // tps, 0, j)),
        ],
        out_specs=pl.BlockSpec((tm, tn), lambda i, j: (i, j)),
        out_shape=jax.ShapeDtypeStruct((n, d), F32),
        compiler_params=_cparams(("arbitrary", "arbitrary")),
        name=name,
    )(a, w_stack, x, mod6)


def _merge_kernel(ya_ref, yb_ref, yc_ref, wa_ref, wb_ref, wc_ref, ga_ref, gb_ref, gc_ref, o_ref):
    acc = None
    for y_ref, w_ref, g_ref in ((ya_ref, wa_ref, ga_ref), (yb_ref, wb_ref, gb_ref), (yc_ref, wc_ref, gc_ref)):
        gate = 1.0 / (1.0 + jnp.exp(-g_ref[...].astype(F32)))
        term = gate * jnp.dot(y_ref[...], w_ref[...], preferred_element_type=F32)
        acc = term if acc is None else acc + term
    o_ref[...] = acc.astype(o_ref.dtype)


def _merge(ya, yb, yc, w_stack, layer, gates, d):
    n, bw = ya.shape
    tn = MERGE_TILE
    tm = min(MERGE_ROW_TILE, n)
    gsteps = d // tn
    y_spec = pl.BlockSpec((tm, bw), lambda i, j: (i, 0))

    def w_spec(k):
        return pl.BlockSpec((None, None, bw, tn), lambda i, j: (layer, k, 0, j))

    def g_spec(k):
        return pl.BlockSpec((tm, tn), lambda i, j: (i, k * gsteps + j))

    return pl.pallas_call(
        _merge_kernel,
        grid=(n // tm, d // tn),
        in_specs=[y_spec, y_spec, y_spec, w_spec(0), w_spec(1), w_spec(2), g_spec(0), g_spec(1), g_spec(2)],
        out_specs=pl.BlockSpec((tm, tn), lambda i, j: (i, j)),
        out_shape=jax.ShapeDtypeStruct((n, d), BF16),
        compiler_params=_cparams(("arbitrary", "arbitrary")),
        name="merge",
    )(ya, yb, yc, w_stack, w_stack, w_stack, gates, gates, gates)


def _hgrn_levels(c):
    lv, m = [], c // 2
    while m >= DIAG:
        lv.append(m)
        m //= 2
    return tuple(lv)


def _hgrn_select(c):
    tri = np.arange(c)[None, :] <= np.arange(c)[:, None]
    return jnp.asarray(tri.astype(np.float32), dtype=BF16)


def _hgrn_kernel(*refs, c, n_chunks, heads, has_init):
    if has_init:
        q_ref, f_ref, i_ref, g_ref, lbc_ref, gn_ref, sel_ref, s0_ref, ya_ref, s_ref, bpad, kpad = refs
    else:
        q_ref, f_ref, i_ref, g_ref, lbc_ref, gn_ref, sel_ref, ya_ref, s_ref, bpad, kpad = refs

    @pl.when(pl.program_id(1) == 0)
    def _():
        s_ref[...] = s0_ref[...] if has_init else jnp.zeros(s_ref.shape, F32)

    _hgrn_chunks(q_ref, f_ref, i_ref, g_ref, lbc_ref, gn_ref, sel_ref, ya_ref, s_ref, bpad, kpad,
                 c=c, n_chunks=n_chunks, heads=heads, unroll=False)


def _hgrn_chunks(q_ref, f_ref, i_ref, g_ref, lbc_ref, gn_ref, sel_ref, ya_ref, s_ref, bpad, kpad,
                 *, c, n_chunks, heads, unroll, fillers=()):
    levels = _hgrn_levels(c)
    nl = len(levels)
    width = heads * A_DK
    head_lanes = [slice(hh * A_DK, (hh + 1) * A_DK) for hh in range(heads)]

    bpad[0:DIAG, :] = jnp.zeros((DIAG, width), F32)
    kpad[0:DIAG, :] = jnp.zeros((DIAG, width), F32)

    sel = sel_ref[...]
    log_lb = lbc_ref[0:1, :]
    log1m_lb = lbc_ref[1:2, :]
    one_m_lb = lbc_ref[2:3, :]
    row = lax.broadcasted_iota(jnp.int32, (c, width), 0)
    rr = lax.broadcasted_iota(jnp.int32, (c, c), 0)
    cc = lax.broadcasted_iota(jnp.int32, (c, c), 1)
    upper = [(row & m) != 0 for m in levels]
    same = [(rr ^ cc) < 2 * m for m in levels]
    dmat = jnp.where(((rr ^ cc) < DIAG) & (cc <= rr), rr - cc, -1)
    nt_dims = (((1,), (1,)), ((), ()))
    tn_dims = (((0,), (0,)), ((), ()))

    def chunk(ci):
        rows = slice(ci * c, (ci + 1) * c) if unroll else pl.ds(pl.multiple_of(ci * c, c), c)
        z = f_ref[rows, :]
        aq = q_ref[rows, :]

        log_sig = jnp.minimum(z, 0.0) - jnp.log(1.0 + jnp.exp(-jnp.abs(z)))
        bb = log1m_lb + log_sig
        log_f = jnp.maximum(log_lb, bb) + jnp.log(1.0 + jnp.exp(-jnp.abs(log_lb - bb)))
        k = one_m_lb / (1.0 + jnp.exp(z))
        q = _silu(aq)

        b = _dot3(sel, log_f * LOG2E)
        bpad[DIAG:DIAG + c, :] = b
        kpad[DIAG:DIAG + c, :] = k
        b_tot = bpad[DIAG + c - 1:DIAG + c, :]
        vb = i_ref[rows, :].astype(BF16)
        q_in = (q * jnp.exp2(b)).astype(BF16)
        k_out = (k * jnp.exp2(b_tot - b)).astype(BF16)
        yield

        att = [jnp.zeros((c, c), F32) for _ in range(heads)]
        for li in range(nl):
            m = levels[li]
            ref_b = jnp.concatenate(
                [jnp.broadcast_to(bpad[DIAG + g0 + m - 1:DIAG + g0 + m, :], (2 * m, width))
                 for g0 in range(0, c, 2 * m)], axis=0)
            e = jnp.exp2(-jnp.abs(b - ref_b))
            qs = jnp.where(upper[li], q * e, 0.0).astype(BF16)
            ks = jnp.where(upper[li], 0.0, k * e).astype(BF16)
            for hh, lanes in enumerate(head_lanes):
                a_l = lax.dot_general(qs[:, lanes], ks[:, lanes], nt_dims, preferred_element_type=F32)
                att[hh] = att[hh] + jnp.where(same[li], a_l, 0.0)
        yield

        for dlt in range(DIAG):
            if dlt == 0:
                w = q * k
            else:
                ksh = kpad[DIAG - dlt:DIAG - dlt + c, :]
                bsh = bpad[DIAG - dlt:DIAG - dlt + c, :]
                w = q * ksh * jnp.exp2(b - bsh)
            for hh, lanes in enumerate(head_lanes):
                col = jnp.sum(w[:, lanes], axis=-1, keepdims=True)
                att[hh] = jnp.where(dmat == dlt, col, att[hh])
            if dlt in (DIAG // 2 - 1, DIAG - 1):
                yield

        decay_row = jnp.exp2(b_tot)
        outs = []
        for hh, lanes in enumerate(head_lanes):
            s_prev = s_ref[0, hh]
            o = jnp.dot(q_in[:, lanes], s_prev.astype(BF16), preferred_element_type=F32)
            o = o + jnp.dot(att[hh].astype(BF16), vb[:, lanes], preferred_element_type=F32)
            decay_col = jnp.broadcast_to(decay_row[:, lanes], (A_DK, A_DK)).T
            s_ref[0, hh] = decay_col * s_prev + lax.dot_general(
                k_out[:, lanes], vb[:, lanes], tn_dims, preferred_element_type=F32)
            outs.append(o * lax.rsqrt(jnp.mean(o * o, axis=-1, keepdims=True) + NORM_EPS))
        y = jnp.concatenate(outs, axis=1) * gn_ref[...] * _silu(g_ref[rows, :])
        ya_ref[rows, :] = y.astype(ya_ref.dtype)
        yield

    if not unroll:
        def body(ci, carry):
            for _ in chunk(ci):
                pass
            return carry

        lax.fori_loop(0, n_chunks, body, 0)
        return

    n_phases = 5
    _emit_interleaved(fillers, [functools.partial(next, gen, None)
                                for gen in map(chunk, range(n_chunks)) for _ in range(n_phases)])


def _hgrn_specs(p, lbc_stack, layer, gn, heads, t_blk, row_block):
    aw = heads * A_DK
    c = min(t_blk, SCAN_CHUNK)
    sel = _hgrn_select(c)

    def col_spec(seg):
        return pl.BlockSpec((t_blk, aw), lambda *g: (row_block(*g), seg))

    in_specs = [col_spec(0), col_spec(1), col_spec(2), col_spec(3),
                pl.BlockSpec((None, SUBLANES, aw), lambda *g: (layer, 0, 0)),
                pl.BlockSpec((1, aw), lambda *g: (0, 0)),
                pl.BlockSpec(sel.shape, lambda *g: (0, 0))]
    args = [p, p, p, p, lbc_stack, gn.reshape(1, aw), sel]
    scratch = [pltpu.VMEM((c + DIAG, aw), F32), pltpu.VMEM((c + DIAG, aw), F32)]
    return c, in_specs, args, pl.BlockSpec((t_blk, aw), lambda *g: (row_block(*g), 0)), scratch


def _mlp_up_hgrn_kernel(x_ref, g_ref, sc_ref, sh_ref, w_ref, q_ref, f_ref, i_ref, gg_ref, lbc_ref, gn_ref, sel_ref,
                        o_ref, ya_ref, s_ref, h_scr, bpad, kpad, *, nseg, c, n_chunks, heads, steps_per_seq):
    j = pl.program_id(1)
    step = pl.program_id(0) * pl.num_programs(1) + j

    @pl.when(j == 0)
    def _():
        _norm_mod_rows(x_ref, g_ref, sc_ref, sh_ref, h_scr, nseg)

    @pl.when(step % steps_per_seq == 0)
    def _():
        s_ref[...] = jnp.zeros(s_ref.shape, F32)

    def sub_dot(c0):
        cols = slice(c0, c0 + EPILOGUE_COLS)
        y = jnp.maximum(jnp.dot(h_scr[...], w_ref[:, cols], preferred_element_type=F32), 0.0)
        o_ref[:, cols] = (y * y).astype(o_ref.dtype)

    fillers = [functools.partial(sub_dot, c0) for c0 in range(0, w_ref.shape[1], EPILOGUE_COLS)]
    _hgrn_chunks(q_ref, f_ref, i_ref, gg_ref, lbc_ref, gn_ref, sel_ref, ya_ref, s_ref, bpad, kpad,
                 c=c, n_chunks=n_chunks, heads=heads, unroll=True, fillers=fillers)


def _mlp_up_hgrn(x, g, mod6, w_stack, layer, seq_len, p_other, lbc_stack, gn, heads):
    n, d = x.shape
    m = w_stack.shape[2]
    tn = COL_TILE
    tm, nseg, tps = _row_tiling(n, seq_len, ROW_TILE)
    ni, nj = n // tm, m // tn
    n_other = p_other.shape[0]
    t_step = n_other // (ni * nj)
    assert t_step * ni * nj == n_other and seq_len % t_step == 0 and t_step % SUBLANES == 0
    steps_per_seq = seq_len // t_step
    aw = heads * A_DK
    c, h_in_specs, h_args, ya_spec, h_scratch = _hgrn_specs(
        p_other, lbc_stack, layer, gn, heads, t_step, lambda i, j: i * nj + j)

    def mod_spec(k):
        return pl.BlockSpec((None, nseg, 1, d), lambda i, j: (k, i // tps, 0, 0))

    state_spec = pl.BlockSpec((1, heads, A_DK, A_DK), lambda i, j: ((i * nj + j) // steps_per_seq, 0, 0, 0))
    return pl.pallas_call(
        functools.partial(_mlp_up_hgrn_kernel, nseg=nseg, c=c, n_chunks=t_step // c, heads=heads,
                          steps_per_seq=steps_per_seq),
        grid=(ni, nj),
        in_specs=[
            pl.BlockSpec((tm, d), lambda i, j: (i, 0)),
            pl.BlockSpec((1, d), lambda i, j: (0, 0)),
            mod_spec(4),
            mod_spec(3),
            pl.BlockSpec((None, d, tn), lambda i, j: (layer, 0, j)),
        ] + h_in_specs,
        out_specs=[pl.BlockSpec((tm, tn), lambda i, j: (i, j)), ya_spec, state_spec],
        out_shape=[jax.ShapeDtypeStruct((n, m), BF16),
                   jax.ShapeDtypeStruct((n_other, aw), BF16),
                   jax.ShapeDtypeStruct((n_other // seq_len, heads, A_DK, A_DK), F32)],
        scratch_shapes=[pltpu.VMEM((tm, d), BF16)] + h_scratch,
        compiler_params=_cparams(("arbitrary", "arbitrary")),
        name="mlp_up_hgrn2",
    )(x, g.reshape(1, d), mod6, mod6, w_stack, *h_args)


def _hgrn(p, lbc_stack, layer, gn, s0, bsz, seq_len, heads):
    c = min(seq_len, SCAN_CHUNK)
    t_blk = min(seq_len, 256)
    nt = seq_len // t_blk
    aw = heads * A_DK

    def col_spec(seg):
        return pl.BlockSpec((t_blk, aw), lambda b, t: (b * nt + t, seg))

    sel = _hgrn_select(c)
    in_specs = [col_spec(0), col_spec(1), col_spec(2), col_spec(3),
                pl.BlockSpec((None, SUBLANES, aw), lambda b, t: (layer, 0, 0)),
                pl.BlockSpec((1, aw), lambda b, t: (0, 0)),
                pl.BlockSpec(sel.shape, lambda b, t: (0, 0))]
    args = [p, p, p, p, lbc_stack, gn.reshape(1, aw), sel]
    state_spec = pl.BlockSpec((1, heads, A_DK, A_DK), lambda b, t: (b, 0, 0, 0))
    if s0 is not None:
        in_specs.append(pl.BlockSpec((None, 1, heads, A_DK, A_DK), lambda b, t: (layer, b, 0, 0, 0)))
        args.append(s0)
    return pl.pallas_call(
        functools.partial(_hgrn_kernel, c=c, n_chunks=t_blk // c, heads=heads, has_init=s0 is not None),
        grid=(bsz, nt),
        in_specs=in_specs,
        out_specs=[pl.BlockSpec((t_blk, aw), lambda b, t: (b * nt + t, 0)), state_spec],
        out_shape=[jax.ShapeDtypeStruct((bsz * seq_len, aw), BF16),
                   jax.ShapeDtypeStruct((bsz, heads, A_DK, A_DK), F32)],
        scratch_shapes=[pltpu.VMEM((c + DIAG, aw), F32), pltpu.VMEM((c + DIAG, aw), F32)],
        compiler_params=_cparams(("arbitrary", "arbitrary")),
        name="hgrn2",
    )(*args)


def _ssd_kernel(*refs, t, bw, has_init):
    n_in = SSD_N_INPUTS
    s0_ref, c0_ref = (refs[n_in], refs[n_in + 1]) if has_init else (None, None)
    rest = refs[n_in + 2:] if has_init else refs[n_in:]
    yb_ref, s_out_ref, conv_out_ref, xpad, st = rest
    ti = pl.program_id(1)
    _ssd_reset(s0_ref, c0_ref, xpad, st, ti == 0)
    for _ in _ssd_phases(*refs[:n_in], yb_ref, xpad, st, t=t, bw=bw):
        pass
    _ssd_flush(conv_out_ref, s_out_ref, xpad, st, ti == pl.num_programs(1) - 1)


SSD_N_INPUTS = 12
SSD_PHASES = 9


def _ssd_phases(z_ref, xs_ref, bc_ref, dt_ref, cw_ref, cb_ref, dtb_ref, alog_ref, dsk_ref, gn_ref, exp_ref, sel_ref,
                yb_ref, xpad, st, *, t, bw):
    gw = bw // B_GROUPS
    n_bc = B_GROUPS * B_DSTATE
    pad = SUBLANES

    cur = jnp.concatenate([xs_ref[...], bc_ref[...]], axis=1)
    prev = xpad[...]
    row8 = lax.broadcasted_iota(jnp.int32, (pad, cur.shape[1]), 0)
    conv = cb_ref[...]
    for j in range(CONV_W):
        s = CONV_W - 1 - j
        if s == 0:
            tap = cur
        else:
            rolled = pltpu.roll(cur, s, 0)
            head = jnp.where(row8 < s, pltpu.roll(prev, s, 0), rolled[0:pad])
            tap = jnp.concatenate([head, rolled[pad:]], axis=0)
        conv = conv + tap * cw_ref[j:j + 1, :]
    xbc = _silu(conv)
    x = xbc[:, 0:bw]
    yield

    dt = _softplus(dt_ref[...] + dtb_ref[...])
    a = dt * (-jnp.exp(alog_ref[...]))
    cs = _dot3(sel_ref[...], a)
    a_cum = cs[0:t]
    a_tot = cs[t:2 * t]
    ex = _dot3_rhs(jnp.concatenate([dt, a_cum, a_tot - a_cum], axis=0), exp_ref[...])
    dt_e = ex[0:t]
    acum_e = ex[t:2 * t]
    dec_e = ex[2 * t:3 * t]
    atot_e = acum_e[t - 1:t, :]

    xdt = x * dt_e
    xw = (xdt * jnp.exp(dec_e)).astype(BF16)
    xdt_b = xdt.astype(BF16)
    a_cum_t = a_cum.T
    rr = lax.broadcasted_iota(jnp.int32, (t, t), 0)
    cc = lax.broadcasted_iota(jnp.int32, (t, t), 1)
    causal = cc <= rr
    lane = lax.broadcasted_iota(jnp.int32, (t, LANES), 1)
    heads_per_group = gw // B_HEADDIM
    pairs_per_group = gw // LANES
    yield

    y_groups = []
    for g in range(B_GROUPS):
        bg = xbc[:, bw + g * B_DSTATE:bw + (g + 1) * B_DSTATE].astype(BF16)
        cg = xbc[:, bw + n_bc + g * B_DSTATE:bw + n_bc + (g + 1) * B_DSTATE].astype(BF16)
        gs = slice(g * gw, (g + 1) * gw)
        st_g = st[:, gs]
        scores = lax.dot_general(cg, bg, (((1,), (1,)), ((), ())), preferred_element_type=F32)
        y_off = jnp.dot(cg, st_g.astype(BF16), preferred_element_type=F32) * jnp.exp(acum_e[:, gs])
        st[:, gs] = jnp.exp(atot_e[:, gs]) * st_g + lax.dot_general(
            bg, xw[:, gs], (((0,), (0,)), ((), ())), preferred_element_type=F32)
        yield
        y_pairs = []
        for pr in range(pairs_per_group):
            cols = slice(g * gw + pr * LANES, g * gw + (pr + 1) * LANES)
            xp = xdt_b[:, cols]
            ms, xs_blocks = [], []
            for half in range(LANES // B_HEADDIM):
                h = g * heads_per_group + pr * (LANES // B_HEADDIM) + half
                diff = a_cum[:, h:h + 1] - a_cum_t[h:h + 1, :]
                lmat = jnp.where(causal, jnp.exp(jnp.where(causal, diff, 0.0)), 0.0)
                ms.append((scores * lmat).astype(BF16))
                in_half = (lane // B_HEADDIM) == half
                xs_blocks.append(jnp.where(in_half, xp, jnp.zeros_like(xp)))
            y_pairs.append(jnp.dot(jnp.concatenate(ms, axis=1), jnp.concatenate(xs_blocks, axis=0),
                                   preferred_element_type=F32))
            if pr % 2 == 1:
                yield
        y_groups.append(jnp.concatenate(y_pairs, axis=1) + y_off)
    y = jnp.concatenate(y_groups, axis=1) + dsk_ref[...] * x
    y = y * _silu(z_ref[...])
    outs = []
    for g in range(B_GROUPS):
        yg = y[:, g * gw:(g + 1) * gw]
        outs.append(yg * lax.rsqrt(jnp.mean(yg * yg, axis=-1, keepdims=True) + NORM_EPS))
    yb_ref[...] = (jnp.concatenate(outs, axis=1) * gn_ref[...]).astype(yb_ref.dtype)

    xpad[...] = cur[t - pad:t, :]
    yield


def _ssd_reset(s0_ref, c0_ref, xpad, st, first):
    @pl.when(first)
    def _():
        if s0_ref is not None:
            xpad[...] = c0_ref[0]
            st[...] = s0_ref[0].reshape(st.shape[1], st.shape[0]).T
        else:
            xpad[...] = jnp.zeros(xpad.shape, F32)
            st[...] = jnp.zeros(st.shape, F32)


def _ssd_flush(conv_out_ref, s_out_ref, xpad, st, last):
    @pl.when(last)
    def _():
        conv_out_ref[0] = xpad[...]
        s_out_ref[0] = st[...].T.reshape(s_out_ref.shape[1:])


def _ssd_operands(p, cols, prm, t, n_rows, n_seq, row_block, seq_block):
    bw = prm["bw"]
    heads = bw // B_HEADDIM
    n_bc = B_GROUPS * B_DSTATE
    cdim = bw + 2 * n_bc
    tri = np.arange(t)[None, :] <= np.arange(t)[:, None]
    sel = jnp.asarray(np.concatenate([tri, np.ones((t, t), bool)], 0).astype(np.float32), dtype=BF16)
    expand = np.zeros((LANES, bw), np.float32)
    expand[np.arange(bw) // B_HEADDIM, np.arange(bw)] = 1.0
    expand = jnp.asarray(expand, dtype=BF16)

    def blk(width, off):
        return pl.BlockSpec((t, width), lambda *g: (row_block(*g), off // width))

    def full(shape):
        return pl.BlockSpec(shape, lambda *g: (0,) * len(shape))

    in_specs = [blk(bw, cols["z"]), blk(bw, cols["xs"]), blk(2 * n_bc, cols["bc"]), blk(LANES, cols["dt"]),
                full((CONV_W, cdim)), full((1, cdim)), full((1, LANES)), full((1, LANES)),
                full((1, bw)), full((1, bw)), full(expand.shape), full(sel.shape)]
    args = [p, p, p, p, prm["conv_w"], prm["conv_b"], prm["dt_bias"], prm["a_log"], prm["d_skip"], prm["gn"],
            expand, sel]
    assert len(args) == SSD_N_INPUTS
    out_specs = [pl.BlockSpec((t, bw), lambda *g: (row_block(*g), 0)),
                 pl.BlockSpec((1, heads, B_HEADDIM, B_DSTATE), lambda *g: (seq_block(*g), 0, 0, 0)),
                 pl.BlockSpec((1, SUBLANES, cdim), lambda *g: (seq_block(*g), 0, 0))]
    out_shape = [jax.ShapeDtypeStruct((n_rows, bw), BF16),
                 jax.ShapeDtypeStruct((n_seq, heads, B_HEADDIM, B_DSTATE), F32),
                 jax.ShapeDtypeStruct((n_seq, SUBLANES, cdim), F32)]
    scratch = [pltpu.VMEM((SUBLANES, cdim), F32), pltpu.VMEM((B_DSTATE, bw), F32)]
    return in_specs, args, out_specs, out_shape, scratch


def _ssd(p, cols, prm, s0, c0, bsz, seq_len):
    bw = prm["bw"]
    heads = bw // B_HEADDIM
    cdim = bw + 2 * B_GROUPS * B_DSTATE
    t = min(seq_len, SSD_ROWS)
    nt = seq_len // t
    in_specs, args, out_specs, out_shape, scratch = _ssd_operands(
        p, cols, prm, t, bsz * seq_len, bsz, lambda b, ti: b * nt + ti, lambda b, ti: b)
    if s0 is not None:
        layer = prm["layer"]
        in_specs += [pl.BlockSpec((None, 1, heads, B_HEADDIM, B_DSTATE), lambda b, ti: (layer, b, 0, 0, 0)),
                     pl.BlockSpec((None, 1, SUBLANES, cdim), lambda b, ti: (layer, b, 0, 0))]
        args += [s0, c0]
    return pl.pallas_call(
        functools.partial(_ssd_kernel, t=t, bw=bw, has_init=s0 is not None),
        grid=(bsz, nt),
        in_specs=in_specs,
        out_specs=out_specs,
        out_shape=out_shape,
        scratch_shapes=scratch,
        compiler_params=_cparams(("arbitrary", "arbitrary")),
        name="ssd",
    )(*args)


def _down_sub_dots(a_ref, w_ref, x_ref, gate_ref, o_ref, acc_scr, nseg):
    kc = a_ref.shape[1] // DOWN_K_SPLIT

    def sub_dot(ki):
        rows = slice(ki * kc, (ki + 1) * kc)
        part = jnp.dot(a_ref[:, rows], w_ref[rows, :], preferred_element_type=F32)
        if ki == 0:
            acc_scr[...] = part
        elif ki < DOWN_K_SPLIT - 1:
            acc_scr[...] += part
        else:
            x = x_ref[...]
            acc = acc_scr[...] + part
            o_ref[...] = (_per_seq(x, nseg) + gate_ref[...] * _per_seq(acc, nseg)).reshape(x.shape)

    return [functools.partial(sub_dot, ki) for ki in range(DOWN_K_SPLIT)]


def _emit_interleaved(fillers, phases):
    per_filler = -(-len(phases) // max(len(fillers), 1))
    pos = 0
    for fill in fillers:
        fill()
        for ph in phases[pos:pos + per_filler]:
            ph()
        pos += per_filler
    for ph in phases[pos:]:
        ph()


def _mlp_down_ssd_kernel(a_ref, w_ref, x_ref, gate_ref, *refs, nseg, t, bw, steps_per_seq):
    ssd_in = refs[:SSD_N_INPUTS]
    o_ref, yb_ref, s_out_ref, conv_out_ref, acc_scr, xpad, st = refs[SSD_N_INPUTS:]
    step = pl.program_id(0) * pl.num_programs(1) + pl.program_id(1)
    pos = step % steps_per_seq
    _ssd_reset(None, None, xpad, st, pos == 0)
    gen = _ssd_phases(*ssd_in, yb_ref, xpad, st, t=t, bw=bw)
    phases = [functools.partial(next, gen, None) for _ in range(SSD_PHASES)]
    _emit_interleaved(_down_sub_dots(a_ref, w_ref, x_ref, gate_ref, o_ref, acc_scr, nseg), phases)
    _ssd_flush(conv_out_ref, s_out_ref, xpad, st, pos == steps_per_seq - 1)


def _mlp_down_ssd(a, w_stack, layer, x, mod6, k_gate, seq_len, p_other, cols, prm):
    n, kdim = a.shape
    d = w_stack.shape[2]
    tn = DOWN_TILE
    tm, nseg, tps = _row_tiling(n, seq_len, ROW_TILE)
    ni, nj = n // tm, d // tn
    n_other = p_other.shape[0]
    t = n_other // (ni * nj)
    assert t * ni * nj == n_other and seq_len % t == 0 and t % SUBLANES == 0 and kdim % DOWN_K_SPLIT == 0
    steps_per_seq = seq_len // t
    s_in_specs, s_args, s_out_specs, s_out_shape, s_scratch = _ssd_operands(
        p_other, cols, prm, t, n_other, n_other // seq_len,
        lambda i, j: i * nj + j, lambda i, j: (i * nj + j) // steps_per_seq)
    return pl.pallas_call(
        functools.partial(_mlp_down_ssd_kernel, nseg=nseg, t=t, bw=prm["bw"], steps_per_seq=steps_per_seq),
        grid=(ni, nj),
        in_specs=[
            pl.BlockSpec((tm, kdim), lambda i, j: (i, 0)),
            pl.BlockSpec((None, kdim, tn), lambda i, j: (layer, 0, j)),
            pl.BlockSpec((tm, tn), lambda i, j: (i, j)),
            pl.BlockSpec((None, nseg, 1, tn), lambda i, j: (k_gate, i // tps, 0, j)),
        ] + s_in_specs,
        out_specs=[pl.BlockSpec((tm, tn), lambda i, j: (i, j))] + s_out_specs,
        out_shape=[jax.ShapeDtypeStruct((n, d), F32)] + s_out_shape,
        scratch_shapes=[pltpu.VMEM((tm, tn), F32)] + s_scratch,
        compiler_params=_cparams(("arbitrary", "arbitrary")),
        name="mlp_down_ssd",
    )(a, w_stack, x, mod6, *s_args)


def _mlp_up_ssd_kernel(x_ref, g_ref, sc_ref, sh_ref, w_ref, *refs, nseg, t, bw, steps_per_seq):
    ssd_in = refs[:SSD_N_INPUTS]
    o_ref, yb_ref, s_out_ref, conv_out_ref, h_scr, xpad, st = refs[SSD_N_INPUTS:]
    j = pl.program_id(1)
    pos = (pl.program_id(0) * pl.num_programs(1) + j) % steps_per_seq

    @pl.when(j == 0)
    def _():
        _norm_mod_rows(x_ref, g_ref, sc_ref, sh_ref, h_scr, nseg)

    _ssd_reset(None, None, xpad, st, pos == 0)

    def sub_dot(c0):
        cols = slice(c0, c0 + EPILOGUE_COLS)
        y = jnp.maximum(jnp.dot(h_scr[...], w_ref[:, cols], preferred_element_type=F32), 0.0)
        o_ref[:, cols] = (y * y).astype(o_ref.dtype)

    gen = _ssd_phases(*ssd_in, yb_ref, xpad, st, t=t, bw=bw)
    _emit_interleaved([functools.partial(sub_dot, c0) for c0 in range(0, w_ref.shape[1], EPILOGUE_COLS)],
                      [functools.partial(next, gen, None) for _ in range(SSD_PHASES)])
    _ssd_flush(conv_out_ref, s_out_ref, xpad, st, pos == steps_per_seq - 1)


def _mlp_up_ssd(x, g, mod6, w_stack, layer, seq_len, p_other, cols, prm):
    n, d = x.shape
    m = w_stack.shape[2]
    tn = COL_TILE
    tm, nseg, tps = _row_tiling(n, seq_len, ROW_TILE)
    ni, nj = n // tm, m // tn
    n_other = p_other.shape[0]
    t = n_other // (ni * nj)
    assert t * ni * nj == n_other and seq_len % t == 0 and t % SUBLANES == 0
    steps_per_seq = seq_len // t
    s_in_specs, s_args, s_out_specs, s_out_shape, s_scratch = _ssd_operands(
        p_other, cols, prm, t, n_other, n_other // seq_len,
        lambda i, j: i * nj + j, lambda i, j: (i * nj + j) // steps_per_seq)

    def mod_spec(k):
        return pl.BlockSpec((None, nseg, 1, d), lambda i, j: (k, i // tps, 0, 0))

    return pl.pallas_call(
        functools.partial(_mlp_up_ssd_kernel, nseg=nseg, t=t, bw=prm["bw"], steps_per_seq=steps_per_seq),
        grid=(ni, nj),
        in_specs=[
            pl.BlockSpec((tm, d), lambda i, j: (i, 0)),
            pl.BlockSpec((1, d), lambda i, j: (0, 0)),
            mod_spec(4),
            mod_spec(3),
            pl.BlockSpec((None, d, tn), lambda i, j: (layer, 0, j)),
        ] + s_in_specs,
        out_specs=[pl.BlockSpec((tm, tn), lambda i, j: (i, j))] + s_out_specs,
        out_shape=[jax.ShapeDtypeStruct((n, m), BF16)] + s_out_shape,
        scratch_shapes=[pltpu.VMEM((tm, d), BF16)] + s_scratch,
        compiler_params=_cparams(("arbitrary", "arbitrary")),
        name="mlp_up_ssd",
    )(x, g.reshape(1, d), mod6, mod6, w_stack, *s_args)


def _cmlp_kernel(u_ref, v_ref, lng_ref, lnb_ref, ws_ref, bst_ref, *out_refs, t, n_chunks, keep_v):
    yc_ref = out_refs[0]
    cw = u_ref.shape[1] // C_GROUPS
    rr = lax.broadcasted_iota(jnp.int32, (t, t), 0)
    cc = lax.broadcasted_iota(jnp.int32, (t, t), 1)
    wts = [jnp.where(cc <= rr, ws_ref[g, 0:t, 0:t], 0.0).astype(BF16) for g in range(C_GROUPS)]
    for ci in range(n_chunks):
        rows = slice(ci * t, (ci + 1) * t)
        u = _gelu(u_ref[rows, :])
        gv = _gelu(v_ref[rows, :])
        mu = jnp.mean(gv, axis=-1, keepdims=True)
        dv = gv - mu
        var = jnp.mean(dv * dv, axis=-1, keepdims=True)
        v = dv * lax.rsqrt(var + NORM_EPS) * lng_ref[...] + lnb_ref[...]
        if keep_v:
            out_refs[1][rows, :] = v
        vb = v.astype(BF16)
        for g in range(C_GROUPS):
            lanes = slice(g * cw, (g + 1) * cw)
            mixed = jnp.dot(wts[g], vb[:, lanes], preferred_element_type=F32) + bst_ref[0:t, g:g + 1]
            yc_ref[rows, lanes] = (u[:, lanes] * mixed).astype(yc_ref.dtype)


def _cmlp(p, col_u, col_v, prm, bsz, seq_len, keep_v):
    cw = prm["cw"]
    t = min(seq_len, CMLP_CHUNK)
    t_blk = min(seq_len, 4 * CMLP_CHUNK)
    n = bsz * seq_len

    def full(shape):
        return pl.BlockSpec(shape, lambda i: (0,) * len(shape))

    out_specs = [pl.BlockSpec((t_blk, cw), lambda i: (i, 0))]
    out_shape = [jax.ShapeDtypeStruct((n, cw), BF16)]
    if keep_v:
        out_specs.append(pl.BlockSpec((t_blk, cw), lambda i: (i, 0)))
        out_shape.append(jax.ShapeDtypeStruct((n, cw), F32))
    return pl.pallas_call(
        functools.partial(_cmlp_kernel, t=t, n_chunks=t_blk // t, keep_v=keep_v),
        grid=(n // t_blk,),
        in_specs=[pl.BlockSpec((t_blk, cw), lambda i: (i, col_u // cw)),
                  pl.BlockSpec((t_blk, cw), lambda i: (i, col_v // cw)),
                  full((1, cw)), full((1, cw)),
                  full((C_GROUPS, CMLP_CHUNK, CMLP_CHUNK)), full((CMLP_CHUNK, C_GROUPS))],
        out_specs=out_specs,
        out_shape=out_shape,
        compiler_params=_cparams(("arbitrary",)),
        name="cmlp",
    )(p, p, prm["ln_g"], prm["ln_b"], prm["ws"], prm["bs_t"])


def _run_trunk(x3, mod, st_hgrn, st_ssm, st_conv, keep_v, w):
    bsz, seq_len, d = x3.shape
    depth = mod.shape[0]
    aw = bw = cw = d // 2
    heads_a = aw // A_DK
    x = x3.reshape(bsz * seq_len, d)
    col = w["cols"]
    hgrn_out, ssm_out, conv_out, v_out = [], [], [], []
    for l in range(depth):
        mod6 = mod[l].reshape(bsz, 6, 1, d).transpose(1, 0, 2, 3)
        p, gates = _in_proj(x, w["norm1_g"][l], mod6, w["w_in"], l, seq_len, col["gate"])

        y_a, s_h = _hgrn(p, w["lbc"], l, w["hgrn_onorm_g"][l], st_hgrn, bsz, seq_len, heads_a)
        ssd_prm = dict(bw=bw, layer=l, conv_w=w["ssm_conv_w"][l], conv_b=w["ssm_conv_b"][l][None],
                       dt_bias=w["dt_bias_pad"][l][None], a_log=w["a_log_pad"][l][None],
                       d_skip=w["d_skip"][l][None], gn=w["ssm_onorm_g"][l][None])
        y_b, s_s, conv_tail = _ssd(p, col, ssd_prm, st_ssm, st_conv, bsz, seq_len)
        cm_prm = dict(cw=cw, ln_g=w["cmlp_ln_g"][l][None], ln_b=w["cmlp_ln_b"][l][None],
                      ws=w["cmlp_ws"][l], bs_t=w["cmlp_bs"][l].T)
        c_res = _cmlp(p, col["u"], col["v"], cm_prm, bsz, seq_len, keep_v)
        merged = _merge(y_a, y_b, c_res[0], w["w_branch"], l, gates, d)
        x = _proj_residual(merged, w["w_out"], l, x, mod6, 2, seq_len, "out_proj")
        hid = _mlp_up(x, w["norm2_g"][l], mod6, w["w_up"], l, seq_len)
        x = _proj_residual(hid, w["w_down"], l, x, mod6, 5, seq_len, "mlp_down")

        hgrn_out.append(s_h)
        ssm_out.append(s_s)
        conv_out.append(conv_tail[:, SUBLANES - (CONV_W - 1):, :])
        if keep_v:
            v_out.append(c_res[1].reshape(bsz, seq_len, cw))
    y = _final_norm(x, w["final_g"]).reshape(bsz, seq_len, d)
    return (y, jnp.stack(hgrn_out), jnp.stack(ssm_out), jnp.stack(conv_out),
            jnp.stack(v_out) if keep_v else None)


def _mlp_down_hgrn_kernel(a_ref, w_ref, x_ref, gate_ref, q_ref, f_ref, i_ref, gg_ref, lbc_ref, gn_ref, sel_ref,
                          o_ref, ya_ref, s_ref, acc_scr, bpad, kpad, *, nseg, c, n_chunks, heads, steps_per_seq):
    step = pl.program_id(0) * pl.num_programs(1) + pl.program_id(1)

    @pl.when(step % steps_per_seq == 0)
    def _():
        s_ref[...] = jnp.zeros(s_ref.shape, F32)

    fillers = _down_sub_dots(a_ref, w_ref, x_ref, gate_ref, o_ref, acc_scr, nseg)
    _hgrn_chunks(q_ref, f_ref, i_ref, gg_ref, lbc_ref, gn_ref, sel_ref, ya_ref, s_ref, bpad, kpad,
                 c=c, n_chunks=n_chunks, heads=heads, unroll=True, fillers=fillers)


def _mlp_down_hgrn(a, w_stack, layer, x, mod6, k_gate, seq_len, p_other, lbc_stack, lbc_layer, gn, heads):
    n, kdim = a.shape
    d = w_stack.shape[2]
    tn = DOWN_TILE
    tm, nseg, tps = _row_tiling(n, seq_len, ROW_TILE)
    ni, nj = n // tm, d // tn
    n_other = p_other.shape[0]
    t_step = n_other // (ni * nj)
    assert t_step * ni * nj == n_other and seq_len % t_step == 0 and t_step % SUBLANES == 0
    assert kdim % DOWN_K_SPLIT == 0
    steps_per_seq = seq_len // t_step
    aw = heads * A_DK
    c, h_in_specs, h_args, ya_spec, h_scratch = _hgrn_specs(
        p_other, lbc_stack, lbc_layer, gn, heads, t_step, lambda i, j: i * nj + j)
    state_spec = pl.BlockSpec((1, heads, A_DK, A_DK), lambda i, j: ((i * nj + j) // steps_per_seq, 0, 0, 0))
    return pl.pallas_call(
        functools.partial(_mlp_down_hgrn_kernel, nseg=nseg, c=c, n_chunks=t_step // c, heads=heads,
                          steps_per_seq=steps_per_seq),
        grid=(ni, nj),
        in_specs=[
            pl.BlockSpec((tm, kdim), lambda i, j: (i, 0)),
            pl.BlockSpec((None, kdim, tn), lambda i, j: (layer, 0, j)),
            pl.BlockSpec((tm, tn), lambda i, j: (i, j)),
            pl.BlockSpec((None, nseg, 1, tn), lambda i, j: (k_gate, i // tps, 0, j)),
        ] + h_in_specs,
        out_specs=[pl.BlockSpec((tm, tn), lambda i, j: (i, j)), ya_spec, state_spec],
        out_shape=[jax.ShapeDtypeStruct((n, d), F32),
                   jax.ShapeDtypeStruct((n_other, aw), BF16),
                   jax.ShapeDtypeStruct((n_other // seq_len, heads, A_DK, A_DK), F32)],
        scratch_shapes=[pltpu.VMEM((tm, tn), F32)] + h_scratch,
        compiler_params=_cparams(("arbitrary", "arbitrary")),
        name="mlp_down_hgrn2",
    )(a, w_stack, x, mod6, *h_args)


def _run_trunk_halves(x3, mod, w):
    bsz, seq_len, d = x3.shape
    depth = mod.shape[0]
    hb = bsz // 2
    aw = bw = cw = d // 2
    heads_a = aw // A_DK
    col = w["cols"]
    n_half = hb * seq_len
    x_full = x3.reshape(bsz * seq_len, d)
    xs = [x_full, x_full]
    row0 = [0, n_half]
    mods = [mod[:, :hb], mod[:, hb:]]
    hgrn_out, ssm_out, conv_out = [], [], []
    pending = None
    ahead = None

    def mod6_of(layer, h):
        return mods[h][layer].reshape(hb, 6, 1, d).transpose(1, 0, 2, 3)

    def ssd_prm_of(layer):
        return dict(bw=bw, layer=layer, conv_w=w["ssm_conv_w"][layer], conv_b=w["ssm_conv_b"][layer][None],
                    dt_bias=w["dt_bias_pad"][layer][None], a_log=w["a_log_pad"][layer][None],
                    d_skip=w["d_skip"][layer][None], gn=w["ssm_onorm_g"][layer][None])

    for l in range(depth):
        mod6 = [mod6_of(l, 0), mod6_of(l, 1)]
        ssd_prm = ssd_prm_of(l)
        cm_prm = dict(cw=cw, ln_g=w["cmlp_ln_g"][l][None], ln_b=w["cmlp_ln_b"][l][None],
                      ws=w["cmlp_ws"][l], bs_t=w["cmlp_bs"][l].T)
        gn_a = w["hgrn_onorm_g"][l]

        def mix_merge(h, p, gates, y_a, ssd_res=None):
            y_b, s_s, conv_tail = ssd_res or _ssd(p, col, ssd_prm, None, None, hb, seq_len)
            y_c = _cmlp(p, col["u"], col["v"], cm_prm, hb, seq_len, False)[0]
            merged = _merge(y_a, y_b, y_c, w["w_branch"], l, gates, d)
            x_new = _proj_residual(merged, w["w_out"], l, xs[h], mod6[h], 2, seq_len, "out_proj", row0[h])
            row0[h] = 0
            return x_new, s_s, conv_tail

        if ahead is None:
            p0, g0 = _in_proj(xs[0], w["norm1_g"][l], mod6[0], w["w_in"], l, seq_len, col["gate"],
                              (row0[0], n_half))
            ssd0 = None
        else:
            p0, g0, ssd0 = ahead
        if pending is None:
            ya0, sh0 = _hgrn(p0, w["lbc"], l, gn_a, None, hb, seq_len, heads_a)
        else:
            hid1, x1_mid, mod6_prev = pending
            xs[1], ya0, sh0 = _mlp_down_hgrn(hid1, w["w_down"], l - 1, x1_mid, mod6_prev, 5, seq_len,
                                              p0, w["lbc"], l, gn_a, heads_a)
        p1, g1 = _in_proj(xs[1], w["norm1_g"][l], mod6[1], w["w_in"], l, seq_len, col["gate"], (row0[1], n_half))
        x0, ss0, ct0 = mix_merge(0, p0, g0, ya0, ssd0)
        hid0, ya1, sh1 = _mlp_up_hgrn(x0, w["norm2_g"][l], mod6[0], w["w_up"], l, seq_len,
                                      p1, w["lbc"], gn_a, heads_a)
        xs[0], *ssd1 = _mlp_down_ssd(hid0, w["w_down"], l, x0, mod6[0], 5, seq_len, p1, col, ssd_prm)
        x1_mid, ss1, ct1 = mix_merge(1, p1, g1, ya1, tuple(ssd1))
        if l + 1 < depth:
            p0n, g0n = _in_proj(xs[0], w["norm1_g"][l + 1], mod6_of(l + 1, 0), w["w_in"], l + 1, seq_len,
                                col["gate"], (0, n_half))
            hid1, *ssd0n = _mlp_up_ssd(x1_mid, w["norm2_g"][l], mod6[1], w["w_up"], l, seq_len,
                                       p0n, col, ssd_prm_of(l + 1))
            ahead = (p0n, g0n, tuple(ssd0n))
        else:
            hid1 = _mlp_up(x1_mid, w["norm2_g"][l], mod6[1], w["w_up"], l, seq_len)
        pending = (hid1, x1_mid, mod6[1])

        hgrn_out.append(jnp.concatenate([sh0, sh1], axis=0))
        ssm_out.append(jnp.concatenate([ss0, ss1], axis=0))
        conv_out.append(jnp.concatenate([ct0, ct1], axis=0)[:, SUBLANES - (CONV_W - 1):, :])
    hid1, x1_mid, mod6_prev = pending
    xs[1] = _proj_residual(hid1, w["w_down"], depth - 1, x1_mid, mod6_prev, 5, seq_len, "mlp_down")
    y = _final_norm_pair(xs[0], xs[1], w["final_g"]).reshape(bsz, seq_len, d)
    return (y, jnp.stack(hgrn_out), jnp.stack(ssm_out), jnp.stack(conv_out), None)


def kernel(x_prompt, x_sample, state_hgrn, state_ssm, state_conv, c_prompt, c_sample, norm1_g, norm2_g,
           w_mod, b_mod, w_in, hgrn_lb, hgrn_onorm_g, ssm_conv_w, ssm_conv_b, ssm_dt_bias, ssm_a_log, ssm_d,
           ssm_onorm_g, cmlp_ln_g, cmlp_ln_b, cmlp_ws, cmlp_bs, w_branch, w_out, w_up, w_down, final_g):
    d = x_prompt.shape[-1]
    depth = w_in.shape[0]
    aw = bw = cw = d // 2
    n_bc = B_GROUPS * B_DSTATE
    heads_b = bw // B_HEADDIM
    assert heads_b <= LANES and bw % LANES == 0 and DT_PAD % LANES == 0

    o_dt = 4 * aw + bw + bw + 2 * n_bc
    o_u = o_dt + heads_b
    w_in_t = jnp.swapaxes(w_in, 1, 2).astype(BF16)
    w_in_r = jnp.concatenate(
        [w_in_t[:, :o_u], jnp.zeros((depth, DT_PAD - heads_b, d), BF16), w_in_t[:, o_u:]], axis=1)
    cols = dict(z=4 * aw, xs=4 * aw + bw, bc=4 * aw + 2 * bw, dt=o_dt)
    cols["u"] = o_dt + DT_PAD
    cols["v"] = cols["u"] + cw
    cols["gate"] = cols["v"] + cw
    assert cols["gate"] % IN_TILE == 0 and w_in_r.shape[1] % IN_TILE == 0

    pad_h = lambda a: jnp.pad(a.astype(F32), ((0, 0), (0, LANES - heads_b)))
    w = dict(
        cols=cols, w_in=w_in_r,
        norm1_g=norm1_g, norm2_g=norm2_g, final_g=final_g,
        lbc=_lb_consts(hgrn_lb), hgrn_onorm_g=hgrn_onorm_g,
        ssm_conv_w=ssm_conv_w, ssm_conv_b=ssm_conv_b,
        dt_bias_pad=pad_h(ssm_dt_bias), a_log_pad=pad_h(ssm_a_log),
        d_skip=jnp.repeat(ssm_d.astype(F32), B_HEADDIM, axis=1), ssm_onorm_g=ssm_onorm_g,
        cmlp_ln_g=cmlp_ln_g, cmlp_ln_b=cmlp_ln_b, cmlp_ws=cmlp_ws, cmlp_bs=cmlp_bs,
        w_branch=w_branch.astype(BF16).reshape(depth, N_BRANCH, aw, d), w_out=w_out.astype(BF16),
        w_up=w_up.astype(BF16), w_down=w_down.astype(BF16),
    )

    nb = x_prompt.shape[0]
    mod = _modulation(jnp.concatenate([c_prompt, c_sample], axis=0), w_mod, b_mod)
    if nb % 2 == 0:
        y_p, hgrn_p, ssm_p, conv_p, _ = _run_trunk_halves(x_prompt, mod[:, :nb], w)
    else:
        y_p, hgrn_p, ssm_p, conv_p, _ = _run_trunk(x_prompt, mod[:, :nb], None, None, None, False, w)
    conv_pad = jnp.pad(state_conv, ((0, 0), (0, 0), (SUBLANES - (CONV_W - 1), 0), (0, 0)))
    y_s, hgrn_s, ssm_s, conv_s, v_s = _run_trunk(x_sample, mod[:, nb:], state_hgrn, state_ssm, conv_pad, True, w)
    return (y_p, y_s, hgrn_p, ssm_p, conv_p, hgrn_s, ssm_s, conv_s, v_s)
```
